```python
import math
import jax, jax.numpy as jnp
from jax import lax
import numpy as np

D_MODEL = 1024
BATCH = 16
SEQ = 256
DEPTH = 4
DEC_BATCH = 8
DEC_SEQ = 1024
PAST_LEN = 512

GRID_W = 64
N_EVEN = (DEPTH + 1) // 2
N_ODD = DEPTH // 2
EPS = 1e-6
ROPE_BASE = 10000.0
Q_BLOCK = 128
S5_WIDTH = D_MODEL // 2
S5_GROUP = 16
S5_GROUPS = S5_WIDTH // S5_GROUP
S5_STATE = 64
DIFF_HEAD_DIM = 64
DIFF_HEADS = (D_MODEL // 2) // (2 * DIFF_HEAD_DIM)
DIFF_WIDTH = DIFF_HEADS * 2 * DIFF_HEAD_DIM
EVEN_IN = S5_WIDTH + 3 * DIFF_WIDTH
EVEN_OUT = S5_WIDTH + DIFF_WIDTH
MLA_HEADS = 16
MLA_NOPE = 64
MLA_ROPE = 32
MLA_V = 64
MLA_Q_RANK = 256
MLA_KV_RANK = 128
ODD_IN = MLA_Q_RANK + MLA_KV_RANK + MLA_ROPE
D_FF = ((8 * D_MODEL // 3 + 255) // 256) * 256

kernel_name = 'hybrid_s5_diffattn_mla_flow_step'


def rmsnorm(x, g):
    xf = x.astype(jnp.float32)
    y = xf * lax.rsqrt(jnp.mean(xf * xf, axis=-1, keepdims=True) + EPS)
    return (y * g.astype(jnp.float32)).astype(x.dtype)


def axial_rope(n_tokens, rot_dim):
    rows = n_tokens // GRID_W
    row = jnp.repeat(jnp.arange(rows, dtype=jnp.float32), GRID_W)
    col = jnp.tile(jnp.arange(GRID_W, dtype=jnp.float32), rows)
    n_freq = rot_dim // 4
    inv = ROPE_BASE ** (-jnp.arange(n_freq, dtype=jnp.float32) / n_freq)
    ang = jnp.concatenate([row[:, None] * inv, col[:, None] * inv], axis=-1)
    return jnp.cos(ang), jnp.sin(ang)


def apply_rope(x, cos, sin):
    half = x.shape[-1] // 2
    shape = (x.shape[1],) + (1,) * (x.ndim - 3) + (half,)
    c = cos.reshape(shape).astype(x.dtype)
    s = sin.reshape(shape).astype(x.dtype)
    x1, x2 = x[..., :half], x[..., half:]
    return jnp.concatenate([x1 * c - x2 * s, x2 * c + x1 * s], axis=-1)


def map_query_blocks(fn, *qs):
    b, n = qs[0].shape[:2]
    nb = n // Q_BLOCK
    blocked = tuple(jnp.moveaxis(q.reshape((b, nb, Q_BLOCK) + q.shape[2:]), 1, 0) for q in qs)
    out = lax.map(lambda qb: fn(*qb), blocked)
    return jnp.moveaxis(out, 0, 1).reshape((b, n) + out.shape[3:])


def complex_affine_combine(e1, e2):
    a1r, a1i, b1r, b1i = e1
    a2r, a2i, b2r, b2i = e2
    ar = a2r * a1r - a2i * a1i
    ai = a2r * a1i + a2i * a1r
    br = a2r * b1r - a2i * b1i + b2r
    bi = a2r * b1i + a2i * b1r + b2i
    return ar, ai, br, bi


def s5_direction(u, h0_re, h0_im, lam_re, lam_im, log_dt, b_re, b_im, c_re, c_im, reverse):
    f32 = jnp.float32
    lr, li = lam_re.astype(f32), lam_im.astype(f32)
    dt = jnp.exp(log_dt.astype(f32))[:, None]
    mag = jnp.exp(lr * dt)
    a_re, a_im = mag * jnp.cos(li * dt), mag * jnp.sin(li * dt)
    den = lr * lr + li * li
    f_re = ((a_re - 1.0) * lr + a_im * li) / den
    f_im = (a_im * lr - (a_re - 1.0) * li) / den
    br, bi = b_re.astype(f32), b_im.astype(f32)
    bb_re = f_re[..., None] * br - f_im[..., None] * bi
    bb_im = f_re[..., None] * bi + f_im[..., None] * br
    bu_re = jnp.einsum('bngh,gph->bngp', u, bb_re)
    bu_im = jnp.einsum('bngh,gph->bngp', u, bb_im)
    edge = -1 if reverse else 0
    h0r, h0i = h0_re.astype(f32), h0_im.astype(f32)
    bu_re = bu_re.at[:, edge].add(a_re * h0r - a_im * h0i)
    bu_im = bu_im.at[:, edge].add(a_re * h0i + a_im * h0r)
    ar = jnp.broadcast_to(a_re, bu_re.shape)
    ai = jnp.broadcast_to(a_im, bu_re.shape)
    _, _, hr, hi = lax.associative_scan(complex_affine_combine, (ar, ai, bu_re, bu_im), reverse=reverse, axis=1)
    y = jnp.einsum('bngp,ghp->bngh', hr, c_re.astype(f32)) - jnp.einsum('bngp,ghp->bngh', hi, c_im.astype(f32))
    fin = 0 if reverse else -1
    return y, hr[:, fin], hi[:, fin]


def s5_mixer(u, h0_re, h0_im, lam_re, lam_im, log_dt, b_re, b_im, c_re, c_im, d, glu_w, glu_b):
    b, n, _ = u.shape
    uf = u.astype(jnp.float32).reshape(b, n, S5_GROUPS, S5_GROUP)
    y_f, fr_f, fi_f = s5_direction(uf, h0_re[:, 0], h0_im[:, 0], lam_re[0], lam_im[0], log_dt[0],
                                   b_re[0], b_im[0], c_re[0], c_im[0], False)
    y_b, fr_b, fi_b = s5_direction(uf, h0_re[:, 1], h0_im[:, 1], lam_re[1], lam_im[1], log_dt[1],
                                   b_re[1], b_im[1], c_re[1], c_im[1], True)
    y = y_f + y_b + d.astype(jnp.float32).reshape(S5_GROUPS, S5_GROUP) * uf
    g = jax.nn.gelu(y.reshape(b, n, S5_WIDTH))
    out = g * jax.nn.sigmoid(g @ glu_w.astype(jnp.float32) + glu_b.astype(jnp.float32))
    fin_re = jnp.stack([fr_f, fr_b], axis=1)
    fin_im = jnp.stack([fi_f, fi_b], axis=1)
    return out.astype(u.dtype), fin_re, fin_im


def diff_attention(q, k, v, lam, lam_init, subln_g):
    b, n = q.shape[:2]
    scale = DIFF_HEAD_DIM ** -0.5
    kf, vf = k.astype(jnp.float32), v.astype(jnp.float32)

    def attend(qb):
        s = jnp.einsum('bqhcd,bkhcd->bhcqk', qb.astype(jnp.float32), kf) * scale
        p = jax.nn.softmax(s, axis=-1)
        w = p[:, :, 0] - lam * p[:, :, 1]
        return jnp.einsum('bhqk,bkhe->bqhe', w, vf)

    o = map_query_blocks(attend, q)
    o = rmsnorm(o, subln_g) * (1.0 - lam_init)
    return o.reshape(b, n, DIFF_WIDTH).astype(q.dtype)


def mla_attention(qn, qr, kn, kr, v):
    b, n = qn.shape[:2]
    scale = (MLA_NOPE + MLA_ROPE) ** -0.5
    knf, krf, vf = kn.astype(jnp.float32), kr.astype(jnp.float32), v.astype(jnp.float32)

    def attend(qnb, qrb):
        s = (jnp.einsum('bqhd,bkhd->bhqk', qnb.astype(jnp.float32), knf)
             + jnp.einsum('bqhr,bkr->bhqk', qrb.astype(jnp.float32), krf)) * scale
        p = jax.nn.softmax(s, axis=-1)
        return jnp.einsum('bhqk,bkhd->bqhd', p, vf)

    o = map_query_blocks(attend, qn, qr)
    return o.reshape(b, n, MLA_HEADS * MLA_V).astype(qn.dtype)


def block(x, cond, w_mod, b_mod, g_pre_mix, g_post_mix, g_pre_ffn, g_post_ffn, w_gate, w_up, w_down, mixer):
    mod = (jax.nn.silu(cond) @ w_mod + b_mod)[:, None, :]
    sh1, sc1, g1, sh2, sc2, g2 = jnp.split(mod, 6, axis=-1)
    h = rmsnorm(x, g_pre_mix) * (1.0 + sc1) + sh1
    y, aux = mixer(h)
    x = x + g1 * rmsnorm(y, g_post_mix)
    h = rmsnorm(x, g_pre_ffn) * (1.0 + sc2) + sh2
    y = (jax.nn.silu(h @ w_gate) * (h @ w_up)) @ w_down
    x = x + g2 * rmsnorm(y, g_post_ffn)
    return x, aux


def setup_inputs(seed: int = 0) -> dict:
    key = jax.random.key(seed)
    ks = iter(jax.random.split(key, 64))
    f32 = jnp.float32

    def nrm(shape, scale=1.0):
        return jax.random.normal(next(ks), shape, f32) * scale

    def gain(shape):
        return 1.0 + nrm(shape, 0.01)

    lam_re = -0.5 + nrm((N_EVEN, 2, S5_GROUPS, S5_STATE), 0.01)
    lam_im = math.pi * jnp.arange(S5_STATE, dtype=f32) + nrm((N_EVEN, 2, S5_GROUPS, S5_STATE), 0.01)
    log_dt = jax.random.uniform(next(ks), (N_EVEN, 2, S5_GROUPS), f32, math.log(1e-3), math.log(1e-1))
    return {
        'x_prompt': nrm((BATCH, SEQ, D_MODEL)),
        'x_sample': nrm((DEC_BATCH, DEC_SEQ, D_MODEL)),
        'state_s5_re': nrm((DEC_BATCH, N_EVEN, 2, S5_GROUPS, S5_STATE), 0.5),
        'state_s5_im': nrm((DEC_BATCH, N_EVEN, 2, S5_GROUPS, S5_STATE), 0.5),
        'cache_diff_k': nrm((DEC_BATCH, N_EVEN, PAST_LEN, DIFF_HEADS, 2, DIFF_HEAD_DIM)),
        'cache_diff_v': nrm((DEC_BATCH, N_EVEN, PAST_LEN, DIFF_HEADS, 2 * DIFF_HEAD_DIM)),
        'cache_mla_ckv': nrm((DEC_BATCH, N_ODD, PAST_LEN, MLA_KV_RANK)),
        'cache_mla_krope': nrm((DEC_BATCH, N_ODD, PAST_LEN, MLA_ROPE)),
        'c': nrm((DEC_BATCH, D_MODEL)),
        'c_ctx': nrm((D_MODEL,)),
        'w_mod': nrm((DEPTH, D_MODEL, 6 * D_MODEL), D_MODEL ** -0.5),
        'b_mod': nrm((DEPTH, 6 * D_MODEL), 0.01),
        'g_pre_mix': gain((DEPTH, D_MODEL)),
        'g_post_mix': gain((DEPTH, D_MODEL)),
        'g_pre_ffn': gain((DEPTH, D_MODEL)),
        'g_post_ffn': gain((DEPTH, D_MODEL)),
        'w_ffn_gate': nrm((DEPTH, D_MODEL, D_FF), D_MODEL ** -0.5),
        'w_ffn_up': nrm((DEPTH, D_MODEL, D_FF), D_MODEL ** -0.5),
        'w_ffn_down': nrm((DEPTH, D_FF, D_MODEL), D_FF ** -0.5),
        'w_in_even': nrm((N_EVEN, D_MODEL, EVEN_IN), D_MODEL ** -0.5),
        'w_out_even': nrm((N_EVEN, EVEN_OUT, D_MODEL), EVEN_OUT ** -0.5),
        's5_lam_re': lam_re,
        's5_lam_im': lam_im,
        's5_log_dt': log_dt,
        's5_b_re': nrm((N_EVEN, 2, S5_GROUPS, S5_STATE, S5_GROUP), (2 * S5_GROUP) ** -0.5),
        's5_b_im': nrm((N_EVEN, 2, S5_GROUPS, S5_STATE, S5_GROUP), (2 * S5_GROUP) ** -0.5),
        's5_c_re': nrm((N_EVEN, 2, S5_GROUPS, S5_GROUP, S5_STATE), (2 * S5_STATE) ** -0.5),
        's5_c_im': nrm((N_EVEN, 2, S5_GROUPS, S5_GROUP, S5_STATE), (2 * S5_STATE) ** -0.5),
        's5_d': nrm((N_EVEN, S5_WIDTH)),
        's5_glu_w': nrm((N_EVEN, S5_WIDTH, S5_WIDTH), S5_WIDTH ** -0.5),
        's5_glu_b': nrm((N_EVEN, S5_WIDTH), 0.01),
        'diff_lam_q1': nrm((N_EVEN, DIFF_HEAD_DIM), 0.1),
        'diff_lam_k1': nrm((N_EVEN, DIFF_HEAD_DIM), 0.1),
        'diff_lam_q2': nrm((N_EVEN, DIFF_HEAD_DIM), 0.1),
        'diff_lam_k2': nrm((N_EVEN, DIFF_HEAD_DIM), 0.1),
        'diff_subln_g': gain((N_EVEN, 2 * DIFF_HEAD_DIM)),
        'w_in_odd': nrm((N_ODD, D_MODEL, ODD_IN), D_MODEL ** -0.5),
        'mla_q_norm_g': gain((N_ODD, MLA_Q_RANK)),
        'mla_w_q_up': nrm((N_ODD, MLA_Q_RANK, MLA_HEADS * (MLA_NOPE + MLA_ROPE)), MLA_Q_RANK ** -0.5),
        'mla_kv_norm_g': gain((N_ODD, MLA_KV_RANK)),
        'mla_w_kv_up': nrm((N_ODD, MLA_KV_RANK, MLA_HEADS * (MLA_NOPE + MLA_V)), MLA_KV_RANK ** -0.5),
        'w_out_odd': nrm((N_ODD, MLA_HEADS * MLA_V, D_MODEL), (MLA_HEADS * MLA_V) ** -0.5),
    }


def reference(x_prompt, x_sample, state_s5_re, state_s5_im, cache_diff_k, cache_diff_v, cache_mla_ckv,
              cache_mla_krope, c, c_ctx, w_mod, b_mod, g_pre_mix, g_post_mix, g_pre_ffn, g_post_ffn,
              w_ffn_gate, w_ffn_up, w_ffn_down, w_in_even, w_out_even, s5_lam_re, s5_lam_im, s5_log_dt,
              s5_b_re, s5_b_im, s5_c_re, s5_c_im, s5_d, s5_glu_w, s5_glu_b, diff_lam_q1, diff_lam_k1,
              diff_lam_q2, diff_lam_k2, diff_subln_g, w_in_odd, mla_q_norm_g, mla_w_q_up, mla_kv_norm_g,
              mla_w_kv_up, w_out_odd):
    n_lat = x_sample.shape[1]
    cos_d, sin_d = axial_rope(n_lat, DIFF_HEAD_DIM)
    cos_m, sin_m = axial_rope(n_lat, MLA_ROPE)
    cond_ctx = c_ctx[None, :]

    def even_mixer(h, e, lam_init, ctx):
        b, n, _ = h.shape
        proj = h @ w_in_even[e]
        u = proj[..., :S5_WIDTH]
        q = proj[..., S5_WIDTH:S5_WIDTH + DIFF_WIDTH].reshape(b, n, DIFF_HEADS, 2, DIFF_HEAD_DIM)
        k = proj[..., S5_WIDTH + DIFF_WIDTH:S5_WIDTH + 2 * DIFF_WIDTH].reshape(b, n, DIFF_HEADS, 2, DIFF_HEAD_DIM)
        v = proj[..., S5_WIDTH + 2 * DIFF_WIDTH:].reshape(b, n, DIFF_HEADS, 2 * DIFF_HEAD_DIM)
        if ctx is None:
            h0_re = jnp.zeros((b, 2, S5_GROUPS, S5_STATE), jnp.float32)
            h0_im = h0_re
            k_all, v_all = k, v
        else:
            h0_re, h0_im, ck, cv = ctx
            q = apply_rope(q, cos_d, sin_d)
            k = apply_rope(k, cos_d, sin_d)
            k_all = jnp.concatenate([ck.astype(k.dtype), k], axis=1)
            v_all = jnp.concatenate([cv.astype(v.dtype), v], axis=1)
        s5_out, fin_re, fin_im = s5_mixer(u, h0_re, h0_im, s5_lam_re[e], s5_lam_im[e], s5_log_dt[e],
                                          s5_b_re[e], s5_b_im[e], s5_c_re[e], s5_c_im[e], s5_d[e],
                                          s5_glu_w[e], s5_glu_b[e])
        lam = (jnp.exp(jnp.sum(diff_lam_q1[e].astype(jnp.float32) * diff_lam_k1[e].astype(jnp.float32)))
               - jnp.exp(jnp.sum(diff_lam_q2[e].astype(jnp.float32) * diff_lam_k2[e].astype(jnp.float32)))
               + lam_init)
        diff_out = diff_attention(q, k_all, v_all, lam, lam_init, diff_subln_g[e])
        y = jnp.concatenate([s5_out, diff_out], axis=-1) @ w_out_even[e]
        return y, (fin_re, fin_im, k, v)

    def odd_mixer(h, o, ctx):
        b, n, _ = h.shape
        proj = h @ w_in_odd[o]
        cq = rmsnorm(proj[..., :MLA_Q_RANK], mla_q_norm_g[o])
        ckv = rmsnorm(proj[..., MLA_Q_RANK:MLA_Q_RANK + MLA_KV_RANK], mla_kv_norm_g[o])
        kr = proj[..., MLA_Q_RANK + MLA_KV_RANK:]
        q = (cq @ mla_w_q_up[o]).reshape(b, n, MLA_HEADS, MLA_NOPE + MLA_ROPE)
        qn, qr = q[..., :MLA_NOPE], q[..., MLA_NOPE:]
        if ctx is None:
            ckv_all, kr_all = ckv, kr
        else:
            c_ckv, c_kr = ctx
            qr = apply_rope(qr, cos_m, sin_m)
            kr_rot = apply_rope(kr, cos_m, sin_m)
            ckv_all = jnp.concatenate([c_ckv.astype(ckv.dtype), ckv], axis=1)
            kr_all = jnp.concatenate([c_kr.astype(kr.dtype), kr_rot], axis=1)
        s_len = ckv_all.shape[1]
        kv = (ckv_all @ mla_w_kv_up[o]).reshape(b, s_len, MLA_HEADS, MLA_NOPE + MLA_V)
        kn, v = kv[..., :MLA_NOPE], kv[..., MLA_NOPE:]
        y = mla_attention(qn, qr, kn, kr_all, v) @ w_out_odd[o]
        return y, (ckv, kr)

    xp, xs = x_prompt, x_sample
    s5_re_list, s5_im_list, dk_list, dv_list, ckv_list, kr_list = [], [], [], [], [], []
    for l in range(DEPTH):
        common = (w_mod[l], b_mod[l], g_pre_mix[l], g_post_mix[l], g_pre_ffn[l], g_post_ffn[l],
                  w_ffn_gate[l], w_ffn_up[l], w_ffn_down[l])
        if l % 2 == 0:
            e = l // 2
            lam_init = 0.8 - 0.6 * math.exp(-0.3 * l)
            xp, (fr, fi, ck, cv) = block(xp, cond_ctx, *common,
                                         mixer=lambda h: even_mixer(h, e, lam_init, None))
            ctx = (state_s5_re[:, e], state_s5_im[:, e], cache_diff_k[:, e], cache_diff_v[:, e])
            xs, _ = block(xs, c, *common, mixer=lambda h: even_mixer(h, e, lam_init, ctx))
            s5_re_list.append(fr)
            s5_im_list.append(fi)
            dk_list.append(ck)
            dv_list.append(cv)
        else:
            o = l // 2
            xp, (ckv, kr) = block(xp, cond_ctx, *common, mixer=lambda h: odd_mixer(h, o, None))
            ctx = (cache_mla_ckv[:, o], cache_mla_krope[:, o])
            xs, _ = block(xs, c, *common, mixer=lambda h: odd_mixer(h, o, ctx))
            ckv_list.append(ckv)
            kr_list.append(kr)

    new_s5_re = jnp.stack(s5_re_list, axis=1).astype(state_s5_re.dtype)
    new_s5_im = jnp.stack(s5_im_list, axis=1).astype(state_s5_im.dtype)
    new_dk = jnp.stack(dk_list, axis=1)
    new_dv = jnp.stack(dv_list, axis=1)
    new_ckv = jnp.stack(ckv_list, axis=1)
    new_kr = jnp.stack(kr_list, axis=1)
    return (xp, xs, new_s5_re, new_s5_im, new_dk, new_dv, new_ckv, new_kr)
```

```python
import functools
import math

import jax
import jax.numpy as jnp
from jax import lax
from jax.experimental import pallas as pl
from jax.experimental.pallas import tpu as pltpu

F32 = jnp.float32
BF16 = jnp.bfloat16

D_MODEL = 1024
BATCH = 16
SEQ = 256
DEPTH = 4
DEC_BATCH = 8
DEC_SEQ = 1024
PAST_LEN = 512
GRID_W = 64
N_EVEN = (DEPTH + 1) // 2
N_ODD = DEPTH // 2
EPS = 1e-6
ROPE_BASE = 10000.0
S5_WIDTH = D_MODEL // 2
S5_GROUP = 16
S5_GROUPS = S5_WIDTH // S5_GROUP
S5_STATE = 64
DIFF_HEAD_DIM = 64
DIFF_HEADS = (D_MODEL // 2) // (2 * DIFF_HEAD_DIM)
DIFF_WIDTH = DIFF_HEADS * 2 * DIFF_HEAD_DIM
EVEN_IN = S5_WIDTH + 3 * DIFF_WIDTH
MLA_HEADS = 16
MLA_NOPE = 64
MLA_ROPE = 32
MLA_V = 64
MLA_Q_RANK = 256
MLA_KV_RANK = 128
D_FF = ((8 * D_MODEL // 3 + 255) // 256) * 256

LANES = 128
SUBLANES = 8
VMEM_LIMIT = 56 * 1024 * 1024

N_P = BATCH * SEQ
N_S = DEC_BATCH * DEC_SEQ
N_TOK = N_P + N_S
TM = 256
N_TILES = N_TOK // TM
P_TILES = N_P // TM
S_TILES_PER_B = DEC_SEQ // TM
N_COND = 16

S5_STATE_W = S5_GROUPS * S5_STATE
S5_BLK = 4
S5_BLK_W = S5_STATE_W // S5_BLK
SCAN_T = 64
SCAN_R = SCAN_T * SUBLANES
N_CHUNKS = N_TOK // SCAN_R
SEQ_CHUNKS = SEQ // SCAN_T


def _cparams(sem):
    return pltpu.CompilerParams(dimension_semantics=sem, vmem_limit_bytes=VMEM_LIMIT)


def _const_spec(shape):
    nd = len(shape)
    return pl.BlockSpec(shape, lambda *_: (0,) * nd, pipeline_mode=pl.Buffered(1))


def _rms(x, g):
    return x * lax.rsqrt(jnp.mean(x * x, axis=-1, keepdims=True) + EPS) * g


def _dot(a, b):
    return jnp.dot(a, b, preferred_element_type=F32)


def _dot_nt(a, b):
    return lax.dot_general(a, b, (((1,), (1,)), ((), ())), preferred_element_type=F32)


def _cond_of_tile(i):
    return jnp.where(i < P_TILES, 0, 1 + (i - P_TILES) // S_TILES_PER_B)


def _tm_row_block(i):
    return jnp.where(i < P_TILES, i // SUBLANES, 2 + (i - P_TILES) % S_TILES_PER_B)


def _tm_col_block(i):
    return jnp.where(i < P_TILES, i % SUBLANES, (i - P_TILES) // S_TILES_PER_B)


def _rope_block(i):
    return jnp.where(i < P_TILES, 0, 1 + (i - P_TILES) % S_TILES_PER_B)


def _mod_kernel(c_ref, w_ref, b_ref, o_ref):
    s = jax.nn.silu(c_ref[...])
    o_ref[0] = _dot(s.astype(BF16), w_ref[0].astype(BF16)) + b_ref[0]


def _modulation(conds, w_mod, b_mod):
    tn = 1536
    return pl.pallas_call(
        _mod_kernel,
        grid=(DEPTH, 6 * D_MODEL // tn),
        in_specs=[
            pl.BlockSpec((N_COND, D_MODEL), lambda l, n: (0, 0)),
            pl.BlockSpec((1, D_MODEL, tn), lambda l, n: (l, 0, n)),
            pl.BlockSpec((1, 1, tn), lambda l, n: (l, 0, n)),
        ],
        out_specs=pl.BlockSpec((1, N_COND, tn), lambda l, n: (l, 0, n)),
        out_shape=jax.ShapeDtypeStruct((DEPTH, N_COND, 6 * D_MODEL), F32),
        compiler_params=_cparams(("parallel", "parallel")),
        name="modulation",
    )(conds, w_mod, b_mod.reshape(DEPTH, 1, 6 * D_MODEL))


def _s5_param_kernel(lr_ref, li_ref, ldt_ref, br_ref, bi_ref, are_ref, aim_ref, bbr_ref, bbi_ref):
    lr, li = lr_ref[...], li_ref[...]
    dt = jnp.exp(ldt_ref[...])
    mag = jnp.exp(lr * dt)
    a_re, a_im = mag * jnp.cos(li * dt), mag * jnp.sin(li * dt)
    den = lr * lr + li * li
    f_re = ((a_re - 1.0) * lr + a_im * li) / den
    f_im = (a_im * lr - (a_re - 1.0) * li) / den
    br, bi = br_ref[...], bi_ref[...]
    are_ref[...] = a_re
    aim_ref[...] = a_im
    bbr_ref[...] = f_re * br - f_im * bi
    bbi_ref[...] = f_re * bi + f_im * br


def _s5_params(lam_re, lam_im, log_dt, b_re, b_im):
    full = (N_EVEN, 2, S5_GROUPS, S5_GROUP, S5_STATE)
    flat = (math.prod(full) // LANES, LANES)
    ex = lambda a: jnp.broadcast_to(a[:, :, :, None, :], full).reshape(flat)
    ldt = jnp.broadcast_to(log_dt[:, :, :, None, None], full).reshape(flat)
    bt = lambda b: jnp.swapaxes(b, -1, -2).reshape(flat)
    spec = pl.BlockSpec(flat, lambda: (0, 0))
    outs = pl.pallas_call(
        _s5_param_kernel,
        in_specs=[spec] * 5,
        out_specs=[spec] * 4,
        out_shape=[jax.ShapeDtypeStruct(flat, F32)] * 4,
        name="s5_params",
    )(ex(lam_re), ex(lam_im), ldt, bt(b_re), bt(b_im))
    a_re, a_im, bb_re, bb_im = (o.reshape(full) for o in outs)
    return a_re[:, :, :, 0, :], a_im[:, :, :, 0, :], bb_re, bb_im


def _block_diag(m):
    eye = jnp.eye(SUBLANES, dtype=m.dtype)
    out = m[..., :, :, None, :] * eye[:, None, :, None]
    return out.reshape(m.shape[:-3] + (SUBLANES * m.shape[-2], SUBLANES * m.shape[-1]))


def _s5_matrices(bb_re, bb_im, c_re, c_im):
    gb = S5_GROUPS // S5_BLK
    shp = (2, S5_BLK, gb, S5_GROUP, S5_STATE)
    b_in = jnp.concatenate([_block_diag(bb_re.reshape(shp)), _block_diag(bb_im.reshape(shp))], axis=-1)
    ct = lambda c: jnp.swapaxes(c.reshape(shp), -1, -2)
    c_out = jnp.concatenate([_block_diag(ct(c_re)), -_block_diag(ct(c_im))], axis=-2)
    return b_in.astype(BF16), c_out.astype(BF16)


def _seq_of_chunk(c):
    return (c >= SEQ_CHUNKS).astype(jnp.int32) + (c >= 2 * SEQ_CHUNKS).astype(jnp.int32)


def _s5_scan_kernel(uf_ref, ub_ref, a_ref, bin_ref, cout_ref, h0f_ref, h0b_ref,
                    yf_ref, yb_ref, finf_ref, finb_ref, hs_ref, st_ref):
    j = pl.program_id(0)
    cf = j
    cb = N_CHUNKS - 1 - j

    @pl.when((cf == 0) | (cf == SEQ_CHUNKS) | (cf == 2 * SEQ_CHUNKS))
    def _():
        st_ref[0] = h0f_ref[...]

    @pl.when((cb == N_CHUNKS - 1) | (cb == 2 * SEQ_CHUNKS - 1) | (cb == SEQ_CHUNKS - 1))
    def _():
        st_ref[1] = h0b_ref[...]

    for d, u_ref in ((0, uf_ref), (1, ub_ref)):
        u = u_ref[...].astype(BF16)
        for k in range(S5_BLK):
            hs_ref[d, k] = _dot(u[:, k * LANES:(k + 1) * LANES], bin_ref[d, k])

    for k in range(S5_BLK):
        lo, hi = k * S5_BLK_W, (k + 1) * S5_BLK_W
        coef = [jnp.broadcast_to(a_ref[r:r + 1, lo:hi], (SUBLANES, S5_BLK_W)) for r in range(4)]
        init = (st_ref[0, :, lo:hi], st_ref[0, :, S5_STATE_W + lo:S5_STATE_W + hi],
                st_ref[1, :, lo:hi], st_ref[1, :, S5_STATE_W + lo:S5_STATE_W + hi])

        def body(t, carry, k=k, coef=coef):
            new = []
            for d in range(2):
                row = pl.multiple_of((t if d == 0 else SCAN_T - 1 - t) * SUBLANES, SUBLANES)
                hr, hi_ = carry[2 * d], carry[2 * d + 1]
                ar, ai = coef[2 * d], coef[2 * d + 1]
                br = hs_ref[d, k, pl.ds(row, SUBLANES), 0:S5_BLK_W]
                bi = hs_ref[d, k, pl.ds(row, SUBLANES), S5_BLK_W:2 * S5_BLK_W]
                nr = ar * hr - ai * hi_ + br
                ni = ar * hi_ + ai * hr + bi
                hs_ref[d, k, pl.ds(row, SUBLANES), 0:S5_BLK_W] = nr
                hs_ref[d, k, pl.ds(row, SUBLANES), S5_BLK_W:2 * S5_BLK_W] = ni
                new += [nr, ni]
            return tuple(new)

        fin = lax.fori_loop(0, SCAN_T, body, init, unroll=8)
        st_ref[0, :, lo:hi] = fin[0]
        st_ref[0, :, S5_STATE_W + lo:S5_STATE_W + hi] = fin[1]
        st_ref[1, :, lo:hi] = fin[2]
        st_ref[1, :, S5_STATE_W + lo:S5_STATE_W + hi] = fin[3]

    for d, y_ref in ((0, yf_ref), (1, yb_ref)):
        for k in range(S5_BLK):
            y_ref[:, k * LANES:(k + 1) * LANES] = _dot(hs_ref[d, k].astype(BF16), cout_ref[d, k])

    finf_ref[...] = st_ref[0]
    finb_ref[...] = st_ref[1]


def _s5_scan(u_tm, a, b_in, c_out, h0f, h0b):
    rev = lambda j: N_CHUNKS - 1 - j
    row_f = pl.BlockSpec((SCAN_R, S5_WIDTH), lambda j: (j, 0))
    row_b = pl.BlockSpec((SCAN_R, S5_WIDTH), lambda j: (rev(j), 0))
    st_f = pl.BlockSpec((None, SUBLANES, 2 * S5_STATE_W), lambda j: (_seq_of_chunk(j), 0, 0))
    st_b = pl.BlockSpec((None, SUBLANES, 2 * S5_STATE_W), lambda j: (_seq_of_chunk(rev(j)), 0, 0))
    st_shape = jax.ShapeDtypeStruct((3, SUBLANES, 2 * S5_STATE_W), F32)
    return pl.pallas_call(
        _s5_scan_kernel,
        grid=(N_CHUNKS,),
        in_specs=[row_f, row_b, _const_spec(a.shape), _const_spec(b_in.shape), _const_spec(c_out.shape),
                  st_f, st_b],
        out_specs=[row_f, row_b, st_f, st_b],
        out_shape=[jax.ShapeDtypeStruct((N_TOK, S5_WIDTH), F32)] * 2 + [st_shape] * 2,
        scratch_shapes=[pltpu.VMEM((2, S5_BLK, SCAN_R, 2 * S5_BLK_W), F32),
                        pltpu.VMEM((2, SUBLANES, 2 * S5_STATE_W), F32)],
        compiler_params=_cparams(("arbitrary",)),
        name="s5_scan",
    )(u_tm, u_tm, a, b_in, c_out, h0f, h0b)


def _even_pre_kernel(x_ref, mod_ref, g_ref, w_ref, cos_ref, sa_ref, sb_ref, u_ref, q_ref, k_ref, v_ref):
    m = mod_ref[...]
    h = _rms(x_ref[...], g_ref[...]) * (1.0 + m[1:2]) + m[0:1]
    proj = _dot(h.astype(BF16), w_ref[...])
    cos, sa, sb = cos_ref[...], sa_ref[...], sb_ref[...]
    half = DIFF_HEAD_DIM // 2

    def rope(z):
        return z * cos + pltpu.roll(z, half, 1) * sa + pltpu.roll(z, DIFF_WIDTH - half, 1) * sb

    u_ref[...] = proj[:, :S5_WIDTH]
    q_ref[...] = rope(proj[:, S5_WIDTH:S5_WIDTH + DIFF_WIDTH])
    k_ref[...] = rope(proj[:, S5_WIDTH + DIFF_WIDTH:S5_WIDTH + 2 * DIFF_WIDTH])
    v_ref[...] = proj[:, S5_WIDTH + 2 * DIFF_WIDTH:]


def _even_pre(x, mod_l, g_pre, w_in, rope_tabs):
    tok = pl.BlockSpec((TM, DIFF_WIDTH), lambda i: (i, 0))
    rope_spec = pl.BlockSpec((TM, DIFF_WIDTH), lambda i: (_rope_block(i), 0))
    return pl.pallas_call(
        _even_pre_kernel,
        grid=(N_TILES,),
        in_specs=[
            pl.BlockSpec((TM, D_MODEL), lambda i: (i, 0)),
            pl.BlockSpec((None, 6, D_MODEL), lambda i: (_cond_of_tile(i), 0, 0)),
            _const_spec((1, D_MODEL)),
            _const_spec((D_MODEL, EVEN_IN)),
            rope_spec, rope_spec, rope_spec,
        ],
        out_specs=[
            pl.BlockSpec((TM, S5_WIDTH), lambda i: (_tm_row_block(i), _tm_col_block(i))),
            tok, tok, tok,
        ],
        out_shape=[jax.ShapeDtypeStruct((N_TOK // SUBLANES, SUBLANES * S5_WIDTH), F32)]
        + [jax.ShapeDtypeStruct((N_TOK, DIFF_WIDTH), F32)] * 3,
        compiler_params=_cparams(("parallel",)),
        name="even_pre",
    )(x, mod_l, g_pre, w_in, *rope_tabs)


def _softmax_parts(parts):
    m = functools.reduce(jnp.maximum, [jnp.max(s, axis=-1, keepdims=True) for s in parts])
    ps = [jnp.exp(s - m) for s in parts]
    l = functools.reduce(jnp.add, [jnp.sum(p, axis=-1, keepdims=True) for p in ps])
    return ps, l


def _diff_attn_kernel(lam_init, has_ctx, *refs):
    if has_ctx:
        q_ref, k_ref, v_ref, kc_ref, vc_ref, lq1, lk1, lq2, lk2, g_ref, o_ref = refs
    else:
        q_ref, k_ref, v_ref, lq1, lk1, lq2, lk2, g_ref, o_ref = refs
    lam = (jnp.exp(jnp.sum(lq1[...] * lk1[...], axis=-1, keepdims=True))
           - jnp.exp(jnp.sum(lq2[...] * lk2[...], axis=-1, keepdims=True)) + lam_init)
    first = lax.broadcasted_iota(jnp.int32, (1, 2 * DIFF_HEAD_DIM), 1) < DIFF_HEAD_DIM
    scale = DIFF_HEAD_DIM ** -0.5
    g = g_ref[...]
    for h in range(DIFF_HEADS):
        sl = slice(h * 2 * DIFF_HEAD_DIM, (h + 1) * 2 * DIFF_HEAD_DIM)
        qh = q_ref[:, sl] * scale
        q1 = jnp.where(first, qh, 0.0).astype(BF16)
        q2 = jnp.where(first, 0.0, qh).astype(BF16)
        keys = [k_ref[:, sl].astype(BF16)]
        vals = [v_ref[:, sl].astype(BF16)]
        if has_ctx:
            keys.insert(0, kc_ref[:, sl].astype(BF16))
            vals.insert(0, vc_ref[:, sl].astype(BF16))
        p1, l1 = _softmax_parts([_dot_nt(q1, kk) for kk in keys])
        p2, l2 = _softmax_parts([_dot_nt(q2, kk) for kk in keys])
        r1 = 1.0 / l1
        r2 = lam / l2
        o = functools.reduce(jnp.add, [_dot((a * r1 - b * r2).astype(BF16), vv) for a, b, vv in zip(p1, p2, vals)])
        o_ref[:, sl] = (_rms(o, g) * (1.0 - lam_init)).astype(o_ref.dtype)


def _diff_attention(q, k, v, ctx, lam_vecs, subln_g, lam_init, *, batch, n, row0, tq):
    nq = n // tq
    has_ctx = ctx is not None
    in_specs = [
        pl.BlockSpec((tq, DIFF_WIDTH), lambda b, i: (row0 // tq + b * nq + i, 0)),
        pl.BlockSpec((n, DIFF_WIDTH), lambda b, i: (row0 // n + b, 0)),
        pl.BlockSpec((n, DIFF_WIDTH), lambda b, i: (row0 // n + b, 0)),
    ]
    args = [q, k, v]
    if has_ctx:
        in_specs += [pl.BlockSpec((None, PAST_LEN, DIFF_WIDTH), lambda b, i: (b, 0, 0))] * 2
        args += list(ctx)
    in_specs += [pl.BlockSpec((1, DIFF_HEAD_DIM), lambda b, i: (0, 0))] * 4
    in_specs += [pl.BlockSpec((1, 2 * DIFF_HEAD_DIM), lambda b, i: (0, 0))]
    args += list(lam_vecs) + [subln_g]
    return pl.pallas_call(
        functools.partial(_diff_attn_kernel, lam_init, has_ctx),
        grid=(batch, nq),
        in_specs=in_specs,
        out_specs=pl.BlockSpec((tq, DIFF_WIDTH), lambda b, i: (b * nq + i, 0)),
        out_shape=jax.ShapeDtypeStruct((batch * n, DIFF_WIDTH), BF16),
        compiler_params=_cparams(("parallel", "parallel")),
        name="diff_attn_ctx" if has_ctx else "diff_attn",
    )(*args)


def _ffn_tail(x1, m, g_pre, g_post, wgu_ref, wd_ref):
    h = _rms(x1, g_pre) * (1.0 + m[4:5]) + m[3:4]
    gu = _dot(h.astype(BF16), wgu_ref[...])
    act = jax.nn.silu(gu[:, :D_FF]) * gu[:, D_FF:]
    y = _dot(act.astype(BF16), wd_ref[...])
    return x1 + m[5:6] * _rms(y, g_post)


def _even_post_kernel(x_ref, mod_ref, yf_ref, yb_ref, u_ref, d_ref, gw_ref, gb_ref, da_ref, wos_ref, wod_ref,
                      gpm_ref, gpf_ref, gqf_ref, wgu_ref, wd_ref, o_ref):
    m = mod_ref[...]
    y = yf_ref[...] + yb_ref[...] + d_ref[...] * u_ref[...]
    g = jax.nn.gelu(y)
    s5 = g * jax.nn.sigmoid(_dot(g.astype(BF16), gw_ref[...]) + gb_ref[...])
    mix = _dot(s5.astype(BF16), wos_ref[...]) + _dot(da_ref[...], wod_ref[...])
    x1 = x_ref[...] + m[2:3] * _rms(mix, gpm_ref[...])
    o_ref[...] = _ffn_tail(x1, m, gpf_ref[...], gqf_ref[...], wgu_ref, wd_ref)


def _even_post(x, mod_l, y_f, y_b, u_tm, s5_d, glu_w, glu_b, diff_out, w_out_s5, w_out_diff,
               g_post_mix, g_pre_ffn, g_post_ffn, w_gu, w_down):
    tm_spec = pl.BlockSpec((TM, S5_WIDTH), lambda i: (_tm_row_block(i), _tm_col_block(i)))
    vec = lambda n: _const_spec((1, n))
    return pl.pallas_call(
        _even_post_kernel,
        grid=(N_TILES,),
        in_specs=[
            pl.BlockSpec((TM, D_MODEL), lambda i: (i, 0)),
            pl.BlockSpec((None, 6, D_MODEL), lambda i: (_cond_of_tile(i), 0, 0)),
            tm_spec, tm_spec, tm_spec,
            vec(S5_WIDTH), _const_spec((S5_WIDTH, S5_WIDTH)), vec(S5_WIDTH),
            pl.BlockSpec((TM, DIFF_WIDTH), lambda i: (i, 0)),
            _const_spec((S5_WIDTH, D_MODEL)), _const_spec((DIFF_WIDTH, D_MODEL)),
            vec(D_MODEL), vec(D_MODEL), vec(D_MODEL),
            _const_spec((D_MODEL, 2 * D_FF)), _const_spec((D_FF, D_MODEL)),
        ],
        out_specs=pl.BlockSpec((TM, D_MODEL), lambda i: (i, 0)),
        out_shape=jax.ShapeDtypeStruct((N_TOK, D_MODEL), F32),
        compiler_params=_cparams(("parallel",)),
        name="even_post_ffn",
    )(x, mod_l, y_f, y_b, u_tm, s5_d, glu_w, glu_b, diff_out, w_out_s5, w_out_diff,
      g_post_mix, g_pre_ffn, g_post_ffn, w_gu, w_down)


MLA_HEAD_PAD = LANES
MLA_Q_W = MLA_HEADS * MLA_HEAD_PAD
MLA_IN_W = MLA_Q_RANK + MLA_KV_RANK + MLA_HEAD_PAD


def _mla_rope(z, cos, sa, sb):
    w = z.shape[-1]
    half = MLA_ROPE // 2
    return z * cos + pltpu.roll(z, half, 1) * sa + pltpu.roll(z, w - half, 1) * sb


def _odd_pre_kernel(x_ref, mod_ref, g_ref, w_ref, gq_ref, gkv_ref, wq_ref, cos_ref, sa_ref, sb_ref,
                    q_ref, ckr_ref, ckv_ref, kr_ref):
    m = mod_ref[...]
    h = _rms(x_ref[...], g_ref[...]) * (1.0 + m[1:2]) + m[0:1]
    proj = _dot(h.astype(BF16), w_ref[...])
    cq = _rms(proj[:, :MLA_Q_RANK], gq_ref[...])
    ckv = _rms(proj[:, MLA_Q_RANK:MLA_Q_RANK + MLA_KV_RANK], gkv_ref[...])
    krp = proj[:, MLA_Q_RANK + MLA_KV_RANK:]
    cos, sa, sb = cos_ref[...], sa_ref[...], sb_ref[...]
    scale = (MLA_NOPE + MLA_ROPE) ** -0.5
    q = _dot(cq.astype(BF16), wq_ref[...]) * scale
    for hd in range(MLA_HEADS):
        sl = slice(hd * MLA_HEAD_PAD, (hd + 1) * MLA_HEAD_PAD)
        q_ref[:, sl] = _mla_rope(q[:, sl], cos, sa, sb).astype(BF16)
    ckv_ref[...] = ckv
    kr_ref[...] = krp
    ckr_ref[:, :MLA_KV_RANK] = ckv.astype(BF16)
    ckr_ref[:, MLA_KV_RANK:] = _mla_rope(krp, cos, sa, sb).astype(BF16)


def _odd_pre(x, mod_l, g_pre, w_in, gq, gkv, wq, rope_tabs):
    rope_spec = pl.BlockSpec((TM, MLA_HEAD_PAD), lambda i: (_rope_block(i), 0))
    tok = lambda w: pl.BlockSpec((TM, w), lambda i: (i, 0))
    return pl.pallas_call(
        _odd_pre_kernel,
        grid=(N_TILES,),
        in_specs=[
            tok(D_MODEL),
            pl.BlockSpec((None, 6, D_MODEL), lambda i: (_cond_of_tile(i), 0, 0)),
            _const_spec((1, D_MODEL)),
            _const_spec((D_MODEL, MLA_IN_W)),
            _const_spec((1, MLA_Q_RANK)), _const_spec((1, MLA_KV_RANK)),
            _const_spec((MLA_Q_RANK, MLA_Q_W)),
            rope_spec, rope_spec, rope_spec,
        ],
        out_specs=[tok(MLA_Q_W), tok(MLA_KV_RANK + MLA_HEAD_PAD), tok(MLA_KV_RANK), tok(MLA_HEAD_PAD)],
        out_shape=[
            jax.ShapeDtypeStruct((N_TOK, MLA_Q_W), BF16),
            jax.ShapeDtypeStruct((N_TOK, MLA_KV_RANK + MLA_HEAD_PAD), BF16),
            jax.ShapeDtypeStruct((N_TOK, MLA_KV_RANK), F32),
            jax.ShapeDtypeStruct((N_TOK, MLA_HEAD_PAD), F32),
        ],
        compiler_params=_cparams(("parallel",)),
        name="odd_pre",
    )(x, mod_l, g_pre, w_in, gq, gkv, wq, *rope_tabs)


def _mla_attn_kernel(has_ctx, *refs):
    if has_ctx:
        q_ref, ckr_ref, ckrc_ref, wk_ref, wv_ref, o_ref, kf_ref, vf_ref = refs
    else:
        q_ref, ckr_ref, wk_ref, wv_ref, o_ref, kf_ref, vf_ref = refs
    n_new = ckr_ref.shape[0]
    off = PAST_LEN if has_ctx else 0

    @pl.when(pl.program_id(1) == 0)
    def _():
        srcs = [(0, ckrc_ref)] if has_ctx else []
        srcs.append((off, ckr_ref))
        for r0, src in srcs:
            c = src[...]
            rows = pl.ds(r0, c.shape[0])
            kf_ref[rows, :] = _dot(c, wk_ref[...]).astype(BF16)
            vf_ref[rows, :] = _dot(c[:, :MLA_KV_RANK], wv_ref[...]).astype(BF16)

    low = lax.broadcasted_iota(jnp.int32, (1, LANES), 1) < MLA_V

    def pair(j, carry):
        c0 = pl.multiple_of(j * 2 * MLA_HEAD_PAD, 2 * MLA_HEAD_PAD)
        vj = vf_ref[:, pl.ds(pl.multiple_of(j * LANES, LANES), LANES)]
        outs = []
        for t in range(2):
            cs = pl.ds(pl.multiple_of(c0 + t * MLA_HEAD_PAD, MLA_HEAD_PAD), MLA_HEAD_PAD)
            s = _dot_nt(q_ref[:, cs], kf_ref[:, cs])
            mx = jnp.max(s, axis=-1, keepdims=True)
            p = jnp.exp(s - mx)
            l = jnp.sum(p, axis=-1, keepdims=True)
            outs.append(_dot(p.astype(BF16), vj) * (1.0 / l))
        o_ref[:, pl.ds(pl.multiple_of(j * LANES, LANES), LANES)] = jnp.where(low, outs[0], outs[1]).astype(o_ref.dtype)
        return carry

    lax.fori_loop(0, MLA_HEADS // 2, pair, 0)


def _mla_attention(q, ckr, ckr_ctx, wk, wv, *, batch, n, row0, tq):
    nq = n // tq
    has_ctx = ckr_ctx is not None
    s_len = n + (PAST_LEN if has_ctx else 0)
    ckr_w = MLA_KV_RANK + MLA_HEAD_PAD
    in_specs = [
        pl.BlockSpec((tq, MLA_Q_W), lambda b, i: (row0 // tq + b * nq + i, 0)),
        pl.BlockSpec((n, ckr_w), lambda b, i: (row0 // n + b, 0)),
    ]
    args = [q, ckr]
    if has_ctx:
        in_specs.append(pl.BlockSpec((None, PAST_LEN, ckr_w), lambda b, i: (b, 0, 0)))
        args.append(ckr_ctx)
    in_specs += [_const_spec(wk.shape), _const_spec(wv.shape)]
    args += [wk, wv]
    return pl.pallas_call(
        functools.partial(_mla_attn_kernel, has_ctx),
        grid=(batch, nq),
        in_specs=in_specs,
        out_specs=pl.BlockSpec((tq, MLA_HEADS * MLA_V), lambda b, i: (b * nq + i, 0)),
        out_shape=jax.ShapeDtypeStruct((batch * n, MLA_HEADS * MLA_V), BF16),
        scratch_shapes=[pltpu.VMEM((s_len, MLA_Q_W), BF16), pltpu.VMEM((s_len, MLA_HEADS * MLA_V), BF16)],
        compiler_params=_cparams(("parallel", "arbitrary")),
        name="mla_attn_ctx" if has_ctx else "mla_attn",
    )(*args)


def _odd_post_kernel(x_ref, mod_ref, a_ref, wo_ref, gpm_ref, gpf_ref, gqf_ref, wgu_ref, wd_ref, o_ref):
    m = mod_ref[...]
    mix = _dot(a_ref[...], wo_ref[...])
    x1 = x_ref[...] + m[2:3] * _rms(mix, gpm_ref[...])
    o_ref[...] = _ffn_tail(x1, m, gpf_ref[...], gqf_ref[...], wgu_ref, wd_ref)


def _odd_post(x, mod_l, attn, w_out, g_post_mix, g_pre_ffn, g_post_ffn, w_gu, w_down):
    vec = lambda n: _const_spec((1, n))
    return pl.pallas_call(
        _odd_post_kernel,
        grid=(N_TILES,),
        in_specs=[
            pl.BlockSpec((TM, D_MODEL), lambda i: (i, 0)),
            pl.BlockSpec((None, 6, D_MODEL), lambda i: (_cond_of_tile(i), 0, 0)),
            pl.BlockSpec((TM, MLA_HEADS * MLA_V), lambda i: (i, 0)),
            _const_spec((MLA_HEADS * MLA_V, D_MODEL)),
            vec(D_MODEL), vec(D_MODEL), vec(D_MODEL),
            _const_spec((D_MODEL, 2 * D_FF)), _const_spec((D_FF, D_MODEL)),
        ],
        out_specs=pl.BlockSpec((TM, D_MODEL), lambda i: (i, 0)),
        out_shape=jax.ShapeDtypeStruct((N_TOK, D_MODEL), F32),
        compiler_params=_cparams(("parallel",)),
        name="odd_post_ffn",
    )(x, mod_l, attn, w_out, g_post_mix, g_pre_ffn, g_post_ffn, w_gu, w_down)


def _rope_angles(rot_dim):
    rows = DEC_SEQ // GRID_W
    row = jnp.repeat(jnp.arange(rows, dtype=F32), GRID_W)
    col = jnp.tile(jnp.arange(GRID_W, dtype=F32), rows)
    n_freq = rot_dim // 4
    inv = ROPE_BASE ** (-jnp.arange(n_freq, dtype=F32) / n_freq)
    ang = jnp.concatenate([row[:, None] * inv, col[:, None] * inv], axis=-1)
    return jnp.cos(ang), jnp.sin(ang)


def _with_identity(cos, sa, sb):
    one = jnp.ones((TM, cos.shape[1]), F32)
    zero = jnp.zeros((TM, cos.shape[1]), F32)
    return (jnp.concatenate([one, cos]), jnp.concatenate([zero, sa]), jnp.concatenate([zero, sb]))


def _diff_rope_tables():
    c, s = _rope_angles(DIFF_HEAD_DIM)
    z = jnp.zeros_like(s)
    reps = DIFF_WIDTH // DIFF_HEAD_DIM
    cos = jnp.tile(jnp.concatenate([c, c], axis=1), (1, reps))
    sa = jnp.tile(jnp.concatenate([z, s], axis=1), (1, reps))
    sb = jnp.tile(jnp.concatenate([-s, z], axis=1), (1, reps))
    return _with_identity(cos, sa, sb)


def _mla_rope_tables():
    c, s = _rope_angles(MLA_ROPE)
    n = c.shape[0]
    z = jnp.zeros_like(s)
    one = jnp.ones((n, MLA_NOPE), F32)
    zn = jnp.zeros((n, MLA_NOPE), F32)
    pad1 = jnp.ones((n, MLA_HEAD_PAD - MLA_NOPE - MLA_ROPE), F32)
    pad0 = jnp.zeros((n, MLA_HEAD_PAD - MLA_NOPE - MLA_ROPE), F32)
    cos = jnp.concatenate([one, c, c, pad1], axis=1)
    sa = jnp.concatenate([zn, z, s, pad0], axis=1)
    sb = jnp.concatenate([zn, -s, z, pad0], axis=1)
    return _with_identity(cos, sa, sb)


def kernel(x_prompt, x_sample, state_s5_re, state_s5_im, cache_diff_k, cache_diff_v, cache_mla_ckv, cache_mla_krope, c, c_ctx, w_mod, b_mod, g_pre_mix, g_post_mix, g_pre_ffn, g_post_ffn, w_ffn_gate, w_ffn_up, w_ffn_down, w_in_even, w_out_even, s5_lam_re, s5_lam_im, s5_log_dt, s5_b_re, s5_b_im, s5_c_re, s5_c_im, s5_d, s5_glu_w, s5_glu_b, diff_lam_q1, diff_lam_k1, diff_lam_q2, diff_lam_k2, diff_subln_g, w_in_odd, mla_q_norm_g, mla_w_q_up, mla_kv_norm_g, mla_w_kv_up, w_out_odd):
    x = jnp.concatenate([x_prompt.reshape(N_P, D_MODEL), x_sample.reshape(N_S, D_MODEL)], axis=0)
    conds = jnp.concatenate([c_ctx[None, :], c, jnp.zeros((N_COND - 1 - DEC_BATCH, D_MODEL), F32)], axis=0)
    mod = _modulation(conds, w_mod, b_mod).reshape(DEPTH, N_COND, 6, D_MODEL)

    a_re, a_im, bb_re, bb_im = _s5_params(s5_lam_re, s5_lam_im, s5_log_dt, s5_b_re, s5_b_im)
    diff_tabs = _diff_rope_tables()
    mla_tabs = _mla_rope_tables()
    row = lambda v: v.reshape(1, -1)

    s5_re_list, s5_im_list, dk_list, dv_list, ckv_list, kr_list = [], [], [], [], [], []
    for l in range(DEPTH):
        w_gu = jnp.concatenate([w_ffn_gate[l], w_ffn_up[l]], axis=1).astype(BF16)
        w_down = w_ffn_down[l].astype(BF16)
        ffn_args = (row(g_post_mix[l]), row(g_pre_ffn[l]), row(g_post_ffn[l]), w_gu, w_down)
        if l % 2 == 0:
            e = l // 2
            lam_init = 0.8 - 0.6 * math.exp(-0.3 * l)
            u_tm, q, k, v = _even_pre(x, mod[l], row(g_pre_mix[l]), w_in_even[e].astype(BF16), diff_tabs)
            u_tm = u_tm.reshape(N_TOK, S5_WIDTH)

            a = jnp.stack([a_re[e, 0], a_im[e, 0], a_re[e, 1], a_im[e, 1]]).reshape(4, S5_STATE_W)
            b_in, c_out = _s5_matrices(bb_re[e], bb_im[e], s5_c_re[e], s5_c_im[e])
            zeros = jnp.zeros((2, SUBLANES, 2 * S5_STATE_W), F32)
            h0 = [jnp.concatenate([zeros, jnp.concatenate(
                [state_s5_re[:, e, d].reshape(1, DEC_BATCH, S5_STATE_W),
                 state_s5_im[:, e, d].reshape(1, DEC_BATCH, S5_STATE_W)], axis=-1)], axis=0) for d in range(2)]
            y_f, y_b, fin_f, fin_b = _s5_scan(u_tm, a, b_in, c_out, h0[0], h0[1])
            fin = jnp.stack([fin_f[:2].reshape(BATCH, 2, S5_GROUPS, S5_STATE),
                             fin_b[:2].reshape(BATCH, 2, S5_GROUPS, S5_STATE)], axis=1)
            s5_re_list.append(fin[:, :, 0])
            s5_im_list.append(fin[:, :, 1])

            lam_vecs = [row(t[e]) for t in (diff_lam_q1, diff_lam_k1, diff_lam_q2, diff_lam_k2)]
            subln = row(diff_subln_g[e])
            ctx = (cache_diff_k[:, e].reshape(DEC_BATCH, PAST_LEN, DIFF_WIDTH),
                   cache_diff_v[:, e].reshape(DEC_BATCH, PAST_LEN, DIFF_WIDTH))
            da_p = _diff_attention(q, k, v, None, lam_vecs, subln, lam_init, batch=BATCH, n=SEQ, row0=0, tq=SEQ)
            da_s = _diff_attention(q, k, v, ctx, lam_vecs, subln, lam_init, batch=DEC_BATCH, n=DEC_SEQ,
                                   row0=N_P, tq=256)
            diff_out = jnp.concatenate([da_p, da_s], axis=0)

            w_out = w_out_even[e].astype(BF16)
            x = _even_post(x, mod[l], y_f.reshape(N_TOK // SUBLANES, -1), y_b.reshape(N_TOK // SUBLANES, -1),
                           u_tm.reshape(N_TOK // SUBLANES, -1), row(s5_d[e]), s5_glu_w[e].astype(BF16),
                           row(s5_glu_b[e]), diff_out, w_out[:S5_WIDTH], w_out[S5_WIDTH:], *ffn_args)
            dk_list.append(k[:N_P].reshape(BATCH, SEQ, DIFF_HEADS, 2, DIFF_HEAD_DIM))
            dv_list.append(v[:N_P].reshape(BATCH, SEQ, DIFF_HEADS, 2 * DIFF_HEAD_DIM))
        else:
            o = l // 2
            pad = MLA_HEAD_PAD - MLA_NOPE - MLA_ROPE
            w_in = jnp.concatenate(
                [w_in_odd[o][:, :MLA_Q_RANK + MLA_KV_RANK], jnp.zeros((D_MODEL, MLA_NOPE), F32),
                 w_in_odd[o][:, MLA_Q_RANK + MLA_KV_RANK:], jnp.zeros((D_MODEL, pad), F32)], axis=1).astype(BF16)
            wq = jnp.pad(mla_w_q_up[o].reshape(MLA_Q_RANK, MLA_HEADS, MLA_NOPE + MLA_ROPE),
                         ((0, 0), (0, 0), (0, pad))).reshape(MLA_Q_RANK, MLA_Q_W).astype(BF16)
            wkv = mla_w_kv_up[o].reshape(MLA_KV_RANK, MLA_HEADS, MLA_NOPE + MLA_V)
            wk_top = jnp.pad(wkv[:, :, :MLA_NOPE], ((0, 0), (0, 0), (0, MLA_HEAD_PAD - MLA_NOPE))).reshape(MLA_KV_RANK, MLA_Q_W)
            sel = jnp.pad(jnp.eye(MLA_ROPE, dtype=F32), ((MLA_NOPE, pad), (MLA_NOPE, pad)))
            wk = jnp.concatenate([wk_top, jnp.tile(sel, (1, MLA_HEADS))], axis=0).astype(BF16)
            wv = wkv[:, :, MLA_NOPE:].reshape(MLA_KV_RANK, MLA_HEADS * MLA_V).astype(BF16)

            q, ckr, ckv, krp = _odd_pre(x, mod[l], row(g_pre_mix[l]), w_in, row(mla_q_norm_g[o]),
                                        row(mla_kv_norm_g[o]), wq, mla_tabs)
            ckr_ctx = jnp.concatenate(
                [cache_mla_ckv[:, o], jnp.pad(cache_mla_krope[:, o], ((0, 0), (0, 0), (MLA_NOPE, pad)))],
                axis=-1).astype(BF16)
            at_p = _mla_attention(q, ckr, None, wk, wv, batch=BATCH, n=SEQ, row0=0, tq=SEQ)
            at_s = _mla_attention(q, ckr, ckr_ctx, wk, wv, batch=DEC_BATCH, n=DEC_SEQ, row0=N_P, tq=256)
            attn = jnp.concatenate([at_p, at_s], axis=0)
            x = _odd_post(x, mod[l], attn, w_out_odd[o].astype(BF16), *ffn_args)
            ckv_list.append(ckv[:N_P].reshape(BATCH, SEQ, MLA_KV_RANK))
            kr_list.append(krp[:N_P, MLA_NOPE:MLA_NOPE + MLA_ROPE].reshape(BATCH, SEQ, MLA_ROPE))

    return (x[:N_P].reshape(BATCH, SEQ, D_MODEL), x[N_P:].reshape(DEC_BATCH, DEC_SEQ, D_MODEL),
            jnp.stack(s5_re_list, axis=1), jnp.stack(s5_im_list, axis=1),
            jnp.stack(dk_list, axis=1), jnp.stack(dv_list, axis=1),
            jnp.stack(ckv_list, axis=1), jnp.stack(kr_list, axis=1))
```

```python
import functools
import math

import jax
import jax.numpy as jnp
from jax import lax
from jax.experimental import pallas as pl
from jax.experimental.pallas import tpu as pltpu

F32 = jnp.float32
BF16 = jnp.bfloat16

D_MODEL = 1024
BATCH = 16
SEQ = 256
DEPTH = 4
DEC_BATCH = 8
DEC_SEQ = 1024
PAST_LEN = 512
GRID_W = 64
N_EVEN = (DEPTH + 1) // 2
N_ODD = DEPTH // 2
EPS = 1e-6
ROPE_BASE = 10000.0
S5_WIDTH = D_MODEL // 2
S5_GROUP = 16
S5_GROUPS = S5_WIDTH // S5_GROUP
S5_STATE = 64
DIFF_HEAD_DIM = 64
DIFF_HEADS = (D_MODEL // 2) // (2 * DIFF_HEAD_DIM)
DIFF_WIDTH = DIFF_HEADS * 2 * DIFF_HEAD_DIM
EVEN_IN = S5_WIDTH + 3 * DIFF_WIDTH
MLA_HEADS = 16
MLA_NOPE = 64
MLA_ROPE = 32
MLA_V = 64
MLA_Q_RANK = 256
MLA_KV_RANK = 128
D_FF = ((8 * D_MODEL // 3 + 255) // 256) * 256

LANES = 128
SUBLANES = 8
VMEM_LIMIT = 56 * 1024 * 1024

N_P = BATCH * SEQ
N_S = DEC_BATCH * DEC_SEQ
N_TOK = N_P + N_S
TM = 256
N_TILES = N_TOK // TM
P_TILES = N_P // TM
S_TILES_PER_B = DEC_SEQ // TM
N_COND = 16

S5_STATE_W = S5_GROUPS * S5_STATE
S5_BLK = 4
S5_BLK_W = S5_STATE_W // S5_BLK
SCAN_T = 64
SCAN_R = SCAN_T * SUBLANES
N_TM_ROWS = N_TOK // SUBLANES
N_CHUNKS = N_TM_ROWS // SCAN_T
SEQ_CHUNKS = SEQ // SCAN_T


def _cparams(sem):
    return pltpu.CompilerParams(dimension_semantics=sem, vmem_limit_bytes=VMEM_LIMIT)


def _const_spec(shape):
    nd = len(shape)
    return pl.BlockSpec(shape, lambda *_: (0,) * nd, pipeline_mode=pl.Buffered(1))


def _rms(x, g):
    return x * lax.rsqrt(jnp.mean(x * x, axis=-1, keepdims=True) + EPS) * g


def _dot(a, b):
    return jnp.dot(a, b, preferred_element_type=F32)


def _dot_nt(a, b):
    return lax.dot_general(a, b, (((1,), (1,)), ((), ())), preferred_element_type=F32)


def _cond_of_tile(i):
    return jnp.where(i < P_TILES, 0, 1 + (i - P_TILES) // S_TILES_PER_B)


def _tm_row_block(i):
    return jnp.where(i < P_TILES, i // SUBLANES, 2 + (i - P_TILES) % S_TILES_PER_B)


def _tm_col_block(i):
    return jnp.where(i < P_TILES, i % SUBLANES, (i - P_TILES) // S_TILES_PER_B)


def _rope_block(i):
    return jnp.where(i < P_TILES, 0, 1 + (i - P_TILES) % S_TILES_PER_B)


def _tok_spec(width):
    return pl.BlockSpec((TM, width), lambda i: (i, 0))


def _ctx_spec(width):
    return pl.BlockSpec((TM, width), lambda i: (jnp.minimum(i, P_TILES - 1), 0))


def _lat_spec(width):
    return pl.BlockSpec((TM, width), lambda i: (jnp.maximum(i - P_TILES, 0), 0))


def _stream_specs(arrays, width):
    return [_tok_spec(width)] if len(arrays) == 1 else [_ctx_spec(width), _lat_spec(width)]


def _read_stream(refs):
    if len(refs) == 1:
        return refs[0][...]
    return jnp.where(pl.program_id(0) < P_TILES, refs[0][...], refs[1][...])


MOD_SPEC = pl.BlockSpec((None, 6, D_MODEL), lambda i: (_cond_of_tile(i), 0, 0))


def _mod_kernel(c_ref, w_ref, b_ref, o_ref):
    s = jax.nn.silu(c_ref[...])
    o_ref[0] = _dot(s.astype(BF16), w_ref[0].astype(BF16)) + b_ref[0]


def _modulation(conds, w_mod, b_mod):
    tn = 1536
    return pl.pallas_call(
        _mod_kernel,
        grid=(DEPTH, 6 * D_MODEL // tn),
        in_specs=[
            pl.BlockSpec((N_COND, D_MODEL), lambda l, n: (0, 0)),
            pl.BlockSpec((1, D_MODEL, tn), lambda l, n: (l, 0, n)),
            pl.BlockSpec((1, 1, tn), lambda l, n: (l, 0, n)),
        ],
        out_specs=pl.BlockSpec((1, N_COND, tn), lambda l, n: (l, 0, n)),
        out_shape=jax.ShapeDtypeStruct((DEPTH, N_COND, 6 * D_MODEL), F32),
        compiler_params=_cparams(("parallel", "parallel")),
        name="modulation",
    )(conds, w_mod, b_mod.reshape(DEPTH, 1, 6 * D_MODEL))


def _s5_param_kernel(lr_ref, li_ref, ldt_ref, br_ref, bi_ref, are_ref, aim_ref, bbr_ref, bbi_ref):
    lr, li = lr_ref[...], li_ref[...]
    dt = jnp.exp(ldt_ref[...])
    mag = jnp.exp(lr * dt)
    a_re, a_im = mag * jnp.cos(li * dt), mag * jnp.sin(li * dt)
    den = lr * lr + li * li
    f_re = ((a_re - 1.0) * lr + a_im * li) / den
    f_im = (a_im * lr - (a_re - 1.0) * li) / den
    br, bi = br_ref[...], bi_ref[...]
    are_ref[...] = a_re
    aim_ref[...] = a_im
    bbr_ref[...] = f_re * br - f_im * bi
    bbi_ref[...] = f_re * bi + f_im * br


def _s5_params(lam_re, lam_im, log_dt, b_re, b_im):
    full = (N_EVEN, 2, S5_GROUPS, S5_GROUP, S5_STATE)
    flat = (math.prod(full) // LANES, LANES)
    ex = lambda a: jnp.broadcast_to(a[:, :, :, None, :], full).reshape(flat)
    ldt = jnp.broadcast_to(log_dt[:, :, :, None, None], full).reshape(flat)
    bt = lambda b: jnp.swapaxes(b, -1, -2).reshape(flat)
    spec = pl.BlockSpec(flat, lambda: (0, 0))
    outs = pl.pallas_call(
        _s5_param_kernel,
        in_specs=[spec] * 5,
        out_specs=[spec] * 4,
        out_shape=[jax.ShapeDtypeStruct(flat, F32)] * 4,
        name="s5_params",
    )(ex(lam_re), ex(lam_im), ldt, bt(b_re), bt(b_im))
    a_re, a_im, bb_re, bb_im = (o.reshape(full) for o in outs)
    return a_re[:, :, :, 0, :], a_im[:, :, :, 0, :], bb_re, bb_im


def _block_diag(m):
    eye = jnp.eye(SUBLANES, dtype=m.dtype)
    out = m[..., :, :, None, :] * eye[:, None, :, None]
    return out.reshape(m.shape[:-3] + (SUBLANES * m.shape[-2], SUBLANES * m.shape[-1]))


def _s5_matrices(bb_re, bb_im, c_re, c_im):
    gb = S5_GROUPS // S5_BLK
    shp = (2, S5_BLK, gb, S5_GROUP, S5_STATE)
    b_in = jnp.concatenate([_block_diag(bb_re.reshape(shp)), _block_diag(bb_im.reshape(shp))], axis=-1)
    ct = lambda c: jnp.swapaxes(c.reshape(shp), -1, -2)
    c_out = jnp.concatenate([_block_diag(ct(c_re)), -_block_diag(ct(c_im))], axis=-2)
    return b_in.astype(BF16), c_out.astype(BF16)


def _seq_of_chunk(c):
    return (c >= SEQ_CHUNKS).astype(jnp.int32) + (c >= 2 * SEQ_CHUNKS).astype(jnp.int32)


def _s5_scan_kernel(uf_ref, ub_ref, a_ref, bin_ref, cout_ref, h0f_ref, h0b_ref,
                    yf_ref, yb_ref, finf_ref, finb_ref, us_ref, hs_ref, ys_ref, st_ref):
    j = pl.program_id(0)
    cf = j
    cb = N_CHUNKS - 1 - j

    @pl.when((cf == 0) | (cf == SEQ_CHUNKS) | (cf == 2 * SEQ_CHUNKS))
    def _():
        st_ref[0] = h0f_ref[...]

    @pl.when((cb == N_CHUNKS - 1) | (cb == 2 * SEQ_CHUNKS - 1) | (cb == SEQ_CHUNKS - 1))
    def _():
        st_ref[1] = h0b_ref[...]

    for d, u_ref in ((0, uf_ref), (1, ub_ref)):
        for b in range(SUBLANES):
            for k in range(S5_BLK):
                c0 = b * S5_WIDTH + k * LANES
                us_ref[d, k, pl.ds(b, SCAN_T, stride=SUBLANES), :] = u_ref[:, c0:c0 + LANES]
        for k in range(S5_BLK):
            hs_ref[d, k] = _dot(us_ref[d, k].astype(BF16), bin_ref[d, k])

    for k in range(S5_BLK):
        lo, hi = k * S5_BLK_W, (k + 1) * S5_BLK_W
        coef = [jnp.broadcast_to(a_ref[r:r + 1, lo:hi], (SUBLANES, S5_BLK_W)) for r in range(4)]
        init = (st_ref[0, :, lo:hi], st_ref[0, :, S5_STATE_W + lo:S5_STATE_W + hi],
                st_ref[1, :, lo:hi], st_ref[1, :, S5_STATE_W + lo:S5_STATE_W + hi])

        def body(t, carry, k=k, coef=coef):
            new = []
            for d in range(2):
                row = pl.multiple_of((t if d == 0 else SCAN_T - 1 - t) * SUBLANES, SUBLANES)
                hr, hi_ = carry[2 * d], carry[2 * d + 1]
                ar, ai = coef[2 * d], coef[2 * d + 1]
                br = hs_ref[d, k, pl.ds(row, SUBLANES), 0:S5_BLK_W]
                bi = hs_ref[d, k, pl.ds(row, SUBLANES), S5_BLK_W:2 * S5_BLK_W]
                nr = ar * hr - ai * hi_ + br
                ni = ar * hi_ + ai * hr + bi
                hs_ref[d, k, pl.ds(row, SUBLANES), 0:S5_BLK_W] = nr
                hs_ref[d, k, pl.ds(row, SUBLANES), S5_BLK_W:2 * S5_BLK_W] = ni
                new += [nr, ni]
            return tuple(new)

        fin = lax.fori_loop(0, SCAN_T, body, init, unroll=8)
        st_ref[0, :, lo:hi] = fin[0]
        st_ref[0, :, S5_STATE_W + lo:S5_STATE_W + hi] = fin[1]
        st_ref[1, :, lo:hi] = fin[2]
        st_ref[1, :, S5_STATE_W + lo:S5_STATE_W + hi] = fin[3]

    for d, y_ref in ((0, yf_ref), (1, yb_ref)):
        for k in range(S5_BLK):
            ys_ref[k] = _dot(hs_ref[d, k].astype(BF16), cout_ref[d, k])
        for b in range(SUBLANES):
            for k in range(S5_BLK):
                c0 = b * S5_WIDTH + k * LANES
                y_ref[:, c0:c0 + LANES] = ys_ref[k, pl.ds(b, SCAN_T, stride=SUBLANES), :]

    finf_ref[...] = st_ref[0]
    finb_ref[...] = st_ref[1]


def _s5_scan(u_tm, a, b_in, c_out, h0f, h0b):
    rev = lambda j: N_CHUNKS - 1 - j
    width = SUBLANES * S5_WIDTH
    row_f = pl.BlockSpec((SCAN_T, width), lambda j: (j, 0))
    row_b = pl.BlockSpec((SCAN_T, width), lambda j: (rev(j), 0))
    st_f = pl.BlockSpec((None, SUBLANES, 2 * S5_STATE_W), lambda j: (_seq_of_chunk(j), 0, 0))
    st_b = pl.BlockSpec((None, SUBLANES, 2 * S5_STATE_W), lambda j: (_seq_of_chunk(rev(j)), 0, 0))
    st_shape = jax.ShapeDtypeStruct((3, SUBLANES, 2 * S5_STATE_W), F32)
    return pl.pallas_call(
        _s5_scan_kernel,
        grid=(N_CHUNKS,),
        in_specs=[row_f, row_b, _const_spec(a.shape), _const_spec(b_in.shape), _const_spec(c_out.shape),
                  st_f, st_b],
        out_specs=[row_f, row_b, st_f, st_b],
        out_shape=[jax.ShapeDtypeStruct((N_TM_ROWS, width), F32)] * 2 + [st_shape] * 2,
        scratch_shapes=[pltpu.VMEM((2, S5_BLK, SCAN_R, LANES), F32),
                        pltpu.VMEM((2, S5_BLK, SCAN_R, 2 * S5_BLK_W), F32),
                        pltpu.VMEM((S5_BLK, SCAN_R, LANES), F32),
                        pltpu.VMEM((2, SUBLANES, 2 * S5_STATE_W), F32)],
        compiler_params=_cparams(("arbitrary",)),
        name="s5_scan",
    )(u_tm, u_tm, a, b_in, c_out, h0f, h0b)


def _even_pre_kernel(n_x, *refs):
    x_refs, refs = refs[:n_x], refs[n_x:]
    mod_ref, g_ref, w_ref, cos_ref, sa_ref, sb_ref, u_ref, q_ref, k_ref, v_ref, kf_ref, vf_ref = refs
    m = mod_ref[...]
    h = _rms(_read_stream(x_refs), g_ref[...]) * (1.0 + m[1:2]) + m[0:1]
    proj = _dot(h.astype(BF16), w_ref[...])
    cos, sa, sb = cos_ref[...], sa_ref[...], sb_ref[...]
    half = DIFF_HEAD_DIM // 2

    def rope(z):
        return z * cos + pltpu.roll(z, half, 1) * sa + pltpu.roll(z, DIFF_WIDTH - half, 1) * sb

    u_ref[...] = proj[:, :S5_WIDTH]
    k = rope(proj[:, S5_WIDTH + DIFF_WIDTH:S5_WIDTH + 2 * DIFF_WIDTH])
    v = proj[:, S5_WIDTH + 2 * DIFF_WIDTH:]
    q_ref[...] = (rope(proj[:, S5_WIDTH:S5_WIDTH + DIFF_WIDTH]) * DIFF_HEAD_DIM ** -0.5).astype(BF16)
    k_ref[...] = k.astype(BF16)
    v_ref[...] = v.astype(BF16)

    @pl.when(pl.program_id(0) < P_TILES)
    def _():
        kf_ref[...] = k
        vf_ref[...] = v


def _even_pre(xs, mod_l, g_pre, w_in, rope_tabs):
    rope_spec = pl.BlockSpec((TM, DIFF_WIDTH), lambda i: (_rope_block(i), 0))
    return pl.pallas_call(
        functools.partial(_even_pre_kernel, len(xs)),
        grid=(N_TILES,),
        in_specs=_stream_specs(xs, D_MODEL) + [
            MOD_SPEC,
            _const_spec((1, D_MODEL)),
            _const_spec((D_MODEL, EVEN_IN)),
            rope_spec, rope_spec, rope_spec,
        ],
        out_specs=[
            pl.BlockSpec((TM, S5_WIDTH), lambda i: (_tm_row_block(i), _tm_col_block(i))),
            _tok_spec(DIFF_WIDTH), _tok_spec(DIFF_WIDTH), _tok_spec(DIFF_WIDTH),
            _ctx_spec(DIFF_WIDTH), _ctx_spec(DIFF_WIDTH),
        ],
        out_shape=[jax.ShapeDtypeStruct((N_TM_ROWS, SUBLANES * S5_WIDTH), F32)]
        + [jax.ShapeDtypeStruct((N_TOK, DIFF_WIDTH), BF16)] * 3
        + [jax.ShapeDtypeStruct((N_P, DIFF_WIDTH), F32)] * 2,
        compiler_params=_cparams(("arbitrary",)),
        name="even_pre",
    )(*xs, mod_l, g_pre, w_in, *rope_tabs)


def _softmax_parts(parts):
    m = functools.reduce(jnp.maximum, [jnp.max(s, axis=-1, keepdims=True) for s in parts])
    ps = [jnp.exp(s - m) for s in parts]
    l = functools.reduce(jnp.add, [jnp.sum(p, axis=-1, keepdims=True) for p in ps])
    return ps, l


def _diff_attn_kernel(lam_init, has_ctx, *refs):
    if has_ctx:
        q_ref, k_ref, v_ref, kc_ref, vc_ref, lq1, lk1, lq2, lk2, g_ref, o_ref = refs
    else:
        q_ref, k_ref, v_ref, lq1, lk1, lq2, lk2, g_ref, o_ref = refs
    lam = (jnp.exp(jnp.sum(lq1[...] * lk1[...], axis=-1, keepdims=True))
           - jnp.exp(jnp.sum(lq2[...] * lk2[...], axis=-1, keepdims=True)) + lam_init)
    first = lax.broadcasted_iota(jnp.int32, (1, 2 * DIFF_HEAD_DIM), 1) < DIFF_HEAD_DIM
    g = g_ref[...]
    for h in range(DIFF_HEADS):
        sl = slice(h * 2 * DIFF_HEAD_DIM, (h + 1) * 2 * DIFF_HEAD_DIM)
        qh = q_ref[:, sl]
        zero = jnp.zeros_like(qh)
        q1 = jnp.where(first, qh, zero)
        q2 = jnp.where(first, zero, qh)
        keys = [k_ref[:, sl]]
        vals = [v_ref[:, sl]]
        if has_ctx:
            keys.insert(0, kc_ref[:, sl])
            vals.insert(0, vc_ref[:, sl])
        p1, l1 = _softmax_parts([_dot_nt(q1, kk) for kk in keys])
        p2, l2 = _softmax_parts([_dot_nt(q2, kk) for kk in keys])
        r1 = 1.0 / l1
        r2 = lam / l2
        o = functools.reduce(jnp.add, [_dot((a * r1 - b * r2).astype(BF16), vv) for a, b, vv in zip(p1, p2, vals)])
        o_ref[:, sl] = (_rms(o, g) * (1.0 - lam_init)).astype(o_ref.dtype)


def _diff_attention(q, k, v, ctx, lam_vecs, subln_g, lam_init, *, batch, n, row0, tq):
    nq = n // tq
    has_ctx = ctx is not None
    in_specs = [
        pl.BlockSpec((tq, DIFF_WIDTH), lambda b, i: (row0 // tq + b * nq + i, 0)),
        pl.BlockSpec((n, DIFF_WIDTH), lambda b, i: (row0 // n + b, 0)),
        pl.BlockSpec((n, DIFF_WIDTH), lambda b, i: (row0 // n + b, 0)),
    ]
    args = [q, k, v]
    if has_ctx:
        in_specs += [pl.BlockSpec((None, PAST_LEN, DIFF_WIDTH), lambda b, i: (b, 0, 0))] * 2
        args += list(ctx)
    in_specs += [pl.BlockSpec((1, DIFF_HEAD_DIM), lambda b, i: (0, 0))] * 4
    in_specs += [pl.BlockSpec((1, 2 * DIFF_HEAD_DIM), lambda b, i: (0, 0))]
    args += list(lam_vecs) + [subln_g]
    return pl.pallas_call(
        functools.partial(_diff_attn_kernel, lam_init, has_ctx),
        grid=(batch, nq),
        in_specs=in_specs,
        out_specs=pl.BlockSpec((tq, DIFF_WIDTH), lambda b, i: (b * nq + i, 0)),
        out_shape=jax.ShapeDtypeStruct((batch * n, DIFF_WIDTH), BF16),
        compiler_params=_cparams(("parallel", "parallel")),
        name="diff_attn_ctx" if has_ctx else "diff_attn",
    )(*args)


def _post_tail(x_refs, m, mix, gpm_ref, gpf_ref, gqf_ref, wg_ref, wu_ref, wd_ref, out_refs):
    x1 = _read_stream(x_refs) + m[2:3] * _rms(mix, gpm_ref[...])
    h = (_rms(x1, gpf_ref[...]) * (1.0 + m[4:5]) + m[3:4]).astype(BF16)
    act = jax.nn.silu(_dot(h, wg_ref[...])) * _dot(h, wu_ref[...])
    y = _dot(act.astype(BF16), wd_ref[...])
    x2 = x1 + m[5:6] * _rms(y, gqf_ref[...])
    if len(out_refs) == 1:
        out_refs[0][...] = x2
    else:
        i = pl.program_id(0)

        @pl.when(i < P_TILES)
        def _():
            out_refs[0][...] = x2

        @pl.when(i >= P_TILES)
        def _():
            out_refs[1][...] = x2


def _ffn_specs():
    vec = _const_spec((1, D_MODEL))
    return [vec, vec, vec, _const_spec((D_MODEL, D_FF)), _const_spec((D_MODEL, D_FF)), _const_spec((D_FF, D_MODEL))]


def _out_stream(split):
    if split:
        return ([_ctx_spec(D_MODEL), _lat_spec(D_MODEL)],
                [jax.ShapeDtypeStruct((N_P, D_MODEL), F32), jax.ShapeDtypeStruct((N_S, D_MODEL), F32)])
    return [_tok_spec(D_MODEL)], [jax.ShapeDtypeStruct((N_TOK, D_MODEL), F32)]


def _even_post_kernel(n_x, n_out, *refs):
    x_refs, refs = refs[:n_x], refs[n_x:]
    (mod_ref, yf_ref, yb_ref, u_ref, d_ref, gw_ref, gb_ref, dap_ref, das_ref, wos_ref, wod_ref,
     gpm_ref, gpf_ref, gqf_ref, wg_ref, wu_ref, wd_ref) = refs[:17]
    m = mod_ref[...]
    y = yf_ref[...] + yb_ref[...] + d_ref[...] * u_ref[...]
    g = jax.nn.gelu(y)
    s5 = g * jax.nn.sigmoid(_dot(g.astype(BF16), gw_ref[...]) + gb_ref[...])
    mix = _dot(s5.astype(BF16), wos_ref[...]) + _dot(_read_stream((dap_ref, das_ref)), wod_ref[...])
    _post_tail(x_refs, m, mix, gpm_ref, gpf_ref, gqf_ref, wg_ref, wu_ref, wd_ref, refs[17:17 + n_out])


def _even_post(xs, mod_l, y_f, y_b, u_tm, s5_d, glu_w, glu_b, da_p, da_s, w_out_s5, w_out_diff, ffn_args,
               split_out):
    tm_spec = pl.BlockSpec((TM, S5_WIDTH), lambda i: (_tm_row_block(i), _tm_col_block(i)))
    vec = lambda n: _const_spec((1, n))
    out_specs, out_shape = _out_stream(split_out)
    return pl.pallas_call(
        functools.partial(_even_post_kernel, len(xs), len(out_specs)),
        grid=(N_TILES,),
        in_specs=_stream_specs(xs, D_MODEL) + [
            MOD_SPEC,
            tm_spec, tm_spec, tm_spec,
            vec(S5_WIDTH), _const_spec((S5_WIDTH, S5_WIDTH)), vec(S5_WIDTH),
            _ctx_spec(DIFF_WIDTH), _lat_spec(DIFF_WIDTH),
            _const_spec((S5_WIDTH, D_MODEL)), _const_spec((DIFF_WIDTH, D_MODEL)),
        ] + _ffn_specs(),
        out_specs=out_specs,
        out_shape=out_shape,
        compiler_params=_cparams(("arbitrary",)),
        name="even_post_ffn",
    )(*xs, mod_l, y_f, y_b, u_tm, s5_d, glu_w, glu_b, da_p, da_s, w_out_s5, w_out_diff, *ffn_args)


MLA_HEAD_PAD = LANES
MLA_Q_W = MLA_HEADS * MLA_HEAD_PAD
MLA_IN_W = MLA_Q_RANK + MLA_KV_RANK + MLA_HEAD_PAD
MLA_CKR_W = MLA_KV_RANK + MLA_HEAD_PAD


def _mla_rope(z, cos, sa, sb):
    w = z.shape[-1]
    half = MLA_ROPE // 2
    return z * cos + pltpu.roll(z, half, 1) * sa + pltpu.roll(z, w - half, 1) * sb


def _odd_pre_kernel(n_x, *refs):
    x_refs, refs = refs[:n_x], refs[n_x:]
    (mod_ref, g_ref, w_ref, gq_ref, gkv_ref, wq_ref, cos_ref, sa_ref, sb_ref,
     q_ref, ckr_ref, ckv_ref, kr_ref) = refs
    m = mod_ref[...]
    h = _rms(_read_stream(x_refs), g_ref[...]) * (1.0 + m[1:2]) + m[0:1]
    proj = _dot(h.astype(BF16), w_ref[...])
    cq = _rms(proj[:, :MLA_Q_RANK], gq_ref[...])
    ckv = _rms(proj[:, MLA_Q_RANK:MLA_Q_RANK + MLA_KV_RANK], gkv_ref[...])
    krp = proj[:, MLA_Q_RANK + MLA_KV_RANK:]
    cos, sa, sb = cos_ref[...], sa_ref[...], sb_ref[...]
    q = _dot(cq.astype(BF16), wq_ref[...]) * (MLA_NOPE + MLA_ROPE) ** -0.5
    for hd in range(MLA_HEADS):
        sl = slice(hd * MLA_HEAD_PAD, (hd + 1) * MLA_HEAD_PAD)
        q_ref[:, sl] = _mla_rope(q[:, sl], cos, sa, sb).astype(BF16)
    ckr_ref[:, :MLA_KV_RANK] = ckv.astype(BF16)
    ckr_ref[:, MLA_KV_RANK:] = _mla_rope(krp, cos, sa, sb).astype(BF16)

    @pl.when(pl.program_id(0) < P_TILES)
    def _():
        ckv_ref[...] = ckv
        kr_ref[...] = krp


def _odd_pre(xs, mod_l, g_pre, w_in, gq, gkv, wq, rope_tabs):
    rope_spec = pl.BlockSpec((TM, MLA_HEAD_PAD), lambda i: (_rope_block(i), 0))
    return pl.pallas_call(
        functools.partial(_odd_pre_kernel, len(xs)),
        grid=(N_TILES,),
        in_specs=_stream_specs(xs, D_MODEL) + [
            MOD_SPEC,
            _const_spec((1, D_MODEL)),
            _const_spec((D_MODEL, MLA_IN_W)),
            _const_spec((1, MLA_Q_RANK)), _const_spec((1, MLA_KV_RANK)),
            _const_spec((MLA_Q_RANK, MLA_Q_W)),
            rope_spec, rope_spec, rope_spec,
        ],
        out_specs=[_tok_spec(MLA_Q_W), _tok_spec(MLA_CKR_W), _ctx_spec(MLA_KV_RANK), _ctx_spec(MLA_HEAD_PAD)],
        out_shape=[
            jax.ShapeDtypeStruct((N_TOK, MLA_Q_W), BF16),
            jax.ShapeDtypeStruct((N_TOK, MLA_CKR_W), BF16),
            jax.ShapeDtypeStruct((N_P, MLA_KV_RANK), F32),
            jax.ShapeDtypeStruct((N_P, MLA_HEAD_PAD), F32),
        ],
        compiler_params=_cparams(("arbitrary",)),
        name="odd_pre",
    )(*xs, mod_l, g_pre, w_in, gq, gkv, wq, *rope_tabs)


def _mla_attn_kernel(has_ctx, *refs):
    if has_ctx:
        q_ref, ckr_ref, ckrc_ref, wk_ref, wv_ref, o_ref, kf_ref, vf_ref = refs
    else:
        q_ref, ckr_ref, wk_ref, wv_ref, o_ref, kf_ref, vf_ref = refs
    off = PAST_LEN if has_ctx else 0

    @pl.when(pl.program_id(1) == 0)
    def _():
        srcs = [(0, ckrc_ref)] if has_ctx else []
        srcs.append((off, ckr_ref))
        for r0, src in srcs:
            c = src[...]
            rows = pl.ds(r0, c.shape[0])
            kf_ref[rows, :] = _dot(c, wk_ref[...]).astype(BF16)
            vf_ref[rows, :] = _dot(c[:, :MLA_KV_RANK], wv_ref[...]).astype(BF16)

    low = lax.broadcasted_iota(jnp.int32, (1, LANES), 1) < MLA_V

    def pair(j, carry):
        c0 = pl.multiple_of(j * 2 * MLA_HEAD_PAD, 2 * MLA_HEAD_PAD)
        vj = vf_ref[:, pl.ds(pl.multiple_of(j * LANES, LANES), LANES)]
        outs = []
        for t in range(2):
            cs = pl.ds(pl.multiple_of(c0 + t * MLA_HEAD_PAD, MLA_HEAD_PAD), MLA_HEAD_PAD)
            s = _dot_nt(q_ref[:, cs], kf_ref[:, cs])
            mx = jnp.max(s, axis=-1, keepdims=True)
            p = jnp.exp(s - mx)
            l = jnp.sum(p, axis=-1, keepdims=True)
            outs.append(_dot(p.astype(BF16), vj) * (1.0 / l))
        o_ref[:, pl.ds(pl.multiple_of(j * LANES, LANES), LANES)] = jnp.where(low, outs[0], outs[1]).astype(o_ref.dtype)
        return carry

    lax.fori_loop(0, MLA_HEADS // 2, pair, 0)


def _mla_attention(q, ckr, ckr_ctx, wk, wv, *, batch, n, row0, tq):
    nq = n // tq
    has_ctx = ckr_ctx is not None
    s_len = n + (PAST_LEN if has_ctx else 0)
    in_specs = [
        pl.BlockSpec((tq, MLA_Q_W), lambda b, i: (row0 // tq + b * nq + i, 0)),
        pl.BlockSpec((n, MLA_CKR_W), lambda b, i: (row0 // n + b, 0)),
    ]
    args = [q, ckr]
    if has_ctx:
        in_specs.append(pl.BlockSpec((None, PAST_LEN, MLA_CKR_W), lambda b, i: (b, 0, 0)))
        args.append(ckr_ctx)
    in_specs += [_const_spec(wk.shape), _const_spec(wv.shape)]
    args += [wk, wv]
    return pl.pallas_call(
        functools.partial(_mla_attn_kernel, has_ctx),
        grid=(batch, nq),
        in_specs=in_specs,
        out_specs=pl.BlockSpec((tq, MLA_HEADS * MLA_V), lambda b, i: (b * nq + i, 0)),
        out_shape=jax.ShapeDtypeStruct((batch * n, MLA_HEADS * MLA_V), BF16),
        scratch_shapes=[pltpu.VMEM((s_len, MLA_Q_W), BF16), pltpu.VMEM((s_len, MLA_HEADS * MLA_V), BF16)],
        compiler_params=_cparams(("parallel", "arbitrary")),
        name="mla_attn_ctx" if has_ctx else "mla_attn",
    )(*args)


def _odd_post_kernel(n_x, n_out, *refs):
    x_refs, refs = refs[:n_x], refs[n_x:]
    mod_ref, ap_ref, as_ref, wo_ref, gpm_ref, gpf_ref, gqf_ref, wg_ref, wu_ref, wd_ref = refs[:10]
    mix = _dot(_read_stream((ap_ref, as_ref)), wo_ref[...])
    _post_tail(x_refs, mod_ref[...], mix, gpm_ref, gpf_ref, gqf_ref, wg_ref, wu_ref, wd_ref, refs[10:10 + n_out])


def _odd_post(xs, mod_l, at_p, at_s, w_out, ffn_args, split_out):
    width = MLA_HEADS * MLA_V
    out_specs, out_shape = _out_stream(split_out)
    return pl.pallas_call(
        functools.partial(_odd_post_kernel, len(xs), len(out_specs)),
        grid=(N_TILES,),
        in_specs=_stream_specs(xs, D_MODEL) + [
            MOD_SPEC,
            _ctx_spec(width), _lat_spec(width),
            _const_spec((width, D_MODEL)),
        ] + _ffn_specs(),
        out_specs=out_specs,
        out_shape=out_shape,
        compiler_params=_cparams(("arbitrary",)),
        name="odd_post_ffn",
    )(*xs, mod_l, at_p, at_s, w_out, *ffn_args)


def _rope_angles(rot_dim):
    rows = DEC_SEQ // GRID_W
    row = jnp.repeat(jnp.arange(rows, dtype=F32), GRID_W)
    col = jnp.tile(jnp.arange(GRID_W, dtype=F32), rows)
    n_freq = rot_dim // 4
    inv = ROPE_BASE ** (-jnp.arange(n_freq, dtype=F32) / n_freq)
    ang = jnp.concatenate([row[:, None] * inv, col[:, None] * inv], axis=-1)
    return jnp.cos(ang), jnp.sin(ang)


def _with_identity(cos, sa, sb):
    one = jnp.ones((TM, cos.shape[1]), F32)
    zero = jnp.zeros((TM, cos.shape[1]), F32)
    return (jnp.concatenate([one, cos]), jnp.concatenate([zero, sa]), jnp.concatenate([zero, sb]))


def _diff_rope_tables():
    c, s = _rope_angles(DIFF_HEAD_DIM)
    z = jnp.zeros_like(s)
    reps = DIFF_WIDTH // DIFF_HEAD_DIM
    cos = jnp.tile(jnp.concatenate([c, c], axis=1), (1, reps))
    sa = jnp.tile(jnp.concatenate([z, s], axis=1), (1, reps))
    sb = jnp.tile(jnp.concatenate([-s, z], axis=1), (1, reps))
    return _with_identity(cos, sa, sb)


def _mla_rope_tables():
    c, s = _rope_angles(MLA_ROPE)
    n = c.shape[0]
    z = jnp.zeros_like(s)
    one = jnp.ones((n, MLA_NOPE), F32)
    zn = jnp.zeros((n, MLA_NOPE), F32)
    pad1 = jnp.ones((n, MLA_HEAD_PAD - MLA_NOPE - MLA_ROPE), F32)
    pad0 = jnp.zeros((n, MLA_HEAD_PAD - MLA_NOPE - MLA_ROPE), F32)
    cos = jnp.concatenate([one, c, c, pad1], axis=1)
    sa = jnp.concatenate([zn, z, s, pad0], axis=1)
    sb = jnp.concatenate([zn, -s, z, pad0], axis=1)
    return _with_identity(cos, sa, sb)


def kernel(x_prompt, x_sample, state_s5_re, state_s5_im, cache_diff_k, cache_diff_v, cache_mla_ckv, cache_mla_krope, c, c_ctx, w_mod, b_mod, g_pre_mix, g_post_mix, g_pre_ffn, g_post_ffn, w_ffn_gate, w_ffn_up, w_ffn_down, w_in_even, w_out_even, s5_lam_re, s5_lam_im, s5_log_dt, s5_b_re, s5_b_im, s5_c_re, s5_c_im, s5_d, s5_glu_w, s5_glu_b, diff_lam_q1, diff_lam_k1, diff_lam_q2, diff_lam_k2, diff_subln_g, w_in_odd, mla_q_norm_g, mla_w_q_up, mla_kv_norm_g, mla_w_kv_up, w_out_odd):
    xs = (x_prompt.reshape(N_P, D_MODEL), x_sample.reshape(N_S, D_MODEL))
    conds = jnp.concatenate([c_ctx[None, :], c, jnp.zeros((N_COND - 1 - DEC_BATCH, D_MODEL), F32)], axis=0)
    mod = _modulation(conds, w_mod, b_mod).reshape(DEPTH, N_COND, 6, D_MODEL)

    a_re, a_im, bb_re, bb_im = _s5_params(s5_lam_re, s5_lam_im, s5_log_dt, s5_b_re, s5_b_im)
    diff_tabs = _diff_rope_tables()
    mla_tabs = _mla_rope_tables()
    row = lambda v: v.reshape(1, -1)

    s5_re_list, s5_im_list, dk_list, dv_list, ckv_list, kr_list = [], [], [], [], [], []
    for l in range(DEPTH):
        last = l == DEPTH - 1
        ffn_args = (row(g_post_mix[l]), row(g_pre_ffn[l]), row(g_post_ffn[l]),
                    w_ffn_gate[l].astype(BF16), w_ffn_up[l].astype(BF16), w_ffn_down[l].astype(BF16))
        if l % 2 == 0:
            e = l // 2
            lam_init = 0.8 - 0.6 * math.exp(-0.3 * l)
            u_tm, q, k, v, kf, vf = _even_pre(xs, mod[l], row(g_pre_mix[l]), w_in_even[e].astype(BF16), diff_tabs)

            a = jnp.stack([a_re[e, 0], a_im[e, 0], a_re[e, 1], a_im[e, 1]]).reshape(4, S5_STATE_W)
            b_in, c_out = _s5_matrices(bb_re[e], bb_im[e], s5_c_re[e], s5_c_im[e])
            zeros = jnp.zeros((2, SUBLANES, 2 * S5_STATE_W), F32)
            h0 = [jnp.concatenate([zeros, jnp.concatenate(
                [state_s5_re[:, e, d].reshape(1, DEC_BATCH, S5_STATE_W),
                 state_s5_im[:, e, d].reshape(1, DEC_BATCH, S5_STATE_W)], axis=-1)], axis=0) for d in range(2)]
            y_f, y_b, fin_f, fin_b = _s5_scan(u_tm, a, b_in, c_out, h0[0], h0[1])
            fin = jnp.stack([fin_f[:2].reshape(BATCH, 2, S5_GROUPS, S5_STATE),
                             fin_b[:2].reshape(BATCH, 2, S5_GROUPS, S5_STATE)], axis=1)
            s5_re_list.append(fin[:, :, 0])
            s5_im_list.append(fin[:, :, 1])

            lam_vecs = [row(t[e]) for t in (diff_lam_q1, diff_lam_k1, diff_lam_q2, diff_lam_k2)]
            subln = row(diff_subln_g[e])
            ctx = (cache_diff_k[:, e].astype(BF16).reshape(DEC_BATCH, PAST_LEN, DIFF_WIDTH),
                   cache_diff_v[:, e].astype(BF16).reshape(DEC_BATCH, PAST_LEN, DIFF_WIDTH))
            da_p = _diff_attention(q, k, v, None, lam_vecs, subln, lam_init, batch=BATCH, n=SEQ, row0=0, tq=SEQ)
            da_s = _diff_attention(q, k, v, ctx, lam_vecs, subln, lam_init, batch=DEC_BATCH, n=DEC_SEQ,
                                   row0=N_P, tq=256)

            w_out = w_out_even[e].astype(BF16)
            xs = _even_post(xs, mod[l], y_f, y_b, u_tm, row(s5_d[e]), s5_glu_w[e].astype(BF16),
                            row(s5_glu_b[e]), da_p, da_s, w_out[:S5_WIDTH], w_out[S5_WIDTH:], ffn_args, last)
            dk_list.append(kf.reshape(BATCH, SEQ, DIFF_HEADS, 2, DIFF_HEAD_DIM))
            dv_list.append(vf.reshape(BATCH, SEQ, DIFF_HEADS, 2 * DIFF_HEAD_DIM))
        else:
            o = l // 2
            pad = MLA_HEAD_PAD - MLA_NOPE - MLA_ROPE
            w_in = jnp.concatenate(
                [w_in_odd[o][:, :MLA_Q_RANK + MLA_KV_RANK], jnp.zeros((D_MODEL, MLA_NOPE), F32),
                 w_in_odd[o][:, MLA_Q_RANK + MLA_KV_RANK:], jnp.zeros((D_MODEL, pad), F32)], axis=1).astype(BF16)
            wq = jnp.pad(mla_w_q_up[o].reshape(MLA_Q_RANK, MLA_HEADS, MLA_NOPE + MLA_ROPE),
                         ((0, 0), (0, 0), (0, pad))).reshape(MLA_Q_RANK, MLA_Q_W).astype(BF16)
            wkv = mla_w_kv_up[o].reshape(MLA_KV_RANK, MLA_HEADS, MLA_NOPE + MLA_V)
            wk_top = jnp.pad(wkv[:, :, :MLA_NOPE], ((0, 0), (0, 0), (0, MLA_HEAD_PAD - MLA_NOPE))).reshape(MLA_KV_RANK, MLA_Q_W)
            sel = jnp.pad(jnp.eye(MLA_ROPE, dtype=F32), ((MLA_NOPE, pad), (MLA_NOPE, pad)))
            wk = jnp.concatenate([wk_top, jnp.tile(sel, (1, MLA_HEADS))], axis=0).astype(BF16)
            wv = wkv[:, :, MLA_NOPE:].reshape(MLA_KV_RANK, MLA_HEADS * MLA_V).astype(BF16)

            q, ckr, ckv, krp = _odd_pre(xs, mod[l], row(g_pre_mix[l]), w_in, row(mla_q_norm_g[o]),
                                        row(mla_kv_norm_g[o]), wq, mla_tabs)
            ckr_ctx = jnp.concatenate(
                [cache_mla_ckv[:, o], jnp.pad(cache_mla_krope[:, o], ((0, 0), (0, 0), (MLA_NOPE, pad)))],
                axis=-1).astype(BF16)
            at_p = _mla_attention(q, ckr, None, wk, wv, batch=BATCH, n=SEQ, row0=0, tq=SEQ)
            at_s = _mla_attention(q, ckr, ckr_ctx, wk, wv, batch=DEC_BATCH, n=DEC_SEQ, row0=N_P, tq=256)
            xs = _odd_post(xs, mod[l], at_p, at_s, w_out_odd[o].astype(BF16), ffn_args, last)
            ckv_list.append(ckv.reshape(BATCH, SEQ, MLA_KV_RANK))
            kr_list.append(krp[:, MLA_NOPE:MLA_NOPE + MLA_ROPE].reshape(BATCH, SEQ, MLA_ROPE))

    return (xs[0].reshape(BATCH, SEQ, D_MODEL), xs[1].reshape(DEC_BATCH, DEC_SEQ, D_MODEL),
            jnp.stack(s5_re_list, axis=1), jnp.stack(s5_im_list, axis=1),
            jnp.stack(dk_list, axis=1), jnp.stack(dv_list, axis=1),
            jnp.stack(ckv_list, axis=1), jnp.stack(kr_list, axis=1))
```

```python
import functools
import math

import jax
import jax.numpy as jnp
from jax import lax
from jax.experimental import pallas as pl
from jax.experimental.pallas import tpu as pltpu

F32 = jnp.float32
BF16 = jnp.bfloat16

D_MODEL = 1024
BATCH = 16
SEQ = 256
DEPTH = 4
DEC_BATCH = 8
DEC_SEQ = 1024
PAST_LEN = 512
GRID_W = 64
N_EVEN = (DEPTH + 1) // 2
N_ODD = DEPTH // 2
EPS = 1e-6
ROPE_BASE = 10000.0
S5_WIDTH = D_MODEL // 2
S5_GROUP = 16
S5_GROUPS = S5_WIDTH // S5_GROUP
S5_STATE = 64
DIFF_HEAD_DIM = 64
DIFF_HEADS = (D_MODEL // 2) // (2 * DIFF_HEAD_DIM)
DIFF_WIDTH = DIFF_HEADS * 2 * DIFF_HEAD_DIM
EVEN_IN = S5_WIDTH + 3 * DIFF_WIDTH
MLA_HEADS = 16
MLA_NOPE = 64
MLA_ROPE = 32
MLA_V = 64
MLA_Q_RANK = 256
MLA_KV_RANK = 128
D_FF = ((8 * D_MODEL // 3 + 255) // 256) * 256

LANES = 128
SUBLANES = 8
VMEM_LIMIT = 56 * 1024 * 1024
LOG2E = math.log2(math.e)

N_P = BATCH * SEQ
N_S = DEC_BATCH * DEC_SEQ
N_TOK = N_P + N_S
TM = 256
N_TILES = N_TOK // TM
P_TILES = N_P // TM
S_TILES_PER_B = DEC_SEQ // TM
N_COND = 16

S5_STATE_W = S5_GROUPS * S5_STATE
S5_BLK = 4
S5_BLK_W = S5_STATE_W // S5_BLK
SCAN_T = 64
SCAN_R = SCAN_T * SUBLANES
N_TM_ROWS = N_TOK // SUBLANES
N_CHUNKS = N_TM_ROWS // SCAN_T
SEQ_CHUNKS = SEQ // SCAN_T

MLA_HEAD_PAD = LANES
MLA_PAD = MLA_HEAD_PAD - MLA_NOPE - MLA_ROPE
MLA_Q_W = MLA_HEADS * MLA_HEAD_PAD
MLA_IN_W = MLA_Q_RANK + MLA_KV_RANK + 2 * MLA_HEAD_PAD
MLA_CKR_W = MLA_KV_RANK + MLA_HEAD_PAD
MLA_O_W = MLA_HEADS * MLA_V


def _cparams(sem):
    return pltpu.CompilerParams(dimension_semantics=sem, vmem_limit_bytes=VMEM_LIMIT)


def _const_spec(shape):
    nd = len(shape)
    return pl.BlockSpec(shape, lambda *_: (0,) * nd, pipeline_mode=pl.Buffered(1))


def _layer_spec(tail, *lead):
    nt = len(tail)
    return pl.BlockSpec((None,) * len(lead) + tuple(tail), lambda *_: tuple(lead) + (0,) * nt,
                        pipeline_mode=pl.Buffered(1))


def _rms(x, g):
    return x * lax.rsqrt(jnp.mean(x * x, axis=-1, keepdims=True) + EPS) * g


def _dot(a, b):
    return jnp.dot(a, b, preferred_element_type=F32)


def _dot_nt(a, b):
    return lax.dot_general(a, b, (((1,), (1,)), ((), ())), preferred_element_type=F32)


def _cond_of_tile(i):
    return jnp.where(i < P_TILES, 0, 1 + (i - P_TILES) // S_TILES_PER_B)


def _tm_row_block(i):
    return jnp.where(i < P_TILES, i // SUBLANES, 2 + (i - P_TILES) % S_TILES_PER_B)


def _tm_col_block(i):
    return jnp.where(i < P_TILES, i % SUBLANES, (i - P_TILES) // S_TILES_PER_B)


def _rope_block(i):
    return jnp.where(i < P_TILES, 0, 1 + (i - P_TILES) % S_TILES_PER_B)


def _tok_spec(width):
    return pl.BlockSpec((TM, width), lambda i: (i, 0))


def _ctx_spec(width):
    return pl.BlockSpec((TM, width), lambda i: (jnp.minimum(i, P_TILES - 1), 0))


def _lat_spec(width):
    return pl.BlockSpec((TM, width), lambda i: (jnp.maximum(i - P_TILES, 0), 0))


def _stream_specs(arrays, width):
    return [_tok_spec(width)] if len(arrays) == 1 else [_ctx_spec(width), _lat_spec(width)]


def _read_stream(refs):
    if len(refs) == 1:
        return refs[0][...]
    return jnp.where(pl.program_id(0) < P_TILES, refs[0][...], refs[1][...])


def _mod_spec(l):
    return pl.BlockSpec((None, None, 6, D_MODEL), lambda i: (l, _cond_of_tile(i), 0, 0))


def _row(ref, r):
    return ref[r:r + 1, :]


def _mod_kernel(c_ref, w_ref, b_ref, o_ref):
    s = jax.nn.silu(c_ref[...])
    o_ref[0] = _dot(s.astype(BF16), w_ref[0].astype(BF16)) + b_ref[0]


def _modulation(conds, w_mod, b_mod):
    tn = 1536
    return pl.pallas_call(
        _mod_kernel,
        grid=(DEPTH, 6 * D_MODEL // tn),
        in_specs=[
            pl.BlockSpec((N_COND, D_MODEL), lambda l, n: (0, 0)),
            pl.BlockSpec((1, D_MODEL, tn), lambda l, n: (l, 0, n)),
            pl.BlockSpec((1, 1, tn), lambda l, n: (l, 0, n)),
        ],
        out_specs=pl.BlockSpec((1, N_COND, tn), lambda l, n: (l, 0, n)),
        out_shape=jax.ShapeDtypeStruct((DEPTH, N_COND, 6 * D_MODEL), F32),
        compiler_params=_cparams(("parallel", "parallel")),
        name="modulation",
    )(conds, w_mod, b_mod.reshape(DEPTH, 1, 6 * D_MODEL))


def _s5_param_kernel(lr_ref, li_ref, ldt_ref, br_ref, bi_ref, are_ref, aim_ref, bbr_ref, bbi_ref):
    lr, li = lr_ref[...], li_ref[...]
    dt = jnp.exp(ldt_ref[...])
    mag = jnp.exp(lr * dt)
    a_re, a_im = mag * jnp.cos(li * dt), mag * jnp.sin(li * dt)
    den = lr * lr + li * li
    f_re = ((a_re - 1.0) * lr + a_im * li) / den
    f_im = (a_im * lr - (a_re - 1.0) * li) / den
    br, bi = br_ref[...], bi_ref[...]
    are_ref[...] = a_re
    aim_ref[...] = a_im
    bbr_ref[...] = f_re * br - f_im * bi
    bbi_ref[...] = f_re * bi + f_im * br


def _s5_params(lam_re, lam_im, log_dt, b_re, b_im):
    full = (N_EVEN, 2, S5_GROUPS, S5_GROUP, S5_STATE)
    flat = (math.prod(full) // LANES, LANES)
    ex = lambda a: jnp.broadcast_to(a[:, :, :, None, :], full).reshape(flat)
    ldt = jnp.broadcast_to(log_dt[:, :, :, None, None], full).reshape(flat)
    bt = lambda b: jnp.swapaxes(b, -1, -2).reshape(flat)
    spec = pl.BlockSpec(flat, lambda: (0, 0))
    outs = pl.pallas_call(
        _s5_param_kernel,
        in_specs=[spec] * 5,
        out_specs=[spec] * 4,
        out_shape=[jax.ShapeDtypeStruct(flat, F32)] * 4,
        name="s5_params",
    )(ex(lam_re), ex(lam_im), ldt, bt(b_re), bt(b_im))
    a_re, a_im, bb_re, bb_im = (o.reshape(full) for o in outs)
    return a_re[:, :, :, 0, :], a_im[:, :, :, 0, :], bb_re, bb_im


def _block_diag(m):
    eye = jnp.eye(SUBLANES, dtype=m.dtype)
    out = m[..., :, :, None, :] * eye[:, None, :, None]
    return out.reshape(m.shape[:-3] + (SUBLANES * m.shape[-2], SUBLANES * m.shape[-1]))


def _s5_matrices(bb_re, bb_im, c_re, c_im):
    gb = S5_GROUPS // S5_BLK
    shp = (N_EVEN, 2, S5_BLK, gb, S5_GROUP, S5_STATE)
    b_in = jnp.concatenate([_block_diag(bb_re.reshape(shp)), _block_diag(bb_im.reshape(shp))], axis=-1)
    ct = lambda c: jnp.swapaxes(c.reshape(shp), -1, -2)
    c_out = jnp.concatenate([_block_diag(ct(c_re)), -_block_diag(ct(c_im))], axis=-2)
    return b_in.astype(BF16), c_out.astype(BF16)


def _seq_of_chunk(c):
    return (c >= SEQ_CHUNKS).astype(jnp.int32) + (c >= 2 * SEQ_CHUNKS).astype(jnp.int32)


def _s5_scan_kernel(uf_ref, ub_ref, a_ref, bin_ref, cout_ref, h0f_ref, h0b_ref,
                    yf_ref, yb_ref, finf_ref, finb_ref, us_ref, hs_ref, ys_ref, st_ref):
    j = pl.program_id(0)
    cf = j
    cb = N_CHUNKS - 1 - j

    @pl.when((cf == 0) | (cf == SEQ_CHUNKS) | (cf == 2 * SEQ_CHUNKS))
    def _():
        st_ref[0] = h0f_ref[...]

    @pl.when((cb == N_CHUNKS - 1) | (cb == 2 * SEQ_CHUNKS - 1) | (cb == SEQ_CHUNKS - 1))
    def _():
        st_ref[1] = h0b_ref[...]

    for d, u_ref in ((0, uf_ref), (1, ub_ref)):
        for b in range(SUBLANES):
            for k in range(S5_BLK):
                c0 = b * S5_WIDTH + k * LANES
                us_ref[d, k, pl.ds(b, SCAN_T, stride=SUBLANES), :] = u_ref[:, c0:c0 + LANES]
        for k in range(S5_BLK):
            hs_ref[d, k] = _dot(us_ref[d, k].astype(BF16), bin_ref[d, k])

    for k in range(S5_BLK):
        lo, hi = k * S5_BLK_W, (k + 1) * S5_BLK_W
        coef = [jnp.broadcast_to(a_ref[r:r + 1, lo:hi], (SUBLANES, S5_BLK_W)) for r in range(4)]
        init = (st_ref[0, :, lo:hi], st_ref[0, :, S5_STATE_W + lo:S5_STATE_W + hi],
                st_ref[1, :, lo:hi], st_ref[1, :, S5_STATE_W + lo:S5_STATE_W + hi])

        def body(t, carry, k=k, coef=coef):
            new = []
            for d in range(2):
                row = pl.multiple_of((t if d == 0 else SCAN_T - 1 - t) * SUBLANES, SUBLANES)
                hr, hi_ = carry[2 * d], carry[2 * d + 1]
                ar, ai = coef[2 * d], coef[2 * d + 1]
                br = hs_ref[d, k, pl.ds(row, SUBLANES), 0:S5_BLK_W]
                bi = hs_ref[d, k, pl.ds(row, SUBLANES), S5_BLK_W:2 * S5_BLK_W]
                nr = ar * hr - ai * hi_ + br
                ni = ar * hi_ + ai * hr + bi
                hs_ref[d, k, pl.ds(row, SUBLANES), 0:S5_BLK_W] = nr
                hs_ref[d, k, pl.ds(row, SUBLANES), S5_BLK_W:2 * S5_BLK_W] = ni
                new += [nr, ni]
            return tuple(new)

        fin = lax.fori_loop(0, SCAN_T, body, init, unroll=8)
        st_ref[0, :, lo:hi] = fin[0]
        st_ref[0, :, S5_STATE_W + lo:S5_STATE_W + hi] = fin[1]
        st_ref[1, :, lo:hi] = fin[2]
        st_ref[1, :, S5_STATE_W + lo:S5_STATE_W + hi] = fin[3]

    for d, y_ref in ((0, yf_ref), (1, yb_ref)):
        for k in range(S5_BLK):
            ys_ref[k] = _dot(hs_ref[d, k].astype(BF16), cout_ref[d, k])
        for b in range(SUBLANES):
            for k in range(S5_BLK):
                c0 = b * S5_WIDTH + k * LANES
                y_ref[:, c0:c0 + LANES] = ys_ref[k, pl.ds(b, SCAN_T, stride=SUBLANES), :]

    finf_ref[...] = st_ref[0]
    finb_ref[...] = st_ref[1]


def _s5_scan(e, u_tm, a, b_in, c_out, h0):
    rev = lambda j: N_CHUNKS - 1 - j
    width = SUBLANES * S5_WIDTH
    st_w = 2 * S5_STATE_W
    row_f = pl.BlockSpec((SCAN_T, width), lambda j: (j, 0))
    row_b = pl.BlockSpec((SCAN_T, width), lambda j: (rev(j), 0))
    h0_f = pl.BlockSpec((None, None, None, SUBLANES, st_w), lambda j: (e, 0, _seq_of_chunk(j), 0, 0))
    h0_b = pl.BlockSpec((None, None, None, SUBLANES, st_w), lambda j: (e, 1, _seq_of_chunk(rev(j)), 0, 0))
    st_f = pl.BlockSpec((None, SUBLANES, st_w), lambda j: (_seq_of_chunk(j), 0, 0))
    st_b = pl.BlockSpec((None, SUBLANES, st_w), lambda j: (_seq_of_chunk(rev(j)), 0, 0))
    st_shape = jax.ShapeDtypeStruct((3, SUBLANES, st_w), F32)
    return pl.pallas_call(
        _s5_scan_kernel,
        grid=(N_CHUNKS,),
        in_specs=[row_f, row_b, _layer_spec(a.shape[1:], e), _layer_spec(b_in.shape[1:], e),
                  _layer_spec(c_out.shape[1:], e), h0_f, h0_b],
        out_specs=[row_f, row_b, st_f, st_b],
        out_shape=[jax.ShapeDtypeStruct((N_TM_ROWS, width), F32)] * 2 + [st_shape] * 2,
        scratch_shapes=[pltpu.VMEM((2, S5_BLK, SCAN_R, LANES), F32),
                        pltpu.VMEM((2, S5_BLK, SCAN_R, 2 * S5_BLK_W), F32),
                        pltpu.VMEM((S5_BLK, SCAN_R, LANES), F32),
                        pltpu.VMEM((2, SUBLANES, st_w), F32)],
        compiler_params=_cparams(("arbitrary",)),
        name="s5_scan",
    )(u_tm, u_tm, a, b_in, c_out, h0, h0)


def _even_pre_kernel(l, n_x, *refs):
    x_refs, refs = refs[:n_x], refs[n_x:]
    mod_ref, g_ref, w_ref, cos_ref, sa_ref, sb_ref, u_ref, q_ref, k_ref, v_ref, kf_ref, vf_ref = refs
    m = mod_ref[...]
    h = _rms(_read_stream(x_refs), _row(g_ref, l)) * (1.0 + m[1:2]) + m[0:1]
    proj = _dot(h.astype(BF16), w_ref[...])
    cos, sa, sb = cos_ref[...], sa_ref[...], sb_ref[...]
    half = DIFF_HEAD_DIM // 2

    def rope(z):
        return z * cos + pltpu.roll(z, half, 1) * sa + pltpu.roll(z, DIFF_WIDTH - half, 1) * sb

    u_ref[...] = proj[:, :S5_WIDTH]
    k = rope(proj[:, S5_WIDTH + DIFF_WIDTH:S5_WIDTH + 2 * DIFF_WIDTH])
    v = proj[:, S5_WIDTH + 2 * DIFF_WIDTH:]
    q_ref[...] = (rope(proj[:, S5_WIDTH:S5_WIDTH + DIFF_WIDTH]) * (DIFF_HEAD_DIM ** -0.5 * LOG2E)).astype(BF16)
    k_ref[...] = k.astype(BF16)
    v_ref[...] = v.astype(BF16)

    @pl.when(pl.program_id(0) < P_TILES)
    def _():
        kf_ref[...] = k
        vf_ref[...] = v


def _even_pre(l, xs, mod, g_pre, w_in, rope_tabs):
    rope_spec = pl.BlockSpec((TM, DIFF_WIDTH), lambda i: (_rope_block(i), 0))
    return pl.pallas_call(
        functools.partial(_even_pre_kernel, l, len(xs)),
        grid=(N_TILES,),
        in_specs=_stream_specs(xs, D_MODEL) + [
            _mod_spec(l),
            _const_spec(g_pre.shape),
            _layer_spec((D_MODEL, EVEN_IN), l // 2),
            rope_spec, rope_spec, rope_spec,
        ],
        out_specs=[
            pl.BlockSpec((TM, S5_WIDTH), lambda i: (_tm_row_block(i), _tm_col_block(i))),
            _tok_spec(DIFF_WIDTH), _tok_spec(DIFF_WIDTH), _tok_spec(DIFF_WIDTH),
            _ctx_spec(DIFF_WIDTH), _ctx_spec(DIFF_WIDTH),
        ],
        out_shape=[jax.ShapeDtypeStruct((N_TM_ROWS, SUBLANES * S5_WIDTH), F32)]
        + [jax.ShapeDtypeStruct((N_TOK, DIFF_WIDTH), BF16)] * 3
        + [jax.ShapeDtypeStruct((N_P, DIFF_WIDTH), F32)] * 2,
        compiler_params=_cparams(("arbitrary",)),
        name="even_pre",
    )(*xs, mod, g_pre, w_in, *rope_tabs)


def _softmax2_parts(parts):
    m = functools.reduce(jnp.maximum, [jnp.max(s, axis=-1, keepdims=True) for s in parts])
    ps = [jnp.exp2(s - m) for s in parts]
    l = functools.reduce(jnp.add, [jnp.sum(p, axis=-1, keepdims=True) for p in ps])
    return ps, l


def _diff_attn_kernel(e, lam_init, has_ctx, *refs):
    if has_ctx:
        q_ref, k_ref, v_ref, kc_ref, vc_ref, lq1, lk1, lq2, lk2, g_ref, o_ref = refs
    else:
        q_ref, k_ref, v_ref, lq1, lk1, lq2, lk2, g_ref, o_ref = refs
    lam = (jnp.exp(jnp.sum(_row(lq1, e) * _row(lk1, e), axis=-1, keepdims=True))
           - jnp.exp(jnp.sum(_row(lq2, e) * _row(lk2, e), axis=-1, keepdims=True)) + lam_init)
    first = lax.broadcasted_iota(jnp.int32, (1, 2 * DIFF_HEAD_DIM), 1) < DIFF_HEAD_DIM
    g = _row(g_ref, e)
    for h in range(DIFF_HEADS):
        sl = slice(h * 2 * DIFF_HEAD_DIM, (h + 1) * 2 * DIFF_HEAD_DIM)
        qh = q_ref[:, sl]
        zero = jnp.zeros_like(qh)
        q1 = jnp.where(first, qh, zero)
        q2 = jnp.where(first, zero, qh)
        keys = [k_ref[:, sl]]
        vals = [v_ref[:, sl]]
        if has_ctx:
            keys.insert(0, kc_ref[:, sl])
            vals.insert(0, vc_ref[:, sl])
        p1, l1 = _softmax2_parts([_dot_nt(q1, kk) for kk in keys])
        p2, l2 = _softmax2_parts([_dot_nt(q2, kk) for kk in keys])
        ratio = lam * l1 / l2
        o = functools.reduce(jnp.add, [_dot((a - b * ratio).astype(BF16), vv) for a, b, vv in zip(p1, p2, vals)])
        o = o * (1.0 / l1)
        o_ref[:, sl] = (_rms(o, g) * (1.0 - lam_init)).astype(o_ref.dtype)


def _diff_attention(e, q, k, v, ctx, lam_vecs, subln_g, lam_init, *, batch, n, row0, tq):
    nq = n // tq
    has_ctx = ctx is not None
    in_specs = [
        pl.BlockSpec((tq, DIFF_WIDTH), lambda b, i: (row0 // tq + b * nq + i, 0)),
        pl.BlockSpec((n, DIFF_WIDTH), lambda b, i: (row0 // n + b, 0)),
        pl.BlockSpec((n, DIFF_WIDTH), lambda b, i: (row0 // n + b, 0)),
    ]
    args = [q, k, v]
    if has_ctx:
        in_specs += [pl.BlockSpec((None, None, PAST_LEN, DIFF_WIDTH), lambda b, i: (b, e, 0, 0))] * 2
        args += list(ctx)
    in_specs += [_const_spec(t.shape) for t in lam_vecs] + [_const_spec(subln_g.shape)]
    args += list(lam_vecs) + [subln_g]
    return pl.pallas_call(
        functools.partial(_diff_attn_kernel, e, lam_init, has_ctx),
        grid=(batch, nq),
        in_specs=in_specs,
        out_specs=pl.BlockSpec((tq, DIFF_WIDTH), lambda b, i: (b * nq + i, 0)),
        out_shape=jax.ShapeDtypeStruct((batch * n, DIFF_WIDTH), BF16),
        compiler_params=_cparams(("parallel", "parallel")),
        name="diff_attn_ctx" if has_ctx else "diff_attn",
    )(*args)


def _post_tail(l, x_refs, m, mix, gpm_ref, gpf_ref, gqf_ref, wg_ref, wu_ref, wd_ref, out_refs):
    x1 = _read_stream(x_refs) + m[2:3] * _rms(mix, _row(gpm_ref, l))
    h = (_rms(x1, _row(gpf_ref, l)) * (1.0 + m[4:5]) + m[3:4]).astype(BF16)
    act = jax.nn.silu(_dot(h, wg_ref[...])) * _dot(h, wu_ref[...])
    y = _dot(act.astype(BF16), wd_ref[...])
    x2 = x1 + m[5:6] * _rms(y, _row(gqf_ref, l))
    if len(out_refs) == 1:
        out_refs[0][...] = x2
    else:
        i = pl.program_id(0)

        @pl.when(i < P_TILES)
        def _():
            out_refs[0][...] = x2

        @pl.when(i >= P_TILES)
        def _():
            out_refs[1][...] = x2


def _ffn_specs(l):
    vec = _const_spec((DEPTH, D_MODEL))
    return [vec, vec, vec, _layer_spec((D_MODEL, D_FF), l), _layer_spec((D_MODEL, D_FF), l),
            _layer_spec((D_FF, D_MODEL), l)]


def _out_stream(split):
    if split:
        return ([_ctx_spec(D_MODEL), _lat_spec(D_MODEL)],
                [jax.ShapeDtypeStruct((N_P, D_MODEL), F32), jax.ShapeDtypeStruct((N_S, D_MODEL), F32)])
    return [_tok_spec(D_MODEL)], [jax.ShapeDtypeStruct((N_TOK, D_MODEL), F32)]


def _even_post_kernel(l, n_x, n_out, *refs):
    e = l // 2
    x_refs, refs = refs[:n_x], refs[n_x:]
    (mod_ref, yf_ref, yb_ref, u_ref, d_ref, gw_ref, gb_ref, dap_ref, das_ref, wos_ref, wod_ref,
     gpm_ref, gpf_ref, gqf_ref, wg_ref, wu_ref, wd_ref) = refs[:17]
    m = mod_ref[...]
    y = yf_ref[...] + yb_ref[...] + _row(d_ref, e) * u_ref[...]
    g = jax.nn.gelu(y)
    s5 = g * jax.nn.sigmoid(_dot(g.astype(BF16), gw_ref[...]) + _row(gb_ref, e))
    mix = _dot(s5.astype(BF16), wos_ref[...]) + _dot(_read_stream((dap_ref, das_ref)), wod_ref[...])
    _post_tail(l, x_refs, m, mix, gpm_ref, gpf_ref, gqf_ref, wg_ref, wu_ref, wd_ref, refs[17:17 + n_out])


def _even_post(l, xs, mod, y_f, y_b, u_tm, s5_d, glu_w, glu_b, da_p, da_s, w_out, ffn_args, split_out):
    e = l // 2
    tm_spec = pl.BlockSpec((TM, S5_WIDTH), lambda i: (_tm_row_block(i), _tm_col_block(i)))
    out_specs, out_shape = _out_stream(split_out)
    w_half = lambda r: pl.BlockSpec((None, S5_WIDTH, D_MODEL), lambda i: (e, r, 0), pipeline_mode=pl.Buffered(1))
    return pl.pallas_call(
        functools.partial(_even_post_kernel, l, len(xs), len(out_specs)),
        grid=(N_TILES,),
        in_specs=_stream_specs(xs, D_MODEL) + [
            _mod_spec(l),
            tm_spec, tm_spec, tm_spec,
            _const_spec(s5_d.shape), _layer_spec((S5_WIDTH, S5_WIDTH), e), _const_spec(glu_b.shape),
            _ctx_spec(DIFF_WIDTH), _lat_spec(DIFF_WIDTH),
            w_half(0), w_half(1),
        ] + _ffn_specs(l),
        out_specs=out_specs,
        out_shape=out_shape,
        compiler_params=_cparams(("arbitrary",)),
        name="even_post_ffn",
    )(*xs, mod, y_f, y_b, u_tm, s5_d, glu_w, glu_b, da_p, da_s, w_out, w_out, *ffn_args)


def _odd_pre_kernel(l, n_x, *refs):
    o = l // 2
    x_refs, refs = refs[:n_x], refs[n_x:]
    (mod_ref, g_ref, w_ref, gq_ref, gkv_ref, wq_ref, cos_ref, sin_ref,
     q_ref, ckr_ref, ckv_ref, kr_ref) = refs
    m = mod_ref[...]
    h = _rms(_read_stream(x_refs), _row(g_ref, l)) * (1.0 + m[1:2]) + m[0:1]
    proj = _dot(h.astype(BF16), w_ref[...])
    cq = _rms(proj[:, :MLA_Q_RANK], _row(gq_ref, o))
    ckv = _rms(proj[:, MLA_Q_RANK:MLA_Q_RANK + MLA_KV_RANK], _row(gkv_ref, o))
    kr0 = MLA_Q_RANK + MLA_KV_RANK
    krp = proj[:, kr0:kr0 + MLA_HEAD_PAD]
    krp_sw = proj[:, kr0 + MLA_HEAD_PAD:]
    cos, sin = cos_ref[...], sin_ref[...]
    q = _dot(cq.astype(BF16), wq_ref[...]) * ((MLA_NOPE + MLA_ROPE) ** -0.5 * LOG2E)
    for hd in range(MLA_HEADS):
        sl = slice(hd * MLA_HEAD_PAD, (hd + 1) * MLA_HEAD_PAD)
        sw = slice(MLA_Q_W + hd * MLA_HEAD_PAD, MLA_Q_W + (hd + 1) * MLA_HEAD_PAD)
        q_ref[:, sl] = (q[:, sl] * cos + q[:, sw] * sin).astype(BF16)
    ckr_ref[:, :MLA_KV_RANK] = ckv.astype(BF16)
    ckr_ref[:, MLA_KV_RANK:] = (krp * cos + krp_sw * sin).astype(BF16)

    @pl.when(pl.program_id(0) < P_TILES)
    def _():
        ckv_ref[...] = ckv
        kr_ref[...] = krp


def _odd_pre(l, xs, mod, g_pre, w_in, gq, gkv, wq, rope_tabs):
    o = l // 2
    rope_spec = pl.BlockSpec((TM, MLA_HEAD_PAD), lambda i: (_rope_block(i), 0))
    return pl.pallas_call(
        functools.partial(_odd_pre_kernel, l, len(xs)),
        grid=(N_TILES,),
        in_specs=_stream_specs(xs, D_MODEL) + [
            _mod_spec(l),
            _const_spec(g_pre.shape),
            _layer_spec((D_MODEL, MLA_IN_W), o),
            _const_spec(gq.shape), _const_spec(gkv.shape),
            _layer_spec((MLA_Q_RANK, 2 * MLA_Q_W), o),
            rope_spec, rope_spec,
        ],
        out_specs=[_tok_spec(MLA_Q_W), _tok_spec(MLA_CKR_W), _ctx_spec(MLA_KV_RANK), _ctx_spec(MLA_HEAD_PAD)],
        out_shape=[
            jax.ShapeDtypeStruct((N_TOK, MLA_Q_W), BF16),
            jax.ShapeDtypeStruct((N_TOK, MLA_CKR_W), BF16),
            jax.ShapeDtypeStruct((N_P, MLA_KV_RANK), F32),
            jax.ShapeDtypeStruct((N_P, MLA_HEAD_PAD), F32),
        ],
        compiler_params=_cparams(("arbitrary",)),
        name="odd_pre",
    )(*xs, mod, g_pre, w_in, gq, gkv, wq, *rope_tabs)


def _mla_attn_kernel(has_ctx, *refs):
    if has_ctx:
        q_ref, ckr_ref, ckrc_ref, wk_ref, wv_ref, o_ref, kf_ref, vf_ref = refs
    else:
        q_ref, ckr_ref, wk_ref, wv_ref, o_ref, kf_ref, vf_ref = refs
    off = PAST_LEN if has_ctx else 0

    @pl.when(pl.program_id(1) == 0)
    def _():
        srcs = [(0, ckrc_ref)] if has_ctx else []
        srcs.append((off, ckr_ref))
        for r0, src in srcs:
            c = src[...]
            rows = pl.ds(r0, c.shape[0])
            kf_ref[rows, :] = _dot(c, wk_ref[...]).astype(BF16)
            vf_ref[rows, :] = _dot(c[:, :MLA_KV_RANK], wv_ref[...]).astype(BF16)

    low = lax.broadcasted_iota(jnp.int32, (1, LANES), 1) < MLA_V
    for j in range(MLA_HEADS // 2):
        vj = vf_ref[:, j * LANES:(j + 1) * LANES]
        outs = []
        for t in range(2):
            cs = slice((2 * j + t) * MLA_HEAD_PAD, (2 * j + t + 1) * MLA_HEAD_PAD)
            s = _dot_nt(q_ref[:, cs], kf_ref[:, cs])
            mx = jnp.max(s, axis=-1, keepdims=True)
            p = jnp.exp2(s - mx)
            l = jnp.sum(p, axis=-1, keepdims=True)
            outs.append(_dot(p.astype(BF16), vj) * (1.0 / l))
        o_ref[:, j * LANES:(j + 1) * LANES] = jnp.where(low, outs[0], outs[1]).astype(o_ref.dtype)


def _mla_attention(o, q, ckr, ckr_ctx, wk, wv, *, batch, n, row0, tq):
    nq = n // tq
    has_ctx = ckr_ctx is not None
    s_len = n + (PAST_LEN if has_ctx else 0)
    in_specs = [
        pl.BlockSpec((tq, MLA_Q_W), lambda b, i: (row0 // tq + b * nq + i, 0)),
        pl.BlockSpec((n, MLA_CKR_W), lambda b, i: (row0 // n + b, 0)),
    ]
    args = [q, ckr]
    if has_ctx:
        in_specs.append(pl.BlockSpec((None, None, PAST_LEN, MLA_CKR_W), lambda b, i: (b, o, 0, 0)))
        args.append(ckr_ctx)
    in_specs += [_layer_spec(wk.shape[1:], o), _layer_spec(wv.shape[1:], o)]
    args += [wk, wv]
    return pl.pallas_call(
        functools.partial(_mla_attn_kernel, has_ctx),
        grid=(batch, nq),
        in_specs=in_specs,
        out_specs=pl.BlockSpec((tq, MLA_O_W), lambda b, i: (b * nq + i, 0)),
        out_shape=jax.ShapeDtypeStruct((batch * n, MLA_O_W), BF16),
        scratch_shapes=[pltpu.VMEM((s_len, MLA_Q_W), BF16), pltpu.VMEM((s_len, MLA_O_W), BF16)],
        compiler_params=_cparams(("parallel", "arbitrary")),
        name="mla_attn_ctx" if has_ctx else "mla_attn",
    )(*args)


def _odd_post_kernel(l, n_x, n_out, *refs):
    x_refs, refs = refs[:n_x], refs[n_x:]
    mod_ref, ap_ref, as_ref, wo_ref, gpm_ref, gpf_ref, gqf_ref, wg_ref, wu_ref, wd_ref = refs[:10]
    mix = _dot(_read_stream((ap_ref, as_ref)), wo_ref[...])
    _post_tail(l, x_refs, mod_ref[...], mix, gpm_ref, gpf_ref, gqf_ref, wg_ref, wu_ref, wd_ref, refs[10:10 + n_out])


def _odd_post(l, xs, mod, at_p, at_s, w_out, ffn_args, split_out):
    out_specs, out_shape = _out_stream(split_out)
    return pl.pallas_call(
        functools.partial(_odd_post_kernel, l, len(xs), len(out_specs)),
        grid=(N_TILES,),
        in_specs=_stream_specs(xs, D_MODEL) + [
            _mod_spec(l),
            _ctx_spec(MLA_O_W), _lat_spec(MLA_O_W),
            _layer_spec((MLA_O_W, D_MODEL), l // 2),
        ] + _ffn_specs(l),
        out_specs=out_specs,
        out_shape=out_shape,
        compiler_params=_cparams(("arbitrary",)),
        name="odd_post_ffn",
    )(*xs, mod, at_p, at_s, w_out, *ffn_args)


def _rope_angles(rot_dim):
    rows = DEC_SEQ // GRID_W
    row = jnp.repeat(jnp.arange(rows, dtype=F32), GRID_W)
    col = jnp.tile(jnp.arange(GRID_W, dtype=F32), rows)
    n_freq = rot_dim // 4
    inv = ROPE_BASE ** (-jnp.arange(n_freq, dtype=F32) / n_freq)
    ang = jnp.concatenate([row[:, None] * inv, col[:, None] * inv], axis=-1)
    return jnp.cos(ang), jnp.sin(ang)


def _with_identity(cos, *sins):
    one = jnp.ones((TM, cos.shape[1]), F32)
    zero = jnp.zeros((TM, cos.shape[1]), F32)
    return (jnp.concatenate([one, cos]),) + tuple(jnp.concatenate([zero, s]) for s in sins)


def _diff_rope_tables():
    c, s = _rope_angles(DIFF_HEAD_DIM)
    z = jnp.zeros_like(s)
    reps = DIFF_WIDTH // DIFF_HEAD_DIM
    cos = jnp.tile(jnp.concatenate([c, c], axis=1), (1, reps))
    sa = jnp.tile(jnp.concatenate([z, s], axis=1), (1, reps))
    sb = jnp.tile(jnp.concatenate([-s, z], axis=1), (1, reps))
    return _with_identity(cos, sa, sb)


def _mla_rope_tables():
    c, s = _rope_angles(MLA_ROPE)
    n = c.shape[0]
    cos = jnp.concatenate([jnp.ones((n, MLA_NOPE), F32), c, c, jnp.ones((n, MLA_PAD), F32)], axis=1)
    sin = jnp.concatenate([jnp.zeros((n, MLA_NOPE), F32), -s, s, jnp.zeros((n, MLA_PAD), F32)], axis=1)
    return _with_identity(cos, sin)


def _swap_rope_halves(w):
    half = MLA_ROPE // 2
    return jnp.concatenate([w[..., half:], w[..., :half]], axis=-1)


def _mla_weights(w_in_odd, w_q_up, w_kv_up):
    zeros = lambda *s: jnp.zeros(s, F32)
    kr0 = MLA_Q_RANK + MLA_KV_RANK
    w_kr = w_in_odd[:, :, kr0:]
    lead = (N_ODD, D_MODEL)
    w_in = jnp.concatenate(
        [w_in_odd[:, :, :kr0],
         zeros(*lead, MLA_NOPE), w_kr, zeros(*lead, MLA_PAD),
         zeros(*lead, MLA_NOPE), _swap_rope_halves(w_kr), zeros(*lead, MLA_PAD)], axis=-1).astype(BF16)
    wq4 = w_q_up.reshape(N_ODD, MLA_Q_RANK, MLA_HEADS, MLA_NOPE + MLA_ROPE)
    lead = (N_ODD, MLA_Q_RANK, MLA_HEADS)
    wq_main = jnp.concatenate([wq4, zeros(*lead, MLA_PAD)], axis=-1)
    wq_swap = jnp.concatenate([zeros(*lead, MLA_NOPE), _swap_rope_halves(wq4[..., MLA_NOPE:]),
                               zeros(*lead, MLA_PAD)], axis=-1)
    wq = jnp.concatenate([wq_main.reshape(N_ODD, MLA_Q_RANK, MLA_Q_W),
                          wq_swap.reshape(N_ODD, MLA_Q_RANK, MLA_Q_W)], axis=-1).astype(BF16)
    wkv = w_kv_up.reshape(N_ODD, MLA_KV_RANK, MLA_HEADS, MLA_NOPE + MLA_V)
    wk_top = jnp.pad(wkv[..., :MLA_NOPE], ((0, 0), (0, 0), (0, 0), (0, MLA_HEAD_PAD - MLA_NOPE)))
    sel = jnp.pad(jnp.eye(MLA_ROPE, dtype=F32), ((MLA_NOPE, MLA_PAD), (MLA_NOPE, MLA_PAD)))
    wk_bot = jnp.broadcast_to(jnp.tile(sel, (1, MLA_HEADS)), (N_ODD, MLA_HEAD_PAD, MLA_Q_W))
    wk = jnp.concatenate([wk_top.reshape(N_ODD, MLA_KV_RANK, MLA_Q_W), wk_bot], axis=1).astype(BF16)
    wv = wkv[..., MLA_NOPE:].reshape(N_ODD, MLA_KV_RANK, MLA_O_W).astype(BF16)
    return w_in, wq, wk, wv


def kernel(x_prompt, x_sample, state_s5_re, state_s5_im, cache_diff_k, cache_diff_v, cache_mla_ckv, cache_mla_krope, c, c_ctx, w_mod, b_mod, g_pre_mix, g_post_mix, g_pre_ffn, g_post_ffn, w_ffn_gate, w_ffn_up, w_ffn_down, w_in_even, w_out_even, s5_lam_re, s5_lam_im, s5_log_dt, s5_b_re, s5_b_im, s5_c_re, s5_c_im, s5_d, s5_glu_w, s5_glu_b, diff_lam_q1, diff_lam_k1, diff_lam_q2, diff_lam_k2, diff_subln_g, w_in_odd, mla_q_norm_g, mla_w_q_up, mla_kv_norm_g, mla_w_kv_up, w_out_odd):
    xs = (x_prompt.reshape(N_P, D_MODEL), x_sample.reshape(N_S, D_MODEL))
    conds = jnp.concatenate([c_ctx[None, :], c, jnp.zeros((N_COND - 1 - DEC_BATCH, D_MODEL), F32)], axis=0)
    mod = _modulation(conds, w_mod, b_mod).reshape(DEPTH, N_COND, 6, D_MODEL)

    ffn_args = (g_post_mix, g_pre_ffn, g_post_ffn, w_ffn_gate.astype(BF16), w_ffn_up.astype(BF16),
                w_ffn_down.astype(BF16))
    w_in_e = w_in_even.astype(BF16)
    w_out_e = w_out_even.astype(BF16)
    glu_w = s5_glu_w.astype(BF16)
    a_re, a_im, bb_re, bb_im = _s5_params(s5_lam_re, s5_lam_im, s5_log_dt, s5_b_re, s5_b_im)
    a = jnp.stack([a_re[:, 0], a_im[:, 0], a_re[:, 1], a_im[:, 1]], axis=1).reshape(N_EVEN, 4, S5_STATE_W)
    b_in, c_out = _s5_matrices(bb_re, bb_im, s5_c_re, s5_c_im)
    st = lambda s: jnp.moveaxis(s, 0, 2).reshape(N_EVEN, 2, 1, DEC_BATCH, S5_STATE_W)
    h0 = jnp.concatenate([jnp.zeros((N_EVEN, 2, 2, SUBLANES, 2 * S5_STATE_W), F32),
                          jnp.concatenate([st(state_s5_re), st(state_s5_im)], axis=-1)], axis=2)
    diff_ctx = (cache_diff_k.astype(BF16).reshape(DEC_BATCH, N_EVEN, PAST_LEN, DIFF_WIDTH),
                cache_diff_v.astype(BF16).reshape(DEC_BATCH, N_EVEN, PAST_LEN, DIFF_WIDTH))
    lam_vecs = (diff_lam_q1, diff_lam_k1, diff_lam_q2, diff_lam_k2)
    w_in_o, wq, wk, wv = _mla_weights(w_in_odd, mla_w_q_up, mla_w_kv_up)
    w_out_o = w_out_odd.astype(BF16)
    ckr_ctx = jnp.concatenate(
        [cache_mla_ckv, jnp.pad(cache_mla_krope, ((0, 0), (0, 0), (0, 0), (MLA_NOPE, MLA_PAD)))], axis=-1).astype(BF16)
    diff_tabs = _diff_rope_tables()
    mla_tabs = _mla_rope_tables()

    s5_re_list, s5_im_list, dk_list, dv_list, ckv_list, kr_list = [], [], [], [], [], []
    for l in range(DEPTH):
        last = l == DEPTH - 1
        if l % 2 == 0:
            e = l // 2
            lam_init = 0.8 - 0.6 * math.exp(-0.3 * l)
            u_tm, q, k, v, kf, vf = _even_pre(l, xs, mod, g_pre_mix, w_in_e, diff_tabs)
            y_f, y_b, fin_f, fin_b = _s5_scan(e, u_tm, a, b_in, c_out, h0)
            fin = jnp.stack([fin_f[:2].reshape(BATCH, 2, S5_GROUPS, S5_STATE),
                             fin_b[:2].reshape(BATCH, 2, S5_GROUPS, S5_STATE)], axis=1)
            s5_re_list.append(fin[:, :, 0])
            s5_im_list.append(fin[:, :, 1])
            da_p = _diff_attention(e, q, k, v, None, lam_vecs, diff_subln_g, lam_init,
                                   batch=BATCH, n=SEQ, row0=0, tq=SEQ)
            da_s = _diff_attention(e, q, k, v, diff_ctx, lam_vecs, diff_subln_g, lam_init,
                                   batch=DEC_BATCH, n=DEC_SEQ, row0=N_P, tq=256)
            xs = _even_post(l, xs, mod, y_f, y_b, u_tm, s5_d, glu_w, s5_glu_b, da_p, da_s, w_out_e, ffn_args, last)
            dk_list.append(kf.reshape(BATCH, SEQ, DIFF_HEADS, 2, DIFF_HEAD_DIM))
            dv_list.append(vf.reshape(BATCH, SEQ, DIFF_HEADS, 2 * DIFF_HEAD_DIM))
        else:
            o = l // 2
            q, ckr, ckv, krp = _odd_pre(l, xs, mod, g_pre_mix, w_in_o, mla_q_norm_g, mla_kv_norm_g, wq, mla_tabs)
            at_p = _mla_attention(o, q, ckr, None, wk, wv, batch=BATCH, n=SEQ, row0=0, tq=SEQ)
            at_s = _mla_attention(o, q, ckr, ckr_ctx, wk, wv, batch=DEC_BATCH, n=DEC_SEQ, row0=N_P, tq=256)
            xs = _odd_post(l, xs, mod, at_p, at_s, w_out_o, ffn_args, last)
            ckv_list.append(ckv.reshape(BATCH, SEQ, MLA_KV_RANK))
            kr_list.append(krp[:, MLA_NOPE:MLA_NOPE + MLA_ROPE].reshape(BATCH, SEQ, MLA_ROPE))

    return (xs[0].reshape(BATCH, SEQ, D_MODEL), xs[1].reshape(DEC_BATCH, DEC_SEQ, D_MODEL),
            jnp.stack(s5_re_list, axis=1), jnp.stack(s5_im_list, axis=1),
            jnp.stack(dk_list, axis=1), jnp.stack(dv_list, axis=1),
            jnp.stack(ckv_list, axis=1), jnp.stack(kr_list, axis=1))
```

```python
import functools
import math

import jax
import jax.numpy as jnp
from jax import lax
from jax.experimental import pallas as pl
from jax.experimental.pallas import tpu as pltpu

F32 = jnp.float32
BF16 = jnp.bfloat16

D_MODEL = 1024
BATCH = 16
SEQ = 256
DEPTH = 4
DEC_BATCH = 8
DEC_SEQ = 1024
PAST_LEN = 512
GRID_W = 64
N_EVEN = (DEPTH + 1) // 2
N_ODD = DEPTH // 2
EPS = 1e-6
ROPE_BASE = 10000.0
S5_WIDTH = D_MODEL // 2
S5_GROUP = 16
S5_GROUPS = S5_WIDTH // S5_GROUP
S5_STATE = 64
DIFF_HEAD_DIM = 64
DIFF_HEADS = (D_MODEL // 2) // (2 * DIFF_HEAD_DIM)
DIFF_WIDTH = DIFF_HEADS * 2 * DIFF_HEAD_DIM
EVEN_IN = S5_WIDTH + 3 * DIFF_WIDTH
MLA_HEADS = 16
MLA_NOPE = 64
MLA_ROPE = 32
MLA_V = 64
MLA_Q_RANK = 256
MLA_KV_RANK = 128
D_FF = ((8 * D_MODEL // 3 + 255) // 256) * 256

LANES = 128
SUBLANES = 8
VMEM_LIMIT = 56 * 1024 * 1024
LOG2E = math.log2(math.e)

N_P = BATCH * SEQ
N_S = DEC_BATCH * DEC_SEQ
N_TOK = N_P + N_S
TM = 256
TM_ODD = 512
N_TILES = N_TOK // TM
P_TILES = N_P // TM
S_TILES_PER_B = DEC_SEQ // TM
N_COND = 16

S5_STATE_W = S5_GROUPS * S5_STATE
S5_BLK = 4
S5_BLK_W = S5_STATE_W // S5_BLK
SCAN_T = 64
SCAN_R = SCAN_T * SUBLANES
N_TM_ROWS = N_TOK // SUBLANES
N_CHUNKS = N_TM_ROWS // SCAN_T
SEQ_CHUNKS = SEQ // SCAN_T

MLA_HEAD_PAD = LANES
MLA_PAD = MLA_HEAD_PAD - MLA_NOPE - MLA_ROPE
MLA_Q_W = MLA_HEADS * MLA_HEAD_PAD
MLA_IN_W = MLA_Q_RANK + MLA_KV_RANK + 2 * MLA_HEAD_PAD
MLA_CKR_W = MLA_KV_RANK + MLA_HEAD_PAD
MLA_O_W = MLA_HEADS * MLA_V


def _cparams(sem):
    return pltpu.CompilerParams(dimension_semantics=sem, vmem_limit_bytes=VMEM_LIMIT)


def _const_spec(shape):
    nd = len(shape)
    return pl.BlockSpec(shape, lambda *_: (0,) * nd, pipeline_mode=pl.Buffered(1))


def _layer_spec(tail, *lead):
    nt = len(tail)
    return pl.BlockSpec((None,) * len(lead) + tuple(tail), lambda *_: tuple(lead) + (0,) * nt,
                        pipeline_mode=pl.Buffered(1))


def _rms(x, g):
    return x * lax.rsqrt(jnp.mean(x * x, axis=-1, keepdims=True) + EPS) * g


def _dot(a, b):
    return jnp.dot(a, b, preferred_element_type=F32)


def _dot_nt(a, b):
    return lax.dot_general(a, b, (((1,), (1,)), ((), ())), preferred_element_type=F32)


def _cond_of_tile(i, tm=TM):
    return jnp.where(i < N_P // tm, 0, 1 + (i - N_P // tm) // (DEC_SEQ // tm))


def _tm_row_block(i):
    return jnp.where(i < P_TILES, i // SUBLANES, 2 + (i - P_TILES) % S_TILES_PER_B)


def _tm_col_block(i):
    return jnp.where(i < P_TILES, i % SUBLANES, (i - P_TILES) // S_TILES_PER_B)


def _rope_block(i, tm=TM):
    return jnp.where(i < N_P // tm, 0, 1 + (i - N_P // tm) % (DEC_SEQ // tm))


def _tok_spec(width, tm=TM):
    return pl.BlockSpec((tm, width), lambda i: (i, 0))


def _ctx_spec(width, tm=TM):
    return pl.BlockSpec((tm, width), lambda i: (jnp.minimum(i, N_P // tm - 1), 0))


def _lat_spec(width, tm=TM):
    return pl.BlockSpec((tm, width), lambda i: (jnp.maximum(i - N_P // tm, 0), 0))


def _stream_specs(arrays, width, tm=TM):
    return [_tok_spec(width, tm)] if len(arrays) == 1 else [_ctx_spec(width, tm), _lat_spec(width, tm)]


def _read_stream(refs, tm=TM):
    if len(refs) == 1:
        return refs[0][...]
    return jnp.where(pl.program_id(0) < N_P // tm, refs[0][...], refs[1][...])


def _mod_spec(l, tm=TM):
    return pl.BlockSpec((None, None, 6, D_MODEL), lambda i: (l, _cond_of_tile(i, tm), 0, 0))


def _row(ref, r):
    return ref[r:r + 1, :]


def _mod_kernel(c_ref, w_ref, b_ref, o_ref):
    s = jax.nn.silu(c_ref[...])
    o_ref[0] = _dot(s.astype(BF16), w_ref[0].astype(BF16)) + b_ref[0]


def _modulation(conds, w_mod, b_mod):
    tn = 1536
    return pl.pallas_call(
        _mod_kernel,
        grid=(DEPTH, 6 * D_MODEL // tn),
        in_specs=[
            pl.BlockSpec((N_COND, D_MODEL), lambda l, n: (0, 0)),
            pl.BlockSpec((1, D_MODEL, tn), lambda l, n: (l, 0, n)),
            pl.BlockSpec((1, 1, tn), lambda l, n: (l, 0, n)),
        ],
        out_specs=pl.BlockSpec((1, N_COND, tn), lambda l, n: (l, 0, n)),
        out_shape=jax.ShapeDtypeStruct((DEPTH, N_COND, 6 * D_MODEL), F32),
        compiler_params=_cparams(("parallel", "parallel")),
        name="modulation",
    )(conds, w_mod, b_mod.reshape(DEPTH, 1, 6 * D_MODEL))


def _s5_param_kernel(lr_ref, li_ref, ldt_ref, br_ref, bi_ref, are_ref, aim_ref, bbr_ref, bbi_ref):
    lr, li = lr_ref[...], li_ref[...]
    dt = jnp.exp(ldt_ref[...])
    mag = jnp.exp(lr * dt)
    a_re, a_im = mag * jnp.cos(li * dt), mag * jnp.sin(li * dt)
    den = lr * lr + li * li
    f_re = ((a_re - 1.0) * lr + a_im * li) / den
    f_im = (a_im * lr - (a_re - 1.0) * li) / den
    br, bi = br_ref[...], bi_ref[...]
    are_ref[...] = a_re
    aim_ref[...] = a_im
    bbr_ref[...] = f_re * br - f_im * bi
    bbi_ref[...] = f_re * bi + f_im * br


def _s5_params(lam_re, lam_im, log_dt, b_re, b_im):
    full = (N_EVEN, 2, S5_GROUPS, S5_GROUP, S5_STATE)
    flat = (math.prod(full) // LANES, LANES)
    ex = lambda a: jnp.broadcast_to(a[:, :, :, None, :], full).reshape(flat)
    ldt = jnp.broadcast_to(log_dt[:, :, :, None, None], full).reshape(flat)
    bt = lambda b: jnp.swapaxes(b, -1, -2).reshape(flat)
    spec = pl.BlockSpec(flat, lambda: (0, 0))
    outs = pl.pallas_call(
        _s5_param_kernel,
        in_specs=[spec] * 5,
        out_specs=[spec] * 4,
        out_shape=[jax.ShapeDtypeStruct(flat, F32)] * 4,
        name="s5_params",
    )(ex(lam_re), ex(lam_im), ldt, bt(b_re), bt(b_im))
    a_re, a_im, bb_re, bb_im = (o.reshape(full) for o in outs)
    return a_re[:, :, :, 0, :], a_im[:, :, :, 0, :], bb_re, bb_im


def _block_diag(m):
    eye = jnp.eye(SUBLANES, dtype=m.dtype)
    out = m[..., :, :, None, :] * eye[:, None, :, None]
    return out.reshape(m.shape[:-3] + (SUBLANES * m.shape[-2], SUBLANES * m.shape[-1]))


def _s5_matrices(bb_re, bb_im, c_re, c_im):
    gb = S5_GROUPS // S5_BLK
    shp = (N_EVEN, 2, S5_BLK, gb, S5_GROUP, S5_STATE)
    b_in = jnp.concatenate([_block_diag(bb_re.reshape(shp)), _block_diag(bb_im.reshape(shp))], axis=-1)
    ct = lambda c: jnp.swapaxes(c.reshape(shp), -1, -2)
    c_out = jnp.concatenate([_block_diag(ct(c_re)), -_block_diag(ct(c_im))], axis=-2)
    return b_in.astype(BF16), c_out.astype(BF16)


def _seq_of_chunk(c):
    return (c >= SEQ_CHUNKS).astype(jnp.int32) + (c >= 2 * SEQ_CHUNKS).astype(jnp.int32)


def _s5_scan_kernel(uf_ref, ub_ref, a_ref, bin_ref, cout_ref, h0f_ref, h0b_ref,
                    yf_ref, yb_ref, finf_ref, finb_ref, us_ref, hs_ref, ys_ref, st_ref):
    j = pl.program_id(0)
    cf = j
    cb = N_CHUNKS - 1 - j

    @pl.when((cf == 0) | (cf == SEQ_CHUNKS) | (cf == 2 * SEQ_CHUNKS))
    def _():
        st_ref[0] = h0f_ref[...]

    @pl.when((cb == N_CHUNKS - 1) | (cb == 2 * SEQ_CHUNKS - 1) | (cb == SEQ_CHUNKS - 1))
    def _():
        st_ref[1] = h0b_ref[...]

    dirs = ((0, uf_ref, yf_ref), (1, ub_ref, yb_ref))
    for k in range(S5_BLK):
        lo, hi = k * S5_BLK_W, (k + 1) * S5_BLK_W
        for d, u_ref, _ in dirs:
            for b in range(SUBLANES):
                c0 = b * S5_WIDTH + k * LANES
                us_ref[d, k, pl.ds(b, SCAN_T, stride=SUBLANES), :] = u_ref[:, c0:c0 + LANES]
            hs_ref[d, k] = _dot(us_ref[d, k].astype(BF16), bin_ref[d, k])

        coef = [jnp.broadcast_to(a_ref[r:r + 1, lo:hi], (SUBLANES, S5_BLK_W)) for r in range(4)]
        state = [st_ref[0, :, lo:hi], st_ref[0, :, S5_STATE_W + lo:S5_STATE_W + hi],
                 st_ref[1, :, lo:hi], st_ref[1, :, S5_STATE_W + lo:S5_STATE_W + hi]]
        for t in range(SCAN_T):
            for d in range(2):
                rows = pl.ds((t if d == 0 else SCAN_T - 1 - t) * SUBLANES, SUBLANES)
                hr, hi_ = state[2 * d], state[2 * d + 1]
                ar, ai = coef[2 * d], coef[2 * d + 1]
                nr = ar * hr - ai * hi_ + hs_ref[d, k, rows, 0:S5_BLK_W]
                ni = ar * hi_ + ai * hr + hs_ref[d, k, rows, S5_BLK_W:2 * S5_BLK_W]
                hs_ref[d, k, rows, 0:S5_BLK_W] = nr
                hs_ref[d, k, rows, S5_BLK_W:2 * S5_BLK_W] = ni
                state[2 * d], state[2 * d + 1] = nr, ni
        st_ref[0, :, lo:hi] = state[0]
        st_ref[0, :, S5_STATE_W + lo:S5_STATE_W + hi] = state[1]
        st_ref[1, :, lo:hi] = state[2]
        st_ref[1, :, S5_STATE_W + lo:S5_STATE_W + hi] = state[3]

        for d, _, y_ref in dirs:
            ys_ref[d, k] = _dot(hs_ref[d, k].astype(BF16), cout_ref[d, k])
            for b in range(SUBLANES):
                c0 = b * S5_WIDTH + k * LANES
                y_ref[:, c0:c0 + LANES] = ys_ref[d, k, pl.ds(b, SCAN_T, stride=SUBLANES), :]

    finf_ref[...] = st_ref[0]
    finb_ref[...] = st_ref[1]


def _s5_scan(e, u_tm, a, b_in, c_out, h0):
    rev = lambda j: N_CHUNKS - 1 - j
    width = SUBLANES * S5_WIDTH
    st_w = 2 * S5_STATE_W
    row_f = pl.BlockSpec((SCAN_T, width), lambda j: (j, 0))
    row_b = pl.BlockSpec((SCAN_T, width), lambda j: (rev(j), 0))
    h0_f = pl.BlockSpec((None, None, None, SUBLANES, st_w), lambda j: (e, 0, _seq_of_chunk(j), 0, 0))
    h0_b = pl.BlockSpec((None, None, None, SUBLANES, st_w), lambda j: (e, 1, _seq_of_chunk(rev(j)), 0, 0))
    st_f = pl.BlockSpec((None, SUBLANES, st_w), lambda j: (_seq_of_chunk(j), 0, 0))
    st_b = pl.BlockSpec((None, SUBLANES, st_w), lambda j: (_seq_of_chunk(rev(j)), 0, 0))
    st_shape = jax.ShapeDtypeStruct((3, SUBLANES, st_w), F32)
    return pl.pallas_call(
        _s5_scan_kernel,
        grid=(N_CHUNKS,),
        in_specs=[row_f, row_b, _layer_spec(a.shape[1:], e), _layer_spec(b_in.shape[1:], e),
                  _layer_spec(c_out.shape[1:], e), h0_f, h0_b],
        out_specs=[row_f, row_b, st_f, st_b],
        out_shape=[jax.ShapeDtypeStruct((N_TM_ROWS, width), F32)] * 2 + [st_shape] * 2,
        scratch_shapes=[pltpu.VMEM((2, S5_BLK, SCAN_R, LANES), F32),
                        pltpu.VMEM((2, S5_BLK, SCAN_R, 2 * S5_BLK_W), F32),
                        pltpu.VMEM((2, S5_BLK, SCAN_R, LANES), F32),
                        pltpu.VMEM((2, SUBLANES, st_w), F32)],
        compiler_params=_cparams(("arbitrary",)),
        name="s5_scan",
    )(u_tm, u_tm, a, b_in, c_out, h0, h0)


def _even_pre_kernel(l, n_x, *refs):
    x_refs, refs = refs[:n_x], refs[n_x:]
    mod_ref, g_ref, w_ref, cos_ref, sa_ref, sb_ref, u_ref, q_ref, k_ref, v_ref, kf_ref, vf_ref = refs
    m = mod_ref[...]
    h = _rms(_read_stream(x_refs), _row(g_ref, l)) * (1.0 + m[1:2]) + m[0:1]
    proj = _dot(h.astype(BF16), w_ref[...])
    cos, sa, sb = cos_ref[...], sa_ref[...], sb_ref[...]
    half = DIFF_HEAD_DIM // 2

    def rope(z):
        return z * cos + pltpu.roll(z, half, 1) * sa + pltpu.roll(z, DIFF_WIDTH - half, 1) * sb

    u_ref[...] = proj[:, :S5_WIDTH]
    k = rope(proj[:, S5_WIDTH + DIFF_WIDTH:S5_WIDTH + 2 * DIFF_WIDTH])
    v = proj[:, S5_WIDTH + 2 * DIFF_WIDTH:]
    q_ref[...] = (rope(proj[:, S5_WIDTH:S5_WIDTH + DIFF_WIDTH]) * (DIFF_HEAD_DIM ** -0.5 * LOG2E)).astype(BF16)
    k_ref[...] = k.astype(BF16)
    v_ref[...] = v.astype(BF16)

    @pl.when(pl.program_id(0) < P_TILES)
    def _():
        kf_ref[...] = k
        vf_ref[...] = v


def _even_pre(l, xs, mod, g_pre, w_in, rope_tabs):
    rope_spec = pl.BlockSpec((TM, DIFF_WIDTH), lambda i: (_rope_block(i), 0))
    return pl.pallas_call(
        functools.partial(_even_pre_kernel, l, len(xs)),
        grid=(N_TILES,),
        in_specs=_stream_specs(xs, D_MODEL) + [
            _mod_spec(l),
            _const_spec(g_pre.shape),
            _layer_spec((D_MODEL, EVEN_IN), l // 2),
            rope_spec, rope_spec, rope_spec,
        ],
        out_specs=[
            pl.BlockSpec((TM, S5_WIDTH), lambda i: (_tm_row_block(i), _tm_col_block(i))),
            _tok_spec(DIFF_WIDTH), _tok_spec(DIFF_WIDTH), _tok_spec(DIFF_WIDTH),
            _ctx_spec(DIFF_WIDTH), _ctx_spec(DIFF_WIDTH),
        ],
        out_shape=[jax.ShapeDtypeStruct((N_TM_ROWS, SUBLANES * S5_WIDTH), F32)]
        + [jax.ShapeDtypeStruct((N_TOK, DIFF_WIDTH), BF16)] * 3
        + [jax.ShapeDtypeStruct((N_P, DIFF_WIDTH), F32)] * 2,
        compiler_params=_cparams(("arbitrary",)),
        name="even_pre",
    )(*xs, mod, g_pre, w_in, *rope_tabs)


def _softmax2_parts(parts):
    m = functools.reduce(jnp.maximum, [jnp.max(s, axis=-1, keepdims=True) for s in parts])
    ps = [jnp.exp2(s - m) for s in parts]
    l = functools.reduce(jnp.add, [jnp.sum(p, axis=-1, keepdims=True) for p in ps])
    return ps, l


def _diff_attn_kernel(e, lam_init, has_ctx, *refs):
    if has_ctx:
        q_ref, k_ref, v_ref, kc_ref, vc_ref, lq1, lk1, lq2, lk2, g_ref, o_ref = refs
    else:
        q_ref, k_ref, v_ref, lq1, lk1, lq2, lk2, g_ref, o_ref = refs
    lam = (jnp.exp(jnp.sum(_row(lq1, e) * _row(lk1, e), axis=-1, keepdims=True))
           - jnp.exp(jnp.sum(_row(lq2, e) * _row(lk2, e), axis=-1, keepdims=True)) + lam_init)
    first = lax.broadcasted_iota(jnp.int32, (1, 2 * DIFF_HEAD_DIM), 1) < DIFF_HEAD_DIM
    g = _row(g_ref, e)
    for h in range(DIFF_HEADS):
        sl = slice(h * 2 * DIFF_HEAD_DIM, (h + 1) * 2 * DIFF_HEAD_DIM)
        qh = q_ref[:, sl]
        zero = jnp.zeros_like(qh)
        q1 = jnp.where(first, qh, zero)
        q2 = jnp.where(first, zero, qh)
        keys = [k_ref[:, sl]]
        vals = [v_ref[:, sl]]
        if has_ctx:
            keys.insert(0, kc_ref[:, sl])
            vals.insert(0, vc_ref[:, sl])
        p1, l1 = _softmax2_parts([_dot_nt(q1, kk) for kk in keys])
        p2, l2 = _softmax2_parts([_dot_nt(q2, kk) for kk in keys])
        ratio = lam * l1 / l2
        o = functools.reduce(jnp.add, [_dot((a - b * ratio).astype(BF16), vv) for a, b, vv in zip(p1, p2, vals)])
        o = o * (1.0 / l1)
        o_ref[:, sl] = (_rms(o, g) * (1.0 - lam_init)).astype(o_ref.dtype)


def _diff_attention(e, q, k, v, ctx, lam_vecs, subln_g, lam_init, *, batch, n, row0, tq):
    nq = n // tq
    has_ctx = ctx is not None
    in_specs = [
        pl.BlockSpec((tq, DIFF_WIDTH), lambda b, i: (row0 // tq + b * nq + i, 0)),
        pl.BlockSpec((n, DIFF_WIDTH), lambda b, i: (row0 // n + b, 0)),
        pl.BlockSpec((n, DIFF_WIDTH), lambda b, i: (row0 // n + b, 0)),
    ]
    args = [q, k, v]
    if has_ctx:
        in_specs += [pl.BlockSpec((None, None, PAST_LEN, DIFF_WIDTH), lambda b, i: (b, e, 0, 0))] * 2
        args += list(ctx)
    in_specs += [_const_spec(t.shape) for t in lam_vecs] + [_const_spec(subln_g.shape)]
    args += list(lam_vecs) + [subln_g]
    return pl.pallas_call(
        functools.partial(_diff_attn_kernel, e, lam_init, has_ctx),
        grid=(batch, nq),
        in_specs=in_specs,
        out_specs=pl.BlockSpec((tq, DIFF_WIDTH), lambda b, i: (b * nq + i, 0)),
        out_shape=jax.ShapeDtypeStruct((batch * n, DIFF_WIDTH), BF16),
        compiler_params=_cparams(("parallel", "parallel")),
        name="diff_attn_ctx" if has_ctx else "diff_attn",
    )(*args)


def _post_tail(l, x, m, mix, gpm_ref, gpf_ref, gqf_ref, wg_ref, wu_ref, wd_ref):
    x1 = x + m[2:3] * _rms(mix, _row(gpm_ref, l))
    h = (_rms(x1, _row(gpf_ref, l)) * (1.0 + m[4:5]) + m[3:4]).astype(BF16)
    act = jax.nn.silu(_dot(h, wg_ref[...])) * _dot(h, wu_ref[...])
    y = _dot(act.astype(BF16), wd_ref[...])
    return x1 + m[5:6] * _rms(y, _row(gqf_ref, l))


def _write_stream(out_refs, rows, x2, tm=TM):
    if len(out_refs) == 1:
        out_refs[0][rows, :] = x2
    else:
        i = pl.program_id(0)

        @pl.when(i < N_P // tm)
        def _():
            out_refs[0][rows, :] = x2

        @pl.when(i >= N_P // tm)
        def _():
            out_refs[1][rows, :] = x2


def _ffn_specs(l):
    vec = _const_spec((DEPTH, D_MODEL))
    return [vec, vec, vec, _layer_spec((D_MODEL, D_FF), l), _layer_spec((D_MODEL, D_FF), l),
            _layer_spec((D_FF, D_MODEL), l)]


def _out_stream(split, tm=TM):
    if split:
        return ([_ctx_spec(D_MODEL, tm), _lat_spec(D_MODEL, tm)],
                [jax.ShapeDtypeStruct((N_P, D_MODEL), F32), jax.ShapeDtypeStruct((N_S, D_MODEL), F32)])
    return [_tok_spec(D_MODEL, tm)], [jax.ShapeDtypeStruct((N_TOK, D_MODEL), F32)]


def _even_post_kernel(l, n_x, n_out, *refs):
    e = l // 2
    x_refs, refs = refs[:n_x], refs[n_x:]
    (mod_ref, yf_ref, yb_ref, u_ref, d_ref, gw_ref, gb_ref, dap_ref, das_ref, wos_ref, wod_ref,
     gpm_ref, gpf_ref, gqf_ref, wg_ref, wu_ref, wd_ref) = refs[:17]
    m = mod_ref[...]
    y = yf_ref[...] + yb_ref[...] + _row(d_ref, e) * u_ref[...]
    g = jax.nn.gelu(y)
    s5 = g * jax.nn.sigmoid(_dot(g.astype(BF16), gw_ref[...]) + _row(gb_ref, e))
    mix = _dot(s5.astype(BF16), wos_ref[...]) + _dot(_read_stream((dap_ref, das_ref)), wod_ref[...])
    x2 = _post_tail(l, _read_stream(x_refs), m, mix, gpm_ref, gpf_ref, gqf_ref, wg_ref, wu_ref, wd_ref)
    _write_stream(refs[17:17 + n_out], slice(None), x2)


def _even_post(l, xs, mod, y_f, y_b, u_tm, s5_d, glu_w, glu_b, da_p, da_s, w_out, ffn_args, split_out):
    e = l // 2
    tm_spec = pl.BlockSpec((TM, S5_WIDTH), lambda i: (_tm_row_block(i), _tm_col_block(i)))
    out_specs, out_shape = _out_stream(split_out)
    w_half = lambda r: pl.BlockSpec((None, S5_WIDTH, D_MODEL), lambda i: (e, r, 0), pipeline_mode=pl.Buffered(1))
    return pl.pallas_call(
        functools.partial(_even_post_kernel, l, len(xs), len(out_specs)),
        grid=(N_TILES,),
        in_specs=_stream_specs(xs, D_MODEL) + [
            _mod_spec(l),
            tm_spec, tm_spec, tm_spec,
            _const_spec(s5_d.shape), _layer_spec((S5_WIDTH, S5_WIDTH), e), _const_spec(glu_b.shape),
            _ctx_spec(DIFF_WIDTH), _lat_spec(DIFF_WIDTH),
            w_half(0), w_half(1),
        ] + _ffn_specs(l),
        out_specs=out_specs,
        out_shape=out_shape,
        compiler_params=_cparams(("arbitrary",)),
        name="even_post_ffn",
    )(*xs, mod, y_f, y_b, u_tm, s5_d, glu_w, glu_b, da_p, da_s, w_out, w_out, *ffn_args)


def _odd_pre_kernel(l, n_x, *refs):
    o = l // 2
    x_refs, refs = refs[:n_x], refs[n_x:]
    (mod_ref, g_ref, w_ref, gq_ref, gkv_ref, wq_ref, cos_ref, sin_ref,
     q_ref, ckr_ref, ckv_ref, kr_ref) = refs
    m = mod_ref[...]
    h = _rms(_read_stream(x_refs, TM_ODD), _row(g_ref, l)) * (1.0 + m[1:2]) + m[0:1]
    proj = _dot(h.astype(BF16), w_ref[...])
    cq = _rms(proj[:, :MLA_Q_RANK], _row(gq_ref, o))
    ckv = _rms(proj[:, MLA_Q_RANK:MLA_Q_RANK + MLA_KV_RANK], _row(gkv_ref, o))
    kr0 = MLA_Q_RANK + MLA_KV_RANK
    krp = proj[:, kr0:kr0 + MLA_HEAD_PAD]
    krp_sw = proj[:, kr0 + MLA_HEAD_PAD:]
    cos, sin = cos_ref[...], sin_ref[...]
    q = _dot(cq.astype(BF16), wq_ref[...]) * ((MLA_NOPE + MLA_ROPE) ** -0.5 * LOG2E)
    for hd in range(MLA_HEADS):
        sl = slice(hd * MLA_HEAD_PAD, (hd + 1) * MLA_HEAD_PAD)
        sw = slice(MLA_Q_W + hd * MLA_HEAD_PAD, MLA_Q_W + (hd + 1) * MLA_HEAD_PAD)
        q_ref[:, sl] = (q[:, sl] * cos + q[:, sw] * sin).astype(BF16)
    ckr_ref[:, :MLA_KV_RANK] = ckv.astype(BF16)
    ckr_ref[:, MLA_KV_RANK:] = (krp * cos + krp_sw * sin).astype(BF16)

    @pl.when(pl.program_id(0) < N_P // TM_ODD)
    def _():
        ckv_ref[...] = ckv
        kr_ref[...] = krp


def _odd_pre(l, xs, mod, g_pre, w_in, gq, gkv, wq, rope_tabs):
    o = l // 2
    tm = TM_ODD
    rope_spec = pl.BlockSpec((tm, MLA_HEAD_PAD), lambda i: (_rope_block(i, tm), 0))
    return pl.pallas_call(
        functools.partial(_odd_pre_kernel, l, len(xs)),
        grid=(N_TOK // tm,),
        in_specs=_stream_specs(xs, D_MODEL, tm) + [
            _mod_spec(l, tm),
            _const_spec(g_pre.shape),
            _layer_spec((D_MODEL, MLA_IN_W), o),
            _const_spec(gq.shape), _const_spec(gkv.shape),
            _layer_spec((MLA_Q_RANK, 2 * MLA_Q_W), o),
            rope_spec, rope_spec,
        ],
        out_specs=[_tok_spec(MLA_Q_W, tm), _tok_spec(MLA_CKR_W, tm), _ctx_spec(MLA_KV_RANK, tm),
                   _ctx_spec(MLA_HEAD_PAD, tm)],
        out_shape=[
            jax.ShapeDtypeStruct((N_TOK, MLA_Q_W), BF16),
            jax.ShapeDtypeStruct((N_TOK, MLA_CKR_W), BF16),
            jax.ShapeDtypeStruct((N_P, MLA_KV_RANK), F32),
            jax.ShapeDtypeStruct((N_P, MLA_HEAD_PAD), F32),
        ],
        compiler_params=_cparams(("arbitrary",)),
        name="odd_pre",
    )(*xs, mod, g_pre, w_in, gq, gkv, wq, *rope_tabs)


def _mla_attn_kernel(has_ctx, *refs):
    if has_ctx:
        q_ref, ckr_ref, ckrc_ref, wk_ref, wv_ref, o_ref, kf_ref, vf_ref = refs
    else:
        q_ref, ckr_ref, wk_ref, wv_ref, o_ref, kf_ref, vf_ref = refs
    off = PAST_LEN if has_ctx else 0

    @pl.when(pl.program_id(1) == 0)
    def _():
        srcs = [(0, ckrc_ref)] if has_ctx else []
        srcs.append((off, ckr_ref))
        for r0, src in srcs:
            c = src[...]
            rows = pl.ds(r0, c.shape[0])
            kf_ref[rows, :] = _dot(c, wk_ref[...]).astype(BF16)
            vf_ref[rows, :] = _dot(c[:, :MLA_KV_RANK], wv_ref[...]).astype(BF16)

    low = lax.broadcasted_iota(jnp.int32, (1, LANES), 1) < MLA_V
    for j in range(MLA_HEADS // 2):
        ps, rl = [], []
        for t in range(2):
            cs = slice((2 * j + t) * MLA_HEAD_PAD, (2 * j + t + 1) * MLA_HEAD_PAD)
            s = _dot_nt(q_ref[:, cs], kf_ref[:, cs])
            mx = jnp.max(s, axis=-1, keepdims=True)
            p = jnp.exp2(s - mx)
            rl.append(1.0 / jnp.sum(p, axis=-1, keepdims=True))
            ps.append(p.astype(BF16))
        o = (_dot(ps[0], vf_ref[:, j * LANES:(j + 1) * LANES])
             + _dot(ps[1], vf_ref[:, MLA_O_W + j * LANES:MLA_O_W + (j + 1) * LANES]))
        o_ref[:, j * LANES:(j + 1) * LANES] = (o * jnp.where(low, rl[0], rl[1])).astype(o_ref.dtype)


def _mla_attention(o, q, ckr, ckr_ctx, wk, wv, *, batch, n, row0, tq):
    nq = n // tq
    has_ctx = ckr_ctx is not None
    s_len = n + (PAST_LEN if has_ctx else 0)
    in_specs = [
        pl.BlockSpec((tq, MLA_Q_W), lambda b, i: (row0 // tq + b * nq + i, 0)),
        pl.BlockSpec((n, MLA_CKR_W), lambda b, i: (row0 // n + b, 0)),
    ]
    args = [q, ckr]
    if has_ctx:
        in_specs.append(pl.BlockSpec((None, None, PAST_LEN, MLA_CKR_W), lambda b, i: (b, o, 0, 0)))
        args.append(ckr_ctx)
    in_specs += [_layer_spec(wk.shape[1:], o), _layer_spec(wv.shape[1:], o)]
    args += [wk, wv]
    return pl.pallas_call(
        functools.partial(_mla_attn_kernel, has_ctx),
        grid=(batch, nq),
        in_specs=in_specs,
        out_specs=pl.BlockSpec((tq, MLA_O_W), lambda b, i: (b * nq + i, 0)),
        out_shape=jax.ShapeDtypeStruct((batch * n, MLA_O_W), BF16),
        scratch_shapes=[pltpu.VMEM((s_len, MLA_Q_W), BF16), pltpu.VMEM((s_len, 2 * MLA_O_W), BF16)],
        compiler_params=_cparams(("parallel", "arbitrary")),
        name="mla_attn_ctx" if has_ctx else "mla_attn",
    )(*args)


def _odd_post_kernel(l, n_x, n_out, *refs):
    x_refs, refs = refs[:n_x], refs[n_x:]
    mod_ref, ap_ref, as_ref, wo_ref, gpm_ref, gpf_ref, gqf_ref, wg_ref, wu_ref, wd_ref = refs[:10]
    mix = _dot(_read_stream((ap_ref, as_ref), TM_ODD), wo_ref[...])
    x2 = _post_tail(l, _read_stream(x_refs, TM_ODD), mod_ref[...], mix, gpm_ref, gpf_ref, gqf_ref,
                    wg_ref, wu_ref, wd_ref)
    _write_stream(refs[10:10 + n_out], slice(None), x2, TM_ODD)


def _odd_post(l, xs, mod, at_p, at_s, w_out, ffn_args, split_out):
    tm = TM_ODD
    out_specs, out_shape = _out_stream(split_out, tm)
    return pl.pallas_call(
        functools.partial(_odd_post_kernel, l, len(xs), len(out_specs)),
        grid=(N_TOK // tm,),
        in_specs=_stream_specs(xs, D_MODEL, tm) + [
            _mod_spec(l, tm),
            _ctx_spec(MLA_O_W, tm), _lat_spec(MLA_O_W, tm),
            _layer_spec((MLA_O_W, D_MODEL), l // 2),
        ] + _ffn_specs(l),
        out_specs=out_specs,
        out_shape=out_shape,
        compiler_params=_cparams(("arbitrary",)),
        name="odd_post_ffn",
    )(*xs, mod, at_p, at_s, w_out, *ffn_args)


def _rope_angles(rot_dim):
    rows = DEC_SEQ // GRID_W
    row = jnp.repeat(jnp.arange(rows, dtype=F32), GRID_W)
    col = jnp.tile(jnp.arange(GRID_W, dtype=F32), rows)
    n_freq = rot_dim // 4
    inv = ROPE_BASE ** (-jnp.arange(n_freq, dtype=F32) / n_freq)
    ang = jnp.concatenate([row[:, None] * inv, col[:, None] * inv], axis=-1)
    return jnp.cos(ang), jnp.sin(ang)


def _with_identity(tm, cos, *sins):
    one = jnp.ones((tm, cos.shape[1]), F32)
    zero = jnp.zeros((tm, cos.shape[1]), F32)
    return (jnp.concatenate([one, cos]),) + tuple(jnp.concatenate([zero, s]) for s in sins)


def _diff_rope_tables():
    c, s = _rope_angles(DIFF_HEAD_DIM)
    z = jnp.zeros_like(s)
    reps = DIFF_WIDTH // DIFF_HEAD_DIM
    cos = jnp.tile(jnp.concatenate([c, c], axis=1), (1, reps))
    sa = jnp.tile(jnp.concatenate([z, s], axis=1), (1, reps))
    sb = jnp.tile(jnp.concatenate([-s, z], axis=1), (1, reps))
    return _with_identity(TM, cos, sa, sb)


def _mla_rope_tables():
    c, s = _rope_angles(MLA_ROPE)
    n = c.shape[0]
    cos = jnp.concatenate([jnp.ones((n, MLA_NOPE), F32), c, c, jnp.ones((n, MLA_PAD), F32)], axis=1)
    sin = jnp.concatenate([jnp.zeros((n, MLA_NOPE), F32), -s, s, jnp.zeros((n, MLA_PAD), F32)], axis=1)
    return _with_identity(TM_ODD, cos, sin)


def _swap_rope_halves(w):
    half = MLA_ROPE // 2
    return jnp.concatenate([w[..., half:], w[..., :half]], axis=-1)


def _mla_weights(w_in_odd, w_q_up, w_kv_up):
    zeros = lambda *s: jnp.zeros(s, F32)
    kr0 = MLA_Q_RANK + MLA_KV_RANK
    w_kr = w_in_odd[:, :, kr0:]
    lead = (N_ODD, D_MODEL)
    w_in = jnp.concatenate(
        [w_in_odd[:, :, :kr0],
         zeros(*lead, MLA_NOPE), w_kr, zeros(*lead, MLA_PAD),
         zeros(*lead, MLA_NOPE), _swap_rope_halves(w_kr), zeros(*lead, MLA_PAD)], axis=-1).astype(BF16)
    wq4 = w_q_up.reshape(N_ODD, MLA_Q_RANK, MLA_HEADS, MLA_NOPE + MLA_ROPE)
    lead = (N_ODD, MLA_Q_RANK, MLA_HEADS)
    wq_main = jnp.concatenate([wq4, zeros(*lead, MLA_PAD)], axis=-1)
    wq_swap = jnp.concatenate([zeros(*lead, MLA_NOPE), _swap_rope_halves(wq4[..., MLA_NOPE:]),
                               zeros(*lead, MLA_PAD)], axis=-1)
    wq = jnp.concatenate([wq_main.reshape(N_ODD, MLA_Q_RANK, MLA_Q_W),
                          wq_swap.reshape(N_ODD, MLA_Q_RANK, MLA_Q_W)], axis=-1).astype(BF16)
    wkv = w_kv_up.reshape(N_ODD, MLA_KV_RANK, MLA_HEADS, MLA_NOPE + MLA_V)
    wk_top = jnp.pad(wkv[..., :MLA_NOPE], ((0, 0), (0, 0), (0, 0), (0, MLA_HEAD_PAD - MLA_NOPE)))
    sel = jnp.pad(jnp.eye(MLA_ROPE, dtype=F32), ((MLA_NOPE, MLA_PAD), (MLA_NOPE, MLA_PAD)))
    wk_bot = jnp.broadcast_to(jnp.tile(sel, (1, MLA_HEADS)), (N_ODD, MLA_HEAD_PAD, MLA_Q_W))
    wk = jnp.concatenate([wk_top.reshape(N_ODD, MLA_KV_RANK, MLA_Q_W), wk_bot], axis=1).astype(BF16)
    even = (jnp.arange(MLA_HEADS) % 2 == 0)[:, None]
    wv4 = wkv[..., MLA_NOPE:]
    wv = jnp.concatenate([jnp.where(even, wv4, 0.0).reshape(N_ODD, MLA_KV_RANK, MLA_O_W),
                          jnp.where(even, 0.0, wv4).reshape(N_ODD, MLA_KV_RANK, MLA_O_W)], axis=-1).astype(BF16)
    return w_in, wq, wk, wv


def kernel(x_prompt, x_sample, state_s5_re, state_s5_im, cache_diff_k, cache_diff_v, cache_mla_ckv, cache_mla_krope, c, c_ctx, w_mod, b_mod, g_pre_mix, g_post_mix, g_pre_ffn, g_post_ffn, w_ffn_gate, w_ffn_up, w_ffn_down, w_in_even, w_out_even, s5_lam_re, s5_lam_im, s5_log_dt, s5_b_re, s5_b_im, s5_c_re, s5_c_im, s5_d, s5_glu_w, s5_glu_b, diff_lam_q1, diff_lam_k1, diff_lam_q2, diff_lam_k2, diff_subln_g, w_in_odd, mla_q_norm_g, mla_w_q_up, mla_kv_norm_g, mla_w_kv_up, w_out_odd):
    xs = (x_prompt.reshape(N_P, D_MODEL), x_sample.reshape(N_S, D_MODEL))
    conds = jnp.concatenate([c_ctx[None, :], c, jnp.zeros((N_COND - 1 - DEC_BATCH, D_MODEL), F32)], axis=0)
    mod = _modulation(conds, w_mod, b_mod).reshape(DEPTH, N_COND, 6, D_MODEL)

    ffn_args = (g_post_mix, g_pre_ffn, g_post_ffn, w_ffn_gate.astype(BF16), w_ffn_up.astype(BF16),
                w_ffn_down.astype(BF16))
    w_in_e = w_in_even.astype(BF16)
    w_out_e = w_out_even.astype(BF16)
    glu_w = s5_glu_w.astype(BF16)
    a_re, a_im, bb_re, bb_im = _s5_params(s5_lam_re, s5_lam_im, s5_log_dt, s5_b_re, s5_b_im)
    a = jnp.stack([a_re[:, 0], a_im[:, 0], a_re[:, 1], a_im[:, 1]], axis=1).reshape(N_EVEN, 4, S5_STATE_W)
    b_in, c_out = _s5_matrices(bb_re, bb_im, s5_c_re, s5_c_im)
    st = lambda s: jnp.moveaxis(s, 0, 2).reshape(N_EVEN, 2, 1, DEC_BATCH, S5_STATE_W)
    h0 = jnp.concatenate([jnp.zeros((N_EVEN, 2, 2, SUBLANES, 2 * S5_STATE_W), F32),
                          jnp.concatenate([st(state_s5_re), st(state_s5_im)], axis=-1)], axis=2)
    diff_ctx = (cache_diff_k.astype(BF16).reshape(DEC_BATCH, N_EVEN, PAST_LEN, DIFF_WIDTH),
                cache_diff_v.astype(BF16).reshape(DEC_BATCH, N_EVEN, PAST_LEN, DIFF_WIDTH))
    lam_vecs = (diff_lam_q1, diff_lam_k1, diff_lam_q2, diff_lam_k2)
    w_in_o, wq, wk, wv = _mla_weights(w_in_odd, mla_w_q_up, mla_w_kv_up)
    w_out_o = w_out_odd.astype(BF16)
    ckr_ctx = jnp.concatenate(
        [cache_mla_ckv, jnp.pad(cache_mla_krope, ((0, 0), (0, 0), (0, 0), (MLA_NOPE, MLA_PAD)))], axis=-1).astype(BF16)
    diff_tabs = _diff_rope_tables()
    mla_tabs = _mla_rope_tables()

    s5_re_list, s5_im_list, dk_list, dv_list, ckv_list, kr_list = [], [], [], [], [], []
    for l in range(DEPTH):
        last = l == DEPTH - 1
        if l % 2 == 0:
            e = l // 2
            lam_init = 0.8 - 0.6 * math.exp(-0.3 * l)
            u_tm, q, k, v, kf, vf = _even_pre(l, xs, mod, g_pre_mix, w_in_e, diff_tabs)
            y_f, y_b, fin_f, fin_b = _s5_scan(e, u_tm, a, b_in, c_out, h0)
            fin = jnp.stack([fin_f[:2].reshape(BATCH, 2, S5_GROUPS, S5_STATE),
                             fin_b[:2].reshape(BATCH, 2, S5_GROUPS, S5_STATE)], axis=1)
            s5_re_list.append(fin[:, :, 0])
            s5_im_list.append(fin[:, :, 1])
            da_p = _diff_attention(e, q, k, v, None, lam_vecs, diff_subln_g, lam_init,
                                   batch=BATCH, n=SEQ, row0=0, tq=SEQ)
            da_s = _diff_attention(e, q, k, v, diff_ctx, lam_vecs, diff_subln_g, lam_init,
                                   batch=DEC_BATCH, n=DEC_SEQ, row0=N_P, tq=256)
            xs = _even_post(l, xs, mod, y_f, y_b, u_tm, s5_d, glu_w, s5_glu_b, da_p, da_s, w_out_e, ffn_args, last)
            dk_list.append(kf.reshape(BATCH, SEQ, DIFF_HEADS, 2, DIFF_HEAD_DIM))
            dv_list.append(vf.reshape(BATCH, SEQ, DIFF_HEADS, 2 * DIFF_HEAD_DIM))
        else:
            o = l // 2
            q, ckr, ckv, krp = _odd_pre(l, xs, mod, g_pre_mix, w_in_o, mla_q_norm_g, mla_kv_norm_g, wq, mla_tabs)
            at_p = _mla_attention(o, q, ckr, None, wk, wv, batch=BATCH, n=SEQ, row0=0, tq=SEQ)
            at_s = _mla_attention(o, q, ckr, ckr_ctx, wk, wv, batch=DEC_BATCH, n=DEC_SEQ, row0=N_P, tq=256)
            xs = _odd_post(l, xs, mod, at_p, at_s, w_out_o, ffn_args, last)
            ckv_list.append(ckv.reshape(BATCH, SEQ, MLA_KV_RANK))
            kr_list.append(krp[:, MLA_NOPE:MLA_NOPE + MLA_ROPE].reshape(BATCH, SEQ, MLA_ROPE))

    return (xs[0].reshape(BATCH, SEQ, D_MODEL), xs[1].reshape(DEC_BATCH, DEC_SEQ, D_MODEL),
            jnp.stack(s5_re_list, axis=1), jnp.stack(s5_im_list, axis=1),
            jnp.stack(dk_list, axis=1), jnp.stack(dv_list, axis=1),
            jnp.stack(ckv_list, axis=1), jnp.stack(kr_list, axis=1))
```

```python
import functools
import math

import jax
import jax.numpy as jnp
from jax import lax
from jax.experimental import pallas as pl
from jax.experimental.pallas import tpu as pltpu

F32 = jnp.float32
BF16 = jnp.bfloat16

D_MODEL = 1024
BATCH = 16
SEQ = 256
DEPTH = 4
DEC_BATCH = 8
DEC_SEQ = 1024
PAST_LEN = 512
GRID_W = 64
N_EVEN = (DEPTH + 1) // 2
N_ODD = DEPTH // 2
EPS = 1e-6
ROPE_BASE = 10000.0
S5_WIDTH = D_MODEL // 2
S5_GROUP = 16
S5_GROUPS = S5_WIDTH // S5_GROUP
S5_STATE = 64
DIFF_HEAD_DIM = 64
DIFF_HEADS = (D_MODEL // 2) // (2 * DIFF_HEAD_DIM)
DIFF_WIDTH = DIFF_HEADS * 2 * DIFF_HEAD_DIM
EVEN_IN = S5_WIDTH + 3 * DIFF_WIDTH
MLA_HEADS = 16
MLA_NOPE = 64
MLA_ROPE = 32
MLA_V = 64
MLA_Q_RANK = 256
MLA_KV_RANK = 128
D_FF = ((8 * D_MODEL // 3 + 255) // 256) * 256

LANES = 128
SUBLANES = 8
VMEM_LIMIT = 56 * 1024 * 1024
LOG2E = math.log2(math.e)

N_P = BATCH * SEQ
N_S = DEC_BATCH * DEC_SEQ
N_TOK = N_P + N_S
TM = 256
TM_ODD = 512
N_TILES = N_TOK // TM
P_TILES = N_P // TM
S_TILES_PER_B = DEC_SEQ // TM
N_COND = 16

S5_STATE_W = S5_GROUPS * S5_STATE
S5_BLK = 4
S5_BLK_W = S5_STATE_W // S5_BLK
SCAN_T = 64
SCAN_R = SCAN_T * SUBLANES
N_TM_ROWS = N_TOK // SUBLANES
N_CHUNKS = N_TM_ROWS // SCAN_T
SEQ_CHUNKS = SEQ // SCAN_T

MLA_HEAD_PAD = LANES
MLA_PAD = MLA_HEAD_PAD - MLA_NOPE - MLA_ROPE
MLA_Q_W = MLA_HEADS * MLA_HEAD_PAD
MLA_IN_W = MLA_Q_RANK + MLA_KV_RANK + 2 * MLA_HEAD_PAD
MLA_CKR_W = MLA_KV_RANK + MLA_HEAD_PAD
MLA_O_W = MLA_HEADS * MLA_V


def _cparams(sem):
    return pltpu.CompilerParams(dimension_semantics=sem, vmem_limit_bytes=VMEM_LIMIT)


def _const_spec(shape):
    nd = len(shape)
    return pl.BlockSpec(shape, lambda *_: (0,) * nd, pipeline_mode=pl.Buffered(1))


def _layer_spec(tail, *lead):
    nt = len(tail)
    return pl.BlockSpec((None,) * len(lead) + tuple(tail), lambda *_: tuple(lead) + (0,) * nt,
                        pipeline_mode=pl.Buffered(1))


def _rms(x, g):
    return x * lax.rsqrt(jnp.mean(x * x, axis=-1, keepdims=True) + EPS) * g


def _dot(a, b):
    return jnp.dot(a, b, preferred_element_type=F32)


def _dot_nt(a, b):
    return lax.dot_general(a, b, (((1,), (1,)), ((), ())), preferred_element_type=F32)


def _cond_of_tile(i, tm=TM):
    return jnp.where(i < N_P // tm, 0, 1 + (i - N_P // tm) // (DEC_SEQ // tm))


def _tm_row_block(i):
    return jnp.where(i < P_TILES, i // SUBLANES, 2 + (i - P_TILES) % S_TILES_PER_B)


def _tm_col_block(i):
    return jnp.where(i < P_TILES, i % SUBLANES, (i - P_TILES) // S_TILES_PER_B)


def _rope_block(i, tm=TM):
    return jnp.where(i < N_P // tm, 0, 1 + (i - N_P // tm) % (DEC_SEQ // tm))


def _tok_spec(width, tm=TM):
    return pl.BlockSpec((tm, width), lambda i: (i, 0))


def _ctx_spec(width, tm=TM):
    return pl.BlockSpec((tm, width), lambda i: (jnp.minimum(i, N_P // tm - 1), 0))


def _lat_spec(width, tm=TM):
    return pl.BlockSpec((tm, width), lambda i: (jnp.maximum(i - N_P // tm, 0), 0))


def _stream_specs(arrays, width, tm=TM):
    return [_tok_spec(width, tm)] if len(arrays) == 1 else [_ctx_spec(width, tm), _lat_spec(width, tm)]


def _read_stream(refs, tm=TM):
    if len(refs) == 1:
        return refs[0][...]
    return jnp.where(pl.program_id(0) < N_P // tm, refs[0][...], refs[1][...])


def _mod_spec(l, tm=TM):
    return pl.BlockSpec((None, None, 6, D_MODEL), lambda i: (l, _cond_of_tile(i, tm), 0, 0))


def _row(ref, r):
    return ref[r:r + 1, :]


def _mod_kernel(c_ref, w_ref, b_ref, o_ref):
    s = jax.nn.silu(c_ref[...])
    o_ref[0] = _dot(s.astype(BF16), w_ref[0].astype(BF16)) + b_ref[0]


def _modulation(conds, w_mod, b_mod):
    tn = 1536
    return pl.pallas_call(
        _mod_kernel,
        grid=(DEPTH, 6 * D_MODEL // tn),
        in_specs=[
            pl.BlockSpec((N_COND, D_MODEL), lambda l, n: (0, 0)),
            pl.BlockSpec((1, D_MODEL, tn), lambda l, n: (l, 0, n)),
            pl.BlockSpec((1, 1, tn), lambda l, n: (l, 0, n)),
        ],
        out_specs=pl.BlockSpec((1, N_COND, tn), lambda l, n: (l, 0, n)),
        out_shape=jax.ShapeDtypeStruct((DEPTH, N_COND, 6 * D_MODEL), F32),
        compiler_params=_cparams(("parallel", "parallel")),
        name="modulation",
    )(conds, w_mod, b_mod.reshape(DEPTH, 1, 6 * D_MODEL))


S5_GB = S5_GROUPS // S5_BLK
S5_IN_ROWS = S5_GB * S5_GROUP
S5_N_BLOCKS = N_EVEN * 2 * S5_BLK


def _block_diag_lanes(x, row_group, col_group):
    xt = jnp.concatenate([x] * S5_GB, axis=1)
    rg = lax.broadcasted_iota(jnp.int32, xt.shape, 0) // row_group
    cg = lax.broadcasted_iota(jnp.int32, xt.shape, 1) // col_group
    return jnp.where(rg == cg, xt, 0.0)


def _s5_param_kernel(lr_ref, li_ref, ldt_ref, br_ref, bi_ref, cr_ref, ci_ref, are_ref, aim_ref, bin_ref, cout_ref):
    lr, li = lr_ref[...], li_ref[...]
    dt = jnp.exp(ldt_ref[...])
    mag = jnp.exp(lr * dt)
    a_re, a_im = mag * jnp.cos(li * dt), mag * jnp.sin(li * dt)
    den = lr * lr + li * li
    f_re = ((a_re - 1.0) * lr + a_im * li) / den
    f_im = (a_im * lr - (a_re - 1.0) * li) / den
    br, bi = br_ref[...], bi_ref[...]
    are_ref[...] = a_re
    aim_ref[...] = a_im
    bin_ref[:, :S5_BLK_W] = _block_diag_lanes(f_re * br - f_im * bi, S5_GROUP, S5_STATE).astype(BF16)
    bin_ref[:, S5_BLK_W:] = _block_diag_lanes(f_re * bi + f_im * br, S5_GROUP, S5_STATE).astype(BF16)
    cout_ref[:S5_BLK_W, :] = _block_diag_lanes(cr_ref[...], S5_STATE, S5_GROUP).astype(BF16)
    cout_ref[S5_BLK_W:, :] = _block_diag_lanes(-ci_ref[...], S5_STATE, S5_GROUP).astype(BF16)


def _s5_params(lam_re, lam_im, log_dt, b_re, b_im, c_re, c_im):
    full = (N_EVEN, 2, S5_GROUPS, S5_GROUP, S5_STATE)
    blk = (S5_N_BLOCKS, S5_IN_ROWS, S5_STATE)
    ex = lambda a: jnp.broadcast_to(a[:, :, :, None, :], full).reshape(blk)
    ldt = jnp.broadcast_to(log_dt[:, :, :, None, None], full).reshape(blk)
    bt = lambda b: jnp.swapaxes(b, -1, -2).reshape(blk)
    ct = lambda c: jnp.swapaxes(c, -1, -2).reshape(S5_N_BLOCKS, S5_BLK_W, S5_GROUP)
    spec = lambda r, w: pl.BlockSpec((None, r, w), lambda i: (i, 0, 0))
    a_re, a_im, b_in, c_out = pl.pallas_call(
        _s5_param_kernel,
        grid=(S5_N_BLOCKS,),
        in_specs=[spec(S5_IN_ROWS, S5_STATE)] * 5 + [spec(S5_BLK_W, S5_GROUP)] * 2,
        out_specs=[spec(S5_IN_ROWS, S5_STATE)] * 2 + [spec(S5_IN_ROWS, 2 * S5_BLK_W), spec(2 * S5_BLK_W, LANES)],
        out_shape=[jax.ShapeDtypeStruct(blk, F32)] * 2
        + [jax.ShapeDtypeStruct((S5_N_BLOCKS, S5_IN_ROWS, 2 * S5_BLK_W), BF16),
           jax.ShapeDtypeStruct((S5_N_BLOCKS, 2 * S5_BLK_W, LANES), BF16)],
        compiler_params=_cparams(("parallel",)),
        name="s5_params",
    )(ex(lam_re), ex(lam_im), ldt, bt(b_re), bt(b_im), ct(c_re), ct(c_im))
    pick = lambda a: a.reshape(full)[:, :, :, 0, :]
    lead = (N_EVEN, 2, S5_BLK)
    return pick(a_re), pick(a_im), b_in.reshape(lead + b_in.shape[1:]), c_out.reshape(lead + c_out.shape[1:])


def _seq_of_chunk(c):
    return (c >= SEQ_CHUNKS).astype(jnp.int32) + (c >= 2 * SEQ_CHUNKS).astype(jnp.int32)


def _s5_scan_kernel(uf_ref, ub_ref, a_ref, bin_ref, cout_ref, h0f_ref, h0b_ref,
                    yf_ref, yb_ref, finf_ref, finb_ref, us_ref, hs_ref, ys_ref, st_ref):
    j = pl.program_id(0)
    cf = j
    cb = N_CHUNKS - 1 - j

    @pl.when((cf == 0) | (cf == SEQ_CHUNKS) | (cf == 2 * SEQ_CHUNKS))
    def _():
        st_ref[0] = h0f_ref[...]

    @pl.when((cb == N_CHUNKS - 1) | (cb == 2 * SEQ_CHUNKS - 1) | (cb == SEQ_CHUNKS - 1))
    def _():
        st_ref[1] = h0b_ref[...]

    dirs = ((0, uf_ref, yf_ref), (1, ub_ref, yb_ref))
    for k in range(S5_BLK):
        lo, hi = k * S5_BLK_W, (k + 1) * S5_BLK_W
        for d, u_ref, _ in dirs:
            for b in range(SUBLANES):
                c0 = b * S5_WIDTH + k * LANES
                us_ref[d, k, pl.ds(b, SCAN_T, stride=SUBLANES), :] = u_ref[:, c0:c0 + LANES]
            hs_ref[d, k] = _dot(us_ref[d, k].astype(BF16), bin_ref[d, k])

        coef = [jnp.broadcast_to(a_ref[r:r + 1, lo:hi], (SUBLANES, S5_BLK_W)) for r in range(4)]
        state = [st_ref[0, :, lo:hi], st_ref[0, :, S5_STATE_W + lo:S5_STATE_W + hi],
                 st_ref[1, :, lo:hi], st_ref[1, :, S5_STATE_W + lo:S5_STATE_W + hi]]
        for t in range(SCAN_T):
            for d in range(2):
                rows = pl.ds((t if d == 0 else SCAN_T - 1 - t) * SUBLANES, SUBLANES)
                hr, hi_ = state[2 * d], state[2 * d + 1]
                ar, ai = coef[2 * d], coef[2 * d + 1]
                nr = ar * hr - ai * hi_ + hs_ref[d, k, rows, 0:S5_BLK_W]
                ni = ar * hi_ + ai * hr + hs_ref[d, k, rows, S5_BLK_W:2 * S5_BLK_W]
                hs_ref[d, k, rows, 0:S5_BLK_W] = nr
                hs_ref[d, k, rows, S5_BLK_W:2 * S5_BLK_W] = ni
                state[2 * d], state[2 * d + 1] = nr, ni
        st_ref[0, :, lo:hi] = state[0]
        st_ref[0, :, S5_STATE_W + lo:S5_STATE_W + hi] = state[1]
        st_ref[1, :, lo:hi] = state[2]
        st_ref[1, :, S5_STATE_W + lo:S5_STATE_W + hi] = state[3]

        for d, _, y_ref in dirs:
            ys_ref[d, k] = _dot(hs_ref[d, k].astype(BF16), cout_ref[d, k])
            for b in range(SUBLANES):
                c0 = b * S5_WIDTH + k * LANES
                y_ref[:, c0:c0 + LANES] = ys_ref[d, k, pl.ds(b, SCAN_T, stride=SUBLANES), :]

    finf_ref[...] = st_ref[0]
    finb_ref[...] = st_ref[1]


def _s5_scan(e, u_tm, a, b_in, c_out, h0):
    rev = lambda j: N_CHUNKS - 1 - j
    width = SUBLANES * S5_WIDTH
    st_w = 2 * S5_STATE_W
    row_f = pl.BlockSpec((SCAN_T, width), lambda j: (j, 0))
    row_b = pl.BlockSpec((SCAN_T, width), lambda j: (rev(j), 0))
    h0_f = pl.BlockSpec((None, None, None, SUBLANES, st_w), lambda j: (e, 0, _seq_of_chunk(j), 0, 0))
    h0_b = pl.BlockSpec((None, None, None, SUBLANES, st_w), lambda j: (e, 1, _seq_of_chunk(rev(j)), 0, 0))
    st_f = pl.BlockSpec((None, SUBLANES, st_w), lambda j: (_seq_of_chunk(j), 0, 0))
    st_b = pl.BlockSpec((None, SUBLANES, st_w), lambda j: (_seq_of_chunk(rev(j)), 0, 0))
    st_shape = jax.ShapeDtypeStruct((3, SUBLANES, st_w), F32)
    return pl.pallas_call(
        _s5_scan_kernel,
        grid=(N_CHUNKS,),
        in_specs=[row_f, row_b, _layer_spec(a.shape[1:], e), _layer_spec(b_in.shape[1:], e),
                  _layer_spec(c_out.shape[1:], e), h0_f, h0_b],
        out_specs=[row_f, row_b, st_f, st_b],
        out_shape=[jax.ShapeDtypeStruct((N_TM_ROWS, width), F32)] * 2 + [st_shape] * 2,
        scratch_shapes=[pltpu.VMEM((2, S5_BLK, SCAN_R, LANES), F32),
                        pltpu.VMEM((2, S5_BLK, SCAN_R, 2 * S5_BLK_W), F32),
                        pltpu.VMEM((2, S5_BLK, SCAN_R, LANES), F32),
                        pltpu.VMEM((2, SUBLANES, st_w), F32)],
        compiler_params=_cparams(("arbitrary",)),
        name="s5_scan",
    )(u_tm, u_tm, a, b_in, c_out, h0, h0)


BF16_SUBLANES = 16


def _cast_specs(w_stack, layer, n_steps):
    _, r, c = w_stack.shape
    rc = BF16_SUBLANES
    while r % rc or r // rc > n_steps:
        rc += BF16_SUBLANES
    last = r // rc - 1
    return (pl.BlockSpec((None, rc, c), lambda i: (layer, jnp.minimum(i, last), 0)),
            pl.BlockSpec((rc, c), lambda i: (jnp.minimum(i, last), 0)),
            jax.ShapeDtypeStruct((r, c), BF16))


def _cast_side_job(in_refs, out_refs):
    for wi, wo in zip(in_refs, out_refs):
        wo[...] = wi[...].astype(BF16)


def _even_pre_kernel(l, n_x, n_cast, *refs):
    x_refs, refs = refs[:n_x], refs[n_x:]
    mod_ref, g_ref, w_ref, cos_ref, sa_ref, sb_ref = refs[:6]
    u_ref, q_ref, k_ref, v_ref, kf_ref, vf_ref = refs[6 + n_cast:12 + n_cast]
    _cast_side_job(refs[6:6 + n_cast], refs[12 + n_cast:])
    m = mod_ref[...]
    h = _rms(_read_stream(x_refs), _row(g_ref, l)) * (1.0 + m[1:2]) + m[0:1]
    proj = _dot(h.astype(BF16), w_ref[...])
    cos, sa, sb = cos_ref[...], sa_ref[...], sb_ref[...]
    half = DIFF_HEAD_DIM // 2

    def rope(z):
        return z * cos + pltpu.roll(z, half, 1) * sa + pltpu.roll(z, DIFF_WIDTH - half, 1) * sb

    u_ref[...] = proj[:, :S5_WIDTH]
    k = rope(proj[:, S5_WIDTH + DIFF_WIDTH:S5_WIDTH + 2 * DIFF_WIDTH])
    v = proj[:, S5_WIDTH + 2 * DIFF_WIDTH:]
    q_ref[...] = (rope(proj[:, S5_WIDTH:S5_WIDTH + DIFF_WIDTH]) * (DIFF_HEAD_DIM ** -0.5 * LOG2E)).astype(BF16)
    k_ref[...] = k.astype(BF16)
    v_ref[...] = v.astype(BF16)

    @pl.when(pl.program_id(0) < P_TILES)
    def _():
        kf_ref[...] = k
        vf_ref[...] = v


def _even_pre(l, xs, mod, g_pre, w_in, rope_tabs, cast=()):
    rope_spec = pl.BlockSpec((TM, DIFF_WIDTH), lambda i: (_rope_block(i), 0))
    casts = [_cast_specs(w, lyr, N_TILES) for w, lyr in cast]
    return pl.pallas_call(
        functools.partial(_even_pre_kernel, l, len(xs), len(casts)),
        grid=(N_TILES,),
        in_specs=_stream_specs(xs, D_MODEL) + [
            _mod_spec(l),
            _const_spec(g_pre.shape),
            _layer_spec((D_MODEL, EVEN_IN), l // 2),
            rope_spec, rope_spec, rope_spec,
        ] + [c[0] for c in casts],
        out_specs=[
            pl.BlockSpec((TM, S5_WIDTH), lambda i: (_tm_row_block(i), _tm_col_block(i))),
            _tok_spec(DIFF_WIDTH), _tok_spec(DIFF_WIDTH), _tok_spec(DIFF_WIDTH),
            _ctx_spec(DIFF_WIDTH), _ctx_spec(DIFF_WIDTH),
        ] + [c[1] for c in casts],
        out_shape=[jax.ShapeDtypeStruct((N_TM_ROWS, SUBLANES * S5_WIDTH), F32)]
        + [jax.ShapeDtypeStruct((N_TOK, DIFF_WIDTH), BF16)] * 3
        + [jax.ShapeDtypeStruct((N_P, DIFF_WIDTH), F32)] * 2 + [c[2] for c in casts],
        compiler_params=_cparams(("arbitrary",)),
        name="even_pre",
    )(*xs, mod, g_pre, w_in, *rope_tabs, *[w for w, _ in cast])


def _softmax2_parts(parts):
    m = functools.reduce(jnp.maximum, [jnp.max(s, axis=-1, keepdims=True) for s in parts])
    ps = [jnp.exp2(s - m) for s in parts]
    l = functools.reduce(jnp.add, [jnp.sum(p, axis=-1, keepdims=True) for p in ps])
    return ps, l


def _diff_attn_kernel(e, lam_init, has_ctx, *refs):
    if has_ctx:
        q_ref, k_ref, v_ref, kc_ref, vc_ref, lq1, lk1, lq2, lk2, g_ref, o_ref = refs
    else:
        q_ref, k_ref, v_ref, lq1, lk1, lq2, lk2, g_ref, o_ref = refs
    lam = (jnp.exp(jnp.sum(_row(lq1, e) * _row(lk1, e), axis=-1, keepdims=True))
           - jnp.exp(jnp.sum(_row(lq2, e) * _row(lk2, e), axis=-1, keepdims=True)) + lam_init)
    first = lax.broadcasted_iota(jnp.int32, (1, 2 * DIFF_HEAD_DIM), 1) < DIFF_HEAD_DIM
    g = _row(g_ref, e)
    for h in range(DIFF_HEADS):
        sl = slice(h * 2 * DIFF_HEAD_DIM, (h + 1) * 2 * DIFF_HEAD_DIM)
        qh = q_ref[:, sl]
        zero = jnp.zeros_like(qh)
        q1 = jnp.where(first, qh, zero)
        q2 = jnp.where(first, zero, qh)
        keys = [k_ref[:, sl]]
        vals = [v_ref[:, sl]]
        if has_ctx:
            keys.insert(0, kc_ref[:, sl])
            vals.insert(0, vc_ref[:, sl])
        p1, l1 = _softmax2_parts([_dot_nt(q1, kk) for kk in keys])
        p2, l2 = _softmax2_parts([_dot_nt(q2, kk) for kk in keys])
        ratio = lam * l1 / l2
        o = functools.reduce(jnp.add, [_dot((a - b * ratio).astype(BF16), vv) for a, b, vv in zip(p1, p2, vals)])
        o = o * (1.0 / l1)
        o_ref[:, sl] = (_rms(o, g) * (1.0 - lam_init)).astype(o_ref.dtype)


def _diff_attention(e, q, k, v, ctx, lam_vecs, subln_g, lam_init, *, batch, n, row0, tq):
    nq = n // tq
    has_ctx = ctx is not None
    in_specs = [
        pl.BlockSpec((tq, DIFF_WIDTH), lambda b, i: (row0 // tq + b * nq + i, 0)),
        pl.BlockSpec((n, DIFF_WIDTH), lambda b, i: (row0 // n + b, 0)),
        pl.BlockSpec((n, DIFF_WIDTH), lambda b, i: (row0 // n + b, 0)),
    ]
    args = [q, k, v]
    if has_ctx:
        in_specs += [pl.BlockSpec((None, None, PAST_LEN, DIFF_WIDTH), lambda b, i: (b, e, 0, 0))] * 2
        args += list(ctx)
    in_specs += [_const_spec(t.shape) for t in lam_vecs] + [_const_spec(subln_g.shape)]
    args += list(lam_vecs) + [subln_g]
    return pl.pallas_call(
        functools.partial(_diff_attn_kernel, e, lam_init, has_ctx),
        grid=(batch, nq),
        in_specs=in_specs,
        out_specs=pl.BlockSpec((tq, DIFF_WIDTH), lambda b, i: (b * nq + i, 0)),
        out_shape=jax.ShapeDtypeStruct((batch * n, DIFF_WIDTH), BF16),
        compiler_params=_cparams(("parallel", "parallel")),
        name="diff_attn_ctx" if has_ctx else "diff_attn",
    )(*args)


def _post_tail(l, x, m, mix, gpm_ref, gpf_ref, gqf_ref, wg_ref, wu_ref, wd_ref):
    x1 = x + m[2:3] * _rms(mix, _row(gpm_ref, l))
    h = (_rms(x1, _row(gpf_ref, l)) * (1.0 + m[4:5]) + m[3:4]).astype(BF16)
    act = jax.nn.silu(_dot(h, wg_ref[...])) * _dot(h, wu_ref[...])
    y = _dot(act.astype(BF16), wd_ref[...])
    return x1 + m[5:6] * _rms(y, _row(gqf_ref, l))


def _write_stream(out_refs, rows, x2, tm=TM):
    if len(out_refs) == 1:
        out_refs[0][rows, :] = x2
    else:
        i = pl.program_id(0)

        @pl.when(i < N_P // tm)
        def _():
            out_refs[0][rows, :] = x2

        @pl.when(i >= N_P // tm)
        def _():
            out_refs[1][rows, :] = x2


def _ffn_specs():
    vec = _const_spec((DEPTH, D_MODEL))
    return [vec, vec, vec, _const_spec((D_MODEL, D_FF)), _const_spec((D_MODEL, D_FF)), _const_spec((D_FF, D_MODEL))]


def _out_stream(split, tm=TM):
    if split:
        return ([_ctx_spec(D_MODEL, tm), _lat_spec(D_MODEL, tm)],
                [jax.ShapeDtypeStruct((N_P, D_MODEL), F32), jax.ShapeDtypeStruct((N_S, D_MODEL), F32)])
    return [_tok_spec(D_MODEL, tm)], [jax.ShapeDtypeStruct((N_TOK, D_MODEL), F32)]


def _even_post_kernel(l, n_x, n_out, n_cast, *refs):
    e = l // 2
    x_refs, refs = refs[:n_x], refs[n_x:]
    (mod_ref, yf_ref, yb_ref, u_ref, d_ref, gw_ref, gb_ref, dap_ref, das_ref, wos_ref, wod_ref,
     gpm_ref, gpf_ref, gqf_ref, wg_ref, wu_ref, wd_ref) = refs[:17]
    m = mod_ref[...]
    y = yf_ref[...] + yb_ref[...] + _row(d_ref, e) * u_ref[...]
    g = jax.nn.gelu(y)
    s5 = g * jax.nn.sigmoid(_dot(g.astype(BF16), gw_ref[...]) + _row(gb_ref, e))
    mix = _dot(s5.astype(BF16), wos_ref[...]) + _dot(_read_stream((dap_ref, das_ref)), wod_ref[...])
    x2 = _post_tail(l, _read_stream(x_refs), m, mix, gpm_ref, gpf_ref, gqf_ref, wg_ref, wu_ref, wd_ref)
    n_in = 17 + n_cast
    _write_stream(refs[n_in:n_in + n_out], slice(None), x2)
    _cast_side_job(refs[17:n_in], refs[n_in + n_out:])


def _even_post(l, xs, mod, y_f, y_b, u_tm, s5_d, glu_w, glu_b, da_p, da_s, w_out, ffn_args, split_out, cast=()):
    e = l // 2
    tm_spec = pl.BlockSpec((TM, S5_WIDTH), lambda i: (_tm_row_block(i), _tm_col_block(i)))
    out_specs, out_shape = _out_stream(split_out)
    casts = [_cast_specs(w, lyr, N_TILES) for w, lyr in cast]
    w_half = lambda r: pl.BlockSpec((None, S5_WIDTH, D_MODEL), lambda i: (e, r, 0), pipeline_mode=pl.Buffered(1))
    return pl.pallas_call(
        functools.partial(_even_post_kernel, l, len(xs), len(out_specs), len(casts)),
        grid=(N_TILES,),
        in_specs=_stream_specs(xs, D_MODEL) + [
            _mod_spec(l),
            tm_spec, tm_spec, tm_spec,
            _const_spec(s5_d.shape), _layer_spec((S5_WIDTH, S5_WIDTH), e), _const_spec(glu_b.shape),
            _ctx_spec(DIFF_WIDTH), _lat_spec(DIFF_WIDTH),
            w_half(0), w_half(1),
        ] + _ffn_specs() + [c[0] for c in casts],
        out_specs=out_specs + [c[1] for c in casts],
        out_shape=out_shape + [c[2] for c in casts],
        compiler_params=_cparams(("arbitrary",)),
        name="even_post_ffn",
    )(*xs, mod, y_f, y_b, u_tm, s5_d, glu_w, glu_b, da_p, da_s, w_out, w_out, *ffn_args, *[w for w, _ in cast])


def _odd_pre_kernel(l, n_x, *refs):
    o = l // 2
    x_refs, refs = refs[:n_x], refs[n_x:]
    (mod_ref, g_ref, w_ref, gq_ref, gkv_ref, wq_ref, cos_ref, sin_ref,
     q_ref, ckr_ref, ckv_ref, kr_ref) = refs
    m = mod_ref[...]
    h = _rms(_read_stream(x_refs, TM_ODD), _row(g_ref, l)) * (1.0 + m[1:2]) + m[0:1]
    proj = _dot(h.astype(BF16), w_ref[...])
    cq = _rms(proj[:, :MLA_Q_RANK], _row(gq_ref, o))
    ckv = _rms(proj[:, MLA_Q_RANK:MLA_Q_RANK + MLA_KV_RANK], _row(gkv_ref, o))
    kr0 = MLA_Q_RANK + MLA_KV_RANK
    krp = proj[:, kr0:kr0 + MLA_HEAD_PAD]
    krp_sw = proj[:, kr0 + MLA_HEAD_PAD:]
    cos, sin = cos_ref[...], sin_ref[...]
    q = _dot(cq.astype(BF16), wq_ref[...]) * ((MLA_NOPE + MLA_ROPE) ** -0.5 * LOG2E)
    for hd in range(MLA_HEADS):
        sl = slice(hd * MLA_HEAD_PAD, (hd + 1) * MLA_HEAD_PAD)
        sw = slice(MLA_Q_W + hd * MLA_HEAD_PAD, MLA_Q_W + (hd + 1) * MLA_HEAD_PAD)
        q_ref[:, sl] = (q[:, sl] * cos + q[:, sw] * sin).astype(BF16)
    ckr_ref[:, :MLA_KV_RANK] = ckv.astype(BF16)
    ckr_ref[:, MLA_KV_RANK:] = (krp * cos + krp_sw * sin).astype(BF16)

    @pl.when(pl.program_id(0) < N_P // TM_ODD)
    def _():
        ckv_ref[...] = ckv
        kr_ref[...] = krp


def _odd_pre(l, xs, mod, g_pre, w_in, gq, gkv, wq, rope_tabs):
    o = l // 2
    tm = TM_ODD
    rope_spec = pl.BlockSpec((tm, MLA_HEAD_PAD), lambda i: (_rope_block(i, tm), 0))
    return pl.pallas_call(
        functools.partial(_odd_pre_kernel, l, len(xs)),
        grid=(N_TOK // tm,),
        in_specs=_stream_specs(xs, D_MODEL, tm) + [
            _mod_spec(l, tm),
            _const_spec(g_pre.shape),
            _layer_spec((D_MODEL, MLA_IN_W), o),
            _const_spec(gq.shape), _const_spec(gkv.shape),
            _layer_spec((MLA_Q_RANK, 2 * MLA_Q_W), o),
            rope_spec, rope_spec,
        ],
        out_specs=[_tok_spec(MLA_Q_W, tm), _tok_spec(MLA_CKR_W, tm), _ctx_spec(MLA_KV_RANK, tm),
                   _ctx_spec(MLA_HEAD_PAD, tm)],
        out_shape=[
            jax.ShapeDtypeStruct((N_TOK, MLA_Q_W), BF16),
            jax.ShapeDtypeStruct((N_TOK, MLA_CKR_W), BF16),
            jax.ShapeDtypeStruct((N_P, MLA_KV_RANK), F32),
            jax.ShapeDtypeStruct((N_P, MLA_HEAD_PAD), F32),
        ],
        compiler_params=_cparams(("arbitrary",)),
        name="odd_pre",
    )(*xs, mod, g_pre, w_in, gq, gkv, wq, *rope_tabs)


def _mla_attn_kernel(has_ctx, *refs):
    if has_ctx:
        q_ref, ckr_ref, ckrc_ref, wk_ref, wv_ref, o_ref, kf_ref, vf_ref = refs
    else:
        q_ref, ckr_ref, wk_ref, wv_ref, o_ref, kf_ref, vf_ref = refs
    off = PAST_LEN if has_ctx else 0

    @pl.when(pl.program_id(1) == 0)
    def _():
        srcs = [(0, ckrc_ref)] if has_ctx else []
        srcs.append((off, ckr_ref))
        wv = wv_ref[...] if has_ctx else wv_ref[:, :MLA_O_W] + wv_ref[:, MLA_O_W:]
        for r0, src in srcs:
            c = src[...]
            rows = pl.ds(r0, c.shape[0])
            kf_ref[rows, :] = _dot(c, wk_ref[...]).astype(BF16)
            vf_ref[rows, :] = _dot(c[:, :MLA_KV_RANK], wv).astype(BF16)

    low = lax.broadcasted_iota(jnp.int32, (1, LANES), 1) < MLA_V
    for j in range(MLA_HEADS // 2):
        ps, rl = [], []
        for t in range(2):
            cs = slice((2 * j + t) * MLA_HEAD_PAD, (2 * j + t + 1) * MLA_HEAD_PAD)
            s = _dot_nt(q_ref[:, cs], kf_ref[:, cs])
            mx = jnp.max(s, axis=-1, keepdims=True)
            p = jnp.exp2(s - mx)
            rl.append(1.0 / jnp.sum(p, axis=-1, keepdims=True))
            ps.append(p.astype(BF16))
        slot = slice(j * LANES, (j + 1) * LANES)
        if has_ctx:
            o = (_dot(ps[0], vf_ref[:, slot])
                 + _dot(ps[1], vf_ref[:, MLA_O_W + j * LANES:MLA_O_W + (j + 1) * LANES])) * jnp.where(low, rl[0], rl[1])
        else:
            o = jnp.where(low, _dot(ps[0], vf_ref[:, slot]) * rl[0], _dot(ps[1], vf_ref[:, slot]) * rl[1])
        o_ref[:, slot] = o.astype(o_ref.dtype)


def _mla_attention(o, q, ckr, ckr_ctx, wk, wv, *, batch, n, row0, tq):
    nq = n // tq
    has_ctx = ckr_ctx is not None
    s_len = n + (PAST_LEN if has_ctx else 0)
    in_specs = [
        pl.BlockSpec((tq, MLA_Q_W), lambda b, i: (row0 // tq + b * nq + i, 0)),
        pl.BlockSpec((n, MLA_CKR_W), lambda b, i: (row0 // n + b, 0)),
    ]
    args = [q, ckr]
    if has_ctx:
        in_specs.append(pl.BlockSpec((None, None, PAST_LEN, MLA_CKR_W), lambda b, i: (b, o, 0, 0)))
        args.append(ckr_ctx)
    in_specs += [_layer_spec(wk.shape[1:], o), _layer_spec(wv.shape[1:], o)]
    args += [wk, wv]
    return pl.pallas_call(
        functools.partial(_mla_attn_kernel, has_ctx),
        grid=(batch, nq),
        in_specs=in_specs,
        out_specs=pl.BlockSpec((tq, MLA_O_W), lambda b, i: (b * nq + i, 0)),
        out_shape=jax.ShapeDtypeStruct((batch * n, MLA_O_W), BF16),
        scratch_shapes=[pltpu.VMEM((s_len, MLA_Q_W), BF16),
                        pltpu.VMEM((s_len, (2 if has_ctx else 1) * MLA_O_W), BF16)],
        compiler_params=_cparams(("parallel", "arbitrary")),
        name="mla_attn_ctx" if has_ctx else "mla_attn",
    )(*args)


def _odd_post_kernel(l, n_x, n_out, n_cast, *refs):
    x_refs, refs = refs[:n_x], refs[n_x:]
    mod_ref, ap_ref, as_ref, wo_ref, gpm_ref, gpf_ref, gqf_ref, wg_ref, wu_ref, wd_ref = refs[:10]
    mix = _dot(_read_stream((ap_ref, as_ref), TM_ODD), wo_ref[...])
    x2 = _post_tail(l, _read_stream(x_refs, TM_ODD), mod_ref[...], mix, gpm_ref, gpf_ref, gqf_ref,
                    wg_ref, wu_ref, wd_ref)
    n_in = 10 + n_cast
    _write_stream(refs[n_in:n_in + n_out], slice(None), x2, TM_ODD)
    _cast_side_job(refs[10:n_in], refs[n_in + n_out:])


def _odd_post(l, xs, mod, at_p, at_s, w_out, ffn_args, split_out, cast=()):
    tm = TM_ODD
    out_specs, out_shape = _out_stream(split_out, tm)
    casts = [_cast_specs(w, lyr, N_TOK // tm) for w, lyr in cast]
    return pl.pallas_call(
        functools.partial(_odd_post_kernel, l, len(xs), len(out_specs), len(casts)),
        grid=(N_TOK // tm,),
        in_specs=_stream_specs(xs, D_MODEL, tm) + [
            _mod_spec(l, tm),
            _ctx_spec(MLA_O_W, tm), _lat_spec(MLA_O_W, tm),
            _layer_spec((MLA_O_W, D_MODEL), l // 2),
        ] + _ffn_specs() + [c[0] for c in casts],
        out_specs=out_specs + [c[1] for c in casts],
        out_shape=out_shape + [c[2] for c in casts],
        compiler_params=_cparams(("arbitrary",)),
        name="odd_post_ffn",
    )(*xs, mod, at_p, at_s, w_out, *ffn_args, *[w for w, _ in cast])


def _rope_angles(rot_dim):
    rows = DEC_SEQ // GRID_W
    row = jnp.repeat(jnp.arange(rows, dtype=F32), GRID_W)
    col = jnp.tile(jnp.arange(GRID_W, dtype=F32), rows)
    n_freq = rot_dim // 4
    inv = ROPE_BASE ** (-jnp.arange(n_freq, dtype=F32) / n_freq)
    ang = jnp.concatenate([row[:, None] * inv, col[:, None] * inv], axis=-1)
    return jnp.cos(ang), jnp.sin(ang)


def _with_identity(tm, cos, *sins):
    one = jnp.ones((tm, cos.shape[1]), F32)
    zero = jnp.zeros((tm, cos.shape[1]), F32)
    return (jnp.concatenate([one, cos]),) + tuple(jnp.concatenate([zero, s]) for s in sins)


def _diff_rope_tables():
    c, s = _rope_angles(DIFF_HEAD_DIM)
    z = jnp.zeros_like(s)
    reps = DIFF_WIDTH // DIFF_HEAD_DIM
    cos = jnp.tile(jnp.concatenate([c, c], axis=1), (1, reps))
    sa = jnp.tile(jnp.concatenate([z, s], axis=1), (1, reps))
    sb = jnp.tile(jnp.concatenate([-s, z], axis=1), (1, reps))
    return _with_identity(TM, cos, sa, sb)


def _mla_rope_tables():
    c, s = _rope_angles(MLA_ROPE)
    n = c.shape[0]
    cos = jnp.concatenate([jnp.ones((n, MLA_NOPE), F32), c, c, jnp.ones((n, MLA_PAD), F32)], axis=1)
    sin = jnp.concatenate([jnp.zeros((n, MLA_NOPE), F32), -s, s, jnp.zeros((n, MLA_PAD), F32)], axis=1)
    return _with_identity(TM_ODD, cos, sin)


def _swap_rope_halves(w):
    half = MLA_ROPE // 2
    return jnp.concatenate([w[..., half:], w[..., :half]], axis=-1)


def _mla_weights(w_in_odd, w_q_up, w_kv_up):
    zeros = lambda *s: jnp.zeros(s, F32)
    kr0 = MLA_Q_RANK + MLA_KV_RANK
    w_kr = w_in_odd[:, :, kr0:]
    lead = (N_ODD, D_MODEL)
    w_in = jnp.concatenate(
        [w_in_odd[:, :, :kr0],
         zeros(*lead, MLA_NOPE), w_kr, zeros(*lead, MLA_PAD),
         zeros(*lead, MLA_NOPE), _swap_rope_halves(w_kr), zeros(*lead, MLA_PAD)], axis=-1).astype(BF16)
    wq4 = w_q_up.reshape(N_ODD, MLA_Q_RANK, MLA_HEADS, MLA_NOPE + MLA_ROPE)
    lead = (N_ODD, MLA_Q_RANK, MLA_HEADS)
    wq_main = jnp.concatenate([wq4, zeros(*lead, MLA_PAD)], axis=-1)
    wq_swap = jnp.concatenate([zeros(*lead, MLA_NOPE), _swap_rope_halves(wq4[..., MLA_NOPE:]),
                               zeros(*lead, MLA_PAD)], axis=-1)
    wq = jnp.concatenate([wq_main.reshape(N_ODD, MLA_Q_RANK, MLA_Q_W),
                          wq_swap.reshape(N_ODD, MLA_Q_RANK, MLA_Q_W)], axis=-1).astype(BF16)
    wkv = w_kv_up.reshape(N_ODD, MLA_KV_RANK, MLA_HEADS, MLA_NOPE + MLA_V)
    wk_top = jnp.pad(wkv[..., :MLA_NOPE], ((0, 0), (0, 0), (0, 0), (0, MLA_HEAD_PAD - MLA_NOPE)))
    sel = jnp.pad(jnp.eye(MLA_ROPE, dtype=F32), ((MLA_NOPE, MLA_PAD), (MLA_NOPE, MLA_PAD)))
    wk_bot = jnp.broadcast_to(jnp.tile(sel, (1, MLA_HEADS)), (N_ODD, MLA_HEAD_PAD, MLA_Q_W))
    wk = jnp.concatenate([wk_top.reshape(N_ODD, MLA_KV_RANK, MLA_Q_W), wk_bot], axis=1).astype(BF16)
    even = (jnp.arange(MLA_HEADS) % 2 == 0)[:, None]
    wv4 = wkv[..., MLA_NOPE:]
    wv = jnp.concatenate([jnp.where(even, wv4, 0.0).reshape(N_ODD, MLA_KV_RANK, MLA_O_W),
                          jnp.where(even, 0.0, wv4).reshape(N_ODD, MLA_KV_RANK, MLA_O_W)], axis=-1).astype(BF16)
    return w_in, wq, wk, wv


def kernel(x_prompt, x_sample, state_s5_re, state_s5_im, cache_diff_k, cache_diff_v, cache_mla_ckv, cache_mla_krope, c, c_ctx, w_mod, b_mod, g_pre_mix, g_post_mix, g_pre_ffn, g_post_ffn, w_ffn_gate, w_ffn_up, w_ffn_down, w_in_even, w_out_even, s5_lam_re, s5_lam_im, s5_log_dt, s5_b_re, s5_b_im, s5_c_re, s5_c_im, s5_d, s5_glu_w, s5_glu_b, diff_lam_q1, diff_lam_k1, diff_lam_q2, diff_lam_k2, diff_subln_g, w_in_odd, mla_q_norm_g, mla_w_q_up, mla_kv_norm_g, mla_w_kv_up, w_out_odd):
    xs = (x_prompt.reshape(N_P, D_MODEL), x_sample.reshape(N_S, D_MODEL))
    conds = jnp.concatenate([c_ctx[None, :], c, jnp.zeros((N_COND - 1 - DEC_BATCH, D_MODEL), F32)], axis=0)
    mod = _modulation(conds, w_mod, b_mod).reshape(DEPTH, N_COND, 6, D_MODEL)

    gains = (g_post_mix, g_pre_ffn, g_post_ffn)
    ffn_f32 = (w_ffn_gate, w_ffn_up, w_ffn_down)
    w_in_e = w_in_even.astype(BF16)
    w_out_e = w_out_even.astype(BF16)
    glu_w = s5_glu_w.astype(BF16)
    a_re, a_im, b_in, c_out = _s5_params(s5_lam_re, s5_lam_im, s5_log_dt, s5_b_re, s5_b_im, s5_c_re, s5_c_im)
    a = jnp.stack([a_re[:, 0], a_im[:, 0], a_re[:, 1], a_im[:, 1]], axis=1).reshape(N_EVEN, 4, S5_STATE_W)
    st = lambda s: jnp.moveaxis(s, 0, 2).reshape(N_EVEN, 2, 1, DEC_BATCH, S5_STATE_W)
    h0 = jnp.concatenate([jnp.zeros((N_EVEN, 2, 2, SUBLANES, 2 * S5_STATE_W), F32),
                          jnp.concatenate([st(state_s5_re), st(state_s5_im)], axis=-1)], axis=2)
    diff_ctx = (cache_diff_k.astype(BF16).reshape(DEC_BATCH, N_EVEN, PAST_LEN, DIFF_WIDTH),
                cache_diff_v.astype(BF16).reshape(DEC_BATCH, N_EVEN, PAST_LEN, DIFF_WIDTH))
    lam_vecs = (diff_lam_q1, diff_lam_k1, diff_lam_q2, diff_lam_k2)
    w_in_o, wq, wk, wv = _mla_weights(w_in_odd, mla_w_q_up, mla_w_kv_up)
    w_out_o = w_out_odd.astype(BF16)
    ckr_ctx = jnp.concatenate(
        [cache_mla_ckv, jnp.pad(cache_mla_krope, ((0, 0), (0, 0), (0, 0), (MLA_NOPE, MLA_PAD)))], axis=-1).astype(BF16)
    diff_tabs = _diff_rope_tables()
    mla_tabs = _mla_rope_tables()

    s5_re_list, s5_im_list, dk_list, dv_list, ckv_list, kr_list = [], [], [], [], [], []
    for l in range(DEPTH):
        last = l == DEPTH - 1
        n_stream = 2 if last else 1
        next_cast = () if last else tuple((w, l + 1) for w in ffn_f32)
        if l % 2 == 0:
            e = l // 2
            lam_init = 0.8 - 0.6 * math.exp(-0.3 * l)
            pre = _even_pre(l, xs, mod, g_pre_mix, w_in_e, diff_tabs,
                            cast=tuple((w, 0) for w in ffn_f32) if l == 0 else ())
            u_tm, q, k, v, kf, vf = pre[:6]
            if l == 0:
                ffn_w = tuple(pre[6:])
            y_f, y_b, fin_f, fin_b = _s5_scan(e, u_tm, a, b_in, c_out, h0)
            fin = jnp.stack([fin_f[:2].reshape(BATCH, 2, S5_GROUPS, S5_STATE),
                             fin_b[:2].reshape(BATCH, 2, S5_GROUPS, S5_STATE)], axis=1)
            s5_re_list.append(fin[:, :, 0])
            s5_im_list.append(fin[:, :, 1])
            da_p = _diff_attention(e, q, k, v, None, lam_vecs, diff_subln_g, lam_init,
                                   batch=BATCH, n=SEQ, row0=0, tq=SEQ)
            da_s = _diff_attention(e, q, k, v, diff_ctx, lam_vecs, diff_subln_g, lam_init,
                                   batch=DEC_BATCH, n=DEC_SEQ, row0=N_P, tq=256)
            post = _even_post(l, xs, mod, y_f, y_b, u_tm, s5_d, glu_w, s5_glu_b, da_p, da_s, w_out_e,
                              gains + ffn_w, last, cast=next_cast)
            xs, ffn_w = post[:n_stream], tuple(post[n_stream:])
            dk_list.append(kf.reshape(BATCH, SEQ, DIFF_HEADS, 2, DIFF_HEAD_DIM))
            dv_list.append(vf.reshape(BATCH, SEQ, DIFF_HEADS, 2 * DIFF_HEAD_DIM))
        else:
            o = l // 2
            q, ckr, ckv, krp = _odd_pre(l, xs, mod, g_pre_mix, w_in_o, mla_q_norm_g, mla_kv_norm_g, wq, mla_tabs)
            at_p = _mla_attention(o, q, ckr, None, wk, wv, batch=BATCH, n=SEQ, row0=0, tq=SEQ)
            at_s = _mla_attention(o, q, ckr, ckr_ctx, wk, wv, batch=DEC_BATCH, n=DEC_SEQ, row0=N_P, tq=256)
            post = _odd_post(l, xs, mod, at_p, at_s, w_out_o, gains + ffn_w, last, cast=next_cast)
            xs, ffn_w = post[:n_stream], tuple(post[n_stream:])
            ckv_list.append(ckv.reshape(BATCH, SEQ, MLA_KV_RANK))
            kr_list.append(krp[:, MLA_NOPE:MLA_NOPE + MLA_ROPE].reshape(BATCH, SEQ, MLA_ROPE))

    return (xs[0].reshape(BATCH, SEQ, D_MODEL), xs[1].reshape(DEC_BATCH, DEC_SEQ, D_MODEL),
            jnp.stack(s5_re_list, axis=1), jnp.stack(s5_im_list, axis=1),
            jnp.stack(dk_list, axis=1), jnp.stack(dv_list, axis=1),
            jnp.stack(ckv_list, axis=1), jnp.stack(kr_list, axis=1))
```

```python
import functools
import math

import jax
import jax.numpy as jnp
from jax import lax
from jax.experimental import pallas as pl
from jax.experimental.pallas import tpu as pltpu

F32 = jnp.float32
BF16 = jnp.bfloat16

D_MODEL = 1024
BATCH = 16
SEQ = 256
DEPTH = 4
DEC_BATCH = 8
DEC_SEQ = 1024
PAST_LEN = 512
GRID_W = 64
N_EVEN = (DEPTH + 1) // 2
N_ODD = DEPTH // 2
EPS = 1e-6
ROPE_BASE = 10000.0
S5_WIDTH = D_MODEL // 2
S5_GROUP = 16
S5_GROUPS = S5_WIDTH // S5_GROUP
S5_STATE = 64
DIFF_HEAD_DIM = 64
DIFF_HEADS = (D_MODEL // 2) // (2 * DIFF_HEAD_DIM)
DIFF_WIDTH = DIFF_HEADS * 2 * DIFF_HEAD_DIM
EVEN_IN = S5_WIDTH + 3 * DIFF_WIDTH
MLA_HEADS = 16
MLA_NOPE = 64
MLA_ROPE = 32
MLA_V = 64
MLA_Q_RANK = 256
MLA_KV_RANK = 128
D_FF = ((8 * D_MODEL // 3 + 255) // 256) * 256

LANES = 128
SUBLANES = 8
VMEM_LIMIT = 56 * 1024 * 1024
VMEM_LIMIT_POST = 61 * 1024 * 1024
LOG2E = math.log2(math.e)

N_P = BATCH * SEQ
N_S = DEC_BATCH * DEC_SEQ
N_TOK = N_P + N_S
TM = 256
TM_ODD = 512
N_TILES = N_TOK // TM
P_TILES = N_P // TM
S_TILES_PER_B = DEC_SEQ // TM
N_COND = 16

S5_STATE_W = S5_GROUPS * S5_STATE
S5_BLK = 4
S5_BLK_W = S5_STATE_W // S5_BLK
SCAN_T = 64
SCAN_R = SCAN_T * SUBLANES
N_TM_ROWS = N_TOK // SUBLANES
N_CHUNKS = N_TM_ROWS // SCAN_T
SEQ_CHUNKS = SEQ // SCAN_T

MLA_HEAD_PAD = LANES
MLA_PAD = MLA_HEAD_PAD - MLA_NOPE - MLA_ROPE
MLA_Q_W = MLA_HEADS * MLA_HEAD_PAD
MLA_IN_W = MLA_Q_RANK + MLA_KV_RANK + 2 * MLA_HEAD_PAD
MLA_CKR_W = MLA_KV_RANK + MLA_HEAD_PAD
MLA_O_W = MLA_HEADS * MLA_V


def _cparams(sem, vmem_limit=VMEM_LIMIT):
    return pltpu.CompilerParams(dimension_semantics=sem, vmem_limit_bytes=vmem_limit)


def _const_spec(shape):
    nd = len(shape)
    return pl.BlockSpec(shape, lambda *_: (0,) * nd, pipeline_mode=pl.Buffered(1))


def _layer_spec(tail, *lead):
    nt = len(tail)
    return pl.BlockSpec((None,) * len(lead) + tuple(tail), lambda *_: tuple(lead) + (0,) * nt,
                        pipeline_mode=pl.Buffered(1))


def _rms(x, g):
    return x * lax.rsqrt(jnp.mean(x * x, axis=-1, keepdims=True) + EPS) * g


def _dot(a, b):
    return jnp.dot(a, b, preferred_element_type=F32)


def _dot_nt(a, b):
    return lax.dot_general(a, b, (((1,), (1,)), ((), ())), preferred_element_type=F32)


def _cond_of_tile(i, tm=TM):
    return jnp.where(i < N_P // tm, 0, 1 + (i - N_P // tm) // (DEC_SEQ // tm))


def _tm_row_block(i):
    return jnp.where(i < P_TILES, i // SUBLANES, 2 + (i - P_TILES) % S_TILES_PER_B)


def _tm_col_block(i):
    return jnp.where(i < P_TILES, i % SUBLANES, (i - P_TILES) // S_TILES_PER_B)


def _rope_block(i, tm=TM):
    return jnp.where(i < N_P // tm, 0, 1 + (i - N_P // tm) % (DEC_SEQ // tm))


def _tok_spec(width, tm=TM):
    return pl.BlockSpec((tm, width), lambda i: (i, 0))


def _ctx_spec(width, tm=TM):
    return pl.BlockSpec((tm, width), lambda i: (jnp.minimum(i, N_P // tm - 1), 0))


def _lat_spec(width, tm=TM):
    return pl.BlockSpec((tm, width), lambda i: (jnp.maximum(i - N_P // tm, 0), 0))


def _stream_specs(arrays, width, tm=TM):
    return [_tok_spec(width, tm)] if len(arrays) == 1 else [_ctx_spec(width, tm), _lat_spec(width, tm)]


def _read_stream(refs, tm=TM):
    if len(refs) == 1:
        return refs[0][...]
    return jnp.where(pl.program_id(0) < N_P // tm, refs[0][...], refs[1][...])


def _mod_spec(l, tm=TM):
    return pl.BlockSpec((None, None, 6, D_MODEL), lambda i: (l, _cond_of_tile(i, tm), 0, 0))


def _row(ref, r):
    return ref[r:r + 1, :]


def _mod_kernel(c_ref, w_ref, b_ref, o_ref):
    s = jax.nn.silu(c_ref[...])
    o_ref[0] = _dot(s.astype(BF16), w_ref[0].astype(BF16)) + b_ref[0]


def _modulation(conds, w_mod, b_mod):
    tn = 1536
    return pl.pallas_call(
        _mod_kernel,
        grid=(DEPTH, 6 * D_MODEL // tn),
        in_specs=[
            pl.BlockSpec((N_COND, D_MODEL), lambda l, n: (0, 0)),
            pl.BlockSpec((1, D_MODEL, tn), lambda l, n: (l, 0, n)),
            pl.BlockSpec((1, 1, tn), lambda l, n: (l, 0, n)),
        ],
        out_specs=pl.BlockSpec((1, N_COND, tn), lambda l, n: (l, 0, n)),
        out_shape=jax.ShapeDtypeStruct((DEPTH, N_COND, 6 * D_MODEL), F32),
        compiler_params=_cparams(("parallel", "parallel")),
        name="modulation",
    )(conds, w_mod, b_mod.reshape(DEPTH, 1, 6 * D_MODEL))


S5_GB = S5_GROUPS // S5_BLK
S5_IN_ROWS = S5_GB * S5_GROUP
S5_N_BLOCKS = N_EVEN * 2 * S5_BLK


def _block_diag_lanes(x, row_group, col_group):
    xt = jnp.concatenate([x] * S5_GB, axis=1)
    rg = lax.broadcasted_iota(jnp.int32, xt.shape, 0) // row_group
    cg = lax.broadcasted_iota(jnp.int32, xt.shape, 1) // col_group
    return jnp.where(rg == cg, xt, 0.0)


def _s5_param_kernel(lr_ref, li_ref, ldt_ref, br_ref, bi_ref, cr_ref, ci_ref, are_ref, aim_ref, bin_ref, cout_ref):
    lr, li = lr_ref[...], li_ref[...]
    dt = jnp.exp(ldt_ref[...])
    mag = jnp.exp(lr * dt)
    a_re, a_im = mag * jnp.cos(li * dt), mag * jnp.sin(li * dt)
    den = lr * lr + li * li
    f_re = ((a_re - 1.0) * lr + a_im * li) / den
    f_im = (a_im * lr - (a_re - 1.0) * li) / den
    br, bi = br_ref[...], bi_ref[...]
    are_ref[...] = a_re
    aim_ref[...] = a_im
    bin_ref[:, :S5_BLK_W] = _block_diag_lanes(f_re * br - f_im * bi, S5_GROUP, S5_STATE).astype(BF16)
    bin_ref[:, S5_BLK_W:] = _block_diag_lanes(f_re * bi + f_im * br, S5_GROUP, S5_STATE).astype(BF16)
    cout_ref[:S5_BLK_W, :] = _block_diag_lanes(cr_ref[...], S5_STATE, S5_GROUP).astype(BF16)
    cout_ref[S5_BLK_W:, :] = _block_diag_lanes(-ci_ref[...], S5_STATE, S5_GROUP).astype(BF16)


def _s5_params(lam_re, lam_im, log_dt, b_re, b_im, c_re, c_im):
    full = (N_EVEN, 2, S5_GROUPS, S5_GROUP, S5_STATE)
    blk = (S5_N_BLOCKS, S5_IN_ROWS, S5_STATE)
    ex = lambda a: jnp.broadcast_to(a[:, :, :, None, :], full).reshape(blk)
    ldt = jnp.broadcast_to(log_dt[:, :, :, None, None], full).reshape(blk)
    bt = lambda b: jnp.swapaxes(b, -1, -2).reshape(blk)
    ct = lambda c: jnp.swapaxes(c, -1, -2).reshape(S5_N_BLOCKS, S5_BLK_W, S5_GROUP)
    spec = lambda r, w: pl.BlockSpec((None, r, w), lambda i: (i, 0, 0))
    a_re, a_im, b_in, c_out = pl.pallas_call(
        _s5_param_kernel,
        grid=(S5_N_BLOCKS,),
        in_specs=[spec(S5_IN_ROWS, S5_STATE)] * 5 + [spec(S5_BLK_W, S5_GROUP)] * 2,
        out_specs=[spec(S5_IN_ROWS, S5_STATE)] * 2 + [spec(S5_IN_ROWS, 2 * S5_BLK_W), spec(2 * S5_BLK_W, LANES)],
        out_shape=[jax.ShapeDtypeStruct(blk, F32)] * 2
        + [jax.ShapeDtypeStruct((S5_N_BLOCKS, S5_IN_ROWS, 2 * S5_BLK_W), BF16),
           jax.ShapeDtypeStruct((S5_N_BLOCKS, 2 * S5_BLK_W, LANES), BF16)],
        compiler_params=_cparams(("parallel",)),
        name="s5_params",
    )(ex(lam_re), ex(lam_im), ldt, bt(b_re), bt(b_im), ct(c_re), ct(c_im))
    pick = lambda a: a.reshape(full)[:, :, :, 0, :]
    lead = (N_EVEN, 2, S5_BLK)
    return pick(a_re), pick(a_im), b_in.reshape(lead + b_in.shape[1:]), c_out.reshape(lead + c_out.shape[1:])


def _seq_of_chunk(c):
    return (c >= SEQ_CHUNKS).astype(jnp.int32) + (c >= 2 * SEQ_CHUNKS).astype(jnp.int32)


def _s5_scan_kernel(uf_ref, ub_ref, a_ref, bin_ref, cout_ref, h0f_ref, h0b_ref,
                    yf_ref, yb_ref, finf_ref, finb_ref, us_ref, hs_ref, ys_ref, st_ref):
    j = pl.program_id(0)
    cf = j
    cb = N_CHUNKS - 1 - j

    @pl.when((cf == 0) | (cf == SEQ_CHUNKS) | (cf == 2 * SEQ_CHUNKS))
    def _():
        st_ref[0] = h0f_ref[...]

    @pl.when((cb == N_CHUNKS - 1) | (cb == 2 * SEQ_CHUNKS - 1) | (cb == SEQ_CHUNKS - 1))
    def _():
        st_ref[1] = h0b_ref[...]

    dirs = ((0, uf_ref, yf_ref), (1, ub_ref, yb_ref))

    def input_map(k):
        for d, u_ref, _ in dirs:
            for b in range(SUBLANES):
                c0 = b * S5_WIDTH + k * LANES
                us_ref[d, k, pl.ds(b, SCAN_T, stride=SUBLANES), :] = u_ref[:, c0:c0 + LANES]
            hs_ref[d, k] = _dot(us_ref[d, k].astype(BF16), bin_ref[d, k])

    for k in range(S5_BLK):
        lo, hi = k * S5_BLK_W, (k + 1) * S5_BLK_W
        input_map(k)
        coef = [jnp.broadcast_to(a_ref[r:r + 1, lo:hi], (SUBLANES, S5_BLK_W)) for r in range(4)]
        state = [st_ref[0, :, lo:hi], st_ref[0, :, S5_STATE_W + lo:S5_STATE_W + hi],
                 st_ref[1, :, lo:hi], st_ref[1, :, S5_STATE_W + lo:S5_STATE_W + hi]]
        for t in range(SCAN_T):
            for d in range(2):
                rows = pl.ds((t if d == 0 else SCAN_T - 1 - t) * SUBLANES, SUBLANES)
                hr, hi_ = state[2 * d], state[2 * d + 1]
                ar, ai = coef[2 * d], coef[2 * d + 1]
                nr = ar * hr - ai * hi_ + hs_ref[d, k, rows, 0:S5_BLK_W]
                ni = ar * hi_ + ai * hr + hs_ref[d, k, rows, S5_BLK_W:2 * S5_BLK_W]
                hs_ref[d, k, rows, 0:S5_BLK_W] = nr
                hs_ref[d, k, rows, S5_BLK_W:2 * S5_BLK_W] = ni
                state[2 * d], state[2 * d + 1] = nr, ni
        st_ref[0, :, lo:hi] = state[0]
        st_ref[0, :, S5_STATE_W + lo:S5_STATE_W + hi] = state[1]
        st_ref[1, :, lo:hi] = state[2]
        st_ref[1, :, S5_STATE_W + lo:S5_STATE_W + hi] = state[3]

        for d, _, y_ref in dirs:
            ys_ref[d, k] = _dot(hs_ref[d, k].astype(BF16), cout_ref[d, k])
            for b in range(SUBLANES):
                c0 = b * S5_WIDTH + k * LANES
                y_ref[:, c0:c0 + LANES] = ys_ref[d, k, pl.ds(b, SCAN_T, stride=SUBLANES), :]

    finf_ref[...] = st_ref[0]
    finb_ref[...] = st_ref[1]


def _s5_scan(e, u_tm, a, b_in, c_out, h0):
    rev = lambda j: N_CHUNKS - 1 - j
    width = SUBLANES * S5_WIDTH
    st_w = 2 * S5_STATE_W
    row_f = pl.BlockSpec((SCAN_T, width), lambda j: (j, 0))
    row_b = pl.BlockSpec((SCAN_T, width), lambda j: (rev(j), 0))
    h0_f = pl.BlockSpec((None, None, None, SUBLANES, st_w), lambda j: (e, 0, _seq_of_chunk(j), 0, 0))
    h0_b = pl.BlockSpec((None, None, None, SUBLANES, st_w), lambda j: (e, 1, _seq_of_chunk(rev(j)), 0, 0))
    st_f = pl.BlockSpec((None, SUBLANES, st_w), lambda j: (_seq_of_chunk(j), 0, 0))
    st_b = pl.BlockSpec((None, SUBLANES, st_w), lambda j: (_seq_of_chunk(rev(j)), 0, 0))
    st_shape = jax.ShapeDtypeStruct((3, SUBLANES, st_w), F32)
    return pl.pallas_call(
        _s5_scan_kernel,
        grid=(N_CHUNKS,),
        in_specs=[row_f, row_b, _layer_spec(a.shape[1:], e), _layer_spec(b_in.shape[1:], e),
                  _layer_spec(c_out.shape[1:], e), h0_f, h0_b],
        out_specs=[row_f, row_b, st_f, st_b],
        out_shape=[jax.ShapeDtypeStruct((N_TM_ROWS, width), F32)] * 2 + [st_shape] * 2,
        scratch_shapes=[pltpu.VMEM((2, S5_BLK, SCAN_R, LANES), F32),
                        pltpu.VMEM((2, S5_BLK, SCAN_R, 2 * S5_BLK_W), F32),
                        pltpu.VMEM((2, S5_BLK, SCAN_R, LANES), F32),
                        pltpu.VMEM((2, SUBLANES, st_w), F32)],
        compiler_params=_cparams(("arbitrary",)),
        name="s5_scan",
    )(u_tm, u_tm, a, b_in, c_out, h0, h0)


BF16_SUBLANES = 16


def _cast_specs(w_stack, layer, n_steps):
    _, r, c = w_stack.shape
    rc = BF16_SUBLANES
    while r % rc or r // rc > n_steps:
        rc += BF16_SUBLANES
    last = r // rc - 1
    return (pl.BlockSpec((None, rc, c), lambda i: (layer, jnp.minimum(i, last), 0)),
            pl.BlockSpec((rc, c), lambda i: (jnp.minimum(i, last), 0)),
            jax.ShapeDtypeStruct((r, c), BF16))


def _cast_side_job(in_refs, out_refs):
    for wi, wo in zip(in_refs, out_refs):
        wo[...] = wi[...].astype(BF16)


def _even_pre_kernel(l, n_x, n_cast, *refs):
    x_refs, refs = refs[:n_x], refs[n_x:]
    mod_ref, g_ref, w_ref, cos_ref, sa_ref, sb_ref = refs[:6]
    u_ref, q_ref, k_ref, v_ref, kf_ref, vf_ref = refs[6 + n_cast:12 + n_cast]
    _cast_side_job(refs[6:6 + n_cast], refs[12 + n_cast:])
    m = mod_ref[...]
    h = _rms(_read_stream(x_refs), _row(g_ref, l)) * (1.0 + m[1:2]) + m[0:1]
    proj = _dot(h.astype(BF16), w_ref[...])
    cos, sa, sb = cos_ref[...], sa_ref[...], sb_ref[...]
    half = DIFF_HEAD_DIM // 2

    def rope(z):
        return z * cos + pltpu.roll(z, half, 1) * sa + pltpu.roll(z, DIFF_WIDTH - half, 1) * sb

    u_ref[...] = proj[:, :S5_WIDTH]
    k = rope(proj[:, S5_WIDTH + DIFF_WIDTH:S5_WIDTH + 2 * DIFF_WIDTH])
    v = proj[:, S5_WIDTH + 2 * DIFF_WIDTH:]
    q_ref[...] = (rope(proj[:, S5_WIDTH:S5_WIDTH + DIFF_WIDTH]) * (DIFF_HEAD_DIM ** -0.5 * LOG2E)).astype(BF16)
    k_ref[...] = k.astype(BF16)
    v_ref[...] = v.astype(BF16)

    @pl.when(pl.program_id(0) < P_TILES)
    def _():
        kf_ref[...] = k
        vf_ref[...] = v


def _even_pre(l, xs, mod, g_pre, w_in, rope_tabs, cast=()):
    rope_spec = pl.BlockSpec((TM, DIFF_WIDTH), lambda i: (_rope_block(i), 0))
    casts = [_cast_specs(w, lyr, N_TILES) for w, lyr in cast]
    return pl.pallas_call(
        functools.partial(_even_pre_kernel, l, len(xs), len(casts)),
        grid=(N_TILES,),
        in_specs=_stream_specs(xs, D_MODEL) + [
            _mod_spec(l),
            _const_spec(g_pre.shape),
            _layer_spec((D_MODEL, EVEN_IN), l // 2),
            rope_spec, rope_spec, rope_spec,
        ] + [c[0] for c in casts],
        out_specs=[
            pl.BlockSpec((TM, S5_WIDTH), lambda i: (_tm_row_block(i), _tm_col_block(i))),
            _tok_spec(DIFF_WIDTH), _tok_spec(DIFF_WIDTH), _tok_spec(DIFF_WIDTH),
            _ctx_spec(DIFF_WIDTH), _ctx_spec(DIFF_WIDTH),
        ] + [c[1] for c in casts],
        out_shape=[jax.ShapeDtypeStruct((N_TM_ROWS, SUBLANES * S5_WIDTH), F32)]
        + [jax.ShapeDtypeStruct((N_TOK, DIFF_WIDTH), BF16)] * 3
        + [jax.ShapeDtypeStruct((N_P, DIFF_WIDTH), F32)] * 2 + [c[2] for c in casts],
        compiler_params=_cparams(("arbitrary",)),
        name="even_pre",
    )(*xs, mod, g_pre, w_in, *rope_tabs, *[w for w, _ in cast])


def _softmax2_parts(parts):
    m = functools.reduce(jnp.maximum, [jnp.max(s, axis=-1, keepdims=True) for s in parts])
    ps = [jnp.exp2(s - m) for s in parts]
    l = functools.reduce(jnp.add, [jnp.sum(p, axis=-1, keepdims=True) for p in ps])
    return ps, l


def _diff_attn_kernel(e, lam_init, has_ctx, *refs):
    if has_ctx:
        q_ref, k_ref, v_ref, kc_ref, vc_ref, lq1, lk1, lq2, lk2, g_ref, o_ref = refs
    else:
        q_ref, k_ref, v_ref, lq1, lk1, lq2, lk2, g_ref, o_ref = refs
    lam = (jnp.exp(jnp.sum(_row(lq1, e) * _row(lk1, e), axis=-1, keepdims=True))
           - jnp.exp(jnp.sum(_row(lq2, e) * _row(lk2, e), axis=-1, keepdims=True)) + lam_init)
    first = lax.broadcasted_iota(jnp.int32, (1, 2 * DIFF_HEAD_DIM), 1) < DIFF_HEAD_DIM
    g = _row(g_ref, e)
    def head_slice(h):
        return slice(h * 2 * DIFF_HEAD_DIM, (h + 1) * 2 * DIFF_HEAD_DIM)

    def scores(h):
        sl = head_slice(h)
        qh = q_ref[:, sl]
        zero = jnp.zeros_like(qh)
        keys = ([kc_ref[:, sl]] if has_ctx else []) + [k_ref[:, sl]]
        return [[_dot_nt(qc, kk) for kk in keys] for qc in (jnp.where(first, qh, zero), jnp.where(first, zero, qh))]

    s_next = scores(0)
    for h in range(DIFF_HEADS):
        sl = head_slice(h)
        s1, s2 = s_next
        if h + 1 < DIFF_HEADS:
            s_next = scores(h + 1)
        vals = ([vc_ref[:, sl]] if has_ctx else []) + [v_ref[:, sl]]
        p1, l1 = _softmax2_parts(s1)
        p2, l2 = _softmax2_parts(s2)
        ratio = lam * l1 / l2
        o = functools.reduce(jnp.add, [_dot((a - b * ratio).astype(BF16), vv) for a, b, vv in zip(p1, p2, vals)])
        o = o * (1.0 / l1)
        o_ref[:, sl] = (_rms(o, g) * (1.0 - lam_init)).astype(o_ref.dtype)


def _diff_attention(e, q, k, v, ctx, lam_vecs, subln_g, lam_init, *, batch, n, row0, tq):
    nq = n // tq
    has_ctx = ctx is not None
    in_specs = [
        pl.BlockSpec((tq, DIFF_WIDTH), lambda b, i: (row0 // tq + b * nq + i, 0)),
        pl.BlockSpec((n, DIFF_WIDTH), lambda b, i: (row0 // n + b, 0)),
        pl.BlockSpec((n, DIFF_WIDTH), lambda b, i: (row0 // n + b, 0)),
    ]
    args = [q, k, v]
    if has_ctx:
        in_specs += [pl.BlockSpec((None, None, PAST_LEN, DIFF_WIDTH), lambda b, i: (b, e, 0, 0))] * 2
        args += list(ctx)
    in_specs += [_const_spec(t.shape) for t in lam_vecs] + [_const_spec(subln_g.shape)]
    args += list(lam_vecs) + [subln_g]
    return pl.pallas_call(
        functools.partial(_diff_attn_kernel, e, lam_init, has_ctx),
        grid=(batch, nq),
        in_specs=in_specs,
        out_specs=pl.BlockSpec((tq, DIFF_WIDTH), lambda b, i: (b * nq + i, 0)),
        out_shape=jax.ShapeDtypeStruct((batch * n, DIFF_WIDTH), BF16),
        compiler_params=_cparams(("parallel", "parallel")),
        name="diff_attn_ctx" if has_ctx else "diff_attn",
    )(*args)


def _post_tail(l, x, m, mix, gpm_ref, gpf_ref, gqf_ref, wg_ref, wu_ref, wd_ref):
    x1 = x + m[2:3] * _rms(mix, _row(gpm_ref, l))
    h = (_rms(x1, _row(gpf_ref, l)) * (1.0 + m[4:5]) + m[3:4]).astype(BF16)
    act = jax.nn.silu(_dot(h, wg_ref[...])) * _dot(h, wu_ref[...])
    y = _dot(act.astype(BF16), wd_ref[...])
    return x1 + m[5:6] * _rms(y, _row(gqf_ref, l))


def _write_stream(out_refs, rows, x2, tm=TM):
    if len(out_refs) == 1:
        out_refs[0][rows, :] = x2
    else:
        i = pl.program_id(0)

        @pl.when(i < N_P // tm)
        def _():
            out_refs[0][rows, :] = x2

        @pl.when(i >= N_P // tm)
        def _():
            out_refs[1][rows, :] = x2


def _ffn_specs():
    vec = _const_spec((DEPTH, D_MODEL))
    return [vec, vec, vec, _const_spec((D_MODEL, D_FF)), _const_spec((D_MODEL, D_FF)), _const_spec((D_FF, D_MODEL))]


def _out_stream(split, tm=TM):
    if split:
        return ([_ctx_spec(D_MODEL, tm), _lat_spec(D_MODEL, tm)],
                [jax.ShapeDtypeStruct((N_P, D_MODEL), F32), jax.ShapeDtypeStruct((N_S, D_MODEL), F32)])
    return [_tok_spec(D_MODEL, tm)], [jax.ShapeDtypeStruct((N_TOK, D_MODEL), F32)]


TM_POST = 512
P_POST = N_P // TM_POST


def _even_post_kernel(l, n_x, n_out, n_cast, *refs):
    e = l // 2
    x_refs, refs = refs[:n_x], refs[n_x:]
    (mod_ref, yfp_ref, ybp_ref, up_ref, yfs_ref, ybs_ref, us_ref, d_ref, gw_ref, gb_ref, dap_ref, das_ref,
     wos_ref, wod_ref, gpm_ref, gpf_ref, gqf_ref, wg_ref, wu_ref, wd_ref) = refs[:20]
    m = mod_ref[...]
    d = _row(d_ref, e)
    yp = yfp_ref[...] + ybp_ref[...] + jnp.concatenate([d, d], axis=1) * up_ref[...]
    ys = yfs_ref[...] + ybs_ref[...] + d * us_ref[...]
    y = jnp.where(pl.program_id(0) < P_POST,
                  jnp.concatenate([yp[:, :S5_WIDTH], yp[:, S5_WIDTH:]], axis=0), ys)
    g = jax.nn.gelu(y)
    s5 = g * jax.nn.sigmoid(_dot(g.astype(BF16), gw_ref[...]) + _row(gb_ref, e))
    mix = _dot(s5.astype(BF16), wos_ref[...]) + _dot(_read_stream((dap_ref, das_ref), TM_POST), wod_ref[...])
    x2 = _post_tail(l, _read_stream(x_refs, TM_POST), m, mix, gpm_ref, gpf_ref, gqf_ref, wg_ref, wu_ref, wd_ref)
    n_in = 20 + n_cast
    _write_stream(refs[n_in:n_in + n_out], slice(None), x2, TM_POST)
    _cast_side_job(refs[20:n_in], refs[n_in + n_out:])


def _even_post(l, xs, mod, y_f, y_b, u_tm, s5_d, glu_w, glu_b, da_p, da_s, w_out, ffn_args, split_out, cast=()):
    e = l // 2
    tm = TM_POST
    halves = DEC_SEQ // tm
    ctx_view = pl.BlockSpec((SEQ, 2 * S5_WIDTH),
                            lambda i: (jnp.minimum(i, P_POST - 1) // 4, jnp.minimum(i, P_POST - 1) % 4))
    lat_view = pl.BlockSpec((tm, S5_WIDTH),
                            lambda i: (2 * SEQ // tm + jnp.maximum(i - P_POST, 0) % halves,
                                       jnp.maximum(i - P_POST, 0) // halves))
    out_specs, out_shape = _out_stream(split_out, tm)
    casts = [_cast_specs(w, lyr, N_TOK // tm) for w, lyr in cast]
    w_half = lambda r: pl.BlockSpec((None, S5_WIDTH, D_MODEL), lambda i: (e, r, 0), pipeline_mode=pl.Buffered(1))
    return pl.pallas_call(
        functools.partial(_even_post_kernel, l, len(xs), len(out_specs), len(casts)),
        grid=(N_TOK // tm,),
        in_specs=_stream_specs(xs, D_MODEL, tm) + [
            _mod_spec(l, tm),
            ctx_view, ctx_view, ctx_view, lat_view, lat_view, lat_view,
            _const_spec(s5_d.shape), _layer_spec((S5_WIDTH, S5_WIDTH), e), _const_spec(glu_b.shape),
            _ctx_spec(DIFF_WIDTH, tm), _lat_spec(DIFF_WIDTH, tm),
            w_half(0), w_half(1),
        ] + _ffn_specs() + [c[0] for c in casts],
        out_specs=out_specs + [c[1] for c in casts],
        out_shape=out_shape + [c[2] for c in casts],
        compiler_params=_cparams(("arbitrary",), VMEM_LIMIT_POST),
        name="even_post_ffn",
    )(*xs, mod, y_f, y_b, u_tm, y_f, y_b, u_tm, s5_d, glu_w, glu_b, da_p, da_s, w_out, w_out, *ffn_args,
      *[w for w, _ in cast])


def _odd_pre_kernel(l, n_x, *refs):
    o = l // 2
    x_refs, refs = refs[:n_x], refs[n_x:]
    (mod_ref, g_ref, w_ref, gq_ref, gkv_ref, wq_ref, cos_ref, sin_ref,
     q_ref, ckr_ref, ckv_ref, kr_ref) = refs
    m = mod_ref[...]
    h = _rms(_read_stream(x_refs, TM_ODD), _row(g_ref, l)) * (1.0 + m[1:2]) + m[0:1]
    proj = _dot(h.astype(BF16), w_ref[...])
    cq = _rms(proj[:, :MLA_Q_RANK], _row(gq_ref, o))
    ckv = _rms(proj[:, MLA_Q_RANK:MLA_Q_RANK + MLA_KV_RANK], _row(gkv_ref, o))
    kr0 = MLA_Q_RANK + MLA_KV_RANK
    krp = proj[:, kr0:kr0 + MLA_HEAD_PAD]
    krp_sw = proj[:, kr0 + MLA_HEAD_PAD:]
    cos, sin = cos_ref[...], sin_ref[...]
    q = _dot(cq.astype(BF16), wq_ref[...]) * ((MLA_NOPE + MLA_ROPE) ** -0.5 * LOG2E)
    for hd in range(MLA_HEADS):
        sl = slice(hd * MLA_HEAD_PAD, (hd + 1) * MLA_HEAD_PAD)
        sw = slice(MLA_Q_W + hd * MLA_HEAD_PAD, MLA_Q_W + (hd + 1) * MLA_HEAD_PAD)
        q_ref[:, sl] = (q[:, sl] * cos + q[:, sw] * sin).astype(BF16)
    ckr_ref[:, :MLA_KV_RANK] = ckv.astype(BF16)
    ckr_ref[:, MLA_KV_RANK:] = (krp * cos + krp_sw * sin).astype(BF16)

    @pl.when(pl.program_id(0) < N_P // TM_ODD)
    def _():
        ckv_ref[...] = ckv
        kr_ref[...] = krp


def _odd_pre(l, xs, mod, g_pre, w_in, gq, gkv, wq, rope_tabs):
    o = l // 2
    tm = TM_ODD
    rope_spec = pl.BlockSpec((tm, MLA_HEAD_PAD), lambda i: (_rope_block(i, tm), 0))
    return pl.pallas_call(
        functools.partial(_odd_pre_kernel, l, len(xs)),
        grid=(N_TOK // tm,),
        in_specs=_stream_specs(xs, D_MODEL, tm) + [
            _mod_spec(l, tm),
            _const_spec(g_pre.shape),
            _layer_spec((D_MODEL, MLA_IN_W), o),
            _const_spec(gq.shape), _const_spec(gkv.shape),
            _layer_spec((MLA_Q_RANK, 2 * MLA_Q_W), o),
            rope_spec, rope_spec,
        ],
        out_specs=[_tok_spec(MLA_Q_W, tm), _tok_spec(MLA_CKR_W, tm), _ctx_spec(MLA_KV_RANK, tm),
                   _ctx_spec(MLA_HEAD_PAD, tm)],
        out_shape=[
            jax.ShapeDtypeStruct((N_TOK, MLA_Q_W), BF16),
            jax.ShapeDtypeStruct((N_TOK, MLA_CKR_W), BF16),
            jax.ShapeDtypeStruct((N_P, MLA_KV_RANK), F32),
            jax.ShapeDtypeStruct((N_P, MLA_HEAD_PAD), F32),
        ],
        compiler_params=_cparams(("arbitrary",)),
        name="odd_pre",
    )(*xs, mod, g_pre, w_in, gq, gkv, wq, *rope_tabs)


def _mla_attn_kernel(has_ctx, *refs):
    if has_ctx:
        q_ref, ckr_ref, ckrc_ref, wk_ref, wv_ref, o_ref, kf_ref, vf_ref = refs
    else:
        q_ref, ckr_ref, wk_ref, wv_ref, o_ref, kf_ref, vf_ref = refs
    off = PAST_LEN if has_ctx else 0

    @pl.when(pl.program_id(1) == 0)
    def _():
        srcs = [(0, ckrc_ref)] if has_ctx else []
        srcs.append((off, ckr_ref))
        wv = wv_ref[...] if has_ctx else wv_ref[:, :MLA_O_W] + wv_ref[:, MLA_O_W:]
        for r0, src in srcs:
            c = src[...]
            rows = pl.ds(r0, c.shape[0])
            kf_ref[rows, :] = _dot(c, wk_ref[...]).astype(BF16)
            vf_ref[rows, :] = _dot(c[:, :MLA_KV_RANK], wv).astype(BF16)

    low = lax.broadcasted_iota(jnp.int32, (1, LANES), 1) < MLA_V
    def scores(hd):
        cs = slice(hd * MLA_HEAD_PAD, (hd + 1) * MLA_HEAD_PAD)
        return _dot_nt(q_ref[:, cs], kf_ref[:, cs])

    s_next = scores(0)
    for j in range(MLA_HEADS // 2):
        slot = slice(j * LANES, (j + 1) * LANES)
        pv, rl = [], []
        for t in range(2):
            s = s_next
            if 2 * j + t + 1 < MLA_HEADS:
                s_next = scores(2 * j + t + 1)
            mx = jnp.max(s, axis=-1, keepdims=True)
            p = jnp.exp2(s - mx)
            rl.append(1.0 / jnp.sum(p, axis=-1, keepdims=True))
            v_cols = slice(t * MLA_O_W + j * LANES, t * MLA_O_W + (j + 1) * LANES) if has_ctx else slot
            pv.append(_dot(p.astype(BF16), vf_ref[:, v_cols]))
        if has_ctx:
            o = (pv[0] + pv[1]) * jnp.where(low, rl[0], rl[1])
        else:
            o = jnp.where(low, pv[0] * rl[0], pv[1] * rl[1])
        o_ref[:, slot] = o.astype(o_ref.dtype)


def _mla_attention(o, q, ckr, ckr_ctx, wk, wv, *, batch, n, row0, tq):
    nq = n // tq
    has_ctx = ckr_ctx is not None
    s_len = n + (PAST_LEN if has_ctx else 0)
    in_specs = [
        pl.BlockSpec((tq, MLA_Q_W), lambda b, i: (row0 // tq + b * nq + i, 0)),
        pl.BlockSpec((n, MLA_CKR_W), lambda b, i: (row0 // n + b, 0)),
    ]
    args = [q, ckr]
    if has_ctx:
        in_specs.append(pl.BlockSpec((None, None, PAST_LEN, MLA_CKR_W), lambda b, i: (b, o, 0, 0)))
        args.append(ckr_ctx)
    in_specs += [_layer_spec(wk.shape[1:], o), _layer_spec(wv.shape[1:], o)]
    args += [wk, wv]
    return pl.pallas_call(
        functools.partial(_mla_attn_kernel, has_ctx),
        grid=(batch, nq),
        in_specs=in_specs,
        out_specs=pl.BlockSpec((tq, MLA_O_W), lambda b, i: (b * nq + i, 0)),
        out_shape=jax.ShapeDtypeStruct((batch * n, MLA_O_W), BF16),
        scratch_shapes=[pltpu.VMEM((s_len, MLA_Q_W), BF16),
                        pltpu.VMEM((s_len, (2 if has_ctx else 1) * MLA_O_W), BF16)],
        compiler_params=_cparams(("parallel", "arbitrary")),
        name="mla_attn_ctx" if has_ctx else "mla_attn",
    )(*args)


def _odd_post_kernel(l, n_x, n_out, n_cast, *refs):
    x_refs, refs = refs[:n_x], refs[n_x:]
    mod_ref, ap_ref, as_ref, wo_ref, gpm_ref, gpf_ref, gqf_ref, wg_ref, wu_ref, wd_ref = refs[:10]
    mix = _dot(_read_stream((ap_ref, as_ref), TM_ODD), wo_ref[...])
    x2 = _post_tail(l, _read_stream(x_refs, TM_ODD), mod_ref[...], mix, gpm_ref, gpf_ref, gqf_ref,
                    wg_ref, wu_ref, wd_ref)
    n_in = 10 + n_cast
    _write_stream(refs[n_in:n_in + n_out], slice(None), x2, TM_ODD)
    _cast_side_job(refs[10:n_in], refs[n_in + n_out:])


def _odd_post(l, xs, mod, at_p, at_s, w_out, ffn_args, split_out, cast=()):
    tm = TM_ODD
    out_specs, out_shape = _out_stream(split_out, tm)
    casts = [_cast_specs(w, lyr, N_TOK // tm) for w, lyr in cast]
    return pl.pallas_call(
        functools.partial(_odd_post_kernel, l, len(xs), len(out_specs), len(casts)),
        grid=(N_TOK // tm,),
        in_specs=_stream_specs(xs, D_MODEL, tm) + [
            _mod_spec(l, tm),
            _ctx_spec(MLA_O_W, tm), _lat_spec(MLA_O_W, tm),
            _layer_spec((MLA_O_W, D_MODEL), l // 2),
        ] + _ffn_specs() + [c[0] for c in casts],
        out_specs=out_specs + [c[1] for c in casts],
        out_shape=out_shape + [c[2] for c in casts],
        compiler_params=_cparams(("arbitrary",)),
        name="odd_post_ffn",
    )(*xs, mod, at_p, at_s, w_out, *ffn_args, *[w for w, _ in cast])


def _rope_angles(rot_dim):
    rows = DEC_SEQ // GRID_W
    row = jnp.repeat(jnp.arange(rows, dtype=F32), GRID_W)
    col = jnp.tile(jnp.arange(GRID_W, dtype=F32), rows)
    n_freq = rot_dim // 4
    inv = ROPE_BASE ** (-jnp.arange(n_freq, dtype=F32) / n_freq)
    ang = jnp.concatenate([row[:, None] * inv, col[:, None] * inv], axis=-1)
    return jnp.cos(ang), jnp.sin(ang)


def _with_identity(tm, cos, *sins):
    one = jnp.ones((tm, cos.shape[1]), F32)
    zero = jnp.zeros((tm, cos.shape[1]), F32)
    return (jnp.concatenate([one, cos]),) + tuple(jnp.concatenate([zero, s]) for s in sins)


def _diff_rope_tables():
    c, s = _rope_angles(DIFF_HEAD_DIM)
    z = jnp.zeros_like(s)
    reps = DIFF_WIDTH // DIFF_HEAD_DIM
    cos = jnp.tile(jnp.concatenate([c, c], axis=1), (1, reps))
    sa = jnp.tile(jnp.concatenate([z, s], axis=1), (1, reps))
    sb = jnp.tile(jnp.concatenate([-s, z], axis=1), (1, reps))
    return _with_identity(TM, cos, sa, sb)


def _mla_rope_tables():
    c, s = _rope_angles(MLA_ROPE)
    n = c.shape[0]
    cos = jnp.concatenate([jnp.ones((n, MLA_NOPE), F32), c, c, jnp.ones((n, MLA_PAD), F32)], axis=1)
    sin = jnp.concatenate([jnp.zeros((n, MLA_NOPE), F32), -s, s, jnp.zeros((n, MLA_PAD), F32)], axis=1)
    return _with_identity(TM_ODD, cos, sin)


def _swap_rope_halves(w):
    half = MLA_ROPE // 2
    return jnp.concatenate([w[..., half:], w[..., :half]], axis=-1)


def _mla_weights(w_in_odd, w_q_up, w_kv_up):
    zeros = lambda *s: jnp.zeros(s, F32)
    kr0 = MLA_Q_RANK + MLA_KV_RANK
    w_kr = w_in_odd[:, :, kr0:]
    lead = (N_ODD, D_MODEL)
    w_in = jnp.concatenate(
        [w_in_odd[:, :, :kr0],
         zeros(*lead, MLA_NOPE), w_kr, zeros(*lead, MLA_PAD),
         zeros(*lead, MLA_NOPE), _swap_rope_halves(w_kr), zeros(*lead, MLA_PAD)], axis=-1).astype(BF16)
    wq4 = w_q_up.reshape(N_ODD, MLA_Q_RANK, MLA_HEADS, MLA_NOPE + MLA_ROPE)
    lead = (N_ODD, MLA_Q_RANK, MLA_HEADS)
    wq_main = jnp.concatenate([wq4, zeros(*lead, MLA_PAD)], axis=-1)
    wq_swap = jnp.concatenate([zeros(*lead, MLA_NOPE), _swap_rope_halves(wq4[..., MLA_NOPE:]),
                               zeros(*lead, MLA_PAD)], axis=-1)
    wq = jnp.concatenate([wq_main.reshape(N_ODD, MLA_Q_RANK, MLA_Q_W),
                          wq_swap.reshape(N_ODD, MLA_Q_RANK, MLA_Q_W)], axis=-1).astype(BF16)
    wkv = w_kv_up.reshape(N_ODD, MLA_KV_RANK, MLA_HEADS, MLA_NOPE + MLA_V)
    wk_top = jnp.pad(wkv[..., :MLA_NOPE], ((0, 0), (0, 0), (0, 0), (0, MLA_HEAD_PAD - MLA_NOPE)))
    sel = jnp.pad(jnp.eye(MLA_ROPE, dtype=F32), ((MLA_NOPE, MLA_PAD), (MLA_NOPE, MLA_PAD)))
    wk_bot = jnp.broadcast_to(jnp.tile(sel, (1, MLA_HEADS)), (N_ODD, MLA_HEAD_PAD, MLA_Q_W))
    wk = jnp.concatenate([wk_top.reshape(N_ODD, MLA_KV_RANK, MLA_Q_W), wk_bot], axis=1).astype(BF16)
    even = (jnp.arange(MLA_HEADS) % 2 == 0)[:, None]
    wv4 = wkv[..., MLA_NOPE:]
    wv = jnp.concatenate([jnp.where(even, wv4, 0.0).reshape(N_ODD, MLA_KV_RANK, MLA_O_W),
                          jnp.where(even, 0.0, wv4).reshape(N_ODD, MLA_KV_RANK, MLA_O_W)], axis=-1).astype(BF16)
    return w_in, wq, wk, wv


def kernel(x_prompt, x_sample, state_s5_re, state_s5_im, cache_diff_k, cache_diff_v, cache_mla_ckv, cache_mla_krope, c, c_ctx, w_mod, b_mod, g_pre_mix, g_post_mix, g_pre_ffn, g_post_ffn, w_ffn_gate, w_ffn_up, w_ffn_down, w_in_even, w_out_even, s5_lam_re, s5_lam_im, s5_log_dt, s5_b_re, s5_b_im, s5_c_re, s5_c_im, s5_d, s5_glu_w, s5_glu_b, diff_lam_q1, diff_lam_k1, diff_lam_q2, diff_lam_k2, diff_subln_g, w_in_odd, mla_q_norm_g, mla_w_q_up, mla_kv_norm_g, mla_w_kv_up, w_out_odd):
    xs = (x_prompt.reshape(N_P, D_MODEL), x_sample.reshape(N_S, D_MODEL))
    conds = jnp.concatenate([c_ctx[None, :], c, jnp.zeros((N_COND - 1 - DEC_BATCH, D_MODEL), F32)], axis=0)
    mod = _modulation(conds, w_mod, b_mod).reshape(DEPTH, N_COND, 6, D_MODEL)

    gains = (g_post_mix, g_pre_ffn, g_post_ffn)
    ffn_f32 = (w_ffn_gate, w_ffn_up, w_ffn_down)
    w_in_e = w_in_even.astype(BF16)
    w_out_e = w_out_even.astype(BF16)
    glu_w = s5_glu_w.astype(BF16)
    a_re, a_im, b_in, c_out = _s5_params(s5_lam_re, s5_lam_im, s5_log_dt, s5_b_re, s5_b_im, s5_c_re, s5_c_im)
    a = jnp.stack([a_re[:, 0], a_im[:, 0], a_re[:, 1], a_im[:, 1]], axis=1).reshape(N_EVEN, 4, S5_STATE_W)
    st = lambda s: jnp.moveaxis(s, 0, 2).reshape(N_EVEN, 2, 1, DEC_BATCH, S5_STATE_W)
    h0 = jnp.concatenate([jnp.zeros((N_EVEN, 2, 2, SUBLANES, 2 * S5_STATE_W), F32),
                          jnp.concatenate([st(state_s5_re), st(state_s5_im)], axis=-1)], axis=2)
    diff_ctx = (cache_diff_k.astype(BF16).reshape(DEC_BATCH, N_EVEN, PAST_LEN, DIFF_WIDTH),
                cache_diff_v.astype(BF16).reshape(DEC_BATCH, N_EVEN, PAST_LEN, DIFF_WIDTH))
    lam_vecs = (diff_lam_q1, diff_lam_k1, diff_lam_q2, diff_lam_k2)
    w_in_o, wq, wk, wv = _mla_weights(w_in_odd, mla_w_q_up, mla_w_kv_up)
    w_out_o = w_out_odd.astype(BF16)
    ckr_ctx = jnp.concatenate(
        [cache_mla_ckv, jnp.pad(cache_mla_krope, ((0, 0), (0, 0), (0, 0), (MLA_NOPE, MLA_PAD)))], axis=-1).astype(BF16)
    diff_tabs = _diff_rope_tables()
    mla_tabs = _mla_rope_tables()

    s5_re_list, s5_im_list, dk_list, dv_list, ckv_list, kr_list = [], [], [], [], [], []
    for l in range(DEPTH):
        last = l == DEPTH - 1
        n_stream = 2 if last else 1
        next_cast = () if last else tuple((w, l + 1) for w in ffn_f32)
        if l % 2 == 0:
            e = l // 2
            lam_init = 0.8 - 0.6 * math.exp(-0.3 * l)
            pre = _even_pre(l, xs, mod, g_pre_mix, w_in_e, diff_tabs,
                            cast=tuple((w, 0) for w in ffn_f32) if l == 0 else ())
            u_tm, q, k, v, kf, vf = pre[:6]
            if l == 0:
                ffn_w = tuple(pre[6:])
            y_f, y_b, fin_f, fin_b = _s5_scan(e, u_tm, a, b_in, c_out, h0)
            fin = jnp.stack([fin_f[:2].reshape(BATCH, 2, S5_GROUPS, S5_STATE),
                             fin_b[:2].reshape(BATCH, 2, S5_GROUPS, S5_STATE)], axis=1)
            s5_re_list.append(fin[:, :, 0])
            s5_im_list.append(fin[:, :, 1])
            da_p = _diff_attention(e, q, k, v, None, lam_vecs, diff_subln_g, lam_init,
                                   batch=BATCH, n=SEQ, row0=0, tq=SEQ)
            da_s = _diff_attention(e, q, k, v, diff_ctx, lam_vecs, diff_subln_g, lam_init,
                                   batch=DEC_BATCH, n=DEC_SEQ, row0=N_P, tq=256)
            post = _even_post(l, xs, mod, y_f, y_b, u_tm, s5_d, glu_w, s5_glu_b, da_p, da_s, w_out_e,
                              gains + ffn_w, last, cast=next_cast)
            xs, ffn_w = post[:n_stream], tuple(post[n_stream:])
            dk_list.append(kf.reshape(BATCH, SEQ, DIFF_HEADS, 2, DIFF_HEAD_DIM))
            dv_list.append(vf.reshape(BATCH, SEQ, DIFF_HEADS, 2 * DIFF_HEAD_DIM))
        else:
            o = l // 2
            q, ckr, ckv, krp = _odd_pre(l, xs, mod, g_pre_mix, w_in_o, mla_q_norm_g, mla_kv_norm_g, wq, mla_tabs)
            at_p = _mla_attention(o, q, ckr, None, wk, wv, batch=BATCH, n=SEQ, row0=0, tq=SEQ)
            at_s = _mla_attention(o, q, ckr, ckr_ctx, wk, wv, batch=DEC_BATCH, n=DEC_SEQ, row0=N_P, tq=256)
            post = _odd_post(l, xs, mod, at_p, at_s, w_out_o, gains + ffn_w, last, cast=next_cast)
            xs, ffn_w = post[:n_stream], tuple(post[n_stream:])
            ckv_list.append(ckv.reshape(BATCH, SEQ, MLA_KV_RANK))
            kr_list.append(krp[:, MLA_NOPE:MLA_NOPE + MLA_ROPE].reshape(BATCH, SEQ, MLA_ROPE))

    return (xs[0].reshape(BATCH, SEQ, D_MODEL), xs[1].reshape(DEC_BATCH, DEC_SEQ, D_MODEL),
            jnp.stack(s5_re_list, axis=1), jnp.stack(s5_im_list, axis=1),
            jnp.stack(dk_list, axis=1), jnp.stack(dv_list, axis=1),
            jnp.stack(ckv_list, axis=1), jnp.stack(kr_list, axis=1))
```

```python
import functools
import math

import jax
import jax.numpy as jnp
from jax import lax
from jax.experimental import pallas as pl
from jax.experimental.pallas import tpu as pltpu

F32 = jnp.float32
BF16 = jnp.bfloat16

D_MODEL = 1024
BATCH = 16
SEQ = 256
DEPTH = 4
DEC_BATCH = 8
DEC_SEQ = 1024
PAST_LEN = 512
GRID_W = 64
N_EVEN = (DEPTH + 1) // 2
N_ODD = DEPTH // 2
EPS = 1e-6
ROPE_BASE = 10000.0
S5_WIDTH = D_MODEL // 2
S5_GROUP = 16
S5_GROUPS = S5_WIDTH // S5_GROUP
S5_STATE = 64
DIFF_HEAD_DIM = 64
DIFF_HEADS = (D_MODEL // 2) // (2 * DIFF_HEAD_DIM)
DIFF_WIDTH = DIFF_HEADS * 2 * DIFF_HEAD_DIM
EVEN_IN = S5_WIDTH + 3 * DIFF_WIDTH
MLA_HEADS = 16
MLA_NOPE = 64
MLA_ROPE = 32
MLA_V = 64
MLA_Q_RANK = 256
MLA_KV_RANK = 128
D_FF = ((8 * D_MODEL // 3 + 255) // 256) * 256

LANES = 128
SUBLANES = 8
VMEM_LIMIT = 56 * 1024 * 1024
VMEM_LIMIT_POST = 61 * 1024 * 1024
LOG2E = math.log2(math.e)

N_P = BATCH * SEQ
N_S = DEC_BATCH * DEC_SEQ
N_TOK = N_P + N_S
TM = 256
TM_ODD = 512
N_TILES = N_TOK // TM
P_TILES = N_P // TM
S_TILES_PER_B = DEC_SEQ // TM
N_COND = 16

S5_STATE_W = S5_GROUPS * S5_STATE
S5_BLK = 4
S5_BLK_W = S5_STATE_W // S5_BLK
SCAN_T = 64
SCAN_R = SCAN_T * SUBLANES
N_TM_ROWS = N_TOK // SUBLANES
N_CHUNKS = N_TM_ROWS // SCAN_T
SEQ_CHUNKS = SEQ // SCAN_T

MLA_HEAD_PAD = LANES
MLA_PAD = MLA_HEAD_PAD - MLA_NOPE - MLA_ROPE
MLA_Q_W = MLA_HEADS * MLA_HEAD_PAD
MLA_IN_W = MLA_Q_RANK + MLA_KV_RANK + 2 * MLA_HEAD_PAD
MLA_CKR_W = MLA_KV_RANK + MLA_HEAD_PAD
MLA_O_W = MLA_HEADS * MLA_V


def _cparams(sem, vmem_limit=VMEM_LIMIT):
    return pltpu.CompilerParams(dimension_semantics=sem, vmem_limit_bytes=vmem_limit)


def _const_spec(shape):
    nd = len(shape)
    return pl.BlockSpec(shape, lambda *_: (0,) * nd, pipeline_mode=pl.Buffered(1))


def _layer_spec(tail, *lead):
    nt = len(tail)
    return pl.BlockSpec((None,) * len(lead) + tuple(tail), lambda *_: tuple(lead) + (0,) * nt,
                        pipeline_mode=pl.Buffered(1))


def _rms(x, g):
    return x * lax.rsqrt(jnp.mean(x * x, axis=-1, keepdims=True) + EPS) * g


def _dot(a, b):
    return jnp.dot(a, b, preferred_element_type=F32)


def _dot_nt(a, b):
    return lax.dot_general(a, b, (((1,), (1,)), ((), ())), preferred_element_type=F32)


def _cond_of_tile(i, tm=TM):
    return jnp.where(i < N_P // tm, 0, 1 + (i - N_P // tm) // (DEC_SEQ // tm))


def _tm_row_block(i):
    return jnp.where(i < P_TILES, i // SUBLANES, 2 + (i - P_TILES) % S_TILES_PER_B)


def _tm_col_block(i):
    return jnp.where(i < P_TILES, i % SUBLANES, (i - P_TILES) // S_TILES_PER_B)


def _rope_block(i, tm=TM):
    return jnp.where(i < N_P // tm, 0, 1 + (i - N_P // tm) % (DEC_SEQ // tm))


def _tok_spec(width, tm=TM):
    return pl.BlockSpec((tm, width), lambda i: (i, 0))


def _ctx_spec(width, tm=TM):
    return pl.BlockSpec((tm, width), lambda i: (jnp.minimum(i, N_P // tm - 1), 0))


def _lat_spec(width, tm=TM):
    return pl.BlockSpec((tm, width), lambda i: (jnp.maximum(i - N_P // tm, 0), 0))


def _stream_specs(arrays, width, tm=TM):
    return [_tok_spec(width, tm)] if len(arrays) == 1 else [_ctx_spec(width, tm), _lat_spec(width, tm)]


def _read_stream(refs, tm=TM):
    if len(refs) == 1:
        return refs[0][...]
    return jnp.where(pl.program_id(0) < N_P // tm, refs[0][...], refs[1][...])


def _mod_spec(l, tm=TM):
    return pl.BlockSpec((None, None, 6, D_MODEL), lambda i: (l, _cond_of_tile(i, tm), 0, 0))


def _row(ref, r):
    return ref[r:r + 1, :]


def _mod_kernel(c_ref, w_ref, b_ref, o_ref):
    s = jax.nn.silu(c_ref[...])
    o_ref[0] = _dot(s.astype(BF16), w_ref[0].astype(BF16)) + b_ref[0]


def _modulation(conds, w_mod, b_mod):
    tn = 1536
    return pl.pallas_call(
        _mod_kernel,
        grid=(DEPTH, 6 * D_MODEL // tn),
        in_specs=[
            pl.BlockSpec((N_COND, D_MODEL), lambda l, n: (0, 0)),
            pl.BlockSpec((1, D_MODEL, tn), lambda l, n: (l, 0, n)),
            pl.BlockSpec((1, 1, tn), lambda l, n: (l, 0, n)),
        ],
        out_specs=pl.BlockSpec((1, N_COND, tn), lambda l, n: (l, 0, n)),
        out_shape=jax.ShapeDtypeStruct((DEPTH, N_COND, 6 * D_MODEL), F32),
        compiler_params=_cparams(("parallel", "parallel")),
        name="modulation",
    )(conds, w_mod, b_mod.reshape(DEPTH, 1, 6 * D_MODEL))


S5_GB = S5_GROUPS // S5_BLK
S5_IN_ROWS = S5_GB * S5_GROUP
S5_N_BLOCKS = N_EVEN * 2 * S5_BLK


def _block_diag_lanes(x, row_group, col_group):
    xt = jnp.concatenate([x] * S5_GB, axis=1)
    rg = lax.broadcasted_iota(jnp.int32, xt.shape, 0) // row_group
    cg = lax.broadcasted_iota(jnp.int32, xt.shape, 1) // col_group
    return jnp.where(rg == cg, xt, 0.0)


def _s5_param_kernel(lr_ref, li_ref, ldt_ref, br_ref, bi_ref, cr_ref, ci_ref, are_ref, aim_ref, bin_ref, cout_ref):
    lr, li = lr_ref[...], li_ref[...]
    dt = jnp.exp(ldt_ref[...])
    mag = jnp.exp(lr * dt)
    a_re, a_im = mag * jnp.cos(li * dt), mag * jnp.sin(li * dt)
    den = lr * lr + li * li
    f_re = ((a_re - 1.0) * lr + a_im * li) / den
    f_im = (a_im * lr - (a_re - 1.0) * li) / den
    br, bi = br_ref[...], bi_ref[...]
    are_ref[...] = a_re
    aim_ref[...] = a_im
    bin_ref[:, :S5_BLK_W] = _block_diag_lanes(f_re * br - f_im * bi, S5_GROUP, S5_STATE).astype(BF16)
    bin_ref[:, S5_BLK_W:] = _block_diag_lanes(f_re * bi + f_im * br, S5_GROUP, S5_STATE).astype(BF16)
    cout_ref[:S5_BLK_W, :] = _block_diag_lanes(cr_ref[...], S5_STATE, S5_GROUP).astype(BF16)
    cout_ref[S5_BLK_W:, :] = _block_diag_lanes(-ci_ref[...], S5_STATE, S5_GROUP).astype(BF16)


def _s5_params(lam_re, lam_im, log_dt, b_re, b_im, c_re, c_im):
    full = (N_EVEN, 2, S5_GROUPS, S5_GROUP, S5_STATE)
    blk = (S5_N_BLOCKS, S5_IN_ROWS, S5_STATE)
    ex = lambda a: jnp.broadcast_to(a[:, :, :, None, :], full).reshape(blk)
    ldt = jnp.broadcast_to(log_dt[:, :, :, None, None], full).reshape(blk)
    bt = lambda b: jnp.swapaxes(b, -1, -2).reshape(blk)
    ct = lambda c: jnp.swapaxes(c, -1, -2).reshape(S5_N_BLOCKS, S5_BLK_W, S5_GROUP)
    spec = lambda r, w: pl.BlockSpec((None, r, w), lambda i: (i, 0, 0))
    a_re, a_im, b_in, c_out = pl.pallas_call(
        _s5_param_kernel,
        grid=(S5_N_BLOCKS,),
        in_specs=[spec(S5_IN_ROWS, S5_STATE)] * 5 + [spec(S5_BLK_W, S5_GROUP)] * 2,
        out_specs=[spec(S5_IN_ROWS, S5_STATE)] * 2 + [spec(S5_IN_ROWS, 2 * S5_BLK_W), spec(2 * S5_BLK_W, LANES)],
        out_shape=[jax.ShapeDtypeStruct(blk, F32)] * 2
        + [jax.ShapeDtypeStruct((S5_N_BLOCKS, S5_IN_ROWS, 2 * S5_BLK_W), BF16),
           jax.ShapeDtypeStruct((S5_N_BLOCKS, 2 * S5_BLK_W, LANES), BF16)],
        compiler_params=_cparams(("parallel",)),
        name="s5_params",
    )(ex(lam_re), ex(lam_im), ldt, bt(b_re), bt(b_im), ct(c_re), ct(c_im))
    pick = lambda a: a.reshape(full)[:, :, :, 0, :]
    lead = (N_EVEN, 2, S5_BLK)
    return pick(a_re), pick(a_im), b_in.reshape(lead + b_in.shape[1:]), c_out.reshape(lead + c_out.shape[1:])


def _seq_of_chunk(c):
    return (c >= SEQ_CHUNKS).astype(jnp.int32) + (c >= 2 * SEQ_CHUNKS).astype(jnp.int32)


def _s5_scan_kernel(uf_ref, ub_ref, a_ref, bin_ref, cout_ref, h0f_ref, h0b_ref,
                    yf_ref, yb_ref, finf_ref, finb_ref, us_ref, hs_ref, ys_ref, st_ref):
    j = pl.program_id(0)
    cf = j
    cb = N_CHUNKS - 1 - j

    @pl.when((cf == 0) | (cf == SEQ_CHUNKS) | (cf == 2 * SEQ_CHUNKS))
    def _():
        st_ref[0] = h0f_ref[...]

    @pl.when((cb == N_CHUNKS - 1) | (cb == 2 * SEQ_CHUNKS - 1) | (cb == SEQ_CHUNKS - 1))
    def _():
        st_ref[1] = h0b_ref[...]

    dirs = ((0, uf_ref, yf_ref), (1, ub_ref, yb_ref))

    def input_map(k):
        for d, u_ref, _ in dirs:
            for b in range(SUBLANES):
                c0 = b * S5_WIDTH + k * LANES
                us_ref[d, k, pl.ds(b, SCAN_T, stride=SUBLANES), :] = u_ref[:, c0:c0 + LANES]
            hs_ref[d, k] = _dot(us_ref[d, k].astype(BF16), bin_ref[d, k])

    for k in range(S5_BLK):
        lo, hi = k * S5_BLK_W, (k + 1) * S5_BLK_W
        input_map(k)
        coef = [jnp.broadcast_to(a_ref[r:r + 1, lo:hi], (SUBLANES, S5_BLK_W)) for r in range(4)]
        state = [st_ref[0, :, lo:hi], st_ref[0, :, S5_STATE_W + lo:S5_STATE_W + hi],
                 st_ref[1, :, lo:hi], st_ref[1, :, S5_STATE_W + lo:S5_STATE_W + hi]]
        for t in range(SCAN_T):
            for d in range(2):
                rows = pl.ds((t if d == 0 else SCAN_T - 1 - t) * SUBLANES, SUBLANES)
                hr, hi_ = state[2 * d], state[2 * d + 1]
                ar, ai = coef[2 * d], coef[2 * d + 1]
                nr = ar * hr - ai * hi_ + hs_ref[d, k, rows, 0:S5_BLK_W]
                ni = ar * hi_ + ai * hr + hs_ref[d, k, rows, S5_BLK_W:2 * S5_BLK_W]
                hs_ref[d, k, rows, 0:S5_BLK_W] = nr
                hs_ref[d, k, rows, S5_BLK_W:2 * S5_BLK_W] = ni
                state[2 * d], state[2 * d + 1] = nr, ni
        st_ref[0, :, lo:hi] = state[0]
        st_ref[0, :, S5_STATE_W + lo:S5_STATE_W + hi] = state[1]
        st_ref[1, :, lo:hi] = state[2]
        st_ref[1, :, S5_STATE_W + lo:S5_STATE_W + hi] = state[3]

        for d, _, y_ref in dirs:
            ys_ref[d, k] = _dot(hs_ref[d, k].astype(BF16), cout_ref[d, k])
            for b in range(SUBLANES):
                c0 = b * S5_WIDTH + k * LANES
                y_ref[:, c0:c0 + LANES] = ys_ref[d, k, pl.ds(b, SCAN_T, stride=SUBLANES), :]

    finf_ref[...] = st_ref[0]
    finb_ref[...] = st_ref[1]


def _s5_scan(e, u_tm, a, b_in, c_out, h0):
    rev = lambda j: N_CHUNKS - 1 - j
    width = SUBLANES * S5_WIDTH
    st_w = 2 * S5_STATE_W
    row_f = pl.BlockSpec((SCAN_T, width), lambda j: (j, 0))
    row_b = pl.BlockSpec((SCAN_T, width), lambda j: (rev(j), 0))
    h0_f = pl.BlockSpec((None, None, None, SUBLANES, st_w), lambda j: (e, 0, _seq_of_chunk(j), 0, 0))
    h0_b = pl.BlockSpec((None, None, None, SUBLANES, st_w), lambda j: (e, 1, _seq_of_chunk(rev(j)), 0, 0))
    st_f = pl.BlockSpec((None, SUBLANES, st_w), lambda j: (_seq_of_chunk(j), 0, 0))
    st_b = pl.BlockSpec((None, SUBLANES, st_w), lambda j: (_seq_of_chunk(rev(j)), 0, 0))
    st_shape = jax.ShapeDtypeStruct((3, SUBLANES, st_w), F32)
    return pl.pallas_call(
        _s5_scan_kernel,
        grid=(N_CHUNKS,),
        in_specs=[row_f, row_b, _layer_spec(a.shape[1:], e), _layer_spec(b_in.shape[1:], e),
                  _layer_spec(c_out.shape[1:], e), h0_f, h0_b],
        out_specs=[row_f, row_b, st_f, st_b],
        out_shape=[jax.ShapeDtypeStruct((N_TM_ROWS, width), F32)] * 2 + [st_shape] * 2,
        scratch_shapes=[pltpu.VMEM((2, S5_BLK, SCAN_R, LANES), F32),
                        pltpu.VMEM((2, S5_BLK, SCAN_R, 2 * S5_BLK_W), F32),
                        pltpu.VMEM((2, S5_BLK, SCAN_R, LANES), F32),
                        pltpu.VMEM((2, SUBLANES, st_w), F32)],
        compiler_params=_cparams(("arbitrary",)),
        name="s5_scan",
    )(u_tm, u_tm, a, b_in, c_out, h0, h0)


BF16_SUBLANES = 16


def _cast_specs(w_stack, layer, n_steps):
    _, r, c = w_stack.shape
    rc = BF16_SUBLANES
    while r % rc or r // rc > n_steps:
        rc += BF16_SUBLANES
    last = r // rc - 1
    return (pl.BlockSpec((None, rc, c), lambda i: (layer, jnp.minimum(i, last), 0)),
            pl.BlockSpec((rc, c), lambda i: (jnp.minimum(i, last), 0)),
            jax.ShapeDtypeStruct((r, c), BF16))


def _cast_side_job(in_refs, out_refs):
    for wi, wo in zip(in_refs, out_refs):
        wo[...] = wi[...].astype(BF16)


def _even_pre_kernel(l, n_x, n_cast, *refs):
    x_refs, refs = refs[:n_x], refs[n_x:]
    mod_ref, g_ref, w_ref, cos_ref, sa_ref, sb_ref = refs[:6]
    u_ref, q_ref, k_ref, v_ref, kf_ref, vf_ref = refs[6 + n_cast:12 + n_cast]
    _cast_side_job(refs[6:6 + n_cast], refs[12 + n_cast:])
    m = mod_ref[...]
    h = _rms(_read_stream(x_refs), _row(g_ref, l)) * (1.0 + m[1:2]) + m[0:1]
    proj = _dot(h.astype(BF16), w_ref[...])
    cos, sa, sb = cos_ref[...], sa_ref[...], sb_ref[...]
    half = DIFF_HEAD_DIM // 2

    def rope(z):
        return z * cos + pltpu.roll(z, half, 1) * sa + pltpu.roll(z, DIFF_WIDTH - half, 1) * sb

    u_ref[...] = proj[:, :S5_WIDTH]
    k = rope(proj[:, S5_WIDTH + DIFF_WIDTH:S5_WIDTH + 2 * DIFF_WIDTH])
    v = proj[:, S5_WIDTH + 2 * DIFF_WIDTH:]
    q_ref[...] = (rope(proj[:, S5_WIDTH:S5_WIDTH + DIFF_WIDTH]) * (DIFF_HEAD_DIM ** -0.5 * LOG2E)).astype(BF16)
    k_ref[...] = k.astype(BF16)
    v_ref[...] = v.astype(BF16)

    @pl.when(pl.program_id(0) < P_TILES)
    def _():
        kf_ref[...] = k
        vf_ref[...] = v


def _even_pre(l, xs, mod, g_pre, w_in, rope_tabs, cast=()):
    rope_spec = pl.BlockSpec((TM, DIFF_WIDTH), lambda i: (_rope_block(i), 0))
    casts = [_cast_specs(w, lyr, N_TILES) for w, lyr in cast]
    return pl.pallas_call(
        functools.partial(_even_pre_kernel, l, len(xs), len(casts)),
        grid=(N_TILES,),
        in_specs=_stream_specs(xs, D_MODEL) + [
            _mod_spec(l),
            _const_spec(g_pre.shape),
            _layer_spec((D_MODEL, EVEN_IN), l // 2),
            rope_spec, rope_spec, rope_spec,
        ] + [c[0] for c in casts],
        out_specs=[
            pl.BlockSpec((TM, S5_WIDTH), lambda i: (_tm_row_block(i), _tm_col_block(i))),
            _tok_spec(DIFF_WIDTH), _tok_spec(DIFF_WIDTH), _tok_spec(DIFF_WIDTH),
            _ctx_spec(DIFF_WIDTH), _ctx_spec(DIFF_WIDTH),
        ] + [c[1] for c in casts],
        out_shape=[jax.ShapeDtypeStruct((N_TM_ROWS, SUBLANES * S5_WIDTH), F32)]
        + [jax.ShapeDtypeStruct((N_TOK, DIFF_WIDTH), BF16)] * 3
        + [jax.ShapeDtypeStruct((N_P, DIFF_WIDTH), F32)] * 2 + [c[2] for c in casts],
        compiler_params=_cparams(("arbitrary",)),
        name="even_pre",
    )(*xs, mod, g_pre, w_in, *rope_tabs, *[w for w, _ in cast])


def _softmax2_parts(parts):
    m = functools.reduce(jnp.maximum, [jnp.max(s, axis=-1, keepdims=True) for s in parts])
    ps = [jnp.exp2(s - m) for s in parts]
    l = functools.reduce(jnp.add, [jnp.sum(p, axis=-1, keepdims=True) for p in ps])
    return ps, l


def _diff_attn_kernel(e, lam_init, has_ctx, *refs):
    if has_ctx:
        q_ref, k_ref, v_ref, kc_ref, vc_ref, lq1, lk1, lq2, lk2, g_ref, o_ref = refs
    else:
        q_ref, k_ref, v_ref, lq1, lk1, lq2, lk2, g_ref, o_ref = refs
    lam = (jnp.exp(jnp.sum(_row(lq1, e) * _row(lk1, e), axis=-1, keepdims=True))
           - jnp.exp(jnp.sum(_row(lq2, e) * _row(lk2, e), axis=-1, keepdims=True)) + lam_init)
    first = lax.broadcasted_iota(jnp.int32, (1, 2 * DIFF_HEAD_DIM), 1) < DIFF_HEAD_DIM
    g = _row(g_ref, e)
    def head_slice(h):
        return slice(h * 2 * DIFF_HEAD_DIM, (h + 1) * 2 * DIFF_HEAD_DIM)

    def scores(h):
        sl = head_slice(h)
        qh = q_ref[:, sl]
        zero = jnp.zeros_like(qh)
        keys = ([kc_ref[:, sl]] if has_ctx else []) + [k_ref[:, sl]]
        return [[_dot_nt(qc, kk) for kk in keys] for qc in (jnp.where(first, qh, zero), jnp.where(first, zero, qh))]

    s_next = scores(0)
    for h in range(DIFF_HEADS):
        sl = head_slice(h)
        s1, s2 = s_next
        if h + 1 < DIFF_HEADS:
            s_next = scores(h + 1)
        vals = ([vc_ref[:, sl]] if has_ctx else []) + [v_ref[:, sl]]
        p1, l1 = _softmax2_parts(s1)
        p2, l2 = _softmax2_parts(s2)
        ratio = lam * l1 / l2
        o = functools.reduce(jnp.add, [_dot((a - b * ratio).astype(BF16), vv) for a, b, vv in zip(p1, p2, vals)])
        o = o * (1.0 / l1)
        o_ref[:, sl] = (_rms(o, g) * (1.0 - lam_init)).astype(o_ref.dtype)


def _diff_attention(e, q, k, v, ctx, lam_vecs, subln_g, lam_init, *, batch, n, row0, tq):
    nq = n // tq
    has_ctx = ctx is not None
    in_specs = [
        pl.BlockSpec((tq, DIFF_WIDTH), lambda b, i: (row0 // tq + b * nq + i, 0)),
        pl.BlockSpec((n, DIFF_WIDTH), lambda b, i: (row0 // n + b, 0)),
        pl.BlockSpec((n, DIFF_WIDTH), lambda b, i: (row0 // n + b, 0)),
    ]
    args = [q, k, v]
    if has_ctx:
        in_specs += [pl.BlockSpec((None, None, PAST_LEN, DIFF_WIDTH), lambda b, i: (b, e, 0, 0))] * 2
        args += list(ctx)
    in_specs += [_const_spec(t.shape) for t in lam_vecs] + [_const_spec(subln_g.shape)]
    args += list(lam_vecs) + [subln_g]
    return pl.pallas_call(
        functools.partial(_diff_attn_kernel, e, lam_init, has_ctx),
        grid=(batch, nq),
        in_specs=in_specs,
        out_specs=pl.BlockSpec((tq, DIFF_WIDTH), lambda b, i: (b * nq + i, 0)),
        out_shape=jax.ShapeDtypeStruct((batch * n, DIFF_WIDTH), BF16),
        compiler_params=_cparams(("parallel", "parallel")),
        name="diff_attn_ctx" if has_ctx else "diff_attn",
    )(*args)


POST_SUB = 2


def _sub_rows(v):
    n = v.shape[0] // POST_SUB
    return [v[r * n:(r + 1) * n] for r in range(POST_SUB)]


def _post_tail(l, xs, m, mixes, gpm_ref, gpf_ref, gqf_ref, wg_ref, wu_ref, wd_ref):
    x1 = [x + m[2:3] * _rms(mix, _row(gpm_ref, l)) for x, mix in zip(xs, mixes)]
    h = [(_rms(v, _row(gpf_ref, l)) * (1.0 + m[4:5]) + m[3:4]).astype(BF16) for v in x1]
    act = [jax.nn.silu(_dot(v, wg_ref[...])) * _dot(v, wu_ref[...]) for v in h]
    y = [_dot(v.astype(BF16), wd_ref[...]) for v in act]
    return jnp.concatenate([a + m[5:6] * _rms(b, _row(gqf_ref, l)) for a, b in zip(x1, y)], axis=0)


def _write_stream(out_refs, rows, x2, tm=TM):
    if len(out_refs) == 1:
        out_refs[0][rows, :] = x2
    else:
        i = pl.program_id(0)

        @pl.when(i < N_P // tm)
        def _():
            out_refs[0][rows, :] = x2

        @pl.when(i >= N_P // tm)
        def _():
            out_refs[1][rows, :] = x2


def _ffn_specs():
    vec = _const_spec((DEPTH, D_MODEL))
    return [vec, vec, vec, _const_spec((D_MODEL, D_FF)), _const_spec((D_MODEL, D_FF)), _const_spec((D_FF, D_MODEL))]


def _out_stream(split, tm=TM):
    if split:
        return ([_ctx_spec(D_MODEL, tm), _lat_spec(D_MODEL, tm)],
                [jax.ShapeDtypeStruct((N_P, D_MODEL), F32), jax.ShapeDtypeStruct((N_S, D_MODEL), F32)])
    return [_tok_spec(D_MODEL, tm)], [jax.ShapeDtypeStruct((N_TOK, D_MODEL), F32)]


TM_POST = 512
P_POST = N_P // TM_POST


def _even_post_kernel(l, n_x, n_out, n_cast, *refs):
    e = l // 2
    x_refs, refs = refs[:n_x], refs[n_x:]
    (mod_ref, yfp_ref, ybp_ref, up_ref, yfs_ref, ybs_ref, us_ref, d_ref, gw_ref, gb_ref, dap_ref, das_ref,
     wos_ref, wod_ref, gpm_ref, gpf_ref, gqf_ref, wg_ref, wu_ref, wd_ref) = refs[:20]
    m = mod_ref[...]
    d = _row(d_ref, e)
    yp = yfp_ref[...] + ybp_ref[...] + jnp.concatenate([d, d], axis=1) * up_ref[...]
    ys = yfs_ref[...] + ybs_ref[...] + d * us_ref[...]
    y = jnp.where(pl.program_id(0) < P_POST,
                  jnp.concatenate([yp[:, :S5_WIDTH], yp[:, S5_WIDTH:]], axis=0), ys)
    g = [jax.nn.gelu(v) for v in _sub_rows(y)]
    s5 = [v * jax.nn.sigmoid(_dot(v.astype(BF16), gw_ref[...]) + _row(gb_ref, e)) for v in g]
    mixes = [_dot(v.astype(BF16), wos_ref[...]) + _dot(da, wod_ref[...])
             for v, da in zip(s5, _sub_rows(_read_stream((dap_ref, das_ref), TM_POST)))]
    x2 = _post_tail(l, _sub_rows(_read_stream(x_refs, TM_POST)), m, mixes, gpm_ref, gpf_ref, gqf_ref,
                    wg_ref, wu_ref, wd_ref)
    n_in = 20 + n_cast
    _write_stream(refs[n_in:n_in + n_out], slice(None), x2, TM_POST)
    _cast_side_job(refs[20:n_in], refs[n_in + n_out:])


def _even_post(l, xs, mod, y_f, y_b, u_tm, s5_d, glu_w, glu_b, da_p, da_s, w_out, ffn_args, split_out, cast=()):
    e = l // 2
    tm = TM_POST
    halves = DEC_SEQ // tm
    ctx_view = pl.BlockSpec((SEQ, 2 * S5_WIDTH),
                            lambda i: (jnp.minimum(i, P_POST - 1) // 4, jnp.minimum(i, P_POST - 1) % 4))
    lat_view = pl.BlockSpec((tm, S5_WIDTH),
                            lambda i: (2 * SEQ // tm + jnp.maximum(i - P_POST, 0) % halves,
                                       jnp.maximum(i - P_POST, 0) // halves))
    out_specs, out_shape = _out_stream(split_out, tm)
    casts = [_cast_specs(w, lyr, N_TOK // tm) for w, lyr in cast]
    w_half = lambda r: pl.BlockSpec((None, S5_WIDTH, D_MODEL), lambda i: (e, r, 0), pipeline_mode=pl.Buffered(1))
    return pl.pallas_call(
        functools.partial(_even_post_kernel, l, len(xs), len(out_specs), len(casts)),
        grid=(N_TOK // tm,),
        in_specs=_stream_specs(xs, D_MODEL, tm) + [
            _mod_spec(l, tm),
            ctx_view, ctx_view, ctx_view, lat_view, lat_view, lat_view,
            _const_spec(s5_d.shape), _layer_spec((S5_WIDTH, S5_WIDTH), e), _const_spec(glu_b.shape),
            _ctx_spec(DIFF_WIDTH, tm), _lat_spec(DIFF_WIDTH, tm),
            w_half(0), w_half(1),
        ] + _ffn_specs() + [c[0] for c in casts],
        out_specs=out_specs + [c[1] for c in casts],
        out_shape=out_shape + [c[2] for c in casts],
        compiler_params=_cparams(("arbitrary",), VMEM_LIMIT_POST),
        name="even_post_ffn",
    )(*xs, mod, y_f, y_b, u_tm, y_f, y_b, u_tm, s5_d, glu_w, glu_b, da_p, da_s, w_out, w_out, *ffn_args,
      *[w for w, _ in cast])


def _odd_pre_kernel(l, n_x, *refs):
    o = l // 2
    x_refs, refs = refs[:n_x], refs[n_x:]
    (mod_ref, g_ref, w_ref, gq_ref, gkv_ref, wq_ref, cos_ref, sin_ref,
     q_ref, ckr_ref, ckv_ref, kr_ref) = refs
    m = mod_ref[...]
    kr0 = MLA_Q_RANK + MLA_KV_RANK
    n_sub = TM_ODD // POST_SUB
    subs = [slice(r * n_sub, (r + 1) * n_sub) for r in range(POST_SUB)]
    x = _read_stream(x_refs, TM_ODD)
    h = [_rms(x[rs], _row(g_ref, l)) * (1.0 + m[1:2]) + m[0:1] for rs in subs]
    proj = [_dot(v.astype(BF16), w_ref[...]) for v in h]
    cq = [_rms(p[:, :MLA_Q_RANK], _row(gq_ref, o)) for p in proj]
    ckv = [_rms(p[:, MLA_Q_RANK:kr0], _row(gkv_ref, o)) for p in proj]
    q = [_dot(v.astype(BF16), wq_ref[...]) * ((MLA_NOPE + MLA_ROPE) ** -0.5 * LOG2E) for v in cq]
    for rs, p, qv, cv in zip(subs, proj, q, ckv):
        cos, sin = cos_ref[rs, :], sin_ref[rs, :]
        krp = p[:, kr0:kr0 + MLA_HEAD_PAD]
        krp_sw = p[:, kr0 + MLA_HEAD_PAD:]
        for hd in range(MLA_HEADS):
            sl = slice(hd * MLA_HEAD_PAD, (hd + 1) * MLA_HEAD_PAD)
            sw = slice(MLA_Q_W + hd * MLA_HEAD_PAD, MLA_Q_W + (hd + 1) * MLA_HEAD_PAD)
            q_ref[rs, sl] = (qv[:, sl] * cos + qv[:, sw] * sin).astype(BF16)
        ckr_ref[rs, :MLA_KV_RANK] = cv.astype(BF16)
        ckr_ref[rs, MLA_KV_RANK:] = (krp * cos + krp_sw * sin).astype(BF16)

    @pl.when(pl.program_id(0) < N_P // TM_ODD)
    def _():
        for rs, p, cv in zip(subs, proj, ckv):
            ckv_ref[rs, :] = cv
            kr_ref[rs, :] = p[:, kr0:kr0 + MLA_HEAD_PAD]


def _odd_pre(l, xs, mod, g_pre, w_in, gq, gkv, wq, rope_tabs):
    o = l // 2
    tm = TM_ODD
    rope_spec = pl.BlockSpec((tm, MLA_HEAD_PAD), lambda i: (_rope_block(i, tm), 0))
    return pl.pallas_call(
        functools.partial(_odd_pre_kernel, l, len(xs)),
        grid=(N_TOK // tm,),
        in_specs=_stream_specs(xs, D_MODEL, tm) + [
            _mod_spec(l, tm),
            _const_spec(g_pre.shape),
            _layer_spec((D_MODEL, MLA_IN_W), o),
            _const_spec(gq.shape), _const_spec(gkv.shape),
            _layer_spec((MLA_Q_RANK, 2 * MLA_Q_W), o),
            rope_spec, rope_spec,
        ],
        out_specs=[_tok_spec(MLA_Q_W, tm), _tok_spec(MLA_CKR_W, tm), _ctx_spec(MLA_KV_RANK, tm),
                   _ctx_spec(MLA_HEAD_PAD, tm)],
        out_shape=[
            jax.ShapeDtypeStruct((N_TOK, MLA_Q_W), BF16),
            jax.ShapeDtypeStruct((N_TOK, MLA_CKR_W), BF16),
            jax.ShapeDtypeStruct((N_P, MLA_KV_RANK), F32),
            jax.ShapeDtypeStruct((N_P, MLA_HEAD_PAD), F32),
        ],
        compiler_params=_cparams(("arbitrary",)),
        name="odd_pre",
    )(*xs, mod, g_pre, w_in, gq, gkv, wq, *rope_tabs)


def _mla_attn_kernel(has_ctx, *refs):
    if has_ctx:
        q_ref, ckr_ref, ckrc_ref, wk_ref, wv_ref, o_ref, kf_ref, vf_ref = refs
    else:
        q_ref, ckr_ref, wk_ref, wv_ref, o_ref, kf_ref, vf_ref = refs
    off = PAST_LEN if has_ctx else 0

    @pl.when(pl.program_id(1) == 0)
    def _():
        srcs = [(0, ckrc_ref)] if has_ctx else []
        srcs.append((off, ckr_ref))
        wv = wv_ref[...] if has_ctx else wv_ref[:, :MLA_O_W] + wv_ref[:, MLA_O_W:]
        for r0, src in srcs:
            c = src[...]
            rows = pl.ds(r0, c.shape[0])
            kf_ref[rows, :] = _dot(c, wk_ref[...]).astype(BF16)
            vf_ref[rows, :] = _dot(c[:, :MLA_KV_RANK], wv).astype(BF16)

    low = lax.broadcasted_iota(jnp.int32, (1, LANES), 1) < MLA_V
    def scores(hd):
        cs = slice(hd * MLA_HEAD_PAD, (hd + 1) * MLA_HEAD_PAD)
        return _dot_nt(q_ref[:, cs], kf_ref[:, cs])

    s_next = scores(0)
    for j in range(MLA_HEADS // 2):
        slot = slice(j * LANES, (j + 1) * LANES)
        pv, rl = [], []
        for t in range(2):
            s = s_next
            if 2 * j + t + 1 < MLA_HEADS:
                s_next = scores(2 * j + t + 1)
            mx = jnp.max(s, axis=-1, keepdims=True)
            p = jnp.exp2(s - mx)
            rl.append(1.0 / jnp.sum(p, axis=-1, keepdims=True))
            v_cols = slice(t * MLA_O_W + j * LANES, t * MLA_O_W + (j + 1) * LANES) if has_ctx else slot
            pv.append(_dot(p.astype(BF16), vf_ref[:, v_cols]))
        if has_ctx:
            o = (pv[0] + pv[1]) * jnp.where(low, rl[0], rl[1])
        else:
            o = jnp.where(low, pv[0] * rl[0], pv[1] * rl[1])
        o_ref[:, slot] = o.astype(o_ref.dtype)


def _mla_attention(o, q, ckr, ckr_ctx, wk, wv, *, batch, n, row0, tq):
    nq = n // tq
    has_ctx = ckr_ctx is not None
    s_len = n + (PAST_LEN if has_ctx else 0)
    in_specs = [
        pl.BlockSpec((tq, MLA_Q_W), lambda b, i: (row0 // tq + b * nq + i, 0)),
        pl.BlockSpec((n, MLA_CKR_W), lambda b, i: (row0 // n + b, 0)),
    ]
    args = [q, ckr]
    if has_ctx:
        in_specs.append(pl.BlockSpec((None, None, PAST_LEN, MLA_CKR_W), lambda b, i: (b, o, 0, 0)))
        args.append(ckr_ctx)
    in_specs += [_layer_spec(wk.shape[1:], o), _layer_spec(wv.shape[1:], o)]
    args += [wk, wv]
    return pl.pallas_call(
        functools.partial(_mla_attn_kernel, has_ctx),
        grid=(batch, nq),
        in_specs=in_specs,
        out_specs=pl.BlockSpec((tq, MLA_O_W), lambda b, i: (b * nq + i, 0)),
        out_shape=jax.ShapeDtypeStruct((batch * n, MLA_O_W), BF16),
        scratch_shapes=[pltpu.VMEM((s_len, MLA_Q_W), BF16),
                        pltpu.VMEM((s_len, (2 if has_ctx else 1) * MLA_O_W), BF16)],
        compiler_params=_cparams(("parallel", "arbitrary")),
        name="mla_attn_ctx" if has_ctx else "mla_attn",
    )(*args)


def _odd_post_kernel(l, n_x, n_out, n_cast, *refs):
    x_refs, refs = refs[:n_x], refs[n_x:]
    mod_ref, ap_ref, as_ref, wo_ref, gpm_ref, gpf_ref, gqf_ref, wg_ref, wu_ref, wd_ref = refs[:10]
    mixes = [_dot(a, wo_ref[...]) for a in _sub_rows(_read_stream((ap_ref, as_ref), TM_ODD))]
    x2 = _post_tail(l, _sub_rows(_read_stream(x_refs, TM_ODD)), mod_ref[...], mixes, gpm_ref, gpf_ref, gqf_ref,
                    wg_ref, wu_ref, wd_ref)
    n_in = 10 + n_cast
    _write_stream(refs[n_in:n_in + n_out], slice(None), x2, TM_ODD)
    _cast_side_job(refs[10:n_in], refs[n_in + n_out:])


def _odd_post(l, xs, mod, at_p, at_s, w_out, ffn_args, split_out, cast=()):
    tm = TM_ODD
    out_specs, out_shape = _out_stream(split_out, tm)
    casts = [_cast_specs(w, lyr, N_TOK // tm) for w, lyr in cast]
    return pl.pallas_call(
        functools.partial(_odd_post_kernel, l, len(xs), len(out_specs), len(casts)),
        grid=(N_TOK // tm,),
        in_specs=_stream_specs(xs, D_MODEL, tm) + [
            _mod_spec(l, tm),
            _ctx_spec(MLA_O_W, tm), _lat_spec(MLA_O_W, tm),
            _layer_spec((MLA_O_W, D_MODEL), l // 2),
        ] + _ffn_specs() + [c[0] for c in casts],
        out_specs=out_specs + [c[1] for c in casts],
        out_shape=out_shape + [c[2] for c in casts],
        compiler_params=_cparams(("arbitrary",)),
        name="odd_post_ffn",
    )(*xs, mod, at_p, at_s, w_out, *ffn_args, *[w for w, _ in cast])


def _rope_angles(rot_dim):
    rows = DEC_SEQ // GRID_W
    row = jnp.repeat(jnp.arange(rows, dtype=F32), GRID_W)
    col = jnp.tile(jnp.arange(GRID_W, dtype=F32), rows)
    n_freq = rot_dim // 4
    inv = ROPE_BASE ** (-jnp.arange(n_freq, dtype=F32) / n_freq)
    ang = jnp.concatenate([row[:, None] * inv, col[:, None] * inv], axis=-1)
    return jnp.cos(ang), jnp.sin(ang)


def _with_identity(tm, cos, *sins):
    one = jnp.ones((tm, cos.shape[1]), F32)
    zero = jnp.zeros((tm, cos.shape[1]), F32)
    return (jnp.concatenate([one, cos]),) + tuple(jnp.concatenate([zero, s]) for s in sins)


def _diff_rope_tables():
    c, s = _rope_angles(DIFF_HEAD_DIM)
    z = jnp.zeros_like(s)
    reps = DIFF_WIDTH // DIFF_HEAD_DIM
    cos = jnp.tile(jnp.concatenate([c, c], axis=1), (1, reps))
    sa = jnp.tile(jnp.concatenate([z, s], axis=1), (1, reps))
    sb = jnp.tile(jnp.concatenate([-s, z], axis=1), (1, reps))
    return _with_identity(TM, cos, sa, sb)


def _mla_rope_tables():
    c, s = _rope_angles(MLA_ROPE)
    n = c.shape[0]
    cos = jnp.concatenate([jnp.ones((n, MLA_NOPE), F32), c, c, jnp.ones((n, MLA_PAD), F32)], axis=1)
    sin = jnp.concatenate([jnp.zeros((n, MLA_NOPE), F32), -s, s, jnp.zeros((n, MLA_PAD), F32)], axis=1)
    return _with_identity(TM_ODD, cos, sin)


def _swap_rope_halves(w):
    half = MLA_ROPE // 2
    return jnp.concatenate([w[..., half:], w[..., :half]], axis=-1)


def _mla_weights(w_in_odd, w_q_up, w_kv_up):
    zeros = lambda *s: jnp.zeros(s, F32)
    kr0 = MLA_Q_RANK + MLA_KV_RANK
    w_kr = w_in_odd[:, :, kr0:]
    lead = (N_ODD, D_MODEL)
    w_in = jnp.concatenate(
        [w_in_odd[:, :, :kr0],
         zeros(*lead, MLA_NOPE), w_kr, zeros(*lead, MLA_PAD),
         zeros(*lead, MLA_NOPE), _swap_rope_halves(w_kr), zeros(*lead, MLA_PAD)], axis=-1).astype(BF16)
    wq4 = w_q_up.reshape(N_ODD, MLA_Q_RANK, MLA_HEADS, MLA_NOPE + MLA_ROPE)
    lead = (N_ODD, MLA_Q_RANK, MLA_HEADS)
    wq_main = jnp.concatenate([wq4, zeros(*lead, MLA_PAD)], axis=-1)
    wq_swap = jnp.concatenate([zeros(*lead, MLA_NOPE), _swap_rope_halves(wq4[..., MLA_NOPE:]),
                               zeros(*lead, MLA_PAD)], axis=-1)
    wq = jnp.concatenate([wq_main.reshape(N_ODD, MLA_Q_RANK, MLA_Q_W),
                          wq_swap.reshape(N_ODD, MLA_Q_RANK, MLA_Q_W)], axis=-1).astype(BF16)
    wkv = w_kv_up.reshape(N_ODD, MLA_KV_RANK, MLA_HEADS, MLA_NOPE + MLA_V)
    wk_top = jnp.pad(wkv[..., :MLA_NOPE], ((0, 0), (0, 0), (0, 0), (0, MLA_HEAD_PAD - MLA_NOPE)))
    sel = jnp.pad(jnp.eye(MLA_ROPE, dtype=F32), ((MLA_NOPE, MLA_PAD), (MLA_NOPE, MLA_PAD)))
    wk_bot = jnp.broadcast_to(jnp.tile(sel, (1, MLA_HEADS)), (N_ODD, MLA_HEAD_PAD, MLA_Q_W))
    wk = jnp.concatenate([wk_top.reshape(N_ODD, MLA_KV_RANK, MLA_Q_W), wk_bot], axis=1).astype(BF16)
    even = (jnp.arange(MLA_HEADS) % 2 == 0)[:, None]
    wv4 = wkv[..., MLA_NOPE:]
    wv = jnp.concatenate([jnp.where(even, wv4, 0.0).reshape(N_ODD, MLA_KV_RANK, MLA_O_W),
                          jnp.where(even, 0.0, wv4).reshape(N_ODD, MLA_KV_RANK, MLA_O_W)], axis=-1).astype(BF16)
    return w_in, wq, wk, wv


def kernel(x_prompt, x_sample, state_s5_re, state_s5_im, cache_diff_k, cache_diff_v, cache_mla_ckv, cache_mla_krope, c, c_ctx, w_mod, b_mod, g_pre_mix, g_post_mix, g_pre_ffn, g_post_ffn, w_ffn_gate, w_ffn_up, w_ffn_down, w_in_even, w_out_even, s5_lam_re, s5_lam_im, s5_log_dt, s5_b_re, s5_b_im, s5_c_re, s5_c_im, s5_d, s5_glu_w, s5_glu_b, diff_lam_q1, diff_lam_k1, diff_lam_q2, diff_lam_k2, diff_subln_g, w_in_odd, mla_q_norm_g, mla_w_q_up, mla_kv_norm_g, mla_w_kv_up, w_out_odd):
    xs = (x_prompt.reshape(N_P, D_MODEL), x_sample.reshape(N_S, D_MODEL))
    conds = jnp.concatenate([c_ctx[None, :], c, jnp.zeros((N_COND - 1 - DEC_BATCH, D_MODEL), F32)], axis=0)
    mod = _modulation(conds, w_mod, b_mod).reshape(DEPTH, N_COND, 6, D_MODEL)

    gains = (g_post_mix, g_pre_ffn, g_post_ffn)
    ffn_f32 = (w_ffn_gate, w_ffn_up, w_ffn_down)
    w_in_e = w_in_even.astype(BF16)
    w_out_e = w_out_even.astype(BF16)
    glu_w = s5_glu_w.astype(BF16)
    a_re, a_im, b_in, c_out = _s5_params(s5_lam_re, s5_lam_im, s5_log_dt, s5_b_re, s5_b_im, s5_c_re, s5_c_im)
    a = jnp.stack([a_re[:, 0], a_im[:, 0], a_re[:, 1], a_im[:, 1]], axis=1).reshape(N_EVEN, 4, S5_STATE_W)
    st = lambda s: jnp.moveaxis(s, 0, 2).reshape(N_EVEN, 2, 1, DEC_BATCH, S5_STATE_W)
    h0 = jnp.concatenate([jnp.zeros((N_EVEN, 2, 2, SUBLANES, 2 * S5_STATE_W), F32),
                          jnp.concatenate([st(state_s5_re), st(state_s5_im)], axis=-1)], axis=2)
    diff_ctx = (cache_diff_k.astype(BF16).reshape(DEC_BATCH, N_EVEN, PAST_LEN, DIFF_WIDTH),
                cache_diff_v.astype(BF16).reshape(DEC_BATCH, N_EVEN, PAST_LEN, DIFF_WIDTH))
    lam_vecs = (diff_lam_q1, diff_lam_k1, diff_lam_q2, diff_lam_k2)
    w_in_o, wq, wk, wv = _mla_weights(w_in_odd, mla_w_q_up, mla_w_kv_up)
    w_out_o = w_out_odd.astype(BF16)
    ckr_ctx = jnp.concatenate(
        [cache_mla_ckv, jnp.pad(cache_mla_krope, ((0, 0), (0, 0), (0, 0), (MLA_NOPE, MLA_PAD)))], axis=-1).astype(BF16)
    diff_tabs = _diff_rope_tables()
    mla_tabs = _mla_rope_tables()

    s5_re_list, s5_im_list, dk_list, dv_list, ckv_list, kr_list = [], [], [], [], [], []
    for l in range(DEPTH):
        last = l == DEPTH - 1
        n_stream = 2 if last else 1
        next_cast = () if last else tuple((w, l + 1) for w in ffn_f32)
        if l % 2 == 0:
            e = l // 2
            lam_init = 0.8 - 0.6 * math.exp(-0.3 * l)
            pre = _even_pre(l, xs, mod, g_pre_mix, w_in_e, diff_tabs,
                            cast=tuple((w, 0) for w in ffn_f32) if l == 0 else ())
            u_tm, q, k, v, kf, vf = pre[:6]
            if l == 0:
                ffn_w = tuple(pre[6:])
            y_f, y_b, fin_f, fin_b = _s5_scan(e, u_tm, a, b_in, c_out, h0)
            fin = jnp.stack([fin_f[:2].reshape(BATCH, 2, S5_GROUPS, S5_STATE),
                             fin_b[:2].reshape(BATCH, 2, S5_GROUPS, S5_STATE)], axis=1)
            s5_re_list.append(fin[:, :, 0])
            s5_im_list.append(fin[:, :, 1])
            da_p = _diff_attention(e, q, k, v, None, lam_vecs, diff_subln_g, lam_init,
                                   batch=BATCH, n=SEQ, row0=0, tq=SEQ)
            da_s = _diff_attention(e, q, k, v, diff_ctx, lam_vecs, diff_subln_g, lam_init,
                                   batch=DEC_BATCH, n=DEC_SEQ, row0=N_P, tq=256)
            post = _even_post(l, xs, mod, y_f, y_b, u_tm, s5_d, glu_w, s5_glu_b, da_p, da_s, w_out_e,
                              gains + ffn_w, last, cast=next_cast)
            xs, ffn_w = post[:n_stream], tuple(post[n_stream:])
            dk_list.append(kf.reshape(BATCH, SEQ, DIFF_HEADS, 2, DIFF_HEAD_DIM))
            dv_list.append(vf.reshape(BATCH, SEQ, DIFF_HEADS, 2 * DIFF_HEAD_DIM))
        else:
            o = l // 2
            q, ckr, ckv, krp = _odd_pre(l, xs, mod, g_pre_mix, w_in_o, mla_q_norm_g, mla_kv_norm_g, wq, mla_tabs)
            at_p = _mla_attention(o, q, ckr, None, wk, wv, batch=BATCH, n=SEQ, row0=0, tq=SEQ)
            at_s = _mla_attention(o, q, ckr, ckr_ctx, wk, wv, batch=DEC_BATCH, n=DEC_SEQ, row0=N_P, tq=256)
            post = _odd_post(l, xs, mod, at_p, at_s, w_out_o, gains + ffn_w, last, cast=next_cast)
            xs, ffn_w = post[:n_stream], tuple(post[n_stream:])
            ckv_list.append(ckv.reshape(BATCH, SEQ, MLA_KV_RANK))
            kr_list.append(krp[:, MLA_NOPE:MLA_NOPE + MLA_ROPE].reshape(BATCH, SEQ, MLA_ROPE))

    return (xs[0].reshape(BATCH, SEQ, D_MODEL), xs[1].reshape(DEC_BATCH, DEC_SEQ, D_MODEL),
            jnp.stack(s5_re_list, axis=1), jnp.stack(s5_im_list, axis=1),
            jnp.stack(dk_list, axis=1), jnp.stack(dv_list, axis=1),
            jnp.stack(ckv_list, axis=1), jnp.stack(kr_list, axis=1))
```

```python
import functools
import math

import jax
import jax.numpy as jnp
import numpy as np
from jax import lax
from jax.experimental import pallas as pl
from jax.experimental.pallas import tpu as pltpu

F32 = jnp.float32
BF16 = jnp.bfloat16

D_MODEL = 1024
BATCH = 16
SEQ = 256
DEPTH = 4
DEC_BATCH = 8
DEC_SEQ = 1024
PAST_LEN = 512
GRID_W = 64
N_EVEN = (DEPTH + 1) // 2
N_ODD = DEPTH // 2
EPS = 1e-6
ROPE_BASE = 10000.0
S5_WIDTH = D_MODEL // 2
S5_GROUP = 16
S5_GROUPS = S5_WIDTH // S5_GROUP
S5_STATE = 64
DIFF_HEAD_DIM = 64
DIFF_HEADS = (D_MODEL // 2) // (2 * DIFF_HEAD_DIM)
DIFF_WIDTH = DIFF_HEADS * 2 * DIFF_HEAD_DIM
EVEN_IN = S5_WIDTH + 3 * DIFF_WIDTH
MLA_HEADS = 16
MLA_NOPE = 64
MLA_ROPE = 32
MLA_V = 64
MLA_Q_RANK = 256
MLA_KV_RANK = 128
D_FF = ((8 * D_MODEL // 3 + 255) // 256) * 256

LANES = 128
SUBLANES = 8
VMEM_LIMIT = 56 * 1024 * 1024
VMEM_LIMIT_POST = 61 * 1024 * 1024
LOG2E = math.log2(math.e)

N_P = BATCH * SEQ
N_S = DEC_BATCH * DEC_SEQ
N_TOK = N_P + N_S
TM = 256
TM_ODD = 512
N_TILES = N_TOK // TM
P_TILES = N_P // TM
S_TILES_PER_B = DEC_SEQ // TM
N_COND = 16

S5_STATE_W = S5_GROUPS * S5_STATE
S5_BLK = 4
S5_BLK_W = S5_STATE_W // S5_BLK
SCAN_T = 64
SCAN_R = SCAN_T * SUBLANES
N_TM_ROWS = N_TOK // SUBLANES
N_CHUNKS = N_TM_ROWS // SCAN_T
SEQ_CHUNKS = SEQ // SCAN_T

MLA_HEAD_PAD = LANES
MLA_PAD = MLA_HEAD_PAD - MLA_NOPE - MLA_ROPE
MLA_Q_W = MLA_HEADS * MLA_HEAD_PAD
MLA_IN_W = MLA_Q_RANK + MLA_KV_RANK + 2 * MLA_HEAD_PAD
MLA_CKR_W = MLA_KV_RANK + MLA_HEAD_PAD
MLA_O_W = MLA_HEADS * MLA_V


def _cparams(sem, vmem_limit=VMEM_LIMIT):
    return pltpu.CompilerParams(dimension_semantics=sem, vmem_limit_bytes=vmem_limit)


def _const_spec(shape):
    nd = len(shape)
    return pl.BlockSpec(shape, lambda *_: (0,) * nd, pipeline_mode=pl.Buffered(1))


def _layer_spec(tail, *lead):
    nt = len(tail)
    return pl.BlockSpec((None,) * len(lead) + tuple(tail), lambda *_: tuple(lead) + (0,) * nt,
                        pipeline_mode=pl.Buffered(1))


def _rms(x, g):
    return x * lax.rsqrt(jnp.mean(x * x, axis=-1, keepdims=True) + EPS) * g


def _dot(a, b):
    return jnp.dot(a, b, preferred_element_type=F32)


def _dot_nt(a, b):
    return lax.dot_general(a, b, (((1,), (1,)), ((), ())), preferred_element_type=F32)


def _cond_of_tile(i, tm=TM):
    return jnp.where(i < N_P // tm, 0, 1 + (i - N_P // tm) // (DEC_SEQ // tm))


def _tm_row_block(i):
    return jnp.where(i < P_TILES, i // SUBLANES, 2 + (i - P_TILES) % S_TILES_PER_B)


def _tm_col_block(i):
    return jnp.where(i < P_TILES, i % SUBLANES, (i - P_TILES) // S_TILES_PER_B)


def _rope_block(i, tm=TM):
    return jnp.where(i < N_P // tm, 0, 1 + (i - N_P // tm) % (DEC_SEQ // tm))


def _tok_spec(width, tm=TM):
    return pl.BlockSpec((tm, width), lambda i: (i, 0))


def _ctx_spec(width, tm=TM):
    return pl.BlockSpec((tm, width), lambda i: (jnp.minimum(i, N_P // tm - 1), 0))


def _lat_spec(width, tm=TM):
    return pl.BlockSpec((tm, width), lambda i: (jnp.maximum(i - N_P // tm, 0), 0))


def _stream_specs(arrays, width, tm=TM):
    return [_tok_spec(width, tm)] if len(arrays) == 1 else [_ctx_spec(width, tm), _lat_spec(width, tm)]


def _read_stream(refs, tm=TM):
    if len(refs) == 1:
        return refs[0][...]
    return jnp.where(pl.program_id(0) < N_P // tm, refs[0][...], refs[1][...])


def _mod_spec(l, tm=TM):
    return pl.BlockSpec((None, None, 6, D_MODEL), lambda i: (l, _cond_of_tile(i, tm), 0, 0))


def _row(ref, r):
    return ref[r:r + 1, :]


def _mod_kernel(c_ref, w_ref, b_ref, o_ref):
    s = jax.nn.silu(c_ref[...])
    o_ref[0] = _dot(s.astype(BF16), w_ref[0].astype(BF16)) + b_ref[0]


def _modulation(conds, w_mod, b_mod):
    tn = 1536
    return pl.pallas_call(
        _mod_kernel,
        grid=(DEPTH, 6 * D_MODEL // tn),
        in_specs=[
            pl.BlockSpec((N_COND, D_MODEL), lambda l, n: (0, 0)),
            pl.BlockSpec((1, D_MODEL, tn), lambda l, n: (l, 0, n)),
            pl.BlockSpec((1, 1, tn), lambda l, n: (l, 0, n)),
        ],
        out_specs=pl.BlockSpec((1, N_COND, tn), lambda l, n: (l, 0, n)),
        out_shape=jax.ShapeDtypeStruct((DEPTH, N_COND, 6 * D_MODEL), F32),
        compiler_params=_cparams(("parallel", "parallel")),
        name="modulation",
    )(conds, w_mod, b_mod.reshape(DEPTH, 1, 6 * D_MODEL))


S5_GB = S5_GROUPS // S5_BLK
S5_IN_ROWS = S5_GB * S5_GROUP
S5_N_BLOCKS = N_EVEN * 2 * S5_BLK


def _block_diag_lanes(x, row_group, col_group):
    xt = jnp.concatenate([x] * S5_GB, axis=1)
    rg = lax.broadcasted_iota(jnp.int32, xt.shape, 0) // row_group
    cg = lax.broadcasted_iota(jnp.int32, xt.shape, 1) // col_group
    return jnp.where(rg == cg, xt, 0.0)


def _s5_param_kernel(lr_ref, li_ref, ldt_ref, br_ref, bi_ref, cr_ref, ci_ref, are_ref, aim_ref, bin_ref, cout_ref):
    lr, li = lr_ref[...], li_ref[...]
    dt = jnp.exp(ldt_ref[...])
    mag = jnp.exp(lr * dt)
    a_re, a_im = mag * jnp.cos(li * dt), mag * jnp.sin(li * dt)
    den = lr * lr + li * li
    f_re = ((a_re - 1.0) * lr + a_im * li) / den
    f_im = (a_im * lr - (a_re - 1.0) * li) / den
    br, bi = br_ref[...], bi_ref[...]
    are_ref[...] = a_re
    aim_ref[...] = a_im
    bin_ref[:, :S5_BLK_W] = _block_diag_lanes(f_re * br - f_im * bi, S5_GROUP, S5_STATE).astype(BF16)
    bin_ref[:, S5_BLK_W:] = _block_diag_lanes(f_re * bi + f_im * br, S5_GROUP, S5_STATE).astype(BF16)
    cout_ref[:S5_BLK_W, :] = _block_diag_lanes(cr_ref[...], S5_STATE, S5_GROUP).astype(BF16)
    cout_ref[S5_BLK_W:, :] = _block_diag_lanes(-ci_ref[...], S5_STATE, S5_GROUP).astype(BF16)


def _s5_params(lam_re, lam_im, log_dt, b_re, b_im, c_re, c_im):
    full = (N_EVEN, 2, S5_GROUPS, S5_GROUP, S5_STATE)
    blk = (S5_N_BLOCKS, S5_IN_ROWS, S5_STATE)
    ex = lambda a: jnp.broadcast_to(a[:, :, :, None, :], full).reshape(blk)
    ldt = jnp.broadcast_to(log_dt[:, :, :, None, None], full).reshape(blk)
    bt = lambda b: jnp.swapaxes(b, -1, -2).reshape(blk)
    ct = lambda c: jnp.swapaxes(c, -1, -2).reshape(S5_N_BLOCKS, S5_BLK_W, S5_GROUP)
    spec = lambda r, w: pl.BlockSpec((None, r, w), lambda i: (i, 0, 0))
    a_re, a_im, b_in, c_out = pl.pallas_call(
        _s5_param_kernel,
        grid=(S5_N_BLOCKS,),
        in_specs=[spec(S5_IN_ROWS, S5_STATE)] * 5 + [spec(S5_BLK_W, S5_GROUP)] * 2,
        out_specs=[spec(S5_IN_ROWS, S5_STATE)] * 2 + [spec(S5_IN_ROWS, 2 * S5_BLK_W), spec(2 * S5_BLK_W, LANES)],
        out_shape=[jax.ShapeDtypeStruct(blk, F32)] * 2
        + [jax.ShapeDtypeStruct((S5_N_BLOCKS, S5_IN_ROWS, 2 * S5_BLK_W), BF16),
           jax.ShapeDtypeStruct((S5_N_BLOCKS, 2 * S5_BLK_W, LANES), BF16)],
        compiler_params=_cparams(("parallel",)),
        name="s5_params",
    )(ex(lam_re), ex(lam_im), ldt, bt(b_re), bt(b_im), ct(c_re), ct(c_im))
    pick = lambda a: a.reshape(full)[:, :, :, 0, :]
    lead = (N_EVEN, 2, S5_BLK)
    return pick(a_re), pick(a_im), b_in.reshape(lead + b_in.shape[1:]), c_out.reshape(lead + c_out.shape[1:])


def _seq_of_chunk(c):
    return (c >= SEQ_CHUNKS).astype(jnp.int32) + (c >= 2 * SEQ_CHUNKS).astype(jnp.int32)


def _s5_scan_kernel(uf_ref, ub_ref, a_ref, bin_ref, cout_ref, h0f_ref, h0b_ref,
                    yf_ref, yb_ref, finf_ref, finb_ref, us_ref, hs_ref, ys_ref, st_ref):
    j = pl.program_id(0)
    cf = j
    cb = N_CHUNKS - 1 - j

    @pl.when((cf == 0) | (cf == SEQ_CHUNKS) | (cf == 2 * SEQ_CHUNKS))
    def _():
        st_ref[0] = h0f_ref[...]

    @pl.when((cb == N_CHUNKS - 1) | (cb == 2 * SEQ_CHUNKS - 1) | (cb == SEQ_CHUNKS - 1))
    def _():
        st_ref[1] = h0b_ref[...]

    dirs = ((0, uf_ref, yf_ref), (1, ub_ref, yb_ref))

    for k in range(S5_BLK):
        lo, hi = k * S5_BLK_W, (k + 1) * S5_BLK_W
        for d, u_ref, y_ref in dirs:
            for b in range(SUBLANES):
                c0 = b * S5_WIDTH + k * LANES
                us_ref[d, k, pl.ds(b, SCAN_T, stride=SUBLANES), :] = u_ref[:, c0:c0 + LANES]
            hs_ref[d, k] = _dot(us_ref[d, k].astype(BF16), bin_ref[d, k])

            ar = jnp.broadcast_to(a_ref[2 * d:2 * d + 1, lo:hi], (SUBLANES, S5_BLK_W))
            ai = jnp.broadcast_to(a_ref[2 * d + 1:2 * d + 2, lo:hi], (SUBLANES, S5_BLK_W))
            hr, hi_ = st_ref[d, :, lo:hi], st_ref[d, :, S5_STATE_W + lo:S5_STATE_W + hi]
            for t in range(SCAN_T):
                rows = pl.ds((t if d == 0 else SCAN_T - 1 - t) * SUBLANES, SUBLANES)
                nr = ar * hr - ai * hi_ + hs_ref[d, k, rows, 0:S5_BLK_W]
                ni = ar * hi_ + ai * hr + hs_ref[d, k, rows, S5_BLK_W:2 * S5_BLK_W]
                hs_ref[d, k, rows, 0:S5_BLK_W] = nr
                hs_ref[d, k, rows, S5_BLK_W:2 * S5_BLK_W] = ni
                hr, hi_ = nr, ni
            st_ref[d, :, lo:hi] = hr
            st_ref[d, :, S5_STATE_W + lo:S5_STATE_W + hi] = hi_

            ys_ref[d, k] = _dot(hs_ref[d, k].astype(BF16), cout_ref[d, k])
            for b in range(SUBLANES):
                c0 = b * S5_WIDTH + k * LANES
                y_ref[:, c0:c0 + LANES] = ys_ref[d, k, pl.ds(b, SCAN_T, stride=SUBLANES), :]

    finf_ref[...] = st_ref[0]
    finb_ref[...] = st_ref[1]


def _s5_scan(e, u_tm, a, b_in, c_out, h0):
    rev = lambda j: N_CHUNKS - 1 - j
    width = SUBLANES * S5_WIDTH
    st_w = 2 * S5_STATE_W
    row_f = pl.BlockSpec((SCAN_T, width), lambda j: (j, 0))
    row_b = pl.BlockSpec((SCAN_T, width), lambda j: (rev(j), 0))
    h0_f = pl.BlockSpec((None, None, None, SUBLANES, st_w), lambda j: (e, 0, _seq_of_chunk(j), 0, 0))
    h0_b = pl.BlockSpec((None, None, None, SUBLANES, st_w), lambda j: (e, 1, _seq_of_chunk(rev(j)), 0, 0))
    st_f = pl.BlockSpec((None, SUBLANES, st_w), lambda j: (_seq_of_chunk(j), 0, 0))
    st_b = pl.BlockSpec((None, SUBLANES, st_w), lambda j: (_seq_of_chunk(rev(j)), 0, 0))
    st_shape = jax.ShapeDtypeStruct((3, SUBLANES, st_w), F32)
    return pl.pallas_call(
        _s5_scan_kernel,
        grid=(N_CHUNKS,),
        in_specs=[row_f, row_b, _layer_spec(a.shape[1:], e), _layer_spec(b_in.shape[1:], e),
                  _layer_spec(c_out.shape[1:], e), h0_f, h0_b],
        out_specs=[row_f, row_b, st_f, st_b],
        out_shape=[jax.ShapeDtypeStruct((N_TM_ROWS, width), F32)] * 2 + [st_shape] * 2,
        scratch_shapes=[pltpu.VMEM((2, S5_BLK, SCAN_R, LANES), F32),
                        pltpu.VMEM((2, S5_BLK, SCAN_R, 2 * S5_BLK_W), F32),
                        pltpu.VMEM((2, S5_BLK, SCAN_R, LANES), F32),
                        pltpu.VMEM((2, SUBLANES, st_w), F32)],
        compiler_params=_cparams(("arbitrary",)),
        name="s5_scan",
    )(u_tm, u_tm, a, b_in, c_out, h0, h0)


BF16_SUBLANES = 16


def _cast_specs(w_stack, layer, n_steps):
    _, r, c = w_stack.shape
    rc = BF16_SUBLANES
    while r % rc or r // rc > n_steps:
        rc += BF16_SUBLANES
    last = r // rc - 1
    return (pl.BlockSpec((None, rc, c), lambda i: (layer, jnp.minimum(i, last), 0)),
            pl.BlockSpec((rc, c), lambda i: (jnp.minimum(i, last), 0)),
            jax.ShapeDtypeStruct((r, c), BF16))


def _cast_side_job(in_refs, out_refs):
    for wi, wo in zip(in_refs, out_refs):
        wo[...] = wi[...].astype(BF16)


def _even_pre_kernel(l, n_x, n_cast, *refs):
    x_refs, refs = refs[:n_x], refs[n_x:]
    mod_ref, g_ref, w_ref, cos_ref, sa_ref, sb_ref = refs[:6]
    u_ref, q_ref, k_ref, v_ref, kf_ref, vf_ref = refs[6 + n_cast:12 + n_cast]
    _cast_side_job(refs[6:6 + n_cast], refs[12 + n_cast:])
    m = mod_ref[...]
    h = _rms(_read_stream(x_refs), _row(g_ref, l)) * (1.0 + m[1:2]) + m[0:1]
    proj = _dot(h.astype(BF16), w_ref[...])
    cos, sa, sb = cos_ref[...], sa_ref[...], sb_ref[...]
    half = DIFF_HEAD_DIM // 2

    def rope(z):
        return z * cos + pltpu.roll(z, half, 1) * sa + pltpu.roll(z, DIFF_WIDTH - half, 1) * sb

    u_ref[...] = proj[:, :S5_WIDTH]
    k = rope(proj[:, S5_WIDTH + DIFF_WIDTH:S5_WIDTH + 2 * DIFF_WIDTH])
    v = proj[:, S5_WIDTH + 2 * DIFF_WIDTH:]
    q_ref[...] = (rope(proj[:, S5_WIDTH:S5_WIDTH + DIFF_WIDTH]) * (DIFF_HEAD_DIM ** -0.5 * LOG2E)).astype(BF16)
    k_ref[...] = k.astype(BF16)
    v_ref[...] = v.astype(BF16)

    @pl.when(pl.program_id(0) < P_TILES)
    def _():
        kf_ref[...] = k
        vf_ref[...] = v


def _even_pre(l, xs, mod, g_pre, w_in, rope_tabs, cast=()):
    rope_spec = pl.BlockSpec((TM, DIFF_WIDTH), lambda i: (_rope_block(i), 0))
    casts = [_cast_specs(w, lyr, N_TILES) for w, lyr in cast]
    return pl.pallas_call(
        functools.partial(_even_pre_kernel, l, len(xs), len(casts)),
        grid=(N_TILES,),
        in_specs=_stream_specs(xs, D_MODEL) + [
            _mod_spec(l),
            _const_spec(g_pre.shape),
            _layer_spec((D_MODEL, EVEN_IN), l // 2),
            rope_spec, rope_spec, rope_spec,
        ] + [c[0] for c in casts],
        out_specs=[
            pl.BlockSpec((TM, S5_WIDTH), lambda i: (_tm_row_block(i), _tm_col_block(i))),
            _tok_spec(DIFF_WIDTH), _tok_spec(DIFF_WIDTH), _tok_spec(DIFF_WIDTH),
            _ctx_spec(DIFF_WIDTH), _ctx_spec(DIFF_WIDTH),
        ] + [c[1] for c in casts],
        out_shape=[jax.ShapeDtypeStruct((N_TM_ROWS, SUBLANES * S5_WIDTH), F32)]
        + [jax.ShapeDtypeStruct((N_TOK, DIFF_WIDTH), BF16)] * 3
        + [jax.ShapeDtypeStruct((N_P, DIFF_WIDTH), F32)] * 2 + [c[2] for c in casts],
        compiler_params=_cparams(("arbitrary",)),
        name="even_pre",
    )(*xs, mod, g_pre, w_in, *rope_tabs, *[w for w, _ in cast])


def _softmax2_parts(parts):
    m = functools.reduce(jnp.maximum, [jnp.max(s, axis=-1, keepdims=True) for s in parts])
    ps = [jnp.exp2(s - m) for s in parts]
    l = functools.reduce(jnp.add, [jnp.sum(p, axis=-1, keepdims=True) for p in ps])
    return ps, l


def _diff_attn_kernel(e, lam_init, has_ctx, *refs):
    if has_ctx:
        q_ref, k_ref, v_ref, kc_ref, vc_ref, lq1, lk1, lq2, lk2, g_ref, o_ref = refs
    else:
        q_ref, k_ref, v_ref, lq1, lk1, lq2, lk2, g_ref, o_ref = refs
    lam = (jnp.exp(jnp.sum(_row(lq1, e) * _row(lk1, e), axis=-1, keepdims=True))
           - jnp.exp(jnp.sum(_row(lq2, e) * _row(lk2, e), axis=-1, keepdims=True)) + lam_init)
    first = lax.broadcasted_iota(jnp.int32, (1, 2 * DIFF_HEAD_DIM), 1) < DIFF_HEAD_DIM
    g = _row(g_ref, e)
    def head_slice(h):
        return slice(h * 2 * DIFF_HEAD_DIM, (h + 1) * 2 * DIFF_HEAD_DIM)

    def scores(h):
        sl = head_slice(h)
        qh = q_ref[:, sl]
        zero = jnp.zeros_like(qh)
        keys = ([kc_ref[:, sl]] if has_ctx else []) + [k_ref[:, sl]]
        return [[_dot_nt(qc, kk) for kk in keys] for qc in (jnp.where(first, qh, zero), jnp.where(first, zero, qh))]

    def finish(w, l1, sl):
        vals = ([vc_ref[:, sl]] if has_ctx else []) + [v_ref[:, sl]]
        o = functools.reduce(jnp.add, [_dot(a, vv) for a, vv in zip(w, vals)]) * (1.0 / l1)
        o_ref[:, sl] = (_rms(o, g) * (1.0 - lam_init)).astype(o_ref.dtype)

    s_next = scores(0)
    pending = None
    for h in range(DIFF_HEADS):
        sl = head_slice(h)
        s1, s2 = s_next
        if h + 1 < DIFF_HEADS:
            s_next = scores(h + 1)
        p1, l1 = _softmax2_parts(s1)
        p2, l2 = _softmax2_parts(s2)
        ratio = lam * l1 / l2
        w = [(a - b * ratio).astype(BF16) for a, b in zip(p1, p2)]
        if pending is not None:
            finish(*pending)
        pending = (w, l1, sl)
    finish(*pending)


def _diff_attention(e, q, k, v, ctx, lam_vecs, subln_g, lam_init, *, batch, n, row0, tq):
    nq = n // tq
    has_ctx = ctx is not None
    in_specs = [
        pl.BlockSpec((tq, DIFF_WIDTH), lambda b, i: (row0 // tq + b * nq + i, 0)),
        pl.BlockSpec((n, DIFF_WIDTH), lambda b, i: (row0 // n + b, 0)),
        pl.BlockSpec((n, DIFF_WIDTH), lambda b, i: (row0 // n + b, 0)),
    ]
    args = [q, k, v]
    if has_ctx:
        in_specs += [pl.BlockSpec((None, None, PAST_LEN, DIFF_WIDTH), lambda b, i: (b, e, 0, 0))] * 2
        args += list(ctx)
    in_specs += [_const_spec(t.shape) for t in lam_vecs] + [_const_spec(subln_g.shape)]
    args += list(lam_vecs) + [subln_g]
    return pl.pallas_call(
        functools.partial(_diff_attn_kernel, e, lam_init, has_ctx),
        grid=(batch, nq),
        in_specs=in_specs,
        out_specs=pl.BlockSpec((tq, DIFF_WIDTH), lambda b, i: (b * nq + i, 0)),
        out_shape=jax.ShapeDtypeStruct((batch * n, DIFF_WIDTH), BF16),
        compiler_params=_cparams(("parallel", "parallel")),
        name="diff_attn_ctx" if has_ctx else "diff_attn",
    )(*args)


POST_SUB = 2


def _sub_rows(v):
    n = v.shape[0] // POST_SUB
    return [v[r * n:(r + 1) * n] for r in range(POST_SUB)]


def _post_tail(l, xs, m, mixes, gpm_ref, gpf_ref, gqf_ref, wg_ref, wu_ref, wd_ref):
    x1 = [x + m[2:3] * _rms(mix, _row(gpm_ref, l)) for x, mix in zip(xs, mixes)]
    h = [(_rms(v, _row(gpf_ref, l)) * (1.0 + m[4:5]) + m[3:4]).astype(BF16) for v in x1]
    act = [jax.nn.silu(_dot(v, wg_ref[...])) * _dot(v, wu_ref[...]) for v in h]
    y = [_dot(v.astype(BF16), wd_ref[...]) for v in act]
    return jnp.concatenate([a + m[5:6] * _rms(b, _row(gqf_ref, l)) for a, b in zip(x1, y)], axis=0)


def _write_stream(out_refs, rows, x2, tm=TM):
    if len(out_refs) == 1:
        out_refs[0][rows, :] = x2
    else:
        i = pl.program_id(0)

        @pl.when(i < N_P // tm)
        def _():
            out_refs[0][rows, :] = x2

        @pl.when(i >= N_P // tm)
        def _():
            out_refs[1][rows, :] = x2


def _ffn_specs():
    vec = _const_spec((DEPTH, D_MODEL))
    return [vec, vec, vec, _const_spec((D_MODEL, D_FF)), _const_spec((D_MODEL, D_FF)), _const_spec((D_FF, D_MODEL))]


def _out_stream(split, tm=TM):
    if split:
        return ([_ctx_spec(D_MODEL, tm), _lat_spec(D_MODEL, tm)],
                [jax.ShapeDtypeStruct((N_P, D_MODEL), F32), jax.ShapeDtypeStruct((N_S, D_MODEL), F32)])
    return [_tok_spec(D_MODEL, tm)], [jax.ShapeDtypeStruct((N_TOK, D_MODEL), F32)]


TM_POST = 512
P_POST = N_P // TM_POST


def _even_post_kernel(l, n_x, n_out, n_cast, *refs):
    e = l // 2
    x_refs, refs = refs[:n_x], refs[n_x:]
    (mod_ref, yfp_ref, ybp_ref, up_ref, yfs_ref, ybs_ref, us_ref, d_ref, gw_ref, gb_ref, dap_ref, das_ref,
     wos_ref, wod_ref, gpm_ref, gpf_ref, gqf_ref, wg_ref, wu_ref, wd_ref) = refs[:20]
    m = mod_ref[...]
    d = _row(d_ref, e)
    yp = yfp_ref[...] + ybp_ref[...] + jnp.concatenate([d, d], axis=1) * up_ref[...]
    ys = yfs_ref[...] + ybs_ref[...] + d * us_ref[...]
    y = jnp.where(pl.program_id(0) < P_POST,
                  jnp.concatenate([yp[:, :S5_WIDTH], yp[:, S5_WIDTH:]], axis=0), ys)
    g = [jax.nn.gelu(v) for v in _sub_rows(y)]
    s5 = [v * jax.nn.sigmoid(_dot(v.astype(BF16), gw_ref[...]) + _row(gb_ref, e)) for v in g]
    mixes = [_dot(v.astype(BF16), wos_ref[...]) + _dot(da, wod_ref[...])
             for v, da in zip(s5, _sub_rows(_read_stream((dap_ref, das_ref), TM_POST)))]
    x2 = _post_tail(l, _sub_rows(_read_stream(x_refs, TM_POST)), m, mixes, gpm_ref, gpf_ref, gqf_ref,
                    wg_ref, wu_ref, wd_ref)
    n_in = 20 + n_cast
    _write_stream(refs[n_in:n_in + n_out], slice(None), x2, TM_POST)
    _cast_side_job(refs[20:n_in], refs[n_in + n_out:])


def _even_post(l, xs, mod, y_f, y_b, u_tm, s5_d, glu_w, glu_b, da_p, da_s, w_out, ffn_args, split_out, cast=()):
    e = l // 2
    tm = TM_POST
    halves = DEC_SEQ // tm
    ctx_view = pl.BlockSpec((SEQ, 2 * S5_WIDTH),
                            lambda i: (jnp.minimum(i, P_POST - 1) // 4, jnp.minimum(i, P_POST - 1) % 4))
    lat_view = pl.BlockSpec((tm, S5_WIDTH),
                            lambda i: (2 * SEQ // tm + jnp.maximum(i - P_POST, 0) % halves,
                                       jnp.maximum(i - P_POST, 0) // halves))
    out_specs, out_shape = _out_stream(split_out, tm)
    casts = [_cast_specs(w, lyr, N_TOK // tm) for w, lyr in cast]
    w_half = lambda r: pl.BlockSpec((None, S5_WIDTH, D_MODEL), lambda i: (e, r, 0), pipeline_mode=pl.Buffered(1))
    return pl.pallas_call(
        functools.partial(_even_post_kernel, l, len(xs), len(out_specs), len(casts)),
        grid=(N_TOK // tm,),
        in_specs=_stream_specs(xs, D_MODEL, tm) + [
            _mod_spec(l, tm),
            ctx_view, ctx_view, ctx_view, lat_view, lat_view, lat_view,
            _const_spec(s5_d.shape), _layer_spec((S5_WIDTH, S5_WIDTH), e), _const_spec(glu_b.shape),
            _ctx_spec(DIFF_WIDTH, tm), _lat_spec(DIFF_WIDTH, tm),
            w_half(0), w_half(1),
        ] + _ffn_specs() + [c[0] for c in casts],
        out_specs=out_specs + [c[1] for c in casts],
        out_shape=out_shape + [c[2] for c in casts],
        compiler_params=_cparams(("arbitrary",), VMEM_LIMIT_POST),
        name="even_post_ffn",
    )(*xs, mod, y_f, y_b, u_tm, y_f, y_b, u_tm, s5_d, glu_w, glu_b, da_p, da_s, w_out, w_out, *ffn_args,
      *[w for w, _ in cast])


def _odd_pre_kernel(l, n_x, *refs):
    o = l // 2
    x_refs, refs = refs[:n_x], refs[n_x:]
    (mod_ref, g_ref, w_ref, gq_ref, gkv_ref, wq_ref, cos_ref, sin_ref,
     q_ref, ckr_ref, ckv_ref, kr_ref) = refs
    m = mod_ref[...]
    kr0 = MLA_Q_RANK + MLA_KV_RANK
    n_sub = TM_ODD // POST_SUB
    subs = [slice(r * n_sub, (r + 1) * n_sub) for r in range(POST_SUB)]
    x = _read_stream(x_refs, TM_ODD)
    h = [_rms(x[rs], _row(g_ref, l)) * (1.0 + m[1:2]) + m[0:1] for rs in subs]
    proj = [_dot(v.astype(BF16), w_ref[...]) for v in h]
    cq = [_rms(p[:, :MLA_Q_RANK], _row(gq_ref, o)) for p in proj]
    ckv = [_rms(p[:, MLA_Q_RANK:kr0], _row(gkv_ref, o)) for p in proj]
    q = [_dot(v.astype(BF16), wq_ref[...]) * ((MLA_NOPE + MLA_ROPE) ** -0.5 * LOG2E) for v in cq]
    for rs, p, qv, cv in zip(subs, proj, q, ckv):
        cos, sin = cos_ref[rs, :], sin_ref[rs, :]
        krp = p[:, kr0:kr0 + MLA_HEAD_PAD]
        krp_sw = p[:, kr0 + MLA_HEAD_PAD:]
        for hd in range(MLA_HEADS):
            sl = slice(hd * MLA_HEAD_PAD, (hd + 1) * MLA_HEAD_PAD)
            sw = slice(MLA_Q_W + hd * MLA_HEAD_PAD, MLA_Q_W + (hd + 1) * MLA_HEAD_PAD)
            q_ref[rs, sl] = (qv[:, sl] * cos + qv[:, sw] * sin).astype(BF16)
        ckr_ref[rs, :MLA_KV_RANK] = cv.astype(BF16)
        ckr_ref[rs, MLA_KV_RANK:] = (krp * cos + krp_sw * sin).astype(BF16)

    @pl.when(pl.program_id(0) < N_P // TM_ODD)
    def _():
        for rs, p, cv in zip(subs, proj, ckv):
            ckv_ref[rs, :] = cv
            kr_ref[rs, :] = p[:, kr0:kr0 + MLA_HEAD_PAD]


def _odd_pre(l, xs, mod, g_pre, w_in, gq, gkv, wq, rope_tabs):
    o = l // 2
    tm = TM_ODD
    rope_spec = pl.BlockSpec((tm, MLA_HEAD_PAD), lambda i: (_rope_block(i, tm), 0))
    return pl.pallas_call(
        functools.partial(_odd_pre_kernel, l, len(xs)),
        grid=(N_TOK // tm,),
        in_specs=_stream_specs(xs, D_MODEL, tm) + [
            _mod_spec(l, tm),
            _const_spec(g_pre.shape),
            _layer_spec((D_MODEL, MLA_IN_W), o),
            _const_spec(gq.shape), _const_spec(gkv.shape),
            _layer_spec((MLA_Q_RANK, 2 * MLA_Q_W), o),
            rope_spec, rope_spec,
        ],
        out_specs=[_tok_spec(MLA_Q_W, tm), _tok_spec(MLA_CKR_W, tm), _ctx_spec(MLA_KV_RANK, tm),
                   _ctx_spec(MLA_HEAD_PAD, tm)],
        out_shape=[
            jax.ShapeDtypeStruct((N_TOK, MLA_Q_W), BF16),
            jax.ShapeDtypeStruct((N_TOK, MLA_CKR_W), BF16),
            jax.ShapeDtypeStruct((N_P, MLA_KV_RANK), F32),
            jax.ShapeDtypeStruct((N_P, MLA_HEAD_PAD), F32),
        ],
        compiler_params=_cparams(("arbitrary",)),
        name="odd_pre",
    )(*xs, mod, g_pre, w_in, gq, gkv, wq, *rope_tabs)


def _mla_attn_kernel(has_ctx, *refs):
    if has_ctx:
        q_ref, ckr_ref, ckrc_ref, wk_ref, wv_ref, o_ref, kf_ref, vf_ref = refs
    else:
        q_ref, ckr_ref, wk_ref, wv_ref, o_ref, kf_ref, vf_ref = refs
    off = PAST_LEN if has_ctx else 0

    @pl.when(pl.program_id(1) == 0)
    def _():
        srcs = [(0, ckrc_ref)] if has_ctx else []
        srcs.append((off, ckr_ref))
        wv = wv_ref[...] if has_ctx else wv_ref[:, :MLA_O_W] + wv_ref[:, MLA_O_W:]
        for r0, src in srcs:
            c = src[...]
            rows = pl.ds(r0, c.shape[0])
            kf_ref[rows, :] = _dot(c, wk_ref[...]).astype(BF16)
            vf_ref[rows, :] = _dot(c[:, :MLA_KV_RANK], wv).astype(BF16)

    low = lax.broadcasted_iota(jnp.int32, (1, LANES), 1) < MLA_V
    def scores(hd):
        cs = slice(hd * MLA_HEAD_PAD, (hd + 1) * MLA_HEAD_PAD)
        return _dot_nt(q_ref[:, cs], kf_ref[:, cs])

    pv, rl = {}, {}

    def value_product(hd, p):
        j, t = divmod(hd, 2)
        slot = slice(j * LANES, (j + 1) * LANES)
        v_cols = slice(t * MLA_O_W + j * LANES, t * MLA_O_W + (j + 1) * LANES) if has_ctx else slot
        pv[hd] = _dot(p, vf_ref[:, v_cols])
        if t == 1:
            a, b = pv.pop(hd - 1), pv.pop(hd)
            if has_ctx:
                o = (a + b) * jnp.where(low, rl[hd - 1], rl[hd])
            else:
                o = jnp.where(low, a * rl[hd - 1], b * rl[hd])
            o_ref[:, slot] = o.astype(o_ref.dtype)

    s_next = scores(0)
    for hd in range(MLA_HEADS):
        s = s_next
        if hd + 1 < MLA_HEADS:
            s_next = scores(hd + 1)
        mx = jnp.max(s, axis=-1, keepdims=True)
        p = jnp.exp2(s - mx)
        rl[hd] = 1.0 / jnp.sum(p, axis=-1, keepdims=True)
        value_product(hd, p.astype(BF16))


def _mla_attention(o, q, ckr, ckr_ctx, wk, wv, *, batch, n, row0, tq):
    nq = n // tq
    has_ctx = ckr_ctx is not None
    s_len = n + (PAST_LEN if has_ctx else 0)
    in_specs = [
        pl.BlockSpec((tq, MLA_Q_W), lambda b, i: (row0 // tq + b * nq + i, 0)),
        pl.BlockSpec((n, MLA_CKR_W), lambda b, i: (row0 // n + b, 0)),
    ]
    args = [q, ckr]
    if has_ctx:
        in_specs.append(pl.BlockSpec((None, None, PAST_LEN, MLA_CKR_W), lambda b, i: (b, o, 0, 0)))
        args.append(ckr_ctx)
    in_specs += [_layer_spec(wk.shape[1:], o), _layer_spec(wv.shape[1:], o)]
    args += [wk, wv]
    return pl.pallas_call(
        functools.partial(_mla_attn_kernel, has_ctx),
        grid=(batch, nq),
        in_specs=in_specs,
        out_specs=pl.BlockSpec((tq, MLA_O_W), lambda b, i: (b * nq + i, 0)),
        out_shape=jax.ShapeDtypeStruct((batch * n, MLA_O_W), BF16),
        scratch_shapes=[pltpu.VMEM((s_len, MLA_Q_W), BF16),
                        pltpu.VMEM((s_len, (2 if has_ctx else 1) * MLA_O_W), BF16)],
        compiler_params=_cparams(("parallel", "arbitrary")),
        name="mla_attn_ctx" if has_ctx else "mla_attn",
    )(*args)


def _odd_post_kernel(l, n_x, n_out, n_cast, *refs):
    x_refs, refs = refs[:n_x], refs[n_x:]
    mod_ref, ap_ref, as_ref, wo_ref, gpm_ref, gpf_ref, gqf_ref, wg_ref, wu_ref, wd_ref = refs[:10]
    mixes = [_dot(a, wo_ref[...]) for a in _sub_rows(_read_stream((ap_ref, as_ref), TM_ODD))]
    x2 = _post_tail(l, _sub_rows(_read_stream(x_refs, TM_ODD)), mod_ref[...], mixes, gpm_ref, gpf_ref, gqf_ref,
                    wg_ref, wu_ref, wd_ref)
    n_in = 10 + n_cast
    _write_stream(refs[n_in:n_in + n_out], slice(None), x2, TM_ODD)
    _cast_side_job(refs[10:n_in], refs[n_in + n_out:])


def _odd_post(l, xs, mod, at_p, at_s, w_out, ffn_args, split_out, cast=()):
    tm = TM_ODD
    out_specs, out_shape = _out_stream(split_out, tm)
    casts = [_cast_specs(w, lyr, N_TOK // tm) for w, lyr in cast]
    return pl.pallas_call(
        functools.partial(_odd_post_kernel, l, len(xs), len(out_specs), len(casts)),
        grid=(N_TOK // tm,),
        in_specs=_stream_specs(xs, D_MODEL, tm) + [
            _mod_spec(l, tm),
            _ctx_spec(MLA_O_W, tm), _lat_spec(MLA_O_W, tm),
            _layer_spec((MLA_O_W, D_MODEL), l // 2),
        ] + _ffn_specs() + [c[0] for c in casts],
        out_specs=out_specs + [c[1] for c in casts],
        out_shape=out_shape + [c[2] for c in casts],
        compiler_params=_cparams(("arbitrary",)),
        name="odd_post_ffn",
    )(*xs, mod, at_p, at_s, w_out, *ffn_args, *[w for w, _ in cast])


def _rope_angles(rot_dim):
    rows = DEC_SEQ // GRID_W
    row = np.repeat(np.arange(rows, dtype=np.float32), GRID_W)
    col = np.tile(np.arange(GRID_W, dtype=np.float32), rows)
    n_freq = rot_dim // 4
    inv = (np.float32(ROPE_BASE) ** (-np.arange(n_freq, dtype=np.float32) / np.float32(n_freq))).astype(np.float32)
    ang = np.concatenate([row[:, None] * inv, col[:, None] * inv], axis=-1).astype(np.float32)
    return np.cos(ang), np.sin(ang)


def _with_identity(tm, cos, *sins):
    one = np.ones((tm, cos.shape[1]), np.float32)
    zero = np.zeros((tm, cos.shape[1]), np.float32)
    return (np.concatenate([one, cos]),) + tuple(np.concatenate([zero, s]) for s in sins)


def _diff_rope_tables():
    c, s = _rope_angles(DIFF_HEAD_DIM)
    z = np.zeros_like(s)
    reps = DIFF_WIDTH // DIFF_HEAD_DIM
    cos = np.tile(np.concatenate([c, c], axis=1), (1, reps))
    sa = np.tile(np.concatenate([z, s], axis=1), (1, reps))
    sb = np.tile(np.concatenate([-s, z], axis=1), (1, reps))
    return _with_identity(TM, cos, sa, sb)


def _mla_rope_tables():
    c, s = _rope_angles(MLA_ROPE)
    n = c.shape[0]
    ones, zeros = (lambda w: np.ones((n, w), np.float32)), (lambda w: np.zeros((n, w), np.float32))
    cos = np.concatenate([ones(MLA_NOPE), c, c, ones(MLA_PAD)], axis=1)
    sin = np.concatenate([zeros(MLA_NOPE), -s, s, zeros(MLA_PAD)], axis=1)
    return _with_identity(TM_ODD, cos, sin)


def _swap_rope_halves(w):
    half = MLA_ROPE // 2
    return jnp.concatenate([w[..., half:], w[..., :half]], axis=-1)


def _mla_weights(w_in_odd, w_q_up, w_kv_up):
    w_in_odd, w_q_up, w_kv_up = (w.astype(BF16) for w in (w_in_odd, w_q_up, w_kv_up))
    zeros = lambda *s: jnp.zeros(s, BF16)
    kr0 = MLA_Q_RANK + MLA_KV_RANK
    w_kr = w_in_odd[:, :, kr0:]
    lead = (N_ODD, D_MODEL)
    w_in = jnp.concatenate(
        [w_in_odd[:, :, :kr0],
         zeros(*lead, MLA_NOPE), w_kr, zeros(*lead, MLA_PAD),
         zeros(*lead, MLA_NOPE), _swap_rope_halves(w_kr), zeros(*lead, MLA_PAD)], axis=-1).astype(BF16)
    wq4 = w_q_up.reshape(N_ODD, MLA_Q_RANK, MLA_HEADS, MLA_NOPE + MLA_ROPE)
    lead = (N_ODD, MLA_Q_RANK, MLA_HEADS)
    wq_main = jnp.concatenate([wq4, zeros(*lead, MLA_PAD)], axis=-1)
    wq_swap = jnp.concatenate([zeros(*lead, MLA_NOPE), _swap_rope_halves(wq4[..., MLA_NOPE:]),
                               zeros(*lead, MLA_PAD)], axis=-1)
    wq = jnp.concatenate([wq_main.reshape(N_ODD, MLA_Q_RANK, MLA_Q_W),
                          wq_swap.reshape(N_ODD, MLA_Q_RANK, MLA_Q_W)], axis=-1).astype(BF16)
    wkv = w_kv_up.reshape(N_ODD, MLA_KV_RANK, MLA_HEADS, MLA_NOPE + MLA_V)
    wk_top = jnp.pad(wkv[..., :MLA_NOPE], ((0, 0), (0, 0), (0, 0), (0, MLA_HEAD_PAD - MLA_NOPE)))
    sel = jnp.pad(jnp.eye(MLA_ROPE, dtype=BF16), ((MLA_NOPE, MLA_PAD), (MLA_NOPE, MLA_PAD)))
    wk_bot = jnp.broadcast_to(jnp.tile(sel, (1, MLA_HEADS)), (N_ODD, MLA_HEAD_PAD, MLA_Q_W))
    wk = jnp.concatenate([wk_top.reshape(N_ODD, MLA_KV_RANK, MLA_Q_W), wk_bot], axis=1).astype(BF16)
    even = (jnp.arange(MLA_HEADS) % 2 == 0)[:, None]
    wv4 = wkv[..., MLA_NOPE:]
    wv = jnp.concatenate([jnp.where(even, wv4, 0.0).reshape(N_ODD, MLA_KV_RANK, MLA_O_W),
                          jnp.where(even, 0.0, wv4).reshape(N_ODD, MLA_KV_RANK, MLA_O_W)], axis=-1).astype(BF16)
    return w_in, wq, wk, wv


def kernel(x_prompt, x_sample, state_s5_re, state_s5_im, cache_diff_k, cache_diff_v, cache_mla_ckv, cache_mla_krope, c, c_ctx, w_mod, b_mod, g_pre_mix, g_post_mix, g_pre_ffn, g_post_ffn, w_ffn_gate, w_ffn_up, w_ffn_down, w_in_even, w_out_even, s5_lam_re, s5_lam_im, s5_log_dt, s5_b_re, s5_b_im, s5_c_re, s5_c_im, s5_d, s5_glu_w, s5_glu_b, diff_lam_q1, diff_lam_k1, diff_lam_q2, diff_lam_k2, diff_subln_g, w_in_odd, mla_q_norm_g, mla_w_q_up, mla_kv_norm_g, mla_w_kv_up, w_out_odd):
    xs = (x_prompt.reshape(N_P, D_MODEL), x_sample.reshape(N_S, D_MODEL))
    conds = jnp.concatenate([c_ctx[None, :], c, jnp.zeros((N_COND - 1 - DEC_BATCH, D_MODEL), F32)], axis=0)
    mod = _modulation(conds, w_mod, b_mod).reshape(DEPTH, N_COND, 6, D_MODEL)

    gains = (g_post_mix, g_pre_ffn, g_post_ffn)
    ffn_f32 = (w_ffn_gate, w_ffn_up, w_ffn_down)
    w_in_e = w_in_even.astype(BF16)
    w_out_e = w_out_even.astype(BF16)
    glu_w = s5_glu_w.astype(BF16)
    a_re, a_im, b_in, c_out = _s5_params(s5_lam_re, s5_lam_im, s5_log_dt, s5_b_re, s5_b_im, s5_c_re, s5_c_im)
    a = jnp.stack([a_re[:, 0], a_im[:, 0], a_re[:, 1], a_im[:, 1]], axis=1).reshape(N_EVEN, 4, S5_STATE_W)
    st = lambda s: jnp.moveaxis(s, 0, 2).reshape(N_EVEN, 2, 1, DEC_BATCH, S5_STATE_W)
    h0 = jnp.concatenate([jnp.zeros((N_EVEN, 2, 2, SUBLANES, 2 * S5_STATE_W), F32),
                          jnp.concatenate([st(state_s5_re), st(state_s5_im)], axis=-1)], axis=2)
    diff_ctx = (cache_diff_k.astype(BF16).reshape(DEC_BATCH, N_EVEN, PAST_LEN, DIFF_WIDTH),
                cache_diff_v.astype(BF16).reshape(DEC_BATCH, N_EVEN, PAST_LEN, DIFF_WIDTH))
    lam_vecs = (diff_lam_q1, diff_lam_k1, diff_lam_q2, diff_lam_k2)
    w_in_o, wq, wk, wv = _mla_weights(w_in_odd, mla_w_q_up, mla_w_kv_up)
    w_out_o = w_out_odd.astype(BF16)
    ckr_ctx = jnp.concatenate(
        [cache_mla_ckv.astype(BF16),
         jnp.pad(cache_mla_krope.astype(BF16), ((0, 0), (0, 0), (0, 0), (MLA_NOPE, MLA_PAD)))], axis=-1)
    diff_tabs = _diff_rope_tables()
    mla_tabs = _mla_rope_tables()

    s5_re_list, s5_im_list, dk_list, dv_list, ckv_list, kr_list = [], [], [], [], [], []
    for l in range(DEPTH):
        last = l == DEPTH - 1
        n_stream = 2 if last else 1
        next_cast = () if last else tuple((w, l + 1) for w in ffn_f32)
        if l % 2 == 0:
            e = l // 2
            lam_init = 0.8 - 0.6 * math.exp(-0.3 * l)
            pre = _even_pre(l, xs, mod, g_pre_mix, w_in_e, diff_tabs,
                            cast=tuple((w, 0) for w in ffn_f32) if l == 0 else ())
            u_tm, q, k, v, kf, vf = pre[:6]
            if l == 0:
                ffn_w = tuple(pre[6:])
            y_f, y_b, fin_f, fin_b = _s5_scan(e, u_tm, a, b_in, c_out, h0)
            fin = jnp.stack([fin_f[:2].reshape(BATCH, 2, S5_GROUPS, S5_STATE),
                             fin_b[:2].reshape(BATCH, 2, S5_GROUPS, S5_STATE)], axis=1)
            s5_re_list.append(fin[:, :, 0])
            s5_im_list.append(fin[:, :, 1])
            da_p = _diff_attention(e, q, k, v, None, lam_vecs, diff_subln_g, lam_init,
                                   batch=BATCH, n=SEQ, row0=0, tq=SEQ)
            da_s = _diff_attention(e, q, k, v, diff_ctx, lam_vecs, diff_subln_g, lam_init,
                                   batch=DEC_BATCH, n=DEC_SEQ, row0=N_P, tq=256)
            post = _even_post(l, xs, mod, y_f, y_b, u_tm, s5_d, glu_w, s5_glu_b, da_p, da_s, w_out_e,
                              gains + ffn_w, last, cast=next_cast)
            xs, ffn_w = post[:n_stream], tuple(post[n_stream:])
            dk_list.append(kf.reshape(BATCH, SEQ, DIFF_HEADS, 2, DIFF_HEAD_DIM))
            dv_list.append(vf.reshape(BATCH, SEQ, DIFF_HEADS, 2 * DIFF_HEAD_DIM))
        else:
            o = l // 2
            q, ckr, ckv, krp = _odd_pre(l, xs, mod, g_pre_mix, w_in_o, mla_q_norm_g, mla_kv_norm_g, wq, mla_tabs)
            at_p = _mla_attention(o, q, ckr, None, wk, wv, batch=BATCH, n=SEQ, row0=0, tq=SEQ)
            at_s = _mla_attention(o, q, ckr, ckr_ctx, wk, wv, batch=DEC_BATCH, n=DEC_SEQ, row0=N_P, tq=256)
            post = _odd_post(l, xs, mod, at_p, at_s, w_out_o, gains + ffn_w, last, cast=next_cast)
            xs, ffn_w = post[:n_stream], tuple(post[n_stream:])
            ckv_list.append(ckv.reshape(BATCH, SEQ, MLA_KV_RANK))
            kr_list.append(krp[:, MLA_NOPE:MLA_NOPE + MLA_ROPE].reshape(BATCH, SEQ, MLA_ROPE))

    return (xs[0].reshape(BATCH, SEQ, D_MODEL), xs[1].reshape(DEC_BATCH, DEC_SEQ, D_MODEL),
            jnp.stack(s5_re_list, axis=1), jnp.stack(s5_im_list, axis=1),
            jnp.stack(dk_list, axis=1), jnp.stack(dv_list, axis=1),
            jnp.stack(ckv_list, axis=1), jnp.stack(kr_list, axis=1))
```

```python
import functools
import math

import jax
import jax.numpy as jnp
import numpy as np
from jax import lax
from jax.experimental import pallas as pl
from jax.experimental.pallas import tpu as pltpu

F32 = jnp.float32
BF16 = jnp.bfloat16

D_MODEL = 1024
BATCH = 16
SEQ = 256
DEPTH = 4
DEC_BATCH = 8
DEC_SEQ = 1024
PAST_LEN = 512
GRID_W = 64
N_EVEN = (DEPTH + 1) // 2
N_ODD = DEPTH // 2
EPS = 1e-6
ROPE_BASE = 10000.0
S5_WIDTH = D_MODEL // 2
S5_GROUP = 16
S5_GROUPS = S5_WIDTH // S5_GROUP
S5_STATE = 64
DIFF_HEAD_DIM = 64
DIFF_HEADS = (D_MODEL // 2) // (2 * DIFF_HEAD_DIM)
DIFF_WIDTH = DIFF_HEADS * 2 * DIFF_HEAD_DIM
EVEN_IN = S5_WIDTH + 3 * DIFF_WIDTH
MLA_HEADS = 16
MLA_NOPE = 64
MLA_ROPE = 32
MLA_V = 64
MLA_Q_RANK = 256
MLA_KV_RANK = 128
D_FF = ((8 * D_MODEL // 3 + 255) // 256) * 256

LANES = 128
SUBLANES = 8
BF16_SUBLANES = 16
VMEM_LIMIT = 56 * 1024 * 1024
VMEM_LIMIT_POST = 61 * 1024 * 1024
LOG2E = math.log2(math.e)

N_P = BATCH * SEQ
N_S = DEC_BATCH * DEC_SEQ
N_TOK = N_P + N_S
TM = 512
N_TILES = N_TOK // TM
P_TILES = N_P // TM
LAT_TILES_PER_SEQ = DEC_SEQ // TM
ROW_SUB = 2
N_COND = 16

S5_STATE_W = S5_GROUPS * S5_STATE
S5_BLK = 4
S5_BLK_W = S5_STATE_W // S5_BLK
S5_GB = S5_GROUPS // S5_BLK
S5_IN_ROWS = S5_GB * S5_GROUP
S5_N_BLOCKS = N_EVEN * 2 * S5_BLK
S5_TM_W = SUBLANES * S5_WIDTH
SCAN_T = 64
SCAN_R = SCAN_T * SUBLANES
N_TM_ROWS = N_TOK // SUBLANES
N_CHUNKS = N_TM_ROWS // SCAN_T
SEQ_CHUNKS = SEQ // SCAN_T
CTX_CHUNKS = 2 * SEQ_CHUNKS

MLA_HEAD_PAD = LANES
MLA_PAD = MLA_HEAD_PAD - MLA_NOPE - MLA_ROPE
MLA_Q_W = MLA_HEADS * MLA_HEAD_PAD
MLA_IN_W = MLA_Q_RANK + MLA_KV_RANK + 2 * MLA_HEAD_PAD
MLA_CKR_W = MLA_KV_RANK + MLA_HEAD_PAD
MLA_O_W = MLA_HEADS * MLA_V


def _cparams(sem, vmem_limit=VMEM_LIMIT):
    return pltpu.CompilerParams(dimension_semantics=sem, vmem_limit_bytes=vmem_limit)


def _const_spec(shape):
    nd = len(shape)
    return pl.BlockSpec(shape, lambda *_: (0,) * nd, pipeline_mode=pl.Buffered(1))


def _layer_spec(tail, *lead):
    nt = len(tail)
    return pl.BlockSpec((None,) * len(lead) + tuple(tail), lambda *_: tuple(lead) + (0,) * nt,
                        pipeline_mode=pl.Buffered(1))


def _rms(x, g):
    return x * lax.rsqrt(jnp.mean(x * x, axis=-1, keepdims=True) + EPS) * g


def _dot(a, b):
    return jnp.dot(a, b, preferred_element_type=F32)


def _dot_nt(a, b):
    return lax.dot_general(a, b, (((1,), (1,)), ((), ())), preferred_element_type=F32)


def _row(ref, r):
    return ref[r:r + 1, :]


def _cond_of_tile(i):
    return jnp.where(i < P_TILES, 0, 1 + (i - P_TILES) // LAT_TILES_PER_SEQ)


def _rope_block(i):
    return jnp.where(i < P_TILES, 0, 1 + (i - P_TILES) % LAT_TILES_PER_SEQ)


def _tok_spec(width):
    return pl.BlockSpec((TM, width), lambda i: (i, 0))


def _ctx_spec(width):
    return pl.BlockSpec((TM, width), lambda i: (jnp.minimum(i, P_TILES - 1), 0))


def _lat_spec(width):
    return pl.BlockSpec((TM, width), lambda i: (jnp.maximum(i - P_TILES, 0), 0))


def _ctx_view_spec():
    pairs = SUBLANES // 2
    return pl.BlockSpec((SEQ, 2 * S5_WIDTH),
                        lambda i: (jnp.minimum(i, P_TILES - 1) // pairs, jnp.minimum(i, P_TILES - 1) % pairs))


def _lat_view_spec(row0):
    return pl.BlockSpec((TM, S5_WIDTH), lambda i: (row0 + jnp.maximum(i - P_TILES, 0) % LAT_TILES_PER_SEQ,
                                                   jnp.maximum(i - P_TILES, 0) // LAT_TILES_PER_SEQ))


def _stream_specs(arrays, width):
    return [_tok_spec(width)] if len(arrays) == 1 else [_ctx_spec(width), _lat_spec(width)]


def _read_stream(refs):
    if len(refs) == 1:
        return refs[0][...]
    return jnp.where(pl.program_id(0) < P_TILES, refs[0][...], refs[1][...])


def _write_stream(out_refs, x2):
    if len(out_refs) == 1:
        out_refs[0][...] = x2
    else:
        i = pl.program_id(0)

        @pl.when(i < P_TILES)
        def _():
            out_refs[0][...] = x2

        @pl.when(i >= P_TILES)
        def _():
            out_refs[1][...] = x2


def _out_stream(split):
    if split:
        return ([_ctx_spec(D_MODEL), _lat_spec(D_MODEL)],
                [jax.ShapeDtypeStruct((N_P, D_MODEL), F32), jax.ShapeDtypeStruct((N_S, D_MODEL), F32)])
    return [_tok_spec(D_MODEL)], [jax.ShapeDtypeStruct((N_TOK, D_MODEL), F32)]


def _mod_spec(l):
    return pl.BlockSpec((None, None, 6, D_MODEL), lambda i: (l, _cond_of_tile(i), 0, 0))


def _sub_slices():
    n = TM // ROW_SUB
    return [slice(r * n, (r + 1) * n) for r in range(ROW_SUB)]


def _sub_rows(v):
    return [v[rs] for rs in _sub_slices()]


def _cast_specs(w_stack, layer):
    _, r, c = w_stack.shape
    rc = BF16_SUBLANES
    while r % rc or r // rc > N_TILES:
        rc += BF16_SUBLANES
    last = r // rc - 1
    return (pl.BlockSpec((None, rc, c), lambda i: (layer, jnp.minimum(i, last), 0)),
            pl.BlockSpec((rc, c), lambda i: (jnp.minimum(i, last), 0)),
            jax.ShapeDtypeStruct((r, c), BF16))


def _cast_side_job(in_refs, out_refs):
    for wi, wo in zip(in_refs, out_refs):
        wo[...] = wi[...].astype(BF16)


def _mod_kernel(c_ref, w_ref, b_ref, o_ref):
    s = jax.nn.silu(c_ref[...])
    o_ref[0] = _dot(s.astype(BF16), w_ref[0].astype(BF16)) + b_ref[0]


def _modulation(conds, w_mod, b_mod):
    tn = 1536
    return pl.pallas_call(
        _mod_kernel,
        grid=(DEPTH, 6 * D_MODEL // tn),
        in_specs=[
            pl.BlockSpec((N_COND, D_MODEL), lambda l, n: (0, 0)),
            pl.BlockSpec((1, D_MODEL, tn), lambda l, n: (l, 0, n)),
            pl.BlockSpec((1, 1, tn), lambda l, n: (l, 0, n)),
        ],
        out_specs=pl.BlockSpec((1, N_COND, tn), lambda l, n: (l, 0, n)),
        out_shape=jax.ShapeDtypeStruct((DEPTH, N_COND, 6 * D_MODEL), F32),
        compiler_params=_cparams(("parallel", "parallel")),
        name="modulation",
    )(conds, w_mod, b_mod.reshape(DEPTH, 1, 6 * D_MODEL))


def _block_diag_lanes(x, row_group, col_group):
    xt = jnp.concatenate([x] * S5_GB, axis=1)
    rg = lax.broadcasted_iota(jnp.int32, xt.shape, 0) // row_group
    cg = lax.broadcasted_iota(jnp.int32, xt.shape, 1) // col_group
    return jnp.where(rg == cg, xt, 0.0)


def _s5_param_kernel(lr_ref, li_ref, ldt_ref, br_ref, bi_ref, cr_ref, ci_ref, are_ref, aim_ref, bin_ref, cout_ref):
    lr, li = lr_ref[...], li_ref[...]
    dt = jnp.exp(ldt_ref[...])
    mag = jnp.exp(lr * dt)
    a_re, a_im = mag * jnp.cos(li * dt), mag * jnp.sin(li * dt)
    den = lr * lr + li * li
    f_re = ((a_re - 1.0) * lr + a_im * li) / den
    f_im = (a_im * lr - (a_re - 1.0) * li) / den
    br, bi = br_ref[...], bi_ref[...]
    are_ref[...] = a_re
    aim_ref[...] = a_im
    bin_ref[:, :S5_BLK_W] = _block_diag_lanes(f_re * br - f_im * bi, S5_GROUP, S5_STATE).astype(BF16)
    bin_ref[:, S5_BLK_W:] = _block_diag_lanes(f_re * bi + f_im * br, S5_GROUP, S5_STATE).astype(BF16)
    cout_ref[:S5_BLK_W, :] = _block_diag_lanes(cr_ref[...], S5_STATE, S5_GROUP).astype(BF16)
    cout_ref[S5_BLK_W:, :] = _block_diag_lanes(-ci_ref[...], S5_STATE, S5_GROUP).astype(BF16)


def _s5_params(lam_re, lam_im, log_dt, b_re, b_im, c_re, c_im):
    full = (N_EVEN, 2, S5_GROUPS, S5_GROUP, S5_STATE)
    blk = (S5_N_BLOCKS, S5_IN_ROWS, S5_STATE)
    ex = lambda a: jnp.broadcast_to(a[:, :, :, None, :], full).reshape(blk)
    ldt = jnp.broadcast_to(log_dt[:, :, :, None, None], full).reshape(blk)
    bt = lambda b: jnp.swapaxes(b, -1, -2).reshape(blk)
    ct = lambda c: jnp.swapaxes(c, -1, -2).reshape(S5_N_BLOCKS, S5_BLK_W, S5_GROUP)
    spec = lambda r, w: pl.BlockSpec((None, r, w), lambda i: (i, 0, 0))
    a_re, a_im, b_in, c_out = pl.pallas_call(
        _s5_param_kernel,
        grid=(S5_N_BLOCKS,),
        in_specs=[spec(S5_IN_ROWS, S5_STATE)] * 5 + [spec(S5_BLK_W, S5_GROUP)] * 2,
        out_specs=[spec(S5_IN_ROWS, S5_STATE)] * 2 + [spec(S5_IN_ROWS, 2 * S5_BLK_W), spec(2 * S5_BLK_W, LANES)],
        out_shape=[jax.ShapeDtypeStruct(blk, F32)] * 2
        + [jax.ShapeDtypeStruct((S5_N_BLOCKS, S5_IN_ROWS, 2 * S5_BLK_W), BF16),
           jax.ShapeDtypeStruct((S5_N_BLOCKS, 2 * S5_BLK_W, LANES), BF16)],
        compiler_params=_cparams(("parallel",)),
        name="s5_params",
    )(ex(lam_re), ex(lam_im), ldt, bt(b_re), bt(b_im), ct(c_re), ct(c_im))
    pick = lambda a: a.reshape(full)[:, :, :, 0, :]
    lead = (N_EVEN, 2, S5_BLK)
    return pick(a_re), pick(a_im), b_in.reshape(lead + b_in.shape[1:]), c_out.reshape(lead + c_out.shape[1:])


def _seq_of_chunk(c):
    return (c >= SEQ_CHUNKS).astype(jnp.int32) + (c >= CTX_CHUNKS).astype(jnp.int32)


def _s5_scan_kernel(ufc_ref, ufl_ref, ubc_ref, ubl_ref, a_ref, bin_ref, cout_ref, h0f_ref, h0b_ref,
                    yf_ref, yb_ref, finf_ref, finb_ref, us_ref, hs_ref, ys_ref, st_ref):
    j = pl.program_id(0)
    cf = j
    cb = N_CHUNKS - 1 - j

    @pl.when((cf == 0) | (cf == SEQ_CHUNKS) | (cf == CTX_CHUNKS))
    def _():
        st_ref[0] = h0f_ref[...]

    @pl.when((cb == N_CHUNKS - 1) | (cb == CTX_CHUNKS - 1) | (cb == SEQ_CHUNKS - 1))
    def _():
        st_ref[1] = h0b_ref[...]

    dirs = ((0, cf, ufc_ref, ufl_ref, yf_ref), (1, cb, ubc_ref, ubl_ref, yb_ref))
    for k in range(S5_BLK):
        lo, hi = k * S5_BLK_W, (k + 1) * S5_BLK_W
        for d, chunk, uc_ref, ul_ref, y_ref in dirs:
            for b in range(SUBLANES):
                c0 = b * S5_WIDTH + k * LANES
                us_ref[d, k, pl.ds(b, SCAN_T, stride=SUBLANES), :] = jnp.where(
                    chunk < CTX_CHUNKS, uc_ref[:, c0:c0 + LANES], ul_ref[:, c0:c0 + LANES])
            hs_ref[d, k] = _dot(us_ref[d, k].astype(BF16), bin_ref[d, k])

            ar = jnp.broadcast_to(a_ref[2 * d:2 * d + 1, lo:hi], (SUBLANES, S5_BLK_W))
            ai = jnp.broadcast_to(a_ref[2 * d + 1:2 * d + 2, lo:hi], (SUBLANES, S5_BLK_W))
            hr, hi_ = st_ref[d, :, lo:hi], st_ref[d, :, S5_STATE_W + lo:S5_STATE_W + hi]
            for t in range(SCAN_T):
                rows = pl.ds((t if d == 0 else SCAN_T - 1 - t) * SUBLANES, SUBLANES)
                nr = ar * hr - ai * hi_ + hs_ref[d, k, rows, 0:S5_BLK_W]
                ni = ar * hi_ + ai * hr + hs_ref[d, k, rows, S5_BLK_W:2 * S5_BLK_W]
                hs_ref[d, k, rows, 0:S5_BLK_W] = nr
                hs_ref[d, k, rows, S5_BLK_W:2 * S5_BLK_W] = ni
                hr, hi_ = nr, ni
            st_ref[d, :, lo:hi] = hr
            st_ref[d, :, S5_STATE_W + lo:S5_STATE_W + hi] = hi_

            ys_ref[d, k] = _dot(hs_ref[d, k].astype(BF16), cout_ref[d, k])
            for b in range(SUBLANES):
                c0 = b * S5_WIDTH + k * LANES
                y_ref[:, c0:c0 + LANES] = ys_ref[d, k, pl.ds(b, SCAN_T, stride=SUBLANES), :]

    finf_ref[...] = st_ref[0]
    finb_ref[...] = st_ref[1]


def _s5_scan(e, u_ctx, u_lat, a, b_in, c_out, h0):
    fwd = lambda j: j
    rev = lambda j: N_CHUNKS - 1 - j
    st_w = 2 * S5_STATE_W
    in_c = lambda f: pl.BlockSpec((SCAN_T, S5_TM_W), lambda j: (jnp.minimum(f(j), CTX_CHUNKS - 1), 0))
    in_l = lambda f: pl.BlockSpec((SCAN_T, S5_TM_W), lambda j: (jnp.maximum(f(j) - CTX_CHUNKS, 0), 0))
    out = lambda f: pl.BlockSpec((SCAN_T, S5_TM_W), lambda j: (f(j), 0))
    h0_spec = lambda d, f: pl.BlockSpec((None, None, None, SUBLANES, st_w),
                                        lambda j: (e, d, _seq_of_chunk(f(j)), 0, 0))
    fin = lambda f: pl.BlockSpec((None, SUBLANES, st_w), lambda j: (_seq_of_chunk(f(j)), 0, 0))
    st_shape = jax.ShapeDtypeStruct((3, SUBLANES, st_w), F32)
    return pl.pallas_call(
        _s5_scan_kernel,
        grid=(N_CHUNKS,),
        in_specs=[in_c(fwd), in_l(fwd), in_c(rev), in_l(rev), _layer_spec(a.shape[1:], e),
                  _layer_spec(b_in.shape[1:], e), _layer_spec(c_out.shape[1:], e), h0_spec(0, fwd), h0_spec(1, rev)],
        out_specs=[out(fwd), out(rev), fin(fwd), fin(rev)],
        out_shape=[jax.ShapeDtypeStruct((N_TM_ROWS, S5_TM_W), F32)] * 2 + [st_shape] * 2,
        scratch_shapes=[pltpu.VMEM((2, S5_BLK, SCAN_R, LANES), F32),
                        pltpu.VMEM((2, S5_BLK, SCAN_R, 2 * S5_BLK_W), F32),
                        pltpu.VMEM((2, S5_BLK, SCAN_R, LANES), F32),
                        pltpu.VMEM((2, SUBLANES, st_w), F32)],
        compiler_params=_cparams(("arbitrary",)),
        name="s5_scan",
    )(u_ctx, u_lat, u_ctx, u_lat, a, b_in, c_out, h0, h0)


def _even_pre_kernel(l, n_x, n_cast, *refs):
    x_refs, refs = refs[:n_x], refs[n_x:]
    mod_ref, g_ref, w_ref, cos_ref, sa_ref, sb_ref = refs[:6]
    uc_ref, ul_ref, q_ref, k_ref, v_ref, kf_ref, vf_ref = refs[6 + n_cast:13 + n_cast]
    _cast_side_job(refs[6:6 + n_cast], refs[13 + n_cast:])
    m = mod_ref[...]
    half = DIFF_HEAD_DIM // 2
    subs = _sub_slices()
    h = [_rms(x, _row(g_ref, l)) * (1.0 + m[1:2]) + m[0:1] for x in _sub_rows(_read_stream(x_refs))]
    proj = [_dot(v.astype(BF16), w_ref[...]) for v in h]
    us, ks, vs = [], [], []
    for rs, p in zip(subs, proj):
        cos, sa, sb = cos_ref[rs, :], sa_ref[rs, :], sb_ref[rs, :]

        def rope(z):
            return z * cos + pltpu.roll(z, half, 1) * sa + pltpu.roll(z, DIFF_WIDTH - half, 1) * sb

        k = rope(p[:, S5_WIDTH + DIFF_WIDTH:S5_WIDTH + 2 * DIFF_WIDTH])
        v = p[:, S5_WIDTH + 2 * DIFF_WIDTH:]
        q_ref[rs, :] = (rope(p[:, S5_WIDTH:S5_WIDTH + DIFF_WIDTH]) * (DIFF_HEAD_DIM ** -0.5 * LOG2E)).astype(BF16)
        k_ref[rs, :] = k.astype(BF16)
        v_ref[rs, :] = v.astype(BF16)
        us.append(p[:, :S5_WIDTH])
        ks.append(k)
        vs.append(v)

    @pl.when(pl.program_id(0) < P_TILES)
    def _():
        uc_ref[...] = jnp.concatenate(us, axis=1)
        for rs, k, v in zip(subs, ks, vs):
            kf_ref[rs, :] = k
            vf_ref[rs, :] = v

    @pl.when(pl.program_id(0) >= P_TILES)
    def _():
        ul_ref[...] = jnp.concatenate(us, axis=0)


def _even_pre(l, xs, mod, g_pre, w_in, rope_tabs, cast=()):
    rope_spec = pl.BlockSpec((TM, DIFF_WIDTH), lambda i: (_rope_block(i), 0))
    casts = [_cast_specs(w, lyr) for w, lyr in cast]
    return pl.pallas_call(
        functools.partial(_even_pre_kernel, l, len(xs), len(casts)),
        grid=(N_TILES,),
        in_specs=_stream_specs(xs, D_MODEL) + [
            _mod_spec(l),
            _const_spec(g_pre.shape),
            _layer_spec((D_MODEL, EVEN_IN), l // 2),
            rope_spec, rope_spec, rope_spec,
        ] + [c[0] for c in casts],
        out_specs=[
            _ctx_view_spec(), _lat_view_spec(0),
            _tok_spec(DIFF_WIDTH), _tok_spec(DIFF_WIDTH), _tok_spec(DIFF_WIDTH),
            _ctx_spec(DIFF_WIDTH), _ctx_spec(DIFF_WIDTH),
        ] + [c[1] for c in casts],
        out_shape=[jax.ShapeDtypeStruct((N_P // SUBLANES, S5_TM_W), F32),
                   jax.ShapeDtypeStruct((N_S // SUBLANES, S5_TM_W), F32)]
        + [jax.ShapeDtypeStruct((N_TOK, DIFF_WIDTH), BF16)] * 3
        + [jax.ShapeDtypeStruct((N_P, DIFF_WIDTH), F32)] * 2 + [c[2] for c in casts],
        compiler_params=_cparams(("arbitrary",)),
        name="even_pre",
    )(*xs, mod, g_pre, w_in, *rope_tabs, *[w for w, _ in cast])


def _softmax2_parts(parts):
    m = functools.reduce(jnp.maximum, [jnp.max(s, axis=-1, keepdims=True) for s in parts])
    ps = [jnp.exp2(s - m) for s in parts]
    l = functools.reduce(jnp.add, [jnp.sum(p, axis=-1, keepdims=True) for p in ps])
    return ps, l


def _diff_attn_kernel(e, lam_init, has_ctx, *refs):
    if has_ctx:
        q_ref, k_ref, v_ref, kc_ref, vc_ref, lq1, lk1, lq2, lk2, g_ref, o_ref = refs
    else:
        q_ref, k_ref, v_ref, lq1, lk1, lq2, lk2, g_ref, o_ref = refs
    lam = (jnp.exp(jnp.sum(_row(lq1, e) * _row(lk1, e), axis=-1, keepdims=True))
           - jnp.exp(jnp.sum(_row(lq2, e) * _row(lk2, e), axis=-1, keepdims=True)) + lam_init)
    first = lax.broadcasted_iota(jnp.int32, (1, 2 * DIFF_HEAD_DIM), 1) < DIFF_HEAD_DIM
    g = _row(g_ref, e)

    def head_slice(h):
        return slice(h * 2 * DIFF_HEAD_DIM, (h + 1) * 2 * DIFF_HEAD_DIM)

    def scores(h):
        sl = head_slice(h)
        qh = q_ref[:, sl]
        zero = jnp.zeros_like(qh)
        keys = ([kc_ref[:, sl]] if has_ctx else []) + [k_ref[:, sl]]
        return [[_dot_nt(qc, kk) for kk in keys] for qc in (jnp.where(first, qh, zero), jnp.where(first, zero, qh))]

    def finish(w, l1, sl):
        vals = ([vc_ref[:, sl]] if has_ctx else []) + [v_ref[:, sl]]
        o = functools.reduce(jnp.add, [_dot(a, vv) for a, vv in zip(w, vals)]) * (1.0 / l1)
        o_ref[:, sl] = (_rms(o, g) * (1.0 - lam_init)).astype(o_ref.dtype)

    s_next = scores(0)
    pending = None
    for h in range(DIFF_HEADS):
        s1, s2 = s_next
        if h + 1 < DIFF_HEADS:
            s_next = scores(h + 1)
        p1, l1 = _softmax2_parts(s1)
        p2, l2 = _softmax2_parts(s2)
        ratio = lam * l1 / l2
        w = [(a - b * ratio).astype(BF16) for a, b in zip(p1, p2)]
        if pending is not None:
            finish(*pending)
        pending = (w, l1, head_slice(h))
    finish(*pending)


def _diff_attention(e, q, k, v, ctx, lam_vecs, subln_g, lam_init, *, batch, n, row0, tq):
    nq = n // tq
    has_ctx = ctx is not None
    in_specs = [
        pl.BlockSpec((tq, DIFF_WIDTH), lambda b, i: (row0 // tq + b * nq + i, 0)),
        pl.BlockSpec((n, DIFF_WIDTH), lambda b, i: (row0 // n + b, 0)),
        pl.BlockSpec((n, DIFF_WIDTH), lambda b, i: (row0 // n + b, 0)),
    ]
    args = [q, k, v]
    if has_ctx:
        in_specs += [pl.BlockSpec((None, None, PAST_LEN, DIFF_WIDTH), lambda b, i: (b, e, 0, 0))] * 2
        args += list(ctx)
    in_specs += [_const_spec(t.shape) for t in lam_vecs] + [_const_spec(subln_g.shape)]
    args += list(lam_vecs) + [subln_g]
    return pl.pallas_call(
        functools.partial(_diff_attn_kernel, e, lam_init, has_ctx),
        grid=(batch, nq),
        in_specs=in_specs,
        out_specs=pl.BlockSpec((tq, DIFF_WIDTH), lambda b, i: (b * nq + i, 0)),
        out_shape=jax.ShapeDtypeStruct((batch * n, DIFF_WIDTH), BF16),
        compiler_params=_cparams(("parallel", "parallel")),
        name="diff_attn_ctx" if has_ctx else "diff_attn",
    )(*args)


def _post_tail(l, xs, m, mixes, gpm_ref, gpf_ref, gqf_ref, wg_ref, wu_ref, wd_ref):
    x1 = [x + m[2:3] * _rms(mix, _row(gpm_ref, l)) for x, mix in zip(xs, mixes)]
    h = [(_rms(v, _row(gpf_ref, l)) * (1.0 + m[4:5]) + m[3:4]).astype(BF16) for v in x1]
    act = [jax.nn.silu(_dot(v, wg_ref[...])) * _dot(v, wu_ref[...]) for v in h]
    y = [_dot(v.astype(BF16), wd_ref[...]) for v in act]
    return jnp.concatenate([a + m[5:6] * _rms(b, _row(gqf_ref, l)) for a, b in zip(x1, y)], axis=0)


def _ffn_specs():
    vec = _const_spec((DEPTH, D_MODEL))
    return [vec, vec, vec, _const_spec((D_MODEL, D_FF)), _const_spec((D_MODEL, D_FF)), _const_spec((D_FF, D_MODEL))]


def _even_post_kernel(l, n_x, n_out, n_cast, *refs):
    e = l // 2
    x_refs, refs = refs[:n_x], refs[n_x:]
    (mod_ref, yfp_ref, ybp_ref, up_ref, yfs_ref, ybs_ref, us_ref, d_ref, gw_ref, gb_ref, dap_ref, das_ref,
     wos_ref, wod_ref, gpm_ref, gpf_ref, gqf_ref, wg_ref, wu_ref, wd_ref) = refs[:20]
    m = mod_ref[...]
    d = _row(d_ref, e)
    yp = yfp_ref[...] + ybp_ref[...] + jnp.concatenate([d, d], axis=1) * up_ref[...]
    ys = yfs_ref[...] + ybs_ref[...] + d * us_ref[...]
    y = jnp.where(pl.program_id(0) < P_TILES,
                  jnp.concatenate([yp[:, :S5_WIDTH], yp[:, S5_WIDTH:]], axis=0), ys)
    g = [jax.nn.gelu(v) for v in _sub_rows(y)]
    s5 = [v * jax.nn.sigmoid(_dot(v.astype(BF16), gw_ref[...]) + _row(gb_ref, e)) for v in g]
    mixes = [_dot(v.astype(BF16), wos_ref[...]) + _dot(da, wod_ref[...])
             for v, da in zip(s5, _sub_rows(_read_stream((dap_ref, das_ref))))]
    x2 = _post_tail(l, _sub_rows(_read_stream(x_refs)), m, mixes, gpm_ref, gpf_ref, gqf_ref, wg_ref, wu_ref, wd_ref)
    n_in = 20 + n_cast
    _write_stream(refs[n_in:n_in + n_out], x2)
    _cast_side_job(refs[20:n_in], refs[n_in + n_out:])


def _even_post(l, xs, mod, y_f, y_b, u_ctx, u_lat, s5_d, glu_w, glu_b, da_p, da_s, w_out, ffn_args, split_out,
               cast=()):
    e = l // 2
    ctx_view = _ctx_view_spec()
    lat_view = _lat_view_spec(N_P // SUBLANES // TM)
    out_specs, out_shape = _out_stream(split_out)
    casts = [_cast_specs(w, lyr) for w, lyr in cast]
    w_half = lambda r: pl.BlockSpec((None, S5_WIDTH, D_MODEL), lambda i: (e, r, 0), pipeline_mode=pl.Buffered(1))
    return pl.pallas_call(
        functools.partial(_even_post_kernel, l, len(xs), len(out_specs), len(casts)),
        grid=(N_TILES,),
        in_specs=_stream_specs(xs, D_MODEL) + [
            _mod_spec(l),
            ctx_view, ctx_view, ctx_view, lat_view, lat_view, _lat_view_spec(0),
            _const_spec(s5_d.shape), _layer_spec((S5_WIDTH, S5_WIDTH), e), _const_spec(glu_b.shape),
            _ctx_spec(DIFF_WIDTH), _lat_spec(DIFF_WIDTH),
            w_half(0), w_half(1),
        ] + _ffn_specs() + [c[0] for c in casts],
        out_specs=out_specs + [c[1] for c in casts],
        out_shape=out_shape + [c[2] for c in casts],
        compiler_params=_cparams(("arbitrary",), VMEM_LIMIT_POST),
        name="even_post_ffn",
    )(*xs, mod, y_f, y_b, u_ctx, y_f, y_b, u_lat, s5_d, glu_w, glu_b, da_p, da_s, w_out, w_out, *ffn_args,
      *[w for w, _ in cast])


def _odd_pre_kernel(l, n_x, *refs):
    o = l // 2
    x_refs, refs = refs[:n_x], refs[n_x:]
    (mod_ref, g_ref, w_ref, gq_ref, gkv_ref, wq_ref, cos_ref, sin_ref,
     q_ref, ckr_ref, ckv_ref, kr_ref) = refs
    m = mod_ref[...]
    kr0 = MLA_Q_RANK + MLA_KV_RANK
    subs = _sub_slices()
    h = [_rms(x, _row(g_ref, l)) * (1.0 + m[1:2]) + m[0:1] for x in _sub_rows(_read_stream(x_refs))]
    proj = [_dot(v.astype(BF16), w_ref[...]) for v in h]
    cq = [_rms(p[:, :MLA_Q_RANK], _row(gq_ref, o)) for p in proj]
    ckv = [_rms(p[:, MLA_Q_RANK:kr0], _row(gkv_ref, o)) for p in proj]
    q = [_dot(v.astype(BF16), wq_ref[...]) * ((MLA_NOPE + MLA_ROPE) ** -0.5 * LOG2E) for v in cq]
    for rs, p, qv, cv in zip(subs, proj, q, ckv):
        cos, sin = cos_ref[rs, :], sin_ref[rs, :]
        krp = p[:, kr0:kr0 + MLA_HEAD_PAD]
        krp_sw = p[:, kr0 + MLA_HEAD_PAD:]
        for hd in range(MLA_HEADS):
            sl = slice(hd * MLA_HEAD_PAD, (hd + 1) * MLA_HEAD_PAD)
            sw = slice(MLA_Q_W + hd * MLA_HEAD_PAD, MLA_Q_W + (hd + 1) * MLA_HEAD_PAD)
            q_ref[rs, sl] = (qv[:, sl] * cos + qv[:, sw] * sin).astype(BF16)
        ckr_ref[rs, :MLA_KV_RANK] = cv.astype(BF16)
        ckr_ref[rs, MLA_KV_RANK:] = (krp * cos + krp_sw * sin).astype(BF16)

    @pl.when(pl.program_id(0) < P_TILES)
    def _():
        for rs, p, cv in zip(subs, proj, ckv):
            ckv_ref[rs, :] = cv
            kr_ref[rs, :] = p[:, kr0:kr0 + MLA_HEAD_PAD]


def _odd_pre(l, xs, mod, g_pre, w_in, gq, gkv, wq, rope_tabs):
    o = l // 2
    rope_spec = pl.BlockSpec((TM, MLA_HEAD_PAD), lambda i: (_rope_block(i), 0))
    return pl.pallas_call(
        functools.partial(_odd_pre_kernel, l, len(xs)),
        grid=(N_TILES,),
        in_specs=_stream_specs(xs, D_MODEL) + [
            _mod_spec(l),
            _const_spec(g_pre.shape),
            _layer_spec((D_MODEL, MLA_IN_W), o),
            _const_spec(gq.shape), _const_spec(gkv.shape),
            _layer_spec((MLA_Q_RANK, 2 * MLA_Q_W), o),
            rope_spec, rope_spec,
        ],
        out_specs=[_tok_spec(MLA_Q_W), _tok_spec(MLA_CKR_W), _ctx_spec(MLA_KV_RANK), _ctx_spec(MLA_HEAD_PAD)],
        out_shape=[
            jax.ShapeDtypeStruct((N_TOK, MLA_Q_W), BF16),
            jax.ShapeDtypeStruct((N_TOK, MLA_CKR_W), BF16),
            jax.ShapeDtypeStruct((N_P, MLA_KV_RANK), F32),
            jax.ShapeDtypeStruct((N_P, MLA_HEAD_PAD), F32),
        ],
        compiler_params=_cparams(("arbitrary",)),
        name="odd_pre",
    )(*xs, mod, g_pre, w_in, gq, gkv, wq, *rope_tabs)


def _mla_attn_kernel(has_ctx, *refs):
    if has_ctx:
        q_ref, ckr_ref, ckrc_ref, wk_ref, wv_ref, o_ref, kf_ref, vf_ref = refs
    else:
        q_ref, ckr_ref, wk_ref, wv_ref, o_ref, kf_ref, vf_ref = refs
    off = PAST_LEN if has_ctx else 0

    @pl.when(pl.program_id(1) == 0)
    def _():
        srcs = [(0, ckrc_ref)] if has_ctx else []
        srcs.append((off, ckr_ref))
        wv = wv_ref[...] if has_ctx else wv_ref[:, :MLA_O_W] + wv_ref[:, MLA_O_W:]
        for r0, src in srcs:
            c = src[...]
            rows = pl.ds(r0, c.shape[0])
            kf_ref[rows, :] = _dot(c, wk_ref[...]).astype(BF16)
            vf_ref[rows, :] = _dot(c[:, :MLA_KV_RANK], wv).astype(BF16)

    low = lax.broadcasted_iota(jnp.int32, (1, LANES), 1) < MLA_V

    def scores(hd):
        cs = slice(hd * MLA_HEAD_PAD, (hd + 1) * MLA_HEAD_PAD)
        return _dot_nt(q_ref[:, cs], kf_ref[:, cs])

    pv, rl = {}, {}

    def value_product(hd, p):
        j, t = divmod(hd, 2)
        slot = slice(j * LANES, (j + 1) * LANES)
        v_cols = slice(t * MLA_O_W + j * LANES, t * MLA_O_W + (j + 1) * LANES) if has_ctx else slot
        pv[hd] = _dot(p, vf_ref[:, v_cols])
        if t == 1:
            a, b = pv.pop(hd - 1), pv.pop(hd)
            if has_ctx:
                o = (a + b) * jnp.where(low, rl[hd - 1], rl[hd])
            else:
                o = jnp.where(low, a * rl[hd - 1], b * rl[hd])
            o_ref[:, slot] = o.astype(o_ref.dtype)

    s_next = scores(0)
    for hd in range(MLA_HEADS):
        s = s_next
        if hd + 1 < MLA_HEADS:
            s_next = scores(hd + 1)
        mx = jnp.max(s, axis=-1, keepdims=True)
        p = jnp.exp2(s - mx)
        rl[hd] = 1.0 / jnp.sum(p, axis=-1, keepdims=True)
        value_product(hd, p.astype(BF16))


def _mla_attention(o, q, ckr, ckr_ctx, wk, wv, *, batch, n, row0, tq):
    nq = n // tq
    has_ctx = ckr_ctx is not None
    s_len = n + (PAST_LEN if has_ctx else 0)
    in_specs = [
        pl.BlockSpec((tq, MLA_Q_W), lambda b, i: (row0 // tq + b * nq + i, 0)),
        pl.BlockSpec((n, MLA_CKR_W), lambda b, i: (row0 // n + b, 0)),
    ]
    args = [q, ckr]
    if has_ctx:
        in_specs.append(pl.BlockSpec((None, None, PAST_LEN, MLA_CKR_W), lambda b, i: (b, o, 0, 0)))
        args.append(ckr_ctx)
    in_specs += [_layer_spec(wk.shape[1:], o), _layer_spec(wv.shape[1:], o)]
    args += [wk, wv]
    return pl.pallas_call(
        functools.partial(_mla_attn_kernel, has_ctx),
        grid=(batch, nq),
        in_specs=in_specs,
        out_specs=pl.BlockSpec((tq, MLA_O_W), lambda b, i: (b * nq + i, 0)),
        out_shape=jax.ShapeDtypeStruct((batch * n, MLA_O_W), BF16),
        scratch_shapes=[pltpu.VMEM((s_len, MLA_Q_W), BF16),
                        pltpu.VMEM((s_len, (2 if has_ctx else 1) * MLA_O_W), BF16)],
        compiler_params=_cparams(("parallel", "arbitrary")),
        name="mla_attn_ctx" if has_ctx else "mla_attn",
    )(*args)


def _odd_post_kernel(l, n_x, n_out, n_cast, *refs):
    x_refs, refs = refs[:n_x], refs[n_x:]
    mod_ref, ap_ref, as_ref, wo_ref, gpm_ref, gpf_ref, gqf_ref, wg_ref, wu_ref, wd_ref = refs[:10]
    mixes = [_dot(a, wo_ref[...]) for a in _sub_rows(_read_stream((ap_ref, as_ref)))]
    x2 = _post_tail(l, _sub_rows(_read_stream(x_refs)), mod_ref[...], mixes, gpm_ref, gpf_ref, gqf_ref,
                    wg_ref, wu_ref, wd_ref)
    n_in = 10 + n_cast
    _write_stream(refs[n_in:n_in + n_out], x2)
    _cast_side_job(refs[10:n_in], refs[n_in + n_out:])


def _odd_post(l, xs, mod, at_p, at_s, w_out, ffn_args, split_out, cast=()):
    out_specs, out_shape = _out_stream(split_out)
    casts = [_cast_specs(w, lyr) for w, lyr in cast]
    return pl.pallas_call(
        functools.partial(_odd_post_kernel, l, len(xs), len(out_specs), len(casts)),
        grid=(N_TILES,),
        in_specs=_stream_specs(xs, D_MODEL) + [
            _mod_spec(l),
            _ctx_spec(MLA_O_W), _lat_spec(MLA_O_W),
            _layer_spec((MLA_O_W, D_MODEL), l // 2),
        ] + _ffn_specs() + [c[0] for c in casts],
        out_specs=out_specs + [c[1] for c in casts],
        out_shape=out_shape + [c[2] for c in casts],
        compiler_params=_cparams(("arbitrary",)),
        name="odd_post_ffn",
    )(*xs, mod, at_p, at_s, w_out, *ffn_args, *[w for w, _ in cast])


def _rope_angles(rot_dim):
    rows = DEC_SEQ // GRID_W
    row = np.repeat(np.arange(rows, dtype=np.float32), GRID_W)
    col = np.tile(np.arange(GRID_W, dtype=np.float32), rows)
    n_freq = rot_dim // 4
    inv = (np.float32(ROPE_BASE) ** (-np.arange(n_freq, dtype=np.float32) / np.float32(n_freq))).astype(np.float32)
    ang = np.concatenate([row[:, None] * inv, col[:, None] * inv], axis=-1).astype(np.float32)
    return np.cos(ang), np.sin(ang)


def _with_identity(cos, *sins):
    one = np.ones((TM, cos.shape[1]), np.float32)
    zero = np.zeros((TM, cos.shape[1]), np.float32)
    return (np.concatenate([one, cos]),) + tuple(np.concatenate([zero, s]) for s in sins)


def _diff_rope_tables():
    c, s = _rope_angles(DIFF_HEAD_DIM)
    z = np.zeros_like(s)
    reps = DIFF_WIDTH // DIFF_HEAD_DIM
    cos = np.tile(np.concatenate([c, c], axis=1), (1, reps))
    sa = np.tile(np.concatenate([z, s], axis=1), (1, reps))
    sb = np.tile(np.concatenate([-s, z], axis=1), (1, reps))
    return _with_identity(cos, sa, sb)


def _mla_rope_tables():
    c, s = _rope_angles(MLA_ROPE)
    n = c.shape[0]
    ones, zeros = (lambda w: np.ones((n, w), np.float32)), (lambda w: np.zeros((n, w), np.float32))
    cos = np.concatenate([ones(MLA_NOPE), c, c, ones(MLA_PAD)], axis=1)
    sin = np.concatenate([zeros(MLA_NOPE), -s, s, zeros(MLA_PAD)], axis=1)
    return _with_identity(cos, sin)


def _swap_rope_halves(w):
    half = MLA_ROPE // 2
    return jnp.concatenate([w[..., half:], w[..., :half]], axis=-1)


def _mla_weights(w_in_odd, w_q_up, w_kv_up):
    w_in_odd, w_q_up, w_kv_up = (w.astype(BF16) for w in (w_in_odd, w_q_up, w_kv_up))
    zeros = lambda *s: jnp.zeros(s, BF16)
    kr0 = MLA_Q_RANK + MLA_KV_RANK
    w_kr = w_in_odd[:, :, kr0:]
    lead = (N_ODD, D_MODEL)
    w_in = jnp.concatenate(
        [w_in_odd[:, :, :kr0],
         zeros(*lead, MLA_NOPE), w_kr, zeros(*lead, MLA_PAD),
         zeros(*lead, MLA_NOPE), _swap_rope_halves(w_kr), zeros(*lead, MLA_PAD)], axis=-1)
    wq4 = w_q_up.reshape(N_ODD, MLA_Q_RANK, MLA_HEADS, MLA_NOPE + MLA_ROPE)
    lead = (N_ODD, MLA_Q_RANK, MLA_HEADS)
    wq_main = jnp.concatenate([wq4, zeros(*lead, MLA_PAD)], axis=-1)
    wq_swap = jnp.concatenate([zeros(*lead, MLA_NOPE), _swap_rope_halves(wq4[..., MLA_NOPE:]),
                               zeros(*lead, MLA_PAD)], axis=-1)
    wq = jnp.concatenate([wq_main.reshape(N_ODD, MLA_Q_RANK, MLA_Q_W),
                          wq_swap.reshape(N_ODD, MLA_Q_RANK, MLA_Q_W)], axis=-1)
    wkv = w_kv_up.reshape(N_ODD, MLA_KV_RANK, MLA_HEADS, MLA_NOPE + MLA_V)
    wk_top = jnp.pad(wkv[..., :MLA_NOPE], ((0, 0), (0, 0), (0, 0), (0, MLA_HEAD_PAD - MLA_NOPE)))
    sel = jnp.pad(jnp.eye(MLA_ROPE, dtype=BF16), ((MLA_NOPE, MLA_PAD), (MLA_NOPE, MLA_PAD)))
    wk_bot = jnp.broadcast_to(jnp.tile(sel, (1, MLA_HEADS)), (N_ODD, MLA_HEAD_PAD, MLA_Q_W))
    wk = jnp.concatenate([wk_top.reshape(N_ODD, MLA_KV_RANK, MLA_Q_W), wk_bot], axis=1)
    even = (jnp.arange(MLA_HEADS) % 2 == 0)[:, None]
    wv4 = wkv[..., MLA_NOPE:]
    zero_v = jnp.zeros_like(wv4)
    wv = jnp.concatenate([jnp.where(even, wv4, zero_v).reshape(N_ODD, MLA_KV_RANK, MLA_O_W),
                          jnp.where(even, zero_v, wv4).reshape(N_ODD, MLA_KV_RANK, MLA_O_W)], axis=-1)
    return w_in, wq, wk, wv


def kernel(x_prompt, x_sample, state_s5_re, state_s5_im, cache_diff_k, cache_diff_v, cache_mla_ckv, cache_mla_krope, c, c_ctx, w_mod, b_mod, g_pre_mix, g_post_mix, g_pre_ffn, g_post_ffn, w_ffn_gate, w_ffn_up, w_ffn_down, w_in_even, w_out_even, s5_lam_re, s5_lam_im, s5_log_dt, s5_b_re, s5_b_im, s5_c_re, s5_c_im, s5_d, s5_glu_w, s5_glu_b, diff_lam_q1, diff_lam_k1, diff_lam_q2, diff_lam_k2, diff_subln_g, w_in_odd, mla_q_norm_g, mla_w_q_up, mla_kv_norm_g, mla_w_kv_up, w_out_odd):
    xs = (x_prompt.reshape(N_P, D_MODEL), x_sample.reshape(N_S, D_MODEL))
    conds = jnp.concatenate([c_ctx[None, :], c, jnp.zeros((N_COND - 1 - DEC_BATCH, D_MODEL), F32)], axis=0)
    mod = _modulation(conds, w_mod, b_mod).reshape(DEPTH, N_COND, 6, D_MODEL)

    gains = (g_post_mix, g_pre_ffn, g_post_ffn)
    ffn_f32 = (w_ffn_gate, w_ffn_up, w_ffn_down)
    w_in_e = w_in_even.astype(BF16)
    w_out_e = w_out_even.astype(BF16)
    glu_w = s5_glu_w.astype(BF16)
    a_re, a_im, b_in, c_out = _s5_params(s5_lam_re, s5_lam_im, s5_log_dt, s5_b_re, s5_b_im, s5_c_re, s5_c_im)
    a = jnp.stack([a_re[:, 0], a_im[:, 0], a_re[:, 1], a_im[:, 1]], axis=1).reshape(N_EVEN, 4, S5_STATE_W)
    st = lambda s: jnp.moveaxis(s, 0, 2).reshape(N_EVEN, 2, 1, DEC_BATCH, S5_STATE_W)
    h0 = jnp.concatenate([jnp.zeros((N_EVEN, 2, 2, SUBLANES, 2 * S5_STATE_W), F32),
                          jnp.concatenate([st(state_s5_re), st(state_s5_im)], axis=-1)], axis=2)
    diff_ctx = (cache_diff_k.astype(BF16).reshape(DEC_BATCH, N_EVEN, PAST_LEN, DIFF_WIDTH),
                cache_diff_v.astype(BF16).reshape(DEC_BATCH, N_EVEN, PAST_LEN, DIFF_WIDTH))
    lam_vecs = (diff_lam_q1, diff_lam_k1, diff_lam_q2, diff_lam_k2)
    w_in_o, wq, wk, wv = _mla_weights(w_in_odd, mla_w_q_up, mla_w_kv_up)
    w_out_o = w_out_odd.astype(BF16)
    ckr_ctx = jnp.concatenate(
        [cache_mla_ckv.astype(BF16),
         jnp.pad(cache_mla_krope.astype(BF16), ((0, 0), (0, 0), (0, 0), (MLA_NOPE, MLA_PAD)))], axis=-1)
    diff_tabs = _diff_rope_tables()
    mla_tabs = _mla_rope_tables()

    s5_re_list, s5_im_list, dk_list, dv_list, ckv_list, kr_list = [], [], [], [], [], []
    for l in range(DEPTH):
        last = l == DEPTH - 1
        n_stream = 2 if last else 1
        next_cast = () if last else tuple((w, l + 1) for w in ffn_f32)
        if l % 2 == 0:
            e = l // 2
            lam_init = 0.8 - 0.6 * math.exp(-0.3 * l)
            pre = _even_pre(l, xs, mod, g_pre_mix, w_in_e, diff_tabs,
                            cast=tuple((w, 0) for w in ffn_f32) if l == 0 else ())
            u_ctx, u_lat, q, k, v, kf, vf = pre[:7]
            if l == 0:
                ffn_w = tuple(pre[7:])
            y_f, y_b, fin_f, fin_b = _s5_scan(e, u_ctx, u_lat, a, b_in, c_out, h0)
            fin = jnp.stack([fin_f[:2].reshape(BATCH, 2, S5_GROUPS, S5_STATE),
                             fin_b[:2].reshape(BATCH, 2, S5_GROUPS, S5_STATE)], axis=1)
            s5_re_list.append(fin[:, :, 0])
            s5_im_list.append(fin[:, :, 1])
            da_p = _diff_attention(e, q, k, v, None, lam_vecs, diff_subln_g, lam_init,
                                   batch=BATCH, n=SEQ, row0=0, tq=SEQ)
            da_s = _diff_attention(e, q, k, v, diff_ctx, lam_vecs, diff_subln_g, lam_init,
                                   batch=DEC_BATCH, n=DEC_SEQ, row0=N_P, tq=256)
            post = _even_post(l, xs, mod, y_f, y_b, u_ctx, u_lat, s5_d, glu_w, s5_glu_b, da_p, da_s, w_out_e,
                              gains + ffn_w, last, cast=next_cast)
            xs, ffn_w = post[:n_stream], tuple(post[n_stream:])
            dk_list.append(kf.reshape(BATCH, SEQ, DIFF_HEADS, 2, DIFF_HEAD_DIM))
            dv_list.append(vf.reshape(BATCH, SEQ, DIFF_HEADS, 2 * DIFF_HEAD_DIM))
        else:
            o = l // 2
            q, ckr, ckv, krp = _odd_pre(l, xs, mod, g_pre_mix, w_in_o, mla_q_norm_g, mla_kv_norm_g, wq, mla_tabs)
            at_p = _mla_attention(o, q, ckr, None, wk, wv, batch=BATCH, n=SEQ, row0=0, tq=SEQ)
            at_s = _mla_attention(o, q, ckr, ckr_ctx, wk, wv, batch=DEC_BATCH, n=DEC_SEQ, row0=N_P, tq=256)
            post = _odd_post(l, xs, mod, at_p, at_s, w_out_o, gains + ffn_w, last, cast=next_cast)
            xs, ffn_w = post[:n_stream], tuple(post[n_stream:])
            ckv_list.append(ckv.reshape(BATCH, SEQ, MLA_KV_RANK))
            kr_list.append(krp[:, MLA_NOPE:MLA_NOPE + MLA_ROPE].reshape(BATCH, SEQ, MLA_ROPE))

    return (xs[0].reshape(BATCH, SEQ, D_MODEL), xs[1].reshape(DEC_BATCH, DEC_SEQ, D_MODEL),
            jnp.stack(s5_re_list, axis=1), jnp.stack(s5_im_list, axis=1),
            jnp.stack(dk_list, axis=1), jnp.stack(dv_list, axis=1),
            jnp.stack(ckv_list, axis=1), jnp.stack(kr_list, axis=1))
```

```python
import functools
import math

import jax
import jax.numpy as jnp
import numpy as np
from jax import lax
from jax.experimental import pallas as pl
from jax.experimental.pallas import tpu as pltpu

F32 = jnp.float32
BF16 = jnp.bfloat16

D_MODEL = 1024
BATCH = 16
SEQ = 256
DEPTH = 4
DEC_BATCH = 8
DEC_SEQ = 1024
PAST_LEN = 512
GRID_W = 64
N_EVEN = (DEPTH + 1) // 2
N_ODD = DEPTH // 2
EPS = 1e-6
ROPE_BASE = 10000.0
S5_WIDTH = D_MODEL // 2
S5_GROUP = 16
S5_GROUPS = S5_WIDTH // S5_GROUP
S5_STATE = 64
DIFF_HEAD_DIM = 64
DIFF_HEADS = (D_MODEL // 2) // (2 * DIFF_HEAD_DIM)
DIFF_WIDTH = DIFF_HEADS * 2 * DIFF_HEAD_DIM
EVEN_IN = S5_WIDTH + 3 * DIFF_WIDTH
MLA_HEADS = 16
MLA_NOPE = 64
MLA_ROPE = 32
MLA_V = 64
MLA_Q_RANK = 256
MLA_KV_RANK = 128
D_FF = ((8 * D_MODEL // 3 + 255) // 256) * 256

LANES = 128
SUBLANES = 8
BF16_SUBLANES = 16
VMEM_LIMIT = 56 * 1024 * 1024
VMEM_LIMIT_POST = 61 * 1024 * 1024
LOG2E = math.log2(math.e)

N_P = BATCH * SEQ
N_S = DEC_BATCH * DEC_SEQ
N_TOK = N_P + N_S
TM = 512
N_TILES = N_TOK // TM
P_TILES = N_P // TM
LAT_TILES_PER_SEQ = DEC_SEQ // TM
ROW_SUB = 2
N_COND = 16
N_MOD = 6
MOD_TN = 1536
DIFF_TQ = 512
MLA_TQ = 256

S5_STATE_W = S5_GROUPS * S5_STATE
S5_BLK = 4
S5_BLK_W = S5_STATE_W // S5_BLK
S5_GB = S5_GROUPS // S5_BLK
S5_IN_ROWS = S5_GB * S5_GROUP
S5_N_BLOCKS = N_EVEN * 2 * S5_BLK
S5_TM_W = SUBLANES * S5_WIDTH
SCAN_T = 64
SCAN_R = SCAN_T * SUBLANES
N_TM_ROWS = N_TOK // SUBLANES
N_CHUNKS = N_TM_ROWS // SCAN_T
SEQ_CHUNKS = SEQ // SCAN_T
CTX_CHUNKS = 2 * SEQ_CHUNKS

MLA_HEAD_PAD = LANES
MLA_PAD = MLA_HEAD_PAD - MLA_NOPE - MLA_ROPE
MLA_Q_W = MLA_HEADS * MLA_HEAD_PAD
MLA_IN_W = MLA_Q_RANK + MLA_KV_RANK + 2 * MLA_HEAD_PAD
MLA_CKR_W = MLA_KV_RANK + MLA_HEAD_PAD
MLA_O_W = MLA_HEADS * MLA_V


def _cparams(sem, vmem_limit=VMEM_LIMIT):
    return pltpu.CompilerParams(dimension_semantics=sem, vmem_limit_bytes=vmem_limit)


def _const_spec(shape):
    nd = len(shape)
    return pl.BlockSpec(shape, lambda *_: (0,) * nd, pipeline_mode=pl.Buffered(1))


def _layer_spec(tail, *lead):
    nt = len(tail)
    return pl.BlockSpec((None,) * len(lead) + tuple(tail), lambda *_: tuple(lead) + (0,) * nt,
                        pipeline_mode=pl.Buffered(1))


def _rms(x, g):
    return x * lax.rsqrt(jnp.mean(x * x, axis=-1, keepdims=True) + EPS) * g


def _dot(a, b):
    return jnp.dot(a, b, preferred_element_type=F32)


def _dot_nt(a, b):
    return lax.dot_general(a, b, (((1,), (1,)), ((), ())), preferred_element_type=F32)


def _row(ref, r):
    return ref[r:r + 1, :]


def _cond_of_tile(i):
    return jnp.where(i < P_TILES, 0, 1 + (i - P_TILES) // LAT_TILES_PER_SEQ)


def _rope_block(i):
    return jnp.where(i < P_TILES, 0, 1 + (i - P_TILES) % LAT_TILES_PER_SEQ)


def _tok_spec(width):
    return pl.BlockSpec((TM, width), lambda i: (i, 0))


def _ctx_spec(width):
    return pl.BlockSpec((TM, width), lambda i: (jnp.minimum(i, P_TILES - 1), 0))


def _lat_spec(width):
    return pl.BlockSpec((TM, width), lambda i: (jnp.maximum(i - P_TILES, 0), 0))


def _ctx_view_spec():
    pairs = SUBLANES // 2
    return pl.BlockSpec((SEQ, 2 * S5_WIDTH),
                        lambda i: (jnp.minimum(i, P_TILES - 1) // pairs, jnp.minimum(i, P_TILES - 1) % pairs))


def _lat_view_spec(row0):
    return pl.BlockSpec((TM, S5_WIDTH), lambda i: (row0 + jnp.maximum(i - P_TILES, 0) % LAT_TILES_PER_SEQ,
                                                   jnp.maximum(i - P_TILES, 0) // LAT_TILES_PER_SEQ))


def _stream_specs(arrays, width):
    return [_tok_spec(width)] if len(arrays) == 1 else [_ctx_spec(width), _lat_spec(width)]


def _read_stream(refs):
    if len(refs) == 1:
        return refs[0][...]
    return jnp.where(pl.program_id(0) < P_TILES, refs[0][...], refs[1][...])


def _write_stream(out_refs, x2):
    if len(out_refs) == 1:
        out_refs[0][...] = x2
    else:
        i = pl.program_id(0)

        @pl.when(i < P_TILES)
        def _():
            out_refs[0][...] = x2

        @pl.when(i >= P_TILES)
        def _():
            out_refs[1][...] = x2


def _out_stream(split):
    if split:
        return ([_ctx_spec(D_MODEL), _lat_spec(D_MODEL)],
                [jax.ShapeDtypeStruct((N_P, D_MODEL), F32), jax.ShapeDtypeStruct((N_S, D_MODEL), F32)])
    return [_tok_spec(D_MODEL)], [jax.ShapeDtypeStruct((N_TOK, D_MODEL), F32)]


def _mod_spec(l):
    return pl.BlockSpec((None, None, N_MOD, D_MODEL), lambda i: (l, _cond_of_tile(i), 0, 0))


def _sub_slices():
    n = TM // ROW_SUB
    return [slice(r * n, (r + 1) * n) for r in range(ROW_SUB)]


def _sub_rows(v):
    return [v[rs] for rs in _sub_slices()]


def _cast_specs(w_stack, layer):
    _, r, c = w_stack.shape
    rc = BF16_SUBLANES
    while r % rc or r // rc > N_TILES:
        rc += BF16_SUBLANES
    last = r // rc - 1
    return (pl.BlockSpec((None, rc, c), lambda i: (layer, jnp.minimum(i, last), 0)),
            pl.BlockSpec((rc, c), lambda i: (jnp.minimum(i, last), 0)),
            jax.ShapeDtypeStruct((r, c), BF16))


def _cast_side_job(in_refs, out_refs):
    for wi, wo in zip(in_refs, out_refs):
        wo[...] = wi[...].astype(BF16)


def _mod_kernel(c_ref, w_ref, b_ref, o_ref):
    s = jax.nn.silu(c_ref[...])
    o_ref[0] = _dot(s.astype(BF16), w_ref[0].astype(BF16)) + b_ref[0]


def _modulation(conds, w_mod, b_mod):
    tn = MOD_TN
    return pl.pallas_call(
        _mod_kernel,
        grid=(DEPTH, N_MOD * D_MODEL // tn),
        in_specs=[
            pl.BlockSpec((N_COND, D_MODEL), lambda l, n: (0, 0)),
            pl.BlockSpec((1, D_MODEL, tn), lambda l, n: (l, 0, n)),
            pl.BlockSpec((1, 1, tn), lambda l, n: (l, 0, n)),
        ],
        out_specs=pl.BlockSpec((1, N_COND, tn), lambda l, n: (l, 0, n)),
        out_shape=jax.ShapeDtypeStruct((DEPTH, N_COND, N_MOD * D_MODEL), F32),
        compiler_params=_cparams(("parallel", "parallel")),
        name="modulation",
    )(conds, w_mod, b_mod.reshape(DEPTH, 1, N_MOD * D_MODEL))


def _block_diag_lanes(x, row_group, col_group):
    xt = jnp.concatenate([x] * S5_GB, axis=1)
    rg = lax.broadcasted_iota(jnp.int32, xt.shape, 0) // row_group
    cg = lax.broadcasted_iota(jnp.int32, xt.shape, 1) // col_group
    return jnp.where(rg == cg, xt, 0.0)


def _s5_param_kernel(lr_ref, li_ref, ldt_ref, br_ref, bi_ref, cr_ref, ci_ref, are_ref, aim_ref, bin_ref, cout_ref):
    lr, li = lr_ref[...], li_ref[...]
    dt = jnp.exp(ldt_ref[...])
    mag = jnp.exp(lr * dt)
    a_re, a_im = mag * jnp.cos(li * dt), mag * jnp.sin(li * dt)
    den = lr * lr + li * li
    f_re = ((a_re - 1.0) * lr + a_im * li) / den
    f_im = (a_im * lr - (a_re - 1.0) * li) / den
    br, bi = br_ref[...], bi_ref[...]
    are_ref[...] = a_re
    aim_ref[...] = a_im
    bin_ref[:, :S5_BLK_W] = _block_diag_lanes(f_re * br - f_im * bi, S5_GROUP, S5_STATE).astype(BF16)
    bin_ref[:, S5_BLK_W:] = _block_diag_lanes(f_re * bi + f_im * br, S5_GROUP, S5_STATE).astype(BF16)
    cout_ref[:S5_BLK_W, :] = _block_diag_lanes(cr_ref[...], S5_STATE, S5_GROUP).astype(BF16)
    cout_ref[S5_BLK_W:, :] = _block_diag_lanes(-ci_ref[...], S5_STATE, S5_GROUP).astype(BF16)


def _s5_params(lam_re, lam_im, log_dt, b_re, b_im, c_re, c_im):
    full = (N_EVEN, 2, S5_GROUPS, S5_GROUP, S5_STATE)
    blk = (S5_N_BLOCKS, S5_IN_ROWS, S5_STATE)
    ex = lambda a: jnp.broadcast_to(a[:, :, :, None, :], full).reshape(blk)
    ldt = jnp.broadcast_to(log_dt[:, :, :, None, None], full).reshape(blk)
    bt = lambda b: jnp.swapaxes(b, -1, -2).reshape(blk)
    ct = lambda c: jnp.swapaxes(c, -1, -2).reshape(S5_N_BLOCKS, S5_BLK_W, S5_GROUP)
    spec = lambda r, w: pl.BlockSpec((None, r, w), lambda i: (i, 0, 0))
    a_re, a_im, b_in, c_out = pl.pallas_call(
        _s5_param_kernel,
        grid=(S5_N_BLOCKS,),
        in_specs=[spec(S5_IN_ROWS, S5_STATE)] * 5 + [spec(S5_BLK_W, S5_GROUP)] * 2,
        out_specs=[spec(S5_IN_ROWS, S5_STATE)] * 2 + [spec(S5_IN_ROWS, 2 * S5_BLK_W), spec(2 * S5_BLK_W, LANES)],
        out_shape=[jax.ShapeDtypeStruct(blk, F32)] * 2
        + [jax.ShapeDtypeStruct((S5_N_BLOCKS, S5_IN_ROWS, 2 * S5_BLK_W), BF16),
           jax.ShapeDtypeStruct((S5_N_BLOCKS, 2 * S5_BLK_W, LANES), BF16)],
        compiler_params=_cparams(("parallel",)),
        name="s5_params",
    )(ex(lam_re), ex(lam_im), ldt, bt(b_re), bt(b_im), ct(c_re), ct(c_im))
    pick = lambda a: a.reshape(full)[:, :, :, 0, :]
    lead = (N_EVEN, 2, S5_BLK)
    return pick(a_re), pick(a_im), b_in.reshape(lead + b_in.shape[1:]), c_out.reshape(lead + c_out.shape[1:])


def _seq_of_chunk(c):
    return (c >= SEQ_CHUNKS).astype(jnp.int32) + (c >= CTX_CHUNKS).astype(jnp.int32)


def _s5_scan_kernel(ufc_ref, ufl_ref, ubc_ref, ubl_ref, a_ref, bin_ref, cout_ref, h0f_ref, h0b_ref,
                    yf_ref, yb_ref, finf_ref, finb_ref, us_ref, hs_ref, ys_ref, st_ref):
    j = pl.program_id(0)
    cf = j
    cb = N_CHUNKS - 1 - j

    @pl.when((cf == 0) | (cf == SEQ_CHUNKS) | (cf == CTX_CHUNKS))
    def _():
        st_ref[0] = h0f_ref[...]

    @pl.when((cb == N_CHUNKS - 1) | (cb == CTX_CHUNKS - 1) | (cb == SEQ_CHUNKS - 1))
    def _():
        st_ref[1] = h0b_ref[...]

    dirs = ((0, cf, ufc_ref, ufl_ref, yf_ref), (1, cb, ubc_ref, ubl_ref, yb_ref))
    for k in range(S5_BLK):
        lo, hi = k * S5_BLK_W, (k + 1) * S5_BLK_W
        for d, chunk, uc_ref, ul_ref, y_ref in dirs:
            for b in range(SUBLANES):
                c0 = b * S5_WIDTH + k * LANES
                us_ref[d, k, pl.ds(b, SCAN_T, stride=SUBLANES), :] = jnp.where(
                    chunk < CTX_CHUNKS, uc_ref[:, c0:c0 + LANES], ul_ref[:, c0:c0 + LANES])
            hs_ref[d, k] = _dot(us_ref[d, k].astype(BF16), bin_ref[d, k])

            ar = jnp.broadcast_to(a_ref[2 * d:2 * d + 1, lo:hi], (SUBLANES, S5_BLK_W))
            ai = jnp.broadcast_to(a_ref[2 * d + 1:2 * d + 2, lo:hi], (SUBLANES, S5_BLK_W))
            hr, hi_ = st_ref[d, :, lo:hi], st_ref[d, :, S5_STATE_W + lo:S5_STATE_W + hi]
            for t in range(SCAN_T):
                rows = pl.ds((t if d == 0 else SCAN_T - 1 - t) * SUBLANES, SUBLANES)
                nr = ar * hr - ai * hi_ + hs_ref[d, k, rows, 0:S5_BLK_W]
                ni = ar * hi_ + ai * hr + hs_ref[d, k, rows, S5_BLK_W:2 * S5_BLK_W]
                hs_ref[d, k, rows, 0:S5_BLK_W] = nr
                hs_ref[d, k, rows, S5_BLK_W:2 * S5_BLK_W] = ni
                hr, hi_ = nr, ni
            st_ref[d, :, lo:hi] = hr
            st_ref[d, :, S5_STATE_W + lo:S5_STATE_W + hi] = hi_

            ys_ref[d, k] = _dot(hs_ref[d, k].astype(BF16), cout_ref[d, k])
            for b in range(SUBLANES):
                c0 = b * S5_WIDTH + k * LANES
                y_ref[:, c0:c0 + LANES] = ys_ref[d, k, pl.ds(b, SCAN_T, stride=SUBLANES), :]

    finf_ref[...] = st_ref[0]
    finb_ref[...] = st_ref[1]


def _s5_scan(e, u_ctx, u_lat, a, b_in, c_out, h0):
    fwd = lambda j: j
    rev = lambda j: N_CHUNKS - 1 - j
    st_w = 2 * S5_STATE_W
    in_c = lambda f: pl.BlockSpec((SCAN_T, S5_TM_W), lambda j: (jnp.minimum(f(j), CTX_CHUNKS - 1), 0))
    in_l = lambda f: pl.BlockSpec((SCAN_T, S5_TM_W), lambda j: (jnp.maximum(f(j) - CTX_CHUNKS, 0), 0))
    out = lambda f: pl.BlockSpec((SCAN_T, S5_TM_W), lambda j: (f(j), 0))
    h0_spec = lambda d, f: pl.BlockSpec((None, None, None, SUBLANES, st_w),
                                        lambda j: (e, d, _seq_of_chunk(f(j)), 0, 0))
    fin = lambda f: pl.BlockSpec((None, SUBLANES, st_w), lambda j: (_seq_of_chunk(f(j)), 0, 0))
    st_shape = jax.ShapeDtypeStruct((3, SUBLANES, st_w), F32)
    return pl.pallas_call(
        _s5_scan_kernel,
        grid=(N_CHUNKS,),
        in_specs=[in_c(fwd), in_l(fwd), in_c(rev), in_l(rev), _layer_spec(a.shape[1:], e),
                  _layer_spec(b_in.shape[1:], e), _layer_spec(c_out.shape[1:], e), h0_spec(0, fwd), h0_spec(1, rev)],
        out_specs=[out(fwd), out(rev), fin(fwd), fin(rev)],
        out_shape=[jax.ShapeDtypeStruct((N_TM_ROWS, S5_TM_W), F32)] * 2 + [st_shape] * 2,
        scratch_shapes=[pltpu.VMEM((2, S5_BLK, SCAN_R, LANES), F32),
                        pltpu.VMEM((2, S5_BLK, SCAN_R, 2 * S5_BLK_W), F32),
                        pltpu.VMEM((2, S5_BLK, SCAN_R, LANES), F32),
                        pltpu.VMEM((2, SUBLANES, st_w), F32)],
        compiler_params=_cparams(("arbitrary",)),
        name="s5_scan",
    )(u_ctx, u_lat, u_ctx, u_lat, a, b_in, c_out, h0, h0)


def _even_pre_kernel(l, n_x, n_cast, *refs):
    x_refs, refs = refs[:n_x], refs[n_x:]
    mod_ref, g_ref, w_ref, cos_ref, sa_ref, sb_ref = refs[:6]
    uc_ref, ul_ref, q_ref, k_ref, v_ref, kf_ref, vf_ref = refs[6 + n_cast:13 + n_cast]
    _cast_side_job(refs[6:6 + n_cast], refs[13 + n_cast:])
    m = mod_ref[...]
    half = DIFF_HEAD_DIM // 2
    subs = _sub_slices()
    h = [_rms(x, _row(g_ref, l)) * (1.0 + m[1:2]) + m[0:1] for x in _sub_rows(_read_stream(x_refs))]
    proj = [_dot(v.astype(BF16), w_ref[...]) for v in h]
    us, ks, vs = [], [], []
    for rs, p in zip(subs, proj):
        cos, sa, sb = cos_ref[rs, :], sa_ref[rs, :], sb_ref[rs, :]

        def rope(z):
            return z * cos + pltpu.roll(z, half, 1) * sa + pltpu.roll(z, DIFF_WIDTH - half, 1) * sb

        k = rope(p[:, S5_WIDTH + DIFF_WIDTH:S5_WIDTH + 2 * DIFF_WIDTH])
        v = p[:, S5_WIDTH + 2 * DIFF_WIDTH:]
        q_ref[rs, :] = (rope(p[:, S5_WIDTH:S5_WIDTH + DIFF_WIDTH]) * (DIFF_HEAD_DIM ** -0.5 * LOG2E)).astype(BF16)
        k_ref[rs, :] = k.astype(BF16)
        v_ref[rs, :] = v.astype(BF16)
        us.append(p[:, :S5_WIDTH])
        ks.append(k)
        vs.append(v)

    @pl.when(pl.program_id(0) < P_TILES)
    def _():
        uc_ref[...] = jnp.concatenate(us, axis=1)
        for rs, k, v in zip(subs, ks, vs):
            kf_ref[rs, :] = k
            vf_ref[rs, :] = v

    @pl.when(pl.program_id(0) >= P_TILES)
    def _():
        ul_ref[...] = jnp.concatenate(us, axis=0)


def _even_pre(l, xs, mod, g_pre, w_in, rope_tabs, cast=()):
    rope_spec = pl.BlockSpec((TM, DIFF_WIDTH), lambda i: (_rope_block(i), 0))
    casts = [_cast_specs(w, lyr) for w, lyr in cast]
    return pl.pallas_call(
        functools.partial(_even_pre_kernel, l, len(xs), len(casts)),
        grid=(N_TILES,),
        in_specs=_stream_specs(xs, D_MODEL) + [
            _mod_spec(l),
            _const_spec(g_pre.shape),
            _layer_spec((D_MODEL, EVEN_IN), l // 2),
            rope_spec, rope_spec, rope_spec,
        ] + [c[0] for c in casts],
        out_specs=[
            _ctx_view_spec(), _lat_view_spec(0),
            _tok_spec(DIFF_WIDTH), _tok_spec(DIFF_WIDTH), _tok_spec(DIFF_WIDTH),
            _ctx_spec(DIFF_WIDTH), _ctx_spec(DIFF_WIDTH),
        ] + [c[1] for c in casts],
        out_shape=[jax.ShapeDtypeStruct((N_P // SUBLANES, S5_TM_W), F32),
                   jax.ShapeDtypeStruct((N_S // SUBLANES, S5_TM_W), F32)]
        + [jax.ShapeDtypeStruct((N_TOK, DIFF_WIDTH), BF16)] * 3
        + [jax.ShapeDtypeStruct((N_P, DIFF_WIDTH), F32)] * 2 + [c[2] for c in casts],
        compiler_params=_cparams(("arbitrary",)),
        name="even_pre",
    )(*xs, mod, g_pre, w_in, *rope_tabs, *[w for w, _ in cast])


def _softmax2_parts(parts):
    m = functools.reduce(jnp.maximum, [jnp.max(s, axis=-1, keepdims=True) for s in parts])
    ps = [jnp.exp2(s - m) for s in parts]
    l = functools.reduce(jnp.add, [jnp.sum(p, axis=-1, keepdims=True) for p in ps])
    return ps, l


def _diff_attn_kernel(e, lam_init, has_ctx, *refs):
    if has_ctx:
        q_ref, k_ref, v_ref, kc_ref, vc_ref, lq1, lk1, lq2, lk2, g_ref, o_ref = refs
    else:
        q_ref, k_ref, v_ref, lq1, lk1, lq2, lk2, g_ref, o_ref = refs
    lam = (jnp.exp(jnp.sum(_row(lq1, e) * _row(lk1, e), axis=-1, keepdims=True))
           - jnp.exp(jnp.sum(_row(lq2, e) * _row(lk2, e), axis=-1, keepdims=True)) + lam_init)
    first = lax.broadcasted_iota(jnp.int32, (1, 2 * DIFF_HEAD_DIM), 1) < DIFF_HEAD_DIM
    g = _row(g_ref, e)

    def head_slice(h):
        return slice(h * 2 * DIFF_HEAD_DIM, (h + 1) * 2 * DIFF_HEAD_DIM)

    def scores(h):
        sl = head_slice(h)
        qh = q_ref[:, sl]
        zero = jnp.zeros_like(qh)
        keys = ([kc_ref[:, sl]] if has_ctx else []) + [k_ref[:, sl]]
        return [[_dot_nt(qc, kk) for kk in keys] for qc in (jnp.where(first, qh, zero), jnp.where(first, zero, qh))]

    def finish(w, l1, sl):
        vals = ([vc_ref[:, sl]] if has_ctx else []) + [v_ref[:, sl]]
        o = functools.reduce(jnp.add, [_dot(a, vv) for a, vv in zip(w, vals)]) * (1.0 / l1)
        o_ref[:, sl] = (_rms(o, g) * (1.0 - lam_init)).astype(o_ref.dtype)

    s_next = scores(0)
    pending = None
    for h in range(DIFF_HEADS):
        s1, s2 = s_next
        if h + 1 < DIFF_HEADS:
            s_next = scores(h + 1)
        p1, l1 = _softmax2_parts(s1)
        p2, l2 = _softmax2_parts(s2)
        ratio = lam * l1 / l2
        w = [(a - b * ratio).astype(BF16) for a, b in zip(p1, p2)]
        if pending is not None:
            finish(*pending)
        pending = (w, l1, head_slice(h))
    finish(*pending)


def _diff_attention(e, q, k, v, ctx, lam_vecs, subln_g, lam_init, *, batch, n, row0, tq):
    nq = n // tq
    has_ctx = ctx is not None
    in_specs = [
        pl.BlockSpec((tq, DIFF_WIDTH), lambda b, i: (row0 // tq + b * nq + i, 0)),
        pl.BlockSpec((n, DIFF_WIDTH), lambda b, i: (row0 // n + b, 0)),
        pl.BlockSpec((n, DIFF_WIDTH), lambda b, i: (row0 // n + b, 0)),
    ]
    args = [q, k, v]
    if has_ctx:
        in_specs += [pl.BlockSpec((None, None, PAST_LEN, DIFF_WIDTH), lambda b, i: (b, e, 0, 0))] * 2
        args += list(ctx)
    in_specs += [_const_spec(t.shape) for t in lam_vecs] + [_const_spec(subln_g.shape)]
    args += list(lam_vecs) + [subln_g]
    return pl.pallas_call(
        functools.partial(_diff_attn_kernel, e, lam_init, has_ctx),
        grid=(batch, nq),
        in_specs=in_specs,
        out_specs=pl.BlockSpec((tq, DIFF_WIDTH), lambda b, i: (b * nq + i, 0)),
        out_shape=jax.ShapeDtypeStruct((batch * n, DIFF_WIDTH), BF16),
        compiler_params=_cparams(("parallel", "parallel")),
        name="diff_attn_ctx" if has_ctx else "diff_attn",
    )(*args)


def _post_tail(l, xs, m, mixes, gpm_ref, gpf_ref, gqf_ref, wg_ref, wu_ref, wd_ref):
    x1 = [x + m[2:3] * _rms(mix, _row(gpm_ref, l)) for x, mix in zip(xs, mixes)]
    h = [(_rms(v, _row(gpf_ref, l)) * (1.0 + m[4:5]) + m[3:4]).astype(BF16) for v in x1]
    act = [jax.nn.silu(_dot(v, wg_ref[...])) * _dot(v, wu_ref[...]) for v in h]
    y = [_dot(v.astype(BF16), wd_ref[...]) for v in act]
    return jnp.concatenate([a + m[5:6] * _rms(b, _row(gqf_ref, l)) for a, b in zip(x1, y)], axis=0)


def _ffn_specs():
    vec = _const_spec((DEPTH, D_MODEL))
    return [vec, vec, vec, _const_spec((D_MODEL, D_FF)), _const_spec((D_MODEL, D_FF)), _const_spec((D_FF, D_MODEL))]


def _even_post_kernel(l, n_x, n_out, n_cast, *refs):
    e = l // 2
    x_refs, refs = refs[:n_x], refs[n_x:]
    (mod_ref, yfp_ref, ybp_ref, up_ref, yfs_ref, ybs_ref, us_ref, d_ref, gw_ref, gb_ref, dap_ref, das_ref,
     wos_ref, wod_ref, gpm_ref, gpf_ref, gqf_ref, wg_ref, wu_ref, wd_ref) = refs[:20]
    m = mod_ref[...]
    d = _row(d_ref, e)
    yp = yfp_ref[...] + ybp_ref[...] + jnp.concatenate([d, d], axis=1) * up_ref[...]
    ys = yfs_ref[...] + ybs_ref[...] + d * us_ref[...]
    y = jnp.where(pl.program_id(0) < P_TILES,
                  jnp.concatenate([yp[:, :S5_WIDTH], yp[:, S5_WIDTH:]], axis=0), ys)
    g = [jax.nn.gelu(v) for v in _sub_rows(y)]
    s5 = [v * jax.nn.sigmoid(_dot(v.astype(BF16), gw_ref[...]) + _row(gb_ref, e)) for v in g]
    mixes = [_dot(v.astype(BF16), wos_ref[...]) + _dot(da, wod_ref[...])
             for v, da in zip(s5, _sub_rows(_read_stream((dap_ref, das_ref))))]
    x2 = _post_tail(l, _sub_rows(_read_stream(x_refs)), m, mixes, gpm_ref, gpf_ref, gqf_ref, wg_ref, wu_ref, wd_ref)
    n_in = 20 + n_cast
    _write_stream(refs[n_in:n_in + n_out], x2)
    _cast_side_job(refs[20:n_in], refs[n_in + n_out:])


def _even_post(l, xs, mod, y_f, y_b, u_ctx, u_lat, s5_d, glu_w, glu_b, da_p, da_s, w_out, ffn_args, split_out,
               cast=()):
    e = l // 2
    ctx_view = _ctx_view_spec()
    lat_view = _lat_view_spec(N_P // SUBLANES // TM)
    out_specs, out_shape = _out_stream(split_out)
    casts = [_cast_specs(w, lyr) for w, lyr in cast]
    w_half = lambda r: pl.BlockSpec((None, S5_WIDTH, D_MODEL), lambda i: (e, r, 0), pipeline_mode=pl.Buffered(1))
    return pl.pallas_call(
        functools.partial(_even_post_kernel, l, len(xs), len(out_specs), len(casts)),
        grid=(N_TILES,),
        in_specs=_stream_specs(xs, D_MODEL) + [
            _mod_spec(l),
            ctx_view, ctx_view, ctx_view, lat_view, lat_view, _lat_view_spec(0),
            _const_spec(s5_d.shape), _layer_spec((S5_WIDTH, S5_WIDTH), e), _const_spec(glu_b.shape),
            _ctx_spec(DIFF_WIDTH), _lat_spec(DIFF_WIDTH),
            w_half(0), w_half(1),
        ] + _ffn_specs() + [c[0] for c in casts],
        out_specs=out_specs + [c[1] for c in casts],
        out_shape=out_shape + [c[2] for c in casts],
        compiler_params=_cparams(("arbitrary",), VMEM_LIMIT_POST),
        name="even_post_ffn",
    )(*xs, mod, y_f, y_b, u_ctx, y_f, y_b, u_lat, s5_d, glu_w, glu_b, da_p, da_s, w_out, w_out, *ffn_args,
      *[w for w, _ in cast])


def _odd_pre_kernel(l, n_x, *refs):
    o = l // 2
    x_refs, refs = refs[:n_x], refs[n_x:]
    (mod_ref, g_ref, w_ref, gq_ref, gkv_ref, wq_ref, cos_ref, sin_ref,
     q_ref, ckr_ref, ckv_ref, kr_ref) = refs
    m = mod_ref[...]
    kr0 = MLA_Q_RANK + MLA_KV_RANK
    subs = _sub_slices()
    h = [_rms(x, _row(g_ref, l)) * (1.0 + m[1:2]) + m[0:1] for x in _sub_rows(_read_stream(x_refs))]
    proj = [_dot(v.astype(BF16), w_ref[...]) for v in h]
    cq = [_rms(p[:, :MLA_Q_RANK], _row(gq_ref, o)) for p in proj]
    ckv = [_rms(p[:, MLA_Q_RANK:kr0], _row(gkv_ref, o)) for p in proj]
    q = [_dot(v.astype(BF16), wq_ref[...]) * ((MLA_NOPE + MLA_ROPE) ** -0.5 * LOG2E) for v in cq]
    for rs, p, qv, cv in zip(subs, proj, q, ckv):
        cos, sin = cos_ref[rs, :], sin_ref[rs, :]
        krp = p[:, kr0:kr0 + MLA_HEAD_PAD]
        krp_sw = p[:, kr0 + MLA_HEAD_PAD:]
        for hd in range(MLA_HEADS):
            sl = slice(hd * MLA_HEAD_PAD, (hd + 1) * MLA_HEAD_PAD)
            sw = slice(MLA_Q_W + hd * MLA_HEAD_PAD, MLA_Q_W + (hd + 1) * MLA_HEAD_PAD)
            q_ref[rs, sl] = (qv[:, sl] * cos + qv[:, sw] * sin).astype(BF16)
        ckr_ref[rs, :MLA_KV_RANK] = cv.astype(BF16)
        ckr_ref[rs, MLA_KV_RANK:] = (krp * cos + krp_sw * sin).astype(BF16)

    @pl.when(pl.program_id(0) < P_TILES)
    def _():
        for rs, p, cv in zip(subs, proj, ckv):
            ckv_ref[rs, :] = cv
            kr_ref[rs, :] = p[:, kr0:kr0 + MLA_HEAD_PAD]


def _odd_pre(l, xs, mod, g_pre, w_in, gq, gkv, wq, rope_tabs):
    o = l // 2
    rope_spec = pl.BlockSpec((TM, MLA_HEAD_PAD), lambda i: (_rope_block(i), 0))
    return pl.pallas_call(
        functools.partial(_odd_pre_kernel, l, len(xs)),
        grid=(N_TILES,),
        in_specs=_stream_specs(xs, D_MODEL) + [
            _mod_spec(l),
            _const_spec(g_pre.shape),
            _layer_spec((D_MODEL, MLA_IN_W), o),
            _const_spec(gq.shape), _const_spec(gkv.shape),
            _layer_spec((MLA_Q_RANK, 2 * MLA_Q_W), o),
            rope_spec, rope_spec,
        ],
        out_specs=[_tok_spec(MLA_Q_W), _tok_spec(MLA_CKR_W), _ctx_spec(MLA_KV_RANK), _ctx_spec(MLA_HEAD_PAD)],
        out_shape=[
            jax.ShapeDtypeStruct((N_TOK, MLA_Q_W), BF16),
            jax.ShapeDtypeStruct((N_TOK, MLA_CKR_W), BF16),
            jax.ShapeDtypeStruct((N_P, MLA_KV_RANK), F32),
            jax.ShapeDtypeStruct((N_P, MLA_HEAD_PAD), F32),
        ],
        compiler_params=_cparams(("arbitrary",)),
        name="odd_pre",
    )(*xs, mod, g_pre, w_in, gq, gkv, wq, *rope_tabs)


def _mla_attn_kernel(has_ctx, *refs):
    if has_ctx:
        q_ref, ckr_ref, ckrc_ref, wk_ref, wv_ref, o_ref, kf_ref, vf_ref = refs
    else:
        q_ref, ckr_ref, wk_ref, wv_ref, o_ref, kf_ref, vf_ref = refs
    off = PAST_LEN if has_ctx else 0

    @pl.when(pl.program_id(1) == 0)
    def _():
        srcs = [(0, ckrc_ref)] if has_ctx else []
        srcs.append((off, ckr_ref))
        wv = wv_ref[...] if has_ctx else wv_ref[:, :MLA_O_W] + wv_ref[:, MLA_O_W:]
        for r0, src in srcs:
            c = src[...]
            rows = pl.ds(r0, c.shape[0])
            kf_ref[rows, :] = _dot(c, wk_ref[...]).astype(BF16)
            vf_ref[rows, :] = _dot(c[:, :MLA_KV_RANK], wv).astype(BF16)

    low = lax.broadcasted_iota(jnp.int32, (1, LANES), 1) < MLA_V

    def scores(hd):
        cs = slice(hd * MLA_HEAD_PAD, (hd + 1) * MLA_HEAD_PAD)
        return _dot_nt(q_ref[:, cs], kf_ref[:, cs])

    pv, rl = {}, {}

    def value_product(hd, p):
        j, t = divmod(hd, 2)
        slot = slice(j * LANES, (j + 1) * LANES)
        v_cols = slice(t * MLA_O_W + j * LANES, t * MLA_O_W + (j + 1) * LANES) if has_ctx else slot
        pv[hd] = _dot(p, vf_ref[:, v_cols])
        if t == 1:
            a, b = pv.pop(hd - 1), pv.pop(hd)
            if has_ctx:
                o = (a + b) * jnp.where(low, rl[hd - 1], rl[hd])
            else:
                o = jnp.where(low, a * rl[hd - 1], b * rl[hd])
            o_ref[:, slot] = o.astype(o_ref.dtype)

    s_next = scores(0)
    for hd in range(MLA_HEADS):
        s = s_next
        if hd + 1 < MLA_HEADS:
            s_next = scores(hd + 1)
        mx = jnp.max(s, axis=-1, keepdims=True)
        p = jnp.exp2(s - mx)
        rl[hd] = 1.0 / jnp.sum(p, axis=-1, keepdims=True)
        value_product(hd, p.astype(BF16))


def _mla_attention(o, q, ckr, ckr_ctx, wk, wv, *, batch, n, row0, tq):
    nq = n // tq
    has_ctx = ckr_ctx is not None
    s_len = n + (PAST_LEN if has_ctx else 0)
    in_specs = [
        pl.BlockSpec((tq, MLA_Q_W), lambda b, i: (row0 // tq + b * nq + i, 0)),
        pl.BlockSpec((n, MLA_CKR_W), lambda b, i: (row0 // n + b, 0)),
    ]
    args = [q, ckr]
    if has_ctx:
        in_specs.append(pl.BlockSpec((None, None, PAST_LEN, MLA_CKR_W), lambda b, i: (b, o, 0, 0)))
        args.append(ckr_ctx)
    in_specs += [_layer_spec(wk.shape[1:], o), _layer_spec(wv.shape[1:], o)]
    args += [wk, wv]
    return pl.pallas_call(
        functools.partial(_mla_attn_kernel, has_ctx),
        grid=(batch, nq),
        in_specs=in_specs,
        out_specs=pl.BlockSpec((tq, MLA_O_W), lambda b, i: (b * nq + i, 0)),
        out_shape=jax.ShapeDtypeStruct((batch * n, MLA_O_W), BF16),
        scratch_shapes=[pltpu.VMEM((s_len, MLA_Q_W), BF16),
                        pltpu.VMEM((s_len, (2 if has_ctx else 1) * MLA_O_W), BF16)],
        compiler_params=_cparams(("parallel", "arbitrary")),
        name="mla_attn_ctx" if has_ctx else "mla_attn",
    )(*args)


def _odd_post_kernel(l, n_x, n_out, n_cast, *refs):
    x_refs, refs = refs[:n_x], refs[n_x:]
    mod_ref, ap_ref, as_ref, wo_ref, gpm_ref, gpf_ref, gqf_ref, wg_ref, wu_ref, wd_ref = refs[:10]
    mixes = [_dot(a, wo_ref[...]) for a in _sub_rows(_read_stream((ap_ref, as_ref)))]
    x2 = _post_tail(l, _sub_rows(_read_stream(x_refs)), mod_ref[...], mixes, gpm_ref, gpf_ref, gqf_ref,
                    wg_ref, wu_ref, wd_ref)
    n_in = 10 + n_cast
    _write_stream(refs[n_in:n_in + n_out], x2)
    _cast_side_job(refs[10:n_in], refs[n_in + n_out:])


def _odd_post(l, xs, mod, at_p, at_s, w_out, ffn_args, split_out, cast=()):
    out_specs, out_shape = _out_stream(split_out)
    casts = [_cast_specs(w, lyr) for w, lyr in cast]
    return pl.pallas_call(
        functools.partial(_odd_post_kernel, l, len(xs), len(out_specs), len(casts)),
        grid=(N_TILES,),
        in_specs=_stream_specs(xs, D_MODEL) + [
            _mod_spec(l),
            _ctx_spec(MLA_O_W), _lat_spec(MLA_O_W),
            _layer_spec((MLA_O_W, D_MODEL), l // 2),
        ] + _ffn_specs() + [c[0] for c in casts],
        out_specs=out_specs + [c[1] for c in casts],
        out_shape=out_shape + [c[2] for c in casts],
        compiler_params=_cparams(("arbitrary",)),
        name="odd_post_ffn",
    )(*xs, mod, at_p, at_s, w_out, *ffn_args, *[w for w, _ in cast])


def _rope_angles(rot_dim):
    rows = DEC_SEQ // GRID_W
    row = np.repeat(np.arange(rows, dtype=np.float32), GRID_W)
    col = np.tile(np.arange(GRID_W, dtype=np.float32), rows)
    n_freq = rot_dim // 4
    inv = (np.float32(ROPE_BASE) ** (-np.arange(n_freq, dtype=np.float32) / np.float32(n_freq))).astype(np.float32)
    ang = np.concatenate([row[:, None] * inv, col[:, None] * inv], axis=-1).astype(np.float32)
    return np.cos(ang), np.sin(ang)


def _with_identity(cos, *sins):
    one = np.ones((TM, cos.shape[1]), np.float32)
    zero = np.zeros((TM, cos.shape[1]), np.float32)
    return (np.concatenate([one, cos]),) + tuple(np.concatenate([zero, s]) for s in sins)


def _diff_rope_tables():
    c, s = _rope_angles(DIFF_HEAD_DIM)
    z = np.zeros_like(s)
    reps = DIFF_WIDTH // DIFF_HEAD_DIM
    cos = np.tile(np.concatenate([c, c], axis=1), (1, reps))
    sa = np.tile(np.concatenate([z, s], axis=1), (1, reps))
    sb = np.tile(np.concatenate([-s, z], axis=1), (1, reps))
    return _with_identity(cos, sa, sb)


def _mla_rope_tables():
    c, s = _rope_angles(MLA_ROPE)
    n = c.shape[0]
    ones, zeros = (lambda w: np.ones((n, w), np.float32)), (lambda w: np.zeros((n, w), np.float32))
    cos = np.concatenate([ones(MLA_NOPE), c, c, ones(MLA_PAD)], axis=1)
    sin = np.concatenate([zeros(MLA_NOPE), -s, s, zeros(MLA_PAD)], axis=1)
    return _with_identity(cos, sin)


def _swap_rope_halves(w):
    half = MLA_ROPE // 2
    return jnp.concatenate([w[..., half:], w[..., :half]], axis=-1)


def _mla_weights(w_in_odd, w_q_up, w_kv_up):
    w_in_odd, w_q_up, w_kv_up = (w.astype(BF16) for w in (w_in_odd, w_q_up, w_kv_up))
    zeros = lambda *s: jnp.zeros(s, BF16)
    kr0 = MLA_Q_RANK + MLA_KV_RANK
    w_kr = w_in_odd[:, :, kr0:]
    lead = (N_ODD, D_MODEL)
    w_in = jnp.concatenate(
        [w_in_odd[:, :, :kr0],
         zeros(*lead, MLA_NOPE), w_kr, zeros(*lead, MLA_PAD),
         zeros(*lead, MLA_NOPE), _swap_rope_halves(w_kr), zeros(*lead, MLA_PAD)], axis=-1)
    wq4 = w_q_up.reshape(N_ODD, MLA_Q_RANK, MLA_HEADS, MLA_NOPE + MLA_ROPE)
    lead = (N_ODD, MLA_Q_RANK, MLA_HEADS)
    wq_main = jnp.concatenate([wq4, zeros(*lead, MLA_PAD)], axis=-1)
    wq_swap = jnp.concatenate([zeros(*lead, MLA_NOPE), _swap_rope_halves(wq4[..., MLA_NOPE:]),
                               zeros(*lead, MLA_PAD)], axis=-1)
    wq = jnp.concatenate([wq_main.reshape(N_ODD, MLA_Q_RANK, MLA_Q_W),
                          wq_swap.reshape(N_ODD, MLA_Q_RANK, MLA_Q_W)], axis=-1)
    wkv = w_kv_up.reshape(N_ODD, MLA_KV_RANK, MLA_HEADS, MLA_NOPE + MLA_V)
    wk_top = jnp.pad(wkv[..., :MLA_NOPE], ((0, 0), (0, 0), (0, 0), (0, MLA_HEAD_PAD - MLA_NOPE)))
    sel = jnp.pad(jnp.eye(MLA_ROPE, dtype=BF16), ((MLA_NOPE, MLA_PAD), (MLA_NOPE, MLA_PAD)))
    wk_bot = jnp.broadcast_to(jnp.tile(sel, (1, MLA_HEADS)), (N_ODD, MLA_HEAD_PAD, MLA_Q_W))
    wk = jnp.concatenate([wk_top.reshape(N_ODD, MLA_KV_RANK, MLA_Q_W), wk_bot], axis=1)
    even = (jnp.arange(MLA_HEADS) % 2 == 0)[:, None]
    wv4 = wkv[..., MLA_NOPE:]
    zero_v = jnp.zeros_like(wv4)
    wv = jnp.concatenate([jnp.where(even, wv4, zero_v).reshape(N_ODD, MLA_KV_RANK, MLA_O_W),
                          jnp.where(even, zero_v, wv4).reshape(N_ODD, MLA_KV_RANK, MLA_O_W)], axis=-1)
    return w_in, wq, wk, wv


def kernel(x_prompt, x_sample, state_s5_re, state_s5_im, cache_diff_k, cache_diff_v, cache_mla_ckv, cache_mla_krope, c, c_ctx, w_mod, b_mod, g_pre_mix, g_post_mix, g_pre_ffn, g_post_ffn, w_ffn_gate, w_ffn_up, w_ffn_down, w_in_even, w_out_even, s5_lam_re, s5_lam_im, s5_log_dt, s5_b_re, s5_b_im, s5_c_re, s5_c_im, s5_d, s5_glu_w, s5_glu_b, diff_lam_q1, diff_lam_k1, diff_lam_q2, diff_lam_k2, diff_subln_g, w_in_odd, mla_q_norm_g, mla_w_q_up, mla_kv_norm_g, mla_w_kv_up, w_out_odd):
    xs = (x_prompt.reshape(N_P, D_MODEL), x_sample.reshape(N_S, D_MODEL))
    conds = jnp.concatenate([c_ctx[None, :], c, jnp.zeros((N_COND - 1 - DEC_BATCH, D_MODEL), F32)], axis=0)
    mod = _modulation(conds, w_mod, b_mod).reshape(DEPTH, N_COND, N_MOD, D_MODEL)

    gains = (g_post_mix, g_pre_ffn, g_post_ffn)
    ffn_f32 = (w_ffn_gate, w_ffn_up, w_ffn_down)
    w_in_e = w_in_even.astype(BF16)
    w_out_e = w_out_even.astype(BF16)
    glu_w = s5_glu_w.astype(BF16)
    a_re, a_im, b_in, c_out = _s5_params(s5_lam_re, s5_lam_im, s5_log_dt, s5_b_re, s5_b_im, s5_c_re, s5_c_im)
    a = jnp.stack([a_re[:, 0], a_im[:, 0], a_re[:, 1], a_im[:, 1]], axis=1).reshape(N_EVEN, 4, S5_STATE_W)
    st = lambda s: jnp.moveaxis(s, 0, 2).reshape(N_EVEN, 2, 1, DEC_BATCH, S5_STATE_W)
    h0 = jnp.concatenate([jnp.zeros((N_EVEN, 2, 2, SUBLANES, 2 * S5_STATE_W), F32),
                          jnp.concatenate([st(state_s5_re), st(state_s5_im)], axis=-1)], axis=2)
    diff_ctx = (cache_diff_k.astype(BF16).reshape(DEC_BATCH, N_EVEN, PAST_LEN, DIFF_WIDTH),
                cache_diff_v.astype(BF16).reshape(DEC_BATCH, N_EVEN, PAST_LEN, DIFF_WIDTH))
    lam_vecs = (diff_lam_q1, diff_lam_k1, diff_lam_q2, diff_lam_k2)
    w_in_o, wq, wk, wv = _mla_weights(w_in_odd, mla_w_q_up, mla_w_kv_up)
    w_out_o = w_out_odd.astype(BF16)
    ckr_ctx = jnp.concatenate(
        [cache_mla_ckv.astype(BF16),
         jnp.pad(cache_mla_krope.astype(BF16), ((0, 0), (0, 0), (0, 0), (MLA_NOPE, MLA_PAD)))], axis=-1)
    diff_tabs = _diff_rope_tables()
    mla_tabs = _mla_rope_tables()

    s5_re_list, s5_im_list, dk_list, dv_list, ckv_list, kr_list = [], [], [], [], [], []
    for l in range(DEPTH):
        last = l == DEPTH - 1
        n_stream = 2 if last else 1
        next_cast = () if last else tuple((w, l + 1) for w in ffn_f32)
        if l % 2 == 0:
            e = l // 2
            lam_init = 0.8 - 0.6 * math.exp(-0.3 * l)
            pre = _even_pre(l, xs, mod, g_pre_mix, w_in_e, diff_tabs,
                            cast=tuple((w, 0) for w in ffn_f32) if l == 0 else ())
            u_ctx, u_lat, q, k, v, kf, vf = pre[:7]
            if l == 0:
                ffn_w = tuple(pre[7:])
            y_f, y_b, fin_f, fin_b = _s5_scan(e, u_ctx, u_lat, a, b_in, c_out, h0)
            fin = jnp.stack([fin_f[:2].reshape(BATCH, 2, S5_GROUPS, S5_STATE),
                             fin_b[:2].reshape(BATCH, 2, S5_GROUPS, S5_STATE)], axis=1)
            s5_re_list.append(fin[:, :, 0])
            s5_im_list.append(fin[:, :, 1])
            da_p = _diff_attention(e, q, k, v, None, lam_vecs, diff_subln_g, lam_init,
                                   batch=BATCH, n=SEQ, row0=0, tq=SEQ)
            da_s = _diff_attention(e, q, k, v, diff_ctx, lam_vecs, diff_subln_g, lam_init,
                                   batch=DEC_BATCH, n=DEC_SEQ, row0=N_P, tq=DIFF_TQ)
            post = _even_post(l, xs, mod, y_f, y_b, u_ctx, u_lat, s5_d, glu_w, s5_glu_b, da_p, da_s, w_out_e,
                              gains + ffn_w, last, cast=next_cast)
            xs, ffn_w = post[:n_stream], tuple(post[n_stream:])
            dk_list.append(kf.reshape(BATCH, SEQ, DIFF_HEADS, 2, DIFF_HEAD_DIM))
            dv_list.append(vf.reshape(BATCH, SEQ, DIFF_HEADS, 2 * DIFF_HEAD_DIM))
        else:
            o = l // 2
            q, ckr, ckv, krp = _odd_pre(l, xs, mod, g_pre_mix, w_in_o, mla_q_norm_g, mla_kv_norm_g, wq, mla_tabs)
            at_p = _mla_attention(o, q, ckr, None, wk, wv, batch=BATCH, n=SEQ, row0=0, tq=SEQ)
            at_s = _mla_attention(o, q, ckr, ckr_ctx, wk, wv, batch=DEC_BATCH, n=DEC_SEQ, row0=N_P, tq=MLA_TQ)
            post = _odd_post(l, xs, mod, at_p, at_s, w_out_o, gains + ffn_w, last, cast=next_cast)
            xs, ffn_w = post[:n_stream], tuple(post[n_stream:])
            ckv_list.append(ckv.reshape(BATCH, SEQ, MLA_KV_RANK))
            kr_list.append(krp[:, MLA_NOPE:MLA_NOPE + MLA_ROPE].reshape(BATCH, SEQ, MLA_ROPE))

    return (xs[0].reshape(BATCH, SEQ, D_MODEL), xs[1].reshape(DEC_BATCH, DEC_SEQ, D_MODEL),
            jnp.stack(s5_re_list, axis=1), jnp.stack(s5_im_list, axis=1),
            jnp.stack(dk_list, axis=1), jnp.stack(dv_list, axis=1),
            jnp.stack(ckv_list, axis=1), jnp.stack(kr_list, axis=1))
```

```python
import functools
import math

import jax
import jax.numpy as jnp
import numpy as np
from jax import lax
from jax.experimental import pallas as pl
from jax.experimental.pallas import tpu as pltpu

F32 = jnp.float32
BF16 = jnp.bfloat16

D_MODEL = 1024
BATCH = 16
SEQ = 256
DEPTH = 4
DEC_BATCH = 8
DEC_SEQ = 1024
PAST_LEN = 512
GRID_W = 64
N_EVEN = (DEPTH + 1) // 2
N_ODD = DEPTH // 2
EPS = 1e-6
ROPE_BASE = 10000.0
S5_WIDTH = D_MODEL // 2
S5_GROUP = 16
S5_GROUPS = S5_WIDTH // S5_GROUP
S5_STATE = 64
DIFF_HEAD_DIM = 64
DIFF_HEADS = (D_MODEL // 2) // (2 * DIFF_HEAD_DIM)
DIFF_WIDTH = DIFF_HEADS * 2 * DIFF_HEAD_DIM
EVEN_IN = S5_WIDTH + 3 * DIFF_WIDTH
MLA_HEADS = 16
MLA_NOPE = 64
MLA_ROPE = 32
MLA_V = 64
MLA_Q_RANK = 256
MLA_KV_RANK = 128
D_FF = ((8 * D_MODEL // 3 + 255) // 256) * 256

LANES = 128
SUBLANES = 8
BF16_SUBLANES = 16
VMEM_LIMIT = 56 * 1024 * 1024
VMEM_LIMIT_POST = 61 * 1024 * 1024
LOG2E = math.log2(math.e)

N_P = BATCH * SEQ
N_S = DEC_BATCH * DEC_SEQ
N_TOK = N_P + N_S
TM = 512
N_TILES = N_TOK // TM
P_TILES = N_P // TM
LAT_TILES_PER_SEQ = DEC_SEQ // TM
ROW_SUB = 2
N_COND = 16
N_MOD = 6
MOD_TN = 1536
DIFF_TQ = 512
MLA_TQ = 256

S5_STATE_W = S5_GROUPS * S5_STATE
S5_BLK = 4
S5_BLK_W = S5_STATE_W // S5_BLK
S5_GB = S5_GROUPS // S5_BLK
S5_IN_ROWS = S5_GB * S5_GROUP
S5_N_BLOCKS = N_EVEN * 2 * S5_BLK
S5_TM_W = SUBLANES * S5_WIDTH
SCAN_T = 64
SCAN_R = SCAN_T * SUBLANES
N_TM_ROWS = N_TOK // SUBLANES
N_CHUNKS = N_TM_ROWS // SCAN_T
SEQ_CHUNKS = SEQ // SCAN_T
CTX_CHUNKS = 2 * SEQ_CHUNKS

MLA_HEAD_PAD = LANES
MLA_PAD = MLA_HEAD_PAD - MLA_NOPE - MLA_ROPE
MLA_Q_W = MLA_HEADS * MLA_HEAD_PAD
MLA_IN_W = MLA_Q_RANK + MLA_KV_RANK + 2 * MLA_HEAD_PAD
MLA_CKR_W = MLA_KV_RANK + MLA_HEAD_PAD
MLA_O_W = MLA_HEADS * MLA_V


def _cparams(sem, vmem_limit=VMEM_LIMIT):
    return pltpu.CompilerParams(dimension_semantics=sem, vmem_limit_bytes=vmem_limit)


def _const_spec(shape):
    nd = len(shape)
    return pl.BlockSpec(shape, lambda *_: (0,) * nd, pipeline_mode=pl.Buffered(1))


def _layer_spec(tail, *lead):
    nt = len(tail)
    return pl.BlockSpec((None,) * len(lead) + tuple(tail), lambda *_: tuple(lead) + (0,) * nt,
                        pipeline_mode=pl.Buffered(1))


def _rms(x, g):
    return x * lax.rsqrt(jnp.mean(x * x, axis=-1, keepdims=True) + EPS) * g


def _dot(a, b):
    return jnp.dot(a, b, preferred_element_type=F32)


def _dot_nt(a, b):
    return lax.dot_general(a, b, (((1,), (1,)), ((), ())), preferred_element_type=F32)


def _row(ref, r):
    return ref[r:r + 1, :]


def _cond_of_tile(i):
    return jnp.where(i < P_TILES, 0, 1 + (i - P_TILES) // LAT_TILES_PER_SEQ)


def _rope_block(i):
    return jnp.where(i < P_TILES, 0, 1 + (i - P_TILES) % LAT_TILES_PER_SEQ)


def _tok_spec(width):
    return pl.BlockSpec((TM, width), lambda i: (i, 0))


def _ctx_spec(width):
    return pl.BlockSpec((TM, width), lambda i: (jnp.minimum(i, P_TILES - 1), 0))


def _lat_spec(width):
    return pl.BlockSpec((TM, width), lambda i: (jnp.maximum(i - P_TILES, 0), 0))


def _ctx_view_spec():
    pairs = SUBLANES // 2
    return pl.BlockSpec((SEQ, 2 * S5_WIDTH),
                        lambda i: (jnp.minimum(i, P_TILES - 1) // pairs, jnp.minimum(i, P_TILES - 1) % pairs))


def _lat_view_spec(row0):
    return pl.BlockSpec((TM, S5_WIDTH), lambda i: (row0 + jnp.maximum(i - P_TILES, 0) % LAT_TILES_PER_SEQ,
                                                   jnp.maximum(i - P_TILES, 0) // LAT_TILES_PER_SEQ))


def _stream_specs(arrays, width):
    return [_tok_spec(width)] if len(arrays) == 1 else [_ctx_spec(width), _lat_spec(width)]


def _read_stream(refs):
    if len(refs) == 1:
        return refs[0][...]
    return jnp.where(pl.program_id(0) < P_TILES, refs[0][...], refs[1][...])


def _write_stream(out_refs, x2):
    if len(out_refs) == 1:
        out_refs[0][...] = x2
    else:
        i = pl.program_id(0)

        @pl.when(i < P_TILES)
        def _():
            out_refs[0][...] = x2

        @pl.when(i >= P_TILES)
        def _():
            out_refs[1][...] = x2


def _out_stream(split):
    if split:
        return ([_ctx_spec(D_MODEL), _lat_spec(D_MODEL)],
                [jax.ShapeDtypeStruct((N_P, D_MODEL), F32), jax.ShapeDtypeStruct((N_S, D_MODEL), F32)])
    return [_tok_spec(D_MODEL)], [jax.ShapeDtypeStruct((N_TOK, D_MODEL), F32)]


def _mod_spec(l):
    return pl.BlockSpec((None, None, N_MOD, D_MODEL), lambda i: (l, _cond_of_tile(i), 0, 0))


def _sub_slices():
    n = TM // ROW_SUB
    return [slice(r * n, (r + 1) * n) for r in range(ROW_SUB)]


def _sub_rows(v):
    return [v[rs] for rs in _sub_slices()]


def _cast_specs(w_stack, layer):
    _, r, c = w_stack.shape
    rc = BF16_SUBLANES
    while r % rc or r // rc > N_TILES:
        rc += BF16_SUBLANES
    last = r // rc - 1
    return (pl.BlockSpec((None, rc, c), lambda i: (layer, jnp.minimum(i, last), 0)),
            pl.BlockSpec((rc, c), lambda i: (jnp.minimum(i, last), 0)),
            jax.ShapeDtypeStruct((r, c), BF16))


def _cast_side_job(in_refs, out_refs):
    for wi, wo in zip(in_refs, out_refs):
        wo[...] = wi[...].astype(BF16)


def _mod_kernel(c_ref, w_ref, b_ref, o_ref):
    s = jax.nn.silu(c_ref[...])
    o_ref[0] = _dot(s.astype(BF16), w_ref[0].astype(BF16)) + b_ref[0]


def _modulation(conds, w_mod, b_mod):
    tn = MOD_TN
    return pl.pallas_call(
        _mod_kernel,
        grid=(DEPTH, N_MOD * D_MODEL // tn),
        in_specs=[
            pl.BlockSpec((N_COND, D_MODEL), lambda l, n: (0, 0)),
            pl.BlockSpec((1, D_MODEL, tn), lambda l, n: (l, 0, n)),
            pl.BlockSpec((1, 1, tn), lambda l, n: (l, 0, n)),
        ],
        out_specs=pl.BlockSpec((1, N_COND, tn), lambda l, n: (l, 0, n)),
        out_shape=jax.ShapeDtypeStruct((DEPTH, N_COND, N_MOD * D_MODEL), F32),
        compiler_params=_cparams(("parallel", "parallel")),
        name="modulation",
    )(conds, w_mod, b_mod.reshape(DEPTH, 1, N_MOD * D_MODEL))


def _block_diag_lanes(x, row_group, col_group):
    xt = jnp.concatenate([x] * S5_GB, axis=1)
    rg = lax.broadcasted_iota(jnp.int32, xt.shape, 0) // row_group
    cg = lax.broadcasted_iota(jnp.int32, xt.shape, 1) // col_group
    return jnp.where(rg == cg, xt, 0.0)


def _s5_param_kernel(lr_ref, li_ref, ldt_ref, br_ref, bi_ref, cr_ref, ci_ref, are_ref, aim_ref, bin_ref, cout_ref):
    lr, li = lr_ref[...], li_ref[...]
    dt = jnp.exp(ldt_ref[...])
    mag = jnp.exp(lr * dt)
    a_re, a_im = mag * jnp.cos(li * dt), mag * jnp.sin(li * dt)
    den = lr * lr + li * li
    f_re = ((a_re - 1.0) * lr + a_im * li) / den
    f_im = (a_im * lr - (a_re - 1.0) * li) / den
    are_ref[...] = a_re
    aim_ref[...] = a_im
    rep = lambda v: jnp.broadcast_to(v[:, None, :], (S5_GB, S5_GROUP, S5_STATE)).reshape(S5_IN_ROWS, S5_STATE)
    f_re, f_im = rep(f_re), rep(f_im)
    br, bi = br_ref[...], bi_ref[...]
    bin_ref[:, :S5_BLK_W] = _block_diag_lanes(f_re * br - f_im * bi, S5_GROUP, S5_STATE).astype(BF16)
    bin_ref[:, S5_BLK_W:] = _block_diag_lanes(f_re * bi + f_im * br, S5_GROUP, S5_STATE).astype(BF16)
    cout_ref[:S5_BLK_W, :] = _block_diag_lanes(cr_ref[...], S5_STATE, S5_GROUP).astype(BF16)
    cout_ref[S5_BLK_W:, :] = _block_diag_lanes(-ci_ref[...], S5_STATE, S5_GROUP).astype(BF16)


def _s5_params(lam_re, lam_im, log_dt, b_re, b_im, c_re, c_im):
    grp = (S5_N_BLOCKS, S5_GB, S5_STATE)
    blk = (S5_N_BLOCKS, S5_IN_ROWS, S5_STATE)
    ldt = jnp.broadcast_to(log_dt[..., None], lam_re.shape).reshape(grp)
    bt = lambda b: jnp.swapaxes(b, -1, -2).reshape(blk)
    ct = lambda c: jnp.swapaxes(c, -1, -2).reshape(S5_N_BLOCKS, S5_BLK_W, S5_GROUP)
    spec = lambda r, w: pl.BlockSpec((None, r, w), lambda i: (i, 0, 0))
    a_re, a_im, b_in, c_out = pl.pallas_call(
        _s5_param_kernel,
        grid=(S5_N_BLOCKS,),
        in_specs=[spec(S5_GB, S5_STATE)] * 3 + [spec(S5_IN_ROWS, S5_STATE)] * 2 + [spec(S5_BLK_W, S5_GROUP)] * 2,
        out_specs=[spec(S5_GB, S5_STATE)] * 2 + [spec(S5_IN_ROWS, 2 * S5_BLK_W), spec(2 * S5_BLK_W, LANES)],
        out_shape=[jax.ShapeDtypeStruct(grp, F32)] * 2
        + [jax.ShapeDtypeStruct((S5_N_BLOCKS, S5_IN_ROWS, 2 * S5_BLK_W), BF16),
           jax.ShapeDtypeStruct((S5_N_BLOCKS, 2 * S5_BLK_W, LANES), BF16)],
        compiler_params=_cparams(("parallel",)),
        name="s5_params",
    )(lam_re.reshape(grp), lam_im.reshape(grp), ldt, bt(b_re), bt(b_im), ct(c_re), ct(c_im))
    pick = lambda a: a.reshape(N_EVEN, 2, S5_GROUPS, S5_STATE)
    lead = (N_EVEN, 2, S5_BLK)
    return pick(a_re), pick(a_im), b_in.reshape(lead + b_in.shape[1:]), c_out.reshape(lead + c_out.shape[1:])


def _seq_of_chunk(c):
    return (c >= SEQ_CHUNKS).astype(jnp.int32) + (c >= CTX_CHUNKS).astype(jnp.int32)


def _s5_scan_kernel(ufc_ref, ufl_ref, ubc_ref, ubl_ref, a_ref, bin_ref, cout_ref, h0f_ref, h0b_ref,
                    yf_ref, yb_ref, finf_ref, finb_ref, us_ref, hs_ref, ys_ref, st_ref):
    j = pl.program_id(0)
    cf = j
    cb = N_CHUNKS - 1 - j

    @pl.when((cf == 0) | (cf == SEQ_CHUNKS) | (cf == CTX_CHUNKS))
    def _():
        st_ref[0] = h0f_ref[...]

    @pl.when((cb == N_CHUNKS - 1) | (cb == CTX_CHUNKS - 1) | (cb == SEQ_CHUNKS - 1))
    def _():
        st_ref[1] = h0b_ref[...]

    dirs = ((0, cf, ufc_ref, ufl_ref, yf_ref), (1, cb, ubc_ref, ubl_ref, yb_ref))
    for k in range(S5_BLK):
        lo, hi = k * S5_BLK_W, (k + 1) * S5_BLK_W
        for d, chunk, uc_ref, ul_ref, y_ref in dirs:
            for b in range(SUBLANES):
                c0 = b * S5_WIDTH + k * LANES
                us_ref[d, k, pl.ds(b, SCAN_T, stride=SUBLANES), :] = jnp.where(
                    chunk < CTX_CHUNKS, uc_ref[:, c0:c0 + LANES], ul_ref[:, c0:c0 + LANES])
            hs_ref[d, k] = _dot(us_ref[d, k].astype(BF16), bin_ref[d, k])

            ar = jnp.broadcast_to(a_ref[2 * d:2 * d + 1, lo:hi], (SUBLANES, S5_BLK_W))
            ai = jnp.broadcast_to(a_ref[2 * d + 1:2 * d + 2, lo:hi], (SUBLANES, S5_BLK_W))
            hr, hi_ = st_ref[d, :, lo:hi], st_ref[d, :, S5_STATE_W + lo:S5_STATE_W + hi]
            for t in range(SCAN_T):
                rows = pl.ds((t if d == 0 else SCAN_T - 1 - t) * SUBLANES, SUBLANES)
                nr = ar * hr - ai * hi_ + hs_ref[d, k, rows, 0:S5_BLK_W]
                ni = ar * hi_ + ai * hr + hs_ref[d, k, rows, S5_BLK_W:2 * S5_BLK_W]
                hs_ref[d, k, rows, 0:S5_BLK_W] = nr
                hs_ref[d, k, rows, S5_BLK_W:2 * S5_BLK_W] = ni
                hr, hi_ = nr, ni
            st_ref[d, :, lo:hi] = hr
            st_ref[d, :, S5_STATE_W + lo:S5_STATE_W + hi] = hi_

            ys_ref[d, k] = _dot(hs_ref[d, k].astype(BF16), cout_ref[d, k])
            for b in range(SUBLANES):
                c0 = b * S5_WIDTH + k * LANES
                y_ref[:, c0:c0 + LANES] = ys_ref[d, k, pl.ds(b, SCAN_T, stride=SUBLANES), :]

    finf_ref[...] = st_ref[0]
    finb_ref[...] = st_ref[1]


def _s5_scan(e, u_ctx, u_lat, a, b_in, c_out, h0):
    fwd = lambda j: j
    rev = lambda j: N_CHUNKS - 1 - j
    st_w = 2 * S5_STATE_W
    in_c = lambda f: pl.BlockSpec((SCAN_T, S5_TM_W), lambda j: (jnp.minimum(f(j), CTX_CHUNKS - 1), 0))
    in_l = lambda f: pl.BlockSpec((SCAN_T, S5_TM_W), lambda j: (jnp.maximum(f(j) - CTX_CHUNKS, 0), 0))
    out = lambda f: pl.BlockSpec((SCAN_T, S5_TM_W), lambda j: (f(j), 0))
    h0_spec = lambda d, f: pl.BlockSpec((None, None, None, SUBLANES, st_w),
                                        lambda j: (e, d, _seq_of_chunk(f(j)), 0, 0))
    fin = lambda f: pl.BlockSpec((None, SUBLANES, st_w), lambda j: (_seq_of_chunk(f(j)), 0, 0))
    st_shape = jax.ShapeDtypeStruct((3, SUBLANES, st_w), F32)
    return pl.pallas_call(
        _s5_scan_kernel,
        grid=(N_CHUNKS,),
        in_specs=[in_c(fwd), in_l(fwd), in_c(rev), in_l(rev), _layer_spec(a.shape[1:], e),
                  _layer_spec(b_in.shape[1:], e), _layer_spec(c_out.shape[1:], e), h0_spec(0, fwd), h0_spec(1, rev)],
        out_specs=[out(fwd), out(rev), fin(fwd), fin(rev)],
        out_shape=[jax.ShapeDtypeStruct((N_TM_ROWS, S5_TM_W), F32)] * 2 + [st_shape] * 2,
        scratch_shapes=[pltpu.VMEM((2, S5_BLK, SCAN_R, LANES), F32),
                        pltpu.VMEM((2, S5_BLK, SCAN_R, 2 * S5_BLK_W), F32),
                        pltpu.VMEM((2, S5_BLK, SCAN_R, LANES), F32),
                        pltpu.VMEM((2, SUBLANES, st_w), F32)],
        compiler_params=_cparams(("arbitrary",)),
        name="s5_scan",
    )(u_ctx, u_lat, u_ctx, u_lat, a, b_in, c_out, h0, h0)


def _even_pre_kernel(l, n_x, n_cast, *refs):
    x_refs, refs = refs[:n_x], refs[n_x:]
    mod_ref, g_ref, w_ref, cos_ref, sa_ref, sb_ref = refs[:6]
    uc_ref, ul_ref, q_ref, k_ref, v_ref, kf_ref, vf_ref = refs[6 + n_cast:13 + n_cast]
    _cast_side_job(refs[6:6 + n_cast], refs[13 + n_cast:])
    m = mod_ref[...]
    half = DIFF_HEAD_DIM // 2
    subs = _sub_slices()
    h = [_rms(x, _row(g_ref, l)) * (1.0 + m[1:2]) + m[0:1] for x in _sub_rows(_read_stream(x_refs))]
    proj = [_dot(v.astype(BF16), w_ref[...]) for v in h]
    us, ks, vs = [], [], []
    for rs, p in zip(subs, proj):
        cos, sa, sb = cos_ref[rs, :], sa_ref[rs, :], sb_ref[rs, :]

        def rope(z):
            return z * cos + pltpu.roll(z, half, 1) * sa + pltpu.roll(z, DIFF_WIDTH - half, 1) * sb

        k = rope(p[:, S5_WIDTH + DIFF_WIDTH:S5_WIDTH + 2 * DIFF_WIDTH])
        v = p[:, S5_WIDTH + 2 * DIFF_WIDTH:]
        q_ref[rs, :] = (rope(p[:, S5_WIDTH:S5_WIDTH + DIFF_WIDTH]) * (DIFF_HEAD_DIM ** -0.5 * LOG2E)).astype(BF16)
        k_ref[rs, :] = k.astype(BF16)
        v_ref[rs, :] = v.astype(BF16)
        us.append(p[:, :S5_WIDTH])
        ks.append(k)
        vs.append(v)

    @pl.when(pl.program_id(0) < P_TILES)
    def _():
        uc_ref[...] = jnp.concatenate(us, axis=1)
        for rs, k, v in zip(subs, ks, vs):
            kf_ref[rs, :] = k
            vf_ref[rs, :] = v

    @pl.when(pl.program_id(0) >= P_TILES)
    def _():
        ul_ref[...] = jnp.concatenate(us, axis=0)


def _even_pre(l, xs, mod, g_pre, w_in, rope_tabs, cast=()):
    rope_spec = pl.BlockSpec((TM, DIFF_WIDTH), lambda i: (_rope_block(i), 0))
    casts = [_cast_specs(w, lyr) for w, lyr in cast]
    return pl.pallas_call(
        functools.partial(_even_pre_kernel, l, len(xs), len(casts)),
        grid=(N_TILES,),
        in_specs=_stream_specs(xs, D_MODEL) + [
            _mod_spec(l),
            _const_spec(g_pre.shape),
            _layer_spec((D_MODEL, EVEN_IN), l // 2),
            rope_spec, rope_spec, rope_spec,
        ] + [c[0] for c in casts],
        out_specs=[
            _ctx_view_spec(), _lat_view_spec(0),
            _tok_spec(DIFF_WIDTH), _tok_spec(DIFF_WIDTH), _tok_spec(DIFF_WIDTH),
            _ctx_spec(DIFF_WIDTH), _ctx_spec(DIFF_WIDTH),
        ] + [c[1] for c in casts],
        out_shape=[jax.ShapeDtypeStruct((N_P // SUBLANES, S5_TM_W), F32),
                   jax.ShapeDtypeStruct((N_S // SUBLANES, S5_TM_W), F32)]
        + [jax.ShapeDtypeStruct((N_TOK, DIFF_WIDTH), BF16)] * 3
        + [jax.ShapeDtypeStruct((N_P, DIFF_WIDTH), F32)] * 2 + [c[2] for c in casts],
        compiler_params=_cparams(("arbitrary",)),
        name="even_pre",
    )(*xs, mod, g_pre, w_in, *rope_tabs, *[w for w, _ in cast])


def _softmax2_parts(parts):
    m = functools.reduce(jnp.maximum, [jnp.max(s, axis=-1, keepdims=True) for s in parts])
    ps = [jnp.exp2(s - m) for s in parts]
    l = functools.reduce(jnp.add, [jnp.sum(p, axis=-1, keepdims=True) for p in ps])
    return ps, l


def _diff_attn_kernel(e, lam_init, has_ctx, *refs):
    if has_ctx:
        q_ref, k_ref, v_ref, kc_ref, vc_ref, lq1, lk1, lq2, lk2, g_ref, o_ref = refs
    else:
        q_ref, k_ref, v_ref, lq1, lk1, lq2, lk2, g_ref, o_ref = refs
    lam = (jnp.exp(jnp.sum(_row(lq1, e) * _row(lk1, e), axis=-1, keepdims=True))
           - jnp.exp(jnp.sum(_row(lq2, e) * _row(lk2, e), axis=-1, keepdims=True)) + lam_init)
    first = lax.broadcasted_iota(jnp.int32, (1, 2 * DIFF_HEAD_DIM), 1) < DIFF_HEAD_DIM
    g = _row(g_ref, e)

    def head_slice(h):
        return slice(h * 2 * DIFF_HEAD_DIM, (h + 1) * 2 * DIFF_HEAD_DIM)

    def scores(h):
        sl = head_slice(h)
        qh = q_ref[:, sl]
        zero = jnp.zeros_like(qh)
        keys = ([kc_ref[:, sl]] if has_ctx else []) + [k_ref[:, sl]]
        return [[_dot_nt(qc, kk) for kk in keys] for qc in (jnp.where(first, qh, zero), jnp.where(first, zero, qh))]

    def finish(w, l1, sl):
        vals = ([vc_ref[:, sl]] if has_ctx else []) + [v_ref[:, sl]]
        o = functools.reduce(jnp.add, [_dot(a, vv) for a, vv in zip(w, vals)]) * (1.0 / l1)
        o_ref[:, sl] = (_rms(o, g) * (1.0 - lam_init)).astype(o_ref.dtype)

    s_next = scores(0)
    pending = None
    for h in range(DIFF_HEADS):
        s1, s2 = s_next
        if h + 1 < DIFF_HEADS:
            s_next = scores(h + 1)
        p1, l1 = _softmax2_parts(s1)
        p2, l2 = _softmax2_parts(s2)
        ratio = lam * l1 / l2
        w = [(a - b * ratio).astype(BF16) for a, b in zip(p1, p2)]
        if pending is not None:
            finish(*pending)
        pending = (w, l1, head_slice(h))
    finish(*pending)


def _diff_attention(e, q, k, v, ctx, lam_vecs, subln_g, lam_init, *, batch, n, row0, tq):
    nq = n // tq
    has_ctx = ctx is not None
    in_specs = [
        pl.BlockSpec((tq, DIFF_WIDTH), lambda b, i: (row0 // tq + b * nq + i, 0)),
        pl.BlockSpec((n, DIFF_WIDTH), lambda b, i: (row0 // n + b, 0)),
        pl.BlockSpec((n, DIFF_WIDTH), lambda b, i: (row0 // n + b, 0)),
    ]
    args = [q, k, v]
    if has_ctx:
        in_specs += [pl.BlockSpec((None, None, PAST_LEN, DIFF_WIDTH), lambda b, i: (b, e, 0, 0))] * 2
        args += list(ctx)
    in_specs += [_const_spec(t.shape) for t in lam_vecs] + [_const_spec(subln_g.shape)]
    args += list(lam_vecs) + [subln_g]
    return pl.pallas_call(
        functools.partial(_diff_attn_kernel, e, lam_init, has_ctx),
        grid=(batch, nq),
        in_specs=in_specs,
        out_specs=pl.BlockSpec((tq, DIFF_WIDTH), lambda b, i: (b * nq + i, 0)),
        out_shape=jax.ShapeDtypeStruct((batch * n, DIFF_WIDTH), BF16),
        compiler_params=_cparams(("parallel", "parallel")),
        name="diff_attn_ctx" if has_ctx else "diff_attn",
    )(*args)


def _post_tail(l, xs, m, mixes, gpm_ref, gpf_ref, gqf_ref, wg_ref, wu_ref, wd_ref):
    x1 = [x + m[2:3] * _rms(mix, _row(gpm_ref, l)) for x, mix in zip(xs, mixes)]
    h = [(_rms(v, _row(gpf_ref, l)) * (1.0 + m[4:5]) + m[3:4]).astype(BF16) for v in x1]
    act = [jax.nn.silu(_dot(v, wg_ref[...])) * _dot(v, wu_ref[...]) for v in h]
    y = [_dot(v.astype(BF16), wd_ref[...]) for v in act]
    return jnp.concatenate([a + m[5:6] * _rms(b, _row(gqf_ref, l)) for a, b in zip(x1, y)], axis=0)


def _ffn_specs():
    vec = _const_spec((DEPTH, D_MODEL))
    return [vec, vec, vec, _const_spec((D_MODEL, D_FF)), _const_spec((D_MODEL, D_FF)), _const_spec((D_FF, D_MODEL))]


def _even_post_kernel(l, n_x, n_out, n_cast, *refs):
    e = l // 2
    x_refs, refs = refs[:n_x], refs[n_x:]
    (mod_ref, yfp_ref, ybp_ref, up_ref, yfs_ref, ybs_ref, us_ref, d_ref, gw_ref, gb_ref, dap_ref, das_ref,
     wos_ref, wod_ref, gpm_ref, gpf_ref, gqf_ref, wg_ref, wu_ref, wd_ref) = refs[:20]
    m = mod_ref[...]
    d = _row(d_ref, e)
    yp = yfp_ref[...] + ybp_ref[...] + jnp.concatenate([d, d], axis=1) * up_ref[...]
    ys = yfs_ref[...] + ybs_ref[...] + d * us_ref[...]
    y = jnp.where(pl.program_id(0) < P_TILES,
                  jnp.concatenate([yp[:, :S5_WIDTH], yp[:, S5_WIDTH:]], axis=0), ys)
    g = [jax.nn.gelu(v) for v in _sub_rows(y)]
    s5 = [v * jax.nn.sigmoid(_dot(v.astype(BF16), gw_ref[...]) + _row(gb_ref, e)) for v in g]
    mixes = [_dot(v.astype(BF16), wos_ref[...]) + _dot(da, wod_ref[...])
             for v, da in zip(s5, _sub_rows(_read_stream((dap_ref, das_ref))))]
    x2 = _post_tail(l, _sub_rows(_read_stream(x_refs)), m, mixes, gpm_ref, gpf_ref, gqf_ref, wg_ref, wu_ref, wd_ref)
    n_in = 20 + n_cast
    _write_stream(refs[n_in:n_in + n_out], x2)
    _cast_side_job(refs[20:n_in], refs[n_in + n_out:])


def _even_post(l, xs, mod, y_f, y_b, u_ctx, u_lat, s5_d, glu_w, glu_b, da_p, da_s, w_out, ffn_args, split_out,
               cast=()):
    e = l // 2
    ctx_view = _ctx_view_spec()
    lat_view = _lat_view_spec(N_P // SUBLANES // TM)
    out_specs, out_shape = _out_stream(split_out)
    casts = [_cast_specs(w, lyr) for w, lyr in cast]
    w_half = lambda r: pl.BlockSpec((None, S5_WIDTH, D_MODEL), lambda i: (e, r, 0), pipeline_mode=pl.Buffered(1))
    return pl.pallas_call(
        functools.partial(_even_post_kernel, l, len(xs), len(out_specs), len(casts)),
        grid=(N_TILES,),
        in_specs=_stream_specs(xs, D_MODEL) + [
            _mod_spec(l),
            ctx_view, ctx_view, ctx_view, lat_view, lat_view, _lat_view_spec(0),
            _const_spec(s5_d.shape), _layer_spec((S5_WIDTH, S5_WIDTH), e), _const_spec(glu_b.shape),
            _ctx_spec(DIFF_WIDTH), _lat_spec(DIFF_WIDTH),
            w_half(0), w_half(1),
        ] + _ffn_specs() + [c[0] for c in casts],
        out_specs=out_specs + [c[1] for c in casts],
        out_shape=out_shape + [c[2] for c in casts],
        compiler_params=_cparams(("arbitrary",), VMEM_LIMIT_POST),
        name="even_post_ffn",
    )(*xs, mod, y_f, y_b, u_ctx, y_f, y_b, u_lat, s5_d, glu_w, glu_b, da_p, da_s, w_out, w_out, *ffn_args,
      *[w for w, _ in cast])


def _odd_pre_kernel(l, n_x, *refs):
    o = l // 2
    x_refs, refs = refs[:n_x], refs[n_x:]
    (mod_ref, g_ref, w_ref, gq_ref, gkv_ref, wq_ref, cos_ref, sin_ref,
     q_ref, ckr_ref, ckv_ref, kr_ref) = refs
    m = mod_ref[...]
    kr0 = MLA_Q_RANK + MLA_KV_RANK
    subs = _sub_slices()
    h = [_rms(x, _row(g_ref, l)) * (1.0 + m[1:2]) + m[0:1] for x in _sub_rows(_read_stream(x_refs))]
    proj = [_dot(v.astype(BF16), w_ref[...]) for v in h]
    cq = [_rms(p[:, :MLA_Q_RANK], _row(gq_ref, o)) for p in proj]
    ckv = [_rms(p[:, MLA_Q_RANK:kr0], _row(gkv_ref, o)) for p in proj]
    q = [_dot(v.astype(BF16), wq_ref[...]) * ((MLA_NOPE + MLA_ROPE) ** -0.5 * LOG2E) for v in cq]
    for rs, p, qv, cv in zip(subs, proj, q, ckv):
        cos, sin = cos_ref[rs, :], sin_ref[rs, :]
        krp = p[:, kr0:kr0 + MLA_HEAD_PAD]
        krp_sw = p[:, kr0 + MLA_HEAD_PAD:]
        for hd in range(MLA_HEADS):
            sl = slice(hd * MLA_HEAD_PAD, (hd + 1) * MLA_HEAD_PAD)
            sw = slice(MLA_Q_W + hd * MLA_HEAD_PAD, MLA_Q_W + (hd + 1) * MLA_HEAD_PAD)
            q_ref[rs, sl] = (qv[:, sl] * cos + qv[:, sw] * sin).astype(BF16)
        ckr_ref[rs, :MLA_KV_RANK] = cv.astype(BF16)
        ckr_ref[rs, MLA_KV_RANK:] = (krp * cos + krp_sw * sin).astype(BF16)

    @pl.when(pl.program_id(0) < P_TILES)
    def _():
        for rs, p, cv in zip(subs, proj, ckv):
            ckv_ref[rs, :] = cv
            kr_ref[rs, :] = p[:, kr0:kr0 + MLA_HEAD_PAD]


def _odd_pre(l, xs, mod, g_pre, w_in, gq, gkv, wq, rope_tabs):
    o = l // 2
    rope_spec = pl.BlockSpec((TM, MLA_HEAD_PAD), lambda i: (_rope_block(i), 0))
    return pl.pallas_call(
        functools.partial(_odd_pre_kernel, l, len(xs)),
        grid=(N_TILES,),
        in_specs=_stream_specs(xs, D_MODEL) + [
            _mod_spec(l),
            _const_spec(g_pre.shape),
            _layer_spec((D_MODEL, MLA_IN_W), o),
            _const_spec(gq.shape), _const_spec(gkv.shape),
            _layer_spec((MLA_Q_RANK, 2 * MLA_Q_W), o),
            rope_spec, rope_spec,
        ],
        out_specs=[_tok_spec(MLA_Q_W), _tok_spec(MLA_CKR_W), _ctx_spec(MLA_KV_RANK), _ctx_spec(MLA_HEAD_PAD)],
        out_shape=[
            jax.ShapeDtypeStruct((N_TOK, MLA_Q_W), BF16),
            jax.ShapeDtypeStruct((N_TOK, MLA_CKR_W), BF16),
            jax.ShapeDtypeStruct((N_P, MLA_KV_RANK), F32),
            jax.ShapeDtypeStruct((N_P, MLA_HEAD_PAD), F32),
        ],
        compiler_params=_cparams(("arbitrary",)),
        name="odd_pre",
    )(*xs, mod, g_pre, w_in, gq, gkv, wq, *rope_tabs)


def _mla_attn_kernel(has_ctx, *refs):
    if has_ctx:
        q_ref, ckr_ref, ckrc_ref, wk_ref, wv_ref, o_ref, kf_ref, vf_ref = refs
    else:
        q_ref, ckr_ref, wk_ref, wv_ref, o_ref, kf_ref, vf_ref = refs
    off = PAST_LEN if has_ctx else 0

    @pl.when(pl.program_id(1) == 0)
    def _():
        srcs = [(0, ckrc_ref)] if has_ctx else []
        srcs.append((off, ckr_ref))
        wv = wv_ref[...] if has_ctx else wv_ref[:, :MLA_O_W] + wv_ref[:, MLA_O_W:]
        for r0, src in srcs:
            c = src[...]
            rows = pl.ds(r0, c.shape[0])
            kf_ref[rows, :] = _dot(c, wk_ref[...]).astype(BF16)
            vf_ref[rows, :] = _dot(c[:, :MLA_KV_RANK], wv).astype(BF16)

    low = lax.broadcasted_iota(jnp.int32, (1, LANES), 1) < MLA_V

    def scores(hd):
        cs = slice(hd * MLA_HEAD_PAD, (hd + 1) * MLA_HEAD_PAD)
        return _dot_nt(q_ref[:, cs], kf_ref[:, cs])

    pv, rl = {}, {}

    def value_product(hd, p):
        j, t = divmod(hd, 2)
        slot = slice(j * LANES, (j + 1) * LANES)
        v_cols = slice(t * MLA_O_W + j * LANES, t * MLA_O_W + (j + 1) * LANES) if has_ctx else slot
        pv[hd] = _dot(p, vf_ref[:, v_cols])
        if t == 1:
            a, b = pv.pop(hd - 1), pv.pop(hd)
            if has_ctx:
                o = (a + b) * jnp.where(low, rl[hd - 1], rl[hd])
            else:
                o = jnp.where(low, a * rl[hd - 1], b * rl[hd])
            o_ref[:, slot] = o.astype(o_ref.dtype)

    s_next = scores(0)
    for hd in range(MLA_HEADS):
        s = s_next
        if hd + 1 < MLA_HEADS:
            s_next = scores(hd + 1)
        mx = jnp.max(s, axis=-1, keepdims=True)
        p = jnp.exp2(s - mx)
        rl[hd] = 1.0 / jnp.sum(p, axis=-1, keepdims=True)
        value_product(hd, p.astype(BF16))


def _mla_attention(o, q, ckr, ckr_ctx, wk, wv, *, batch, n, row0, tq):
    nq = n // tq
    has_ctx = ckr_ctx is not None
    s_len = n + (PAST_LEN if has_ctx else 0)
    in_specs = [
        pl.BlockSpec((tq, MLA_Q_W), lambda b, i: (row0 // tq + b * nq + i, 0)),
        pl.BlockSpec((n, MLA_CKR_W), lambda b, i: (row0 // n + b, 0)),
    ]
    args = [q, ckr]
    if has_ctx:
        in_specs.append(pl.BlockSpec((None, None, PAST_LEN, MLA_CKR_W), lambda b, i: (b, o, 0, 0)))
        args.append(ckr_ctx)
    in_specs += [_layer_spec(wk.shape[1:], o), _layer_spec(wv.shape[1:], o)]
    args += [wk, wv]
    return pl.pallas_call(
        functools.partial(_mla_attn_kernel, has_ctx),
        grid=(batch, nq),
        in_specs=in_specs,
        out_specs=pl.BlockSpec((tq, MLA_O_W), lambda b, i: (b * nq + i, 0)),
        out_shape=jax.ShapeDtypeStruct((batch * n, MLA_O_W), BF16),
        scratch_shapes=[pltpu.VMEM((s_len, MLA_Q_W), BF16),
                        pltpu.VMEM((s_len, (2 if has_ctx else 1) * MLA_O_W), BF16)],
        compiler_params=_cparams(("parallel", "arbitrary")),
        name="mla_attn_ctx" if has_ctx else "mla_attn",
    )(*args)


def _odd_post_kernel(l, n_x, n_out, n_cast, *refs):
    x_refs, refs = refs[:n_x], refs[n_x:]
    mod_ref, ap_ref, as_ref, wo_ref, gpm_ref, gpf_ref, gqf_ref, wg_ref, wu_ref, wd_ref = refs[:10]
    mixes = [_dot(a, wo_ref[...]) for a in _sub_rows(_read_stream((ap_ref, as_ref)))]
    x2 = _post_tail(l, _sub_rows(_read_stream(x_refs)), mod_ref[...], mixes, gpm_ref, gpf_ref, gqf_ref,
                    wg_ref, wu_ref, wd_ref)
    n_in = 10 + n_cast
    _write_stream(refs[n_in:n_in + n_out], x2)
    _cast_side_job(refs[10:n_in], refs[n_in + n_out:])


def _odd_post(l, xs, mod, at_p, at_s, w_out, ffn_args, split_out, cast=()):
    out_specs, out_shape = _out_stream(split_out)
    casts = [_cast_specs(w, lyr) for w, lyr in cast]
    return pl.pallas_call(
        functools.partial(_odd_post_kernel, l, len(xs), len(out_specs), len(casts)),
        grid=(N_TILES,),
        in_specs=_stream_specs(xs, D_MODEL) + [
            _mod_spec(l),
            _ctx_spec(MLA_O_W), _lat_spec(MLA_O_W),
            _layer_spec((MLA_O_W, D_MODEL), l // 2),
        ] + _ffn_specs() + [c[0] for c in casts],
        out_specs=out_specs + [c[1] for c in casts],
        out_shape=out_shape + [c[2] for c in casts],
        compiler_params=_cparams(("arbitrary",)),
        name="odd_post_ffn",
    )(*xs, mod, at_p, at_s, w_out, *ffn_args, *[w for w, _ in cast])


def _rope_angles(rot_dim):
    rows = DEC_SEQ // GRID_W
    row = np.repeat(np.arange(rows, dtype=np.float32), GRID_W)
    col = np.tile(np.arange(GRID_W, dtype=np.float32), rows)
    n_freq = rot_dim // 4
    inv = (np.float32(ROPE_BASE) ** (-np.arange(n_freq, dtype=np.float32) / np.float32(n_freq))).astype(np.float32)
    ang = np.concatenate([row[:, None] * inv, col[:, None] * inv], axis=-1).astype(np.float32)
    return np.cos(ang), np.sin(ang)


def _with_identity(cos, *sins):
    one = np.ones((TM, cos.shape[1]), np.float32)
    zero = np.zeros((TM, cos.shape[1]), np.float32)
    return (np.concatenate([one, cos]),) + tuple(np.concatenate([zero, s]) for s in sins)


def _diff_rope_tables():
    c, s = _rope_angles(DIFF_HEAD_DIM)
    z = np.zeros_like(s)
    reps = DIFF_WIDTH // DIFF_HEAD_DIM
    cos = np.tile(np.concatenate([c, c], axis=1), (1, reps))
    sa = np.tile(np.concatenate([z, s], axis=1), (1, reps))
    sb = np.tile(np.concatenate([-s, z], axis=1), (1, reps))
    return _with_identity(cos, sa, sb)


def _mla_rope_tables():
    c, s = _rope_angles(MLA_ROPE)
    n = c.shape[0]
    ones, zeros = (lambda w: np.ones((n, w), np.float32)), (lambda w: np.zeros((n, w), np.float32))
    cos = np.concatenate([ones(MLA_NOPE), c, c, ones(MLA_PAD)], axis=1)
    sin = np.concatenate([zeros(MLA_NOPE), -s, s, zeros(MLA_PAD)], axis=1)
    return _with_identity(cos, sin)


def _swap_rope_halves(w):
    half = MLA_ROPE // 2
    return jnp.concatenate([w[..., half:], w[..., :half]], axis=-1)


def _mla_weights(w_in_odd, w_q_up, w_kv_up):
    w_in_odd, w_q_up, w_kv_up = (w.astype(BF16) for w in (w_in_odd, w_q_up, w_kv_up))
    zeros = lambda *s: jnp.zeros(s, BF16)
    kr0 = MLA_Q_RANK + MLA_KV_RANK
    w_kr = w_in_odd[:, :, kr0:]
    lead = (N_ODD, D_MODEL)
    w_in = jnp.concatenate(
        [w_in_odd[:, :, :kr0],
         zeros(*lead, MLA_NOPE), w_kr, zeros(*lead, MLA_PAD),
         zeros(*lead, MLA_NOPE), _swap_rope_halves(w_kr), zeros(*lead, MLA_PAD)], axis=-1)
    wq4 = w_q_up.reshape(N_ODD, MLA_Q_RANK, MLA_HEADS, MLA_NOPE + MLA_ROPE)
    lead = (N_ODD, MLA_Q_RANK, MLA_HEADS)
    wq_main = jnp.concatenate([wq4, zeros(*lead, MLA_PAD)], axis=-1)
    wq_swap = jnp.concatenate([zeros(*lead, MLA_NOPE), _swap_rope_halves(wq4[..., MLA_NOPE:]),
                               zeros(*lead, MLA_PAD)], axis=-1)
    wq = jnp.concatenate([wq_main.reshape(N_ODD, MLA_Q_RANK, MLA_Q_W),
                          wq_swap.reshape(N_ODD, MLA_Q_RANK, MLA_Q_W)], axis=-1)
    wkv = w_kv_up.reshape(N_ODD, MLA_KV_RANK, MLA_HEADS, MLA_NOPE + MLA_V)
    wk_top = jnp.pad(wkv[..., :MLA_NOPE], ((0, 0), (0, 0), (0, 0), (0, MLA_HEAD_PAD - MLA_NOPE)))
    sel = jnp.pad(jnp.eye(MLA_ROPE, dtype=BF16), ((MLA_NOPE, MLA_PAD), (MLA_NOPE, MLA_PAD)))
    wk_bot = jnp.broadcast_to(jnp.tile(sel, (1, MLA_HEADS)), (N_ODD, MLA_HEAD_PAD, MLA_Q_W))
    wk = jnp.concatenate([wk_top.reshape(N_ODD, MLA_KV_RANK, MLA_Q_W), wk_bot], axis=1)
    even = (jnp.arange(MLA_HEADS) % 2 == 0)[:, None]
    wv4 = wkv[..., MLA_NOPE:]
    zero_v = jnp.zeros_like(wv4)
    wv = jnp.concatenate([jnp.where(even, wv4, zero_v).reshape(N_ODD, MLA_KV_RANK, MLA_O_W),
                          jnp.where(even, zero_v, wv4).reshape(N_ODD, MLA_KV_RANK, MLA_O_W)], axis=-1)
    return w_in, wq, wk, wv


def kernel(x_prompt, x_sample, state_s5_re, state_s5_im, cache_diff_k, cache_diff_v, cache_mla_ckv, cache_mla_krope, c, c_ctx, w_mod, b_mod, g_pre_mix, g_post_mix, g_pre_ffn, g_post_ffn, w_ffn_gate, w_ffn_up, w_ffn_down, w_in_even, w_out_even, s5_lam_re, s5_lam_im, s5_log_dt, s5_b_re, s5_b_im, s5_c_re, s5_c_im, s5_d, s5_glu_w, s5_glu_b, diff_lam_q1, diff_lam_k1, diff_lam_q2, diff_lam_k2, diff_subln_g, w_in_odd, mla_q_norm_g, mla_w_q_up, mla_kv_norm_g, mla_w_kv_up, w_out_odd):
    xs = (x_prompt.reshape(N_P, D_MODEL), x_sample.reshape(N_S, D_MODEL))
    conds = jnp.concatenate([c_ctx[None, :], c, jnp.zeros((N_COND - 1 - DEC_BATCH, D_MODEL), F32)], axis=0)
    mod = _modulation(conds, w_mod, b_mod).reshape(DEPTH, N_COND, N_MOD, D_MODEL)

    gains = (g_post_mix, g_pre_ffn, g_post_ffn)
    ffn_f32 = (w_ffn_gate, w_ffn_up, w_ffn_down)
    w_in_e = w_in_even.astype(BF16)
    w_out_e = w_out_even.astype(BF16)
    glu_w = s5_glu_w.astype(BF16)
    a_re, a_im, b_in, c_out = _s5_params(s5_lam_re, s5_lam_im, s5_log_dt, s5_b_re, s5_b_im, s5_c_re, s5_c_im)
    a = jnp.stack([a_re[:, 0], a_im[:, 0], a_re[:, 1], a_im[:, 1]], axis=1).reshape(N_EVEN, 4, S5_STATE_W)
    st = lambda s: jnp.moveaxis(s, 0, 2).reshape(N_EVEN, 2, 1, DEC_BATCH, S5_STATE_W)
    h0 = jnp.concatenate([jnp.zeros((N_EVEN, 2, 2, SUBLANES, 2 * S5_STATE_W), F32),
                          jnp.concatenate([st(state_s5_re), st(state_s5_im)], axis=-1)], axis=2)
    diff_ctx = (cache_diff_k.astype(BF16).reshape(DEC_BATCH, N_EVEN, PAST_LEN, DIFF_WIDTH),
                cache_diff_v.astype(BF16).reshape(DEC_BATCH, N_EVEN, PAST_LEN, DIFF_WIDTH))
    lam_vecs = (diff_lam_q1, diff_lam_k1, diff_lam_q2, diff_lam_k2)
    w_in_o, wq, wk, wv = _mla_weights(w_in_odd, mla_w_q_up, mla_w_kv_up)
    w_out_o = w_out_odd.astype(BF16)
    ckr_ctx = jnp.concatenate(
        [cache_mla_ckv.astype(BF16),
         jnp.pad(cache_mla_krope.astype(BF16), ((0, 0), (0, 0), (0, 0), (MLA_NOPE, MLA_PAD)))], axis=-1)
    diff_tabs = _diff_rope_tables()
    mla_tabs = _mla_rope_tables()

    s5_re_list, s5_im_list, dk_list, dv_list, ckv_list, kr_list = [], [], [], [], [], []
    for l in range(DEPTH):
        last = l == DEPTH - 1
        n_stream = 2 if last else 1
        next_cast = () if last else tuple((w, l + 1) for w in ffn_f32)
        if l % 2 == 0:
            e = l // 2
            lam_init = 0.8 - 0.6 * math.exp(-0.3 * l)
            pre = _even_pre(l, xs, mod, g_pre_mix, w_in_e, diff_tabs,
                            cast=tuple((w, 0) for w in ffn_f32) if l == 0 else ())
            u_ctx, u_lat, q, k, v, kf, vf = pre[:7]
            if l == 0:
                ffn_w = tuple(pre[7:])
            y_f, y_b, fin_f, fin_b = _s5_scan(e, u_ctx, u_lat, a, b_in, c_out, h0)
            fin = jnp.stack([fin_f[:2].reshape(BATCH, 2, S5_GROUPS, S5_STATE),
                             fin_b[:2].reshape(BATCH, 2, S5_GROUPS, S5_STATE)], axis=1)
            s5_re_list.append(fin[:, :, 0])
            s5_im_list.append(fin[:, :, 1])
            da_p = _diff_attention(e, q, k, v, None, lam_vecs, diff_subln_g, lam_init,
                                   batch=BATCH, n=SEQ, row0=0, tq=SEQ)
            da_s = _diff_attention(e, q, k, v, diff_ctx, lam_vecs, diff_subln_g, lam_init,
                                   batch=DEC_BATCH, n=DEC_SEQ, row0=N_P, tq=DIFF_TQ)
            post = _even_post(l, xs, mod, y_f, y_b, u_ctx, u_lat, s5_d, glu_w, s5_glu_b, da_p, da_s, w_out_e,
                              gains + ffn_w, last, cast=next_cast)
            xs, ffn_w = post[:n_stream], tuple(post[n_stream:])
            dk_list.append(kf.reshape(BATCH, SEQ, DIFF_HEADS, 2, DIFF_HEAD_DIM))
            dv_list.append(vf.reshape(BATCH, SEQ, DIFF_HEADS, 2 * DIFF_HEAD_DIM))
        else:
            o = l // 2
            q, ckr, ckv, krp = _odd_pre(l, xs, mod, g_pre_mix, w_in_o, mla_q_norm_g, mla_kv_norm_g, wq, mla_tabs)
            at_p = _mla_attention(o, q, ckr, None, wk, wv, batch=BATCH, n=SEQ, row0=0, tq=SEQ)
            at_s = _mla_attention(o, q, ckr, ckr_ctx, wk, wv, batch=DEC_BATCH, n=DEC_SEQ, row0=N_P, tq=MLA_TQ)
            post = _odd_post(l, xs, mod, at_p, at_s, w_out_o, gains + ffn_w, last, cast=next_cast)
            xs, ffn_w = post[:n_stream], tuple(post[n_stream:])
            ckv_list.append(ckv.reshape(BATCH, SEQ, MLA_KV_RANK))
            kr_list.append(krp[:, MLA_NOPE:MLA_NOPE + MLA_ROPE].reshape(BATCH, SEQ, MLA_ROPE))

    return (xs[0].reshape(BATCH, SEQ, D_MODEL), xs[1].reshape(DEC_BATCH, DEC_SEQ, D_MODEL),
            jnp.stack(s5_re_list, axis=1), jnp.stack(s5_im_list, axis=1),
            jnp.stack(dk_list, axis=1), jnp.stack(dv_list, axis=1),
            jnp.stack(ckv_list, axis=1), jnp.stack(kr_list, axis=1))
```

```python
import functools
import math

import jax
import jax.numpy as jnp
import numpy as np
from jax import lax
from jax.experimental import pallas as pl
from jax.experimental.pallas import tpu as pltpu

F32 = jnp.float32
BF16 = jnp.bfloat16

D_MODEL = 1024
BATCH = 16
SEQ = 256
DEPTH = 4
DEC_BATCH = 8
DEC_SEQ = 1024
PAST_LEN = 512
GRID_W = 64
N_EVEN = (DEPTH + 1) // 2
N_ODD = DEPTH // 2
EPS = 1e-6
ROPE_BASE = 10000.0
S5_WIDTH = D_MODEL // 2
S5_GROUP = 16
S5_GROUPS = S5_WIDTH // S5_GROUP
S5_STATE = 64
DIFF_HEAD_DIM = 64
DIFF_HEADS = (D_MODEL // 2) // (2 * DIFF_HEAD_DIM)
DIFF_WIDTH = DIFF_HEADS * 2 * DIFF_HEAD_DIM
EVEN_IN = S5_WIDTH + 3 * DIFF_WIDTH
MLA_HEADS = 16
MLA_NOPE = 64
MLA_ROPE = 32
MLA_V = 64
MLA_Q_RANK = 256
MLA_KV_RANK = 128
D_FF = ((8 * D_MODEL // 3 + 255) // 256) * 256

LANES = 128
SUBLANES = 8
BF16_SUBLANES = 16
VMEM_LIMIT = 56 * 1024 * 1024
VMEM_LIMIT_POST = 61 * 1024 * 1024
LOG2E = math.log2(math.e)

N_P = BATCH * SEQ
N_S = DEC_BATCH * DEC_SEQ
N_TOK = N_P + N_S
TM = 512
N_TILES = N_TOK // TM
P_TILES = N_P // TM
LAT_TILES_PER_SEQ = DEC_SEQ // TM
ROW_SUB = 2
N_COND = 16
N_MOD = 6
MOD_TN = 1536
DIFF_TQ = 512
MLA_TQ = 256

S5_STATE_W = S5_GROUPS * S5_STATE
S5_BLK = 4
S5_BLK_W = S5_STATE_W // S5_BLK
S5_GB = S5_GROUPS // S5_BLK
S5_IN_ROWS = S5_GB * S5_GROUP
S5_N_BLOCKS = N_EVEN * 2 * S5_BLK
S5_TM_W = SUBLANES * S5_WIDTH
SCAN_T = 64
SCAN_R = SCAN_T * SUBLANES
N_TM_ROWS = N_TOK // SUBLANES
N_CHUNKS = N_TM_ROWS // SCAN_T
SEQ_CHUNKS = SEQ // SCAN_T
CTX_CHUNKS = 2 * SEQ_CHUNKS

MLA_HEAD_PAD = LANES
MLA_PAD = MLA_HEAD_PAD - MLA_NOPE - MLA_ROPE
MLA_Q_W = MLA_HEADS * MLA_HEAD_PAD
MLA_IN_W = MLA_Q_RANK + MLA_KV_RANK + 2 * MLA_HEAD_PAD
MLA_CKR_W = MLA_KV_RANK + MLA_HEAD_PAD
MLA_O_W = MLA_HEADS * MLA_V


def _cparams(sem, vmem_limit=VMEM_LIMIT):
    return pltpu.CompilerParams(dimension_semantics=sem, vmem_limit_bytes=vmem_limit)


def _const_spec(shape):
    nd = len(shape)
    return pl.BlockSpec(shape, lambda *_: (0,) * nd, pipeline_mode=pl.Buffered(1))


def _layer_spec(tail, *lead):
    nt = len(tail)
    return pl.BlockSpec((None,) * len(lead) + tuple(tail), lambda *_: tuple(lead) + (0,) * nt,
                        pipeline_mode=pl.Buffered(1))


def _rms(x, g):
    return x * lax.rsqrt(jnp.mean(x * x, axis=-1, keepdims=True) + EPS) * g


def _dot(a, b):
    return jnp.dot(a, b, preferred_element_type=F32)


def _dot_nt(a, b):
    return lax.dot_general(a, b, (((1,), (1,)), ((), ())), preferred_element_type=F32)


def _row(ref, r):
    return ref[r:r + 1, :]


def _cond_of_tile(i):
    return jnp.where(i < P_TILES, 0, 1 + (i - P_TILES) // LAT_TILES_PER_SEQ)


def _rope_block(i):
    return jnp.where(i < P_TILES, 0, 1 + (i - P_TILES) % LAT_TILES_PER_SEQ)


def _tok_spec(width):
    return pl.BlockSpec((TM, width), lambda i: (i, 0))


def _ctx_spec(width):
    return pl.BlockSpec((TM, width), lambda i: (jnp.minimum(i, P_TILES - 1), 0))


def _lat_spec(width):
    return pl.BlockSpec((TM, width), lambda i: (jnp.maximum(i - P_TILES, 0), 0))


def _ctx_view_spec():
    pairs = SUBLANES // 2
    return pl.BlockSpec((SEQ, 2 * S5_WIDTH),
                        lambda i: (jnp.minimum(i, P_TILES - 1) // pairs, jnp.minimum(i, P_TILES - 1) % pairs))


def _lat_view_spec(row0):
    return pl.BlockSpec((TM, S5_WIDTH), lambda i: (row0 + jnp.maximum(i - P_TILES, 0) % LAT_TILES_PER_SEQ,
                                                   jnp.maximum(i - P_TILES, 0) // LAT_TILES_PER_SEQ))


def _stream_specs(arrays, width):
    return [_tok_spec(width)] if len(arrays) == 1 else [_ctx_spec(width), _lat_spec(width)]


def _read_stream(refs):
    if len(refs) == 1:
        return refs[0][...]
    return jnp.where(pl.program_id(0) < P_TILES, refs[0][...], refs[1][...])


def _write_stream(out_refs, x2):
    if len(out_refs) == 1:
        out_refs[0][...] = x2
    else:
        i = pl.program_id(0)

        @pl.when(i < P_TILES)
        def _():
            out_refs[0][...] = x2

        @pl.when(i >= P_TILES)
        def _():
            out_refs[1][...] = x2


def _out_stream(split):
    if split:
        return ([_ctx_spec(D_MODEL), _lat_spec(D_MODEL)],
                [jax.ShapeDtypeStruct((N_P, D_MODEL), F32), jax.ShapeDtypeStruct((N_S, D_MODEL), F32)])
    return [_tok_spec(D_MODEL)], [jax.ShapeDtypeStruct((N_TOK, D_MODEL), F32)]


def _mod_spec(l):
    return pl.BlockSpec((None, None, N_MOD, D_MODEL), lambda i: (l, _cond_of_tile(i), 0, 0))


def _sub_slices():
    n = TM // ROW_SUB
    return [slice(r * n, (r + 1) * n) for r in range(ROW_SUB)]


def _sub_rows(v):
    return [v[rs] for rs in _sub_slices()]


def _cast_specs(w_stack, layer):
    _, r, c = w_stack.shape
    rc = BF16_SUBLANES
    while r % rc or r // rc > N_TILES:
        rc += BF16_SUBLANES
    last = r // rc - 1
    return (pl.BlockSpec((None, rc, c), lambda i: (layer, jnp.minimum(i, last), 0)),
            pl.BlockSpec((rc, c), lambda i: (jnp.minimum(i, last), 0)),
            jax.ShapeDtypeStruct((r, c), BF16))


def _cast_side_job(in_refs, out_refs):
    for wi, wo in zip(in_refs, out_refs):
        wo[...] = wi[...].astype(BF16)


def _mod_kernel(c_ref, w_ref, b_ref, o_ref):
    s = jax.nn.silu(c_ref[...])
    o_ref[0] = _dot(s.astype(BF16), w_ref[0].astype(BF16)) + b_ref[0]


def _modulation(conds, w_mod, b_mod):
    tn = MOD_TN
    return pl.pallas_call(
        _mod_kernel,
        grid=(DEPTH, N_MOD * D_MODEL // tn),
        in_specs=[
            pl.BlockSpec((N_COND, D_MODEL), lambda l, n: (0, 0)),
            pl.BlockSpec((1, D_MODEL, tn), lambda l, n: (l, 0, n)),
            pl.BlockSpec((1, 1, tn), lambda l, n: (l, 0, n)),
        ],
        out_specs=pl.BlockSpec((1, N_COND, tn), lambda l, n: (l, 0, n)),
        out_shape=jax.ShapeDtypeStruct((DEPTH, N_COND, N_MOD * D_MODEL), F32),
        compiler_params=_cparams(("parallel", "parallel")),
        name="modulation",
    )(conds, w_mod, b_mod.reshape(DEPTH, 1, N_MOD * D_MODEL))


def _block_diag_lanes(x, row_group, col_group):
    xt = jnp.concatenate([x] * S5_GB, axis=1)
    rg = lax.broadcasted_iota(jnp.int32, xt.shape, 0) // row_group
    cg = lax.broadcasted_iota(jnp.int32, xt.shape, 1) // col_group
    return jnp.where(rg == cg, xt, 0.0)


def _s5_param_kernel(lr_ref, li_ref, ldt_ref, br_ref, bi_ref, cr_ref, ci_ref, are_ref, aim_ref, bin_ref, cout_ref):
    lr, li = lr_ref[...], li_ref[...]
    dt = jnp.exp(ldt_ref[...])
    mag = jnp.exp(lr * dt)
    a_re, a_im = mag * jnp.cos(li * dt), mag * jnp.sin(li * dt)
    den = lr * lr + li * li
    f_re = ((a_re - 1.0) * lr + a_im * li) / den
    f_im = (a_im * lr - (a_re - 1.0) * li) / den
    are_ref[...] = a_re
    aim_ref[...] = a_im
    rep = lambda v: jnp.broadcast_to(v[:, None, :], (S5_GB, S5_GROUP, S5_STATE)).reshape(S5_IN_ROWS, S5_STATE)
    f_re, f_im = rep(f_re), rep(f_im)
    br, bi = br_ref[...], bi_ref[...]
    bin_ref[:, :S5_BLK_W] = _block_diag_lanes(f_re * br - f_im * bi, S5_GROUP, S5_STATE).astype(BF16)
    bin_ref[:, S5_BLK_W:] = _block_diag_lanes(f_re * bi + f_im * br, S5_GROUP, S5_STATE).astype(BF16)
    cout_ref[:S5_BLK_W, :] = _block_diag_lanes(cr_ref[...], S5_STATE, S5_GROUP).astype(BF16)
    cout_ref[S5_BLK_W:, :] = _block_diag_lanes(-ci_ref[...], S5_STATE, S5_GROUP).astype(BF16)


def _s5_params(lam_re, lam_im, log_dt, b_re, b_im, c_re, c_im):
    grp = (S5_N_BLOCKS, S5_GB, S5_STATE)
    blk = (S5_N_BLOCKS, S5_IN_ROWS, S5_STATE)
    ldt = jnp.broadcast_to(log_dt[..., None], lam_re.shape).reshape(grp)
    bt = lambda b: jnp.swapaxes(b, -1, -2).reshape(blk)
    ct = lambda c: jnp.swapaxes(c, -1, -2).reshape(S5_N_BLOCKS, S5_BLK_W, S5_GROUP)
    spec = lambda r, w: pl.BlockSpec((None, r, w), lambda i: (i, 0, 0))
    a_re, a_im, b_in, c_out = pl.pallas_call(
        _s5_param_kernel,
        grid=(S5_N_BLOCKS,),
        in_specs=[spec(S5_GB, S5_STATE)] * 3 + [spec(S5_IN_ROWS, S5_STATE)] * 2 + [spec(S5_BLK_W, S5_GROUP)] * 2,
        out_specs=[spec(S5_GB, S5_STATE)] * 2 + [spec(S5_IN_ROWS, 2 * S5_BLK_W), spec(2 * S5_BLK_W, LANES)],
        out_shape=[jax.ShapeDtypeStruct(grp, F32)] * 2
        + [jax.ShapeDtypeStruct((S5_N_BLOCKS, S5_IN_ROWS, 2 * S5_BLK_W), BF16),
           jax.ShapeDtypeStruct((S5_N_BLOCKS, 2 * S5_BLK_W, LANES), BF16)],
        compiler_params=_cparams(("parallel",)),
        name="s5_params",
    )(lam_re.reshape(grp), lam_im.reshape(grp), ldt, bt(b_re), bt(b_im), ct(c_re), ct(c_im))
    pick = lambda a: a.reshape(N_EVEN, 2, S5_GROUPS, S5_STATE)
    lead = (N_EVEN, 2, S5_BLK)
    return pick(a_re), pick(a_im), b_in.reshape(lead + b_in.shape[1:]), c_out.reshape(lead + c_out.shape[1:])


def _seq_of_chunk(c):
    return (c >= SEQ_CHUNKS).astype(jnp.int32) + (c >= CTX_CHUNKS).astype(jnp.int32)


def _s5_scan_kernel(ufc_ref, ufl_ref, ubc_ref, ubl_ref, a_ref, bin_ref, cout_ref, h0f_ref, h0b_ref,
                    yf_ref, yb_ref, finf_ref, finb_ref, us_ref, hs_ref, ys_ref, st_ref):
    j = pl.program_id(0)
    cf = j
    cb = N_CHUNKS - 1 - j

    @pl.when((cf == 0) | (cf == SEQ_CHUNKS) | (cf == CTX_CHUNKS))
    def _():
        st_ref[0] = h0f_ref[...]

    @pl.when((cb == N_CHUNKS - 1) | (cb == CTX_CHUNKS - 1) | (cb == SEQ_CHUNKS - 1))
    def _():
        st_ref[1] = h0b_ref[...]

    dirs = ((0, cf, ufc_ref, ufl_ref, yf_ref), (1, cb, ubc_ref, ubl_ref, yb_ref))
    for k in range(S5_BLK):
        lo, hi = k * S5_BLK_W, (k + 1) * S5_BLK_W
        for d, chunk, uc_ref, ul_ref, y_ref in dirs:
            for b in range(SUBLANES):
                c0 = b * S5_WIDTH + k * LANES
                us_ref[d, k, pl.ds(b, SCAN_T, stride=SUBLANES), :] = jnp.where(
                    chunk < CTX_CHUNKS, uc_ref[:, c0:c0 + LANES], ul_ref[:, c0:c0 + LANES])
            hs_ref[d, k] = _dot(us_ref[d, k].astype(BF16), bin_ref[d, k])

            ar = jnp.broadcast_to(a_ref[2 * d:2 * d + 1, lo:hi], (SUBLANES, S5_BLK_W))
            ai = jnp.broadcast_to(a_ref[2 * d + 1:2 * d + 2, lo:hi], (SUBLANES, S5_BLK_W))
            hr, hi_ = st_ref[d, :, lo:hi], st_ref[d, :, S5_STATE_W + lo:S5_STATE_W + hi]
            for t in range(SCAN_T):
                rows = pl.ds((t if d == 0 else SCAN_T - 1 - t) * SUBLANES, SUBLANES)
                nr = ar * hr - ai * hi_ + hs_ref[d, k, rows, 0:S5_BLK_W]
                ni = ar * hi_ + ai * hr + hs_ref[d, k, rows, S5_BLK_W:2 * S5_BLK_W]
                hs_ref[d, k, rows, 0:S5_BLK_W] = nr
                hs_ref[d, k, rows, S5_BLK_W:2 * S5_BLK_W] = ni
                hr, hi_ = nr, ni
            st_ref[d, :, lo:hi] = hr
            st_ref[d, :, S5_STATE_W + lo:S5_STATE_W + hi] = hi_

            ys_ref[d, k] = _dot(hs_ref[d, k].astype(BF16), cout_ref[d, k])
            for b in range(SUBLANES):
                c0 = b * S5_WIDTH + k * LANES
                y_ref[:, c0:c0 + LANES] = ys_ref[d, k, pl.ds(b, SCAN_T, stride=SUBLANES), :]

    finf_ref[...] = st_ref[0]
    finb_ref[...] = st_ref[1]


def _s5_scan(e, u_ctx, u_lat, a, b_in, c_out, h0):
    fwd = lambda j: j
    rev = lambda j: N_CHUNKS - 1 - j
    st_w = 2 * S5_STATE_W
    in_c = lambda f: pl.BlockSpec((SCAN_T, S5_TM_W), lambda j: (jnp.minimum(f(j), CTX_CHUNKS - 1), 0))
    in_l = lambda f: pl.BlockSpec((SCAN_T, S5_TM_W), lambda j: (jnp.maximum(f(j) - CTX_CHUNKS, 0), 0))
    out = lambda f: pl.BlockSpec((SCAN_T, S5_TM_W), lambda j: (f(j), 0))
    h0_spec = lambda d, f: pl.BlockSpec((None, None, None, SUBLANES, st_w),
                                        lambda j: (e, d, _seq_of_chunk(f(j)), 0, 0))
    fin = lambda f: pl.BlockSpec((None, SUBLANES, st_w), lambda j: (_seq_of_chunk(f(j)), 0, 0))
    st_shape = jax.ShapeDtypeStruct((3, SUBLANES, st_w), F32)
    return pl.pallas_call(
        _s5_scan_kernel,
        grid=(N_CHUNKS,),
        in_specs=[in_c(fwd), in_l(fwd), in_c(rev), in_l(rev), _layer_spec(a.shape[1:], e),
                  _layer_spec(b_in.shape[1:], e), _layer_spec(c_out.shape[1:], e), h0_spec(0, fwd), h0_spec(1, rev)],
        out_specs=[out(fwd), out(rev), fin(fwd), fin(rev)],
        out_shape=[jax.ShapeDtypeStruct((N_TM_ROWS, S5_TM_W), F32)] * 2 + [st_shape] * 2,
        scratch_shapes=[pltpu.VMEM((2, S5_BLK, SCAN_R, LANES), F32),
                        pltpu.VMEM((2, S5_BLK, SCAN_R, 2 * S5_BLK_W), F32),
                        pltpu.VMEM((2, S5_BLK, SCAN_R, LANES), F32),
                        pltpu.VMEM((2, SUBLANES, st_w), F32)],
        compiler_params=_cparams(("arbitrary",)),
        name="s5_scan",
    )(u_ctx, u_lat, u_ctx, u_lat, a, b_in, c_out, h0, h0)


def _even_pre_kernel(l, n_x, n_cast, *refs):
    x_refs, refs = refs[:n_x], refs[n_x:]
    mod_ref, g_ref, w_ref, cos_ref, sa_ref, sb_ref = refs[:6]
    uc_ref, ul_ref, q_ref, k_ref, v_ref, kf_ref, vf_ref = refs[6 + n_cast:13 + n_cast]
    _cast_side_job(refs[6:6 + n_cast], refs[13 + n_cast:])
    m = mod_ref[...]
    half = DIFF_HEAD_DIM // 2
    subs = _sub_slices()
    h = [_rms(x, _row(g_ref, l)) * (1.0 + m[1:2]) + m[0:1] for x in _sub_rows(_read_stream(x_refs))]
    proj = [_dot(v.astype(BF16), w_ref[...]) for v in h]
    us, ks, vs = [], [], []
    for rs, p in zip(subs, proj):
        cos, sa, sb = cos_ref[rs, :], sa_ref[rs, :], sb_ref[rs, :]

        def rope(z):
            return z * cos + pltpu.roll(z, half, 1) * sa + pltpu.roll(z, DIFF_WIDTH - half, 1) * sb

        k = rope(p[:, S5_WIDTH + DIFF_WIDTH:S5_WIDTH + 2 * DIFF_WIDTH])
        v = p[:, S5_WIDTH + 2 * DIFF_WIDTH:]
        q_ref[rs, :] = (rope(p[:, S5_WIDTH:S5_WIDTH + DIFF_WIDTH]) * (DIFF_HEAD_DIM ** -0.5 * LOG2E)).astype(BF16)
        k_ref[rs, :] = k.astype(BF16)
        v_ref[rs, :] = v.astype(BF16)
        us.append(p[:, :S5_WIDTH])
        ks.append(k)
        vs.append(v)

    @pl.when(pl.program_id(0) < P_TILES)
    def _():
        uc_ref[...] = jnp.concatenate(us, axis=1)
        for rs, k, v in zip(subs, ks, vs):
            kf_ref[rs, :] = k
            vf_ref[rs, :] = v

    @pl.when(pl.program_id(0) >= P_TILES)
    def _():
        ul_ref[...] = jnp.concatenate(us, axis=0)


def _even_pre(l, xs, mod, g_pre, w_in, rope_tabs, cast=()):
    rope_spec = pl.BlockSpec((TM, DIFF_WIDTH), lambda i: (_rope_block(i), 0))
    casts = [_cast_specs(w, lyr) for w, lyr in cast]
    return pl.pallas_call(
        functools.partial(_even_pre_kernel, l, len(xs), len(casts)),
        grid=(N_TILES,),
        in_specs=_stream_specs(xs, D_MODEL) + [
            _mod_spec(l),
            _const_spec(g_pre.shape),
            _layer_spec((D_MODEL, EVEN_IN), l // 2),
            rope_spec, rope_spec, rope_spec,
        ] + [c[0] for c in casts],
        out_specs=[
            _ctx_view_spec(), _lat_view_spec(0),
            _tok_spec(DIFF_WIDTH), _tok_spec(DIFF_WIDTH), _tok_spec(DIFF_WIDTH),
            _ctx_spec(DIFF_WIDTH), _ctx_spec(DIFF_WIDTH),
        ] + [c[1] for c in casts],
        out_shape=[jax.ShapeDtypeStruct((N_P // SUBLANES, S5_TM_W), F32),
                   jax.ShapeDtypeStruct((N_S // SUBLANES, S5_TM_W), F32)]
        + [jax.ShapeDtypeStruct((N_TOK, DIFF_WIDTH), BF16)] * 3
        + [jax.ShapeDtypeStruct((N_P, DIFF_WIDTH), F32)] * 2 + [c[2] for c in casts],
        compiler_params=_cparams(("arbitrary",)),
        name="even_pre",
    )(*xs, mod, g_pre, w_in, *rope_tabs, *[w for w, _ in cast])


def _softmax2_parts(parts):
    m = functools.reduce(jnp.maximum, [jnp.max(s, axis=-1, keepdims=True) for s in parts])
    ps = [jnp.exp2(s - m) for s in parts]
    l = functools.reduce(jnp.add, [jnp.sum(p, axis=-1, keepdims=True) for p in ps])
    return ps, l


def _diff_attn_kernel(e, lam_init, has_ctx, *refs):
    if has_ctx:
        q_ref, k_ref, v_ref, kc_ref, vc_ref, lq1, lk1, lq2, lk2, g_ref, o_ref = refs
    else:
        q_ref, k_ref, v_ref, lq1, lk1, lq2, lk2, g_ref, o_ref = refs
    lam = (jnp.exp(jnp.sum(_row(lq1, e) * _row(lk1, e), axis=-1, keepdims=True))
           - jnp.exp(jnp.sum(_row(lq2, e) * _row(lk2, e), axis=-1, keepdims=True)) + lam_init)
    first = lax.broadcasted_iota(jnp.int32, (1, 2 * DIFF_HEAD_DIM), 1) < DIFF_HEAD_DIM
    g = _row(g_ref, e)

    def head_slice(h):
        return slice(h * 2 * DIFF_HEAD_DIM, (h + 1) * 2 * DIFF_HEAD_DIM)

    def scores(h):
        sl = head_slice(h)
        qh = q_ref[:, sl]
        zero = jnp.zeros_like(qh)
        keys = ([kc_ref[:, sl]] if has_ctx else []) + [k_ref[:, sl]]
        return [[_dot_nt(qc, kk) for kk in keys] for qc in (jnp.where(first, qh, zero), jnp.where(first, zero, qh))]

    def finish(w, l1, sl):
        vals = ([vc_ref[:, sl]] if has_ctx else []) + [v_ref[:, sl]]
        o = functools.reduce(jnp.add, [_dot(a, vv) for a, vv in zip(w, vals)]) * (1.0 / l1)
        o_ref[:, sl] = (_rms(o, g) * (1.0 - lam_init)).astype(o_ref.dtype)

    s_next = scores(0)
    pending = None
    for h in range(DIFF_HEADS):
        s1, s2 = s_next
        if h + 1 < DIFF_HEADS:
            s_next = scores(h + 1)
        p1, l1 = _softmax2_parts(s1)
        p2, l2 = _softmax2_parts(s2)
        ratio = lam * l1 / l2
        w = [(a - b * ratio).astype(BF16) for a, b in zip(p1, p2)]
        if pending is not None:
            finish(*pending)
        pending = (w, l1, head_slice(h))
    finish(*pending)


def _diff_attention(e, q, k, v, ctx, lam_vecs, subln_g, lam_init, *, batch, n, row0, tq):
    nq = n // tq
    has_ctx = ctx is not None
    in_specs = [
        pl.BlockSpec((tq, DIFF_WIDTH), lambda b, i: (row0 // tq + b * nq + i, 0)),
        pl.BlockSpec((n, DIFF_WIDTH), lambda b, i: (row0 // n + b, 0)),
        pl.BlockSpec((n, DIFF_WIDTH), lambda b, i: (row0 // n + b, 0)),
    ]
    args = [q, k, v]
    if has_ctx:
        in_specs += [pl.BlockSpec((None, None, PAST_LEN, DIFF_WIDTH), lambda b, i: (b, e, 0, 0))] * 2
        args += list(ctx)
    in_specs += [_const_spec(t.shape) for t in lam_vecs] + [_const_spec(subln_g.shape)]
    args += list(lam_vecs) + [subln_g]
    return pl.pallas_call(
        functools.partial(_diff_attn_kernel, e, lam_init, has_ctx),
        grid=(batch, nq),
        in_specs=in_specs,
        out_specs=pl.BlockSpec((tq, DIFF_WIDTH), lambda b, i: (b * nq + i, 0)),
        out_shape=jax.ShapeDtypeStruct((batch * n, DIFF_WIDTH), BF16),
        compiler_params=_cparams(("parallel", "parallel")),
        name="diff_attn_ctx" if has_ctx else "diff_attn",
    )(*args)


def _post_tail(l, xs, m, mixes, gpm_ref, gpf_ref, gqf_ref, wg_ref, wu_ref, wd_ref):
    x1 = [x + m[2:3] * _rms(mix, _row(gpm_ref, l)) for x, mix in zip(xs, mixes)]
    h = [(_rms(v, _row(gpf_ref, l)) * (1.0 + m[4:5]) + m[3:4]).astype(BF16) for v in x1]
    act = [jax.nn.silu(_dot(v, wg_ref[...])) * _dot(v, wu_ref[...]) for v in h]
    y = [_dot(v.astype(BF16), wd_ref[...]) for v in act]
    return jnp.concatenate([a + m[5:6] * _rms(b, _row(gqf_ref, l)) for a, b in zip(x1, y)], axis=0)


def _ffn_specs():
    vec = _const_spec((DEPTH, D_MODEL))
    return [vec, vec, vec, _const_spec((D_MODEL, D_FF)), _const_spec((D_MODEL, D_FF)), _const_spec((D_FF, D_MODEL))]


def _even_post_kernel(l, n_x, n_out, n_cast, *refs):
    e = l // 2
    x_refs, refs = refs[:n_x], refs[n_x:]
    (mod_ref, yfp_ref, ybp_ref, up_ref, yfs_ref, ybs_ref, us_ref, d_ref, gw_ref, gb_ref, dap_ref, das_ref,
     wos_ref, wod_ref, gpm_ref, gpf_ref, gqf_ref, wg_ref, wu_ref, wd_ref) = refs[:20]
    m = mod_ref[...]
    d = _row(d_ref, e)
    yp = yfp_ref[...] + ybp_ref[...] + jnp.concatenate([d, d], axis=1) * up_ref[...]
    ys = yfs_ref[...] + ybs_ref[...] + d * us_ref[...]
    y = jnp.where(pl.program_id(0) < P_TILES,
                  jnp.concatenate([yp[:, :S5_WIDTH], yp[:, S5_WIDTH:]], axis=0), ys)
    g = [jax.nn.gelu(v) for v in _sub_rows(y)]
    s5 = [v * jax.nn.sigmoid(_dot(v.astype(BF16), gw_ref[...]) + _row(gb_ref, e)) for v in g]
    mixes = [_dot(v.astype(BF16), wos_ref[...]) + _dot(da, wod_ref[...])
             for v, da in zip(s5, _sub_rows(_read_stream((dap_ref, das_ref))))]
    x2 = _post_tail(l, _sub_rows(_read_stream(x_refs)), m, mixes, gpm_ref, gpf_ref, gqf_ref, wg_ref, wu_ref, wd_ref)
    n_in = 20 + n_cast
    _write_stream(refs[n_in:n_in + n_out], x2)
    _cast_side_job(refs[20:n_in], refs[n_in + n_out:])


def _even_post(l, xs, mod, y_f, y_b, u_ctx, u_lat, s5_d, glu_w, glu_b, da_p, da_s, w_out, ffn_args, split_out,
               cast=()):
    e = l // 2
    ctx_view = _ctx_view_spec()
    lat_view = _lat_view_spec(N_P // SUBLANES // TM)
    out_specs, out_shape = _out_stream(split_out)
    casts = [_cast_specs(w, lyr) for w, lyr in cast]
    w_half = lambda r: pl.BlockSpec((None, S5_WIDTH, D_MODEL), lambda i: (e, r, 0), pipeline_mode=pl.Buffered(1))
    return pl.pallas_call(
        functools.partial(_even_post_kernel, l, len(xs), len(out_specs), len(casts)),
        grid=(N_TILES,),
        in_specs=_stream_specs(xs, D_MODEL) + [
            _mod_spec(l),
            ctx_view, ctx_view, ctx_view, lat_view, lat_view, _lat_view_spec(0),
            _const_spec(s5_d.shape), _layer_spec((S5_WIDTH, S5_WIDTH), e), _const_spec(glu_b.shape),
            _ctx_spec(DIFF_WIDTH), _lat_spec(DIFF_WIDTH),
            w_half(0), w_half(1),
        ] + _ffn_specs() + [c[0] for c in casts],
        out_specs=out_specs + [c[1] for c in casts],
        out_shape=out_shape + [c[2] for c in casts],
        compiler_params=_cparams(("arbitrary",), VMEM_LIMIT_POST),
        name="even_post_ffn",
    )(*xs, mod, y_f, y_b, u_ctx, y_f, y_b, u_lat, s5_d, glu_w, glu_b, da_p, da_s, w_out, w_out, *ffn_args,
      *[w for w, _ in cast])


def _odd_pre_kernel(l, n_x, *refs):
    o = l // 2
    x_refs, refs = refs[:n_x], refs[n_x:]
    (mod_ref, g_ref, w_ref, gq_ref, gkv_ref, wq_ref, cos_ref, sin_ref,
     q_ref, ckr_ref, ckv_ref, kr_ref) = refs
    m = mod_ref[...]
    kr0 = MLA_Q_RANK + MLA_KV_RANK
    subs = _sub_slices()
    h = [_rms(x, _row(g_ref, l)) * (1.0 + m[1:2]) + m[0:1] for x in _sub_rows(_read_stream(x_refs))]
    proj = [_dot(v.astype(BF16), w_ref[...]) for v in h]
    cq = [_rms(p[:, :MLA_Q_RANK], _row(gq_ref, o)) for p in proj]
    ckv = [_rms(p[:, MLA_Q_RANK:kr0], _row(gkv_ref, o)) for p in proj]
    q = [_dot(v.astype(BF16), wq_ref[...]) * ((MLA_NOPE + MLA_ROPE) ** -0.5 * LOG2E) for v in cq]
    for rs, p, qv, cv in zip(subs, proj, q, ckv):
        cos, sin = cos_ref[rs, :], sin_ref[rs, :]
        krp = p[:, kr0:kr0 + MLA_HEAD_PAD]
        krp_sw = p[:, kr0 + MLA_HEAD_PAD:]
        for hd in range(MLA_HEADS):
            sl = slice(hd * MLA_HEAD_PAD, (hd + 1) * MLA_HEAD_PAD)
            sw = slice(MLA_Q_W + hd * MLA_HEAD_PAD, MLA_Q_W + (hd + 1) * MLA_HEAD_PAD)
            q_ref[rs, sl] = (qv[:, sl] * cos + qv[:, sw] * sin).astype(BF16)
        ckr_ref[rs, :MLA_KV_RANK] = cv.astype(BF16)
        ckr_ref[rs, MLA_KV_RANK:] = (krp * cos + krp_sw * sin).astype(BF16)

    @pl.when(pl.program_id(0) < P_TILES)
    def _():
        for rs, p, cv in zip(subs, proj, ckv):
            ckv_ref[rs, :] = cv
            kr_ref[rs, :] = p[:, kr0:kr0 + MLA_HEAD_PAD]


def _odd_pre(l, xs, mod, g_pre, w_in, gq, gkv, wq, rope_tabs):
    o = l // 2
    rope_spec = pl.BlockSpec((TM, MLA_HEAD_PAD), lambda i: (_rope_block(i), 0))
    return pl.pallas_call(
        functools.partial(_odd_pre_kernel, l, len(xs)),
        grid=(N_TILES,),
        in_specs=_stream_specs(xs, D_MODEL) + [
            _mod_spec(l),
            _const_spec(g_pre.shape),
            _layer_spec((D_MODEL, MLA_IN_W), o),
            _const_spec(gq.shape), _const_spec(gkv.shape),
            _layer_spec((MLA_Q_RANK, 2 * MLA_Q_W), o),
            rope_spec, rope_spec,
        ],
        out_specs=[_tok_spec(MLA_Q_W), _tok_spec(MLA_CKR_W), _ctx_spec(MLA_KV_RANK), _ctx_spec(MLA_HEAD_PAD)],
        out_shape=[
            jax.ShapeDtypeStruct((N_TOK, MLA_Q_W), BF16),
            jax.ShapeDtypeStruct((N_TOK, MLA_CKR_W), BF16),
            jax.ShapeDtypeStruct((N_P, MLA_KV_RANK), F32),
            jax.ShapeDtypeStruct((N_P, MLA_HEAD_PAD), F32),
        ],
        compiler_params=_cparams(("arbitrary",)),
        name="odd_pre",
    )(*xs, mod, g_pre, w_in, gq, gkv, wq, *rope_tabs)


def _mla_attn_kernel(has_ctx, *refs):
    if has_ctx:
        q_ref, ckr_ref, ckrc_ref, wk_ref, wv_ref, o_ref, kf_ref, vf_ref = refs
    else:
        q_ref, ckr_ref, wk_ref, wv_ref, o_ref, kf_ref, vf_ref = refs
    off = PAST_LEN if has_ctx else 0

    @pl.when(pl.program_id(1) == 0)
    def _():
        srcs = [(0, ckrc_ref)] if has_ctx else []
        srcs.append((off, ckr_ref))
        wv = wv_ref[:, :MLA_O_W] + wv_ref[:, MLA_O_W:]
        even = (lax.broadcasted_iota(jnp.int32, (1, MLA_O_W), 1) // MLA_V) % 2 == 0
        for r0, src in srcs:
            c = src[...]
            rows = pl.ds(r0, c.shape[0])
            kf_ref[rows, :] = _dot(c, wk_ref[...]).astype(BF16)
            v = _dot(c[:, :MLA_KV_RANK], wv).astype(BF16)
            if has_ctx:
                zero = jnp.zeros_like(v)
                vf_ref[rows, :MLA_O_W] = jnp.where(even, v, zero)
                vf_ref[rows, MLA_O_W:] = jnp.where(even, zero, v)
            else:
                vf_ref[rows, :] = v

    low = lax.broadcasted_iota(jnp.int32, (1, LANES), 1) < MLA_V

    def scores(hd):
        cs = slice(hd * MLA_HEAD_PAD, (hd + 1) * MLA_HEAD_PAD)
        return _dot_nt(q_ref[:, cs], kf_ref[:, cs])

    pv, rl = {}, {}

    def value_product(hd, p):
        j, t = divmod(hd, 2)
        slot = slice(j * LANES, (j + 1) * LANES)
        v_cols = slice(t * MLA_O_W + j * LANES, t * MLA_O_W + (j + 1) * LANES) if has_ctx else slot
        pv[hd] = _dot(p, vf_ref[:, v_cols])
        if t == 1:
            a, b = pv.pop(hd - 1), pv.pop(hd)
            if has_ctx:
                o = (a + b) * jnp.where(low, rl[hd - 1], rl[hd])
            else:
                o = jnp.where(low, a * rl[hd - 1], b * rl[hd])
            o_ref[:, slot] = o.astype(o_ref.dtype)

    s_next = scores(0)
    for hd in range(MLA_HEADS):
        s = s_next
        if hd + 1 < MLA_HEADS:
            s_next = scores(hd + 1)
        mx = jnp.max(s, axis=-1, keepdims=True)
        p = jnp.exp2(s - mx)
        rl[hd] = 1.0 / jnp.sum(p, axis=-1, keepdims=True)
        value_product(hd, p.astype(BF16))


def _mla_attention(o, q, ckr, ckr_ctx, wk, wv, *, batch, n, row0, tq):
    nq = n // tq
    has_ctx = ckr_ctx is not None
    s_len = n + (PAST_LEN if has_ctx else 0)
    in_specs = [
        pl.BlockSpec((tq, MLA_Q_W), lambda b, i: (row0 // tq + b * nq + i, 0)),
        pl.BlockSpec((n, MLA_CKR_W), lambda b, i: (row0 // n + b, 0)),
    ]
    args = [q, ckr]
    if has_ctx:
        in_specs.append(pl.BlockSpec((None, None, PAST_LEN, MLA_CKR_W), lambda b, i: (b, o, 0, 0)))
        args.append(ckr_ctx)
    in_specs += [_layer_spec(wk.shape[1:], o), _layer_spec(wv.shape[1:], o)]
    args += [wk, wv]
    return pl.pallas_call(
        functools.partial(_mla_attn_kernel, has_ctx),
        grid=(batch, nq),
        in_specs=in_specs,
        out_specs=pl.BlockSpec((tq, MLA_O_W), lambda b, i: (b * nq + i, 0)),
        out_shape=jax.ShapeDtypeStruct((batch * n, MLA_O_W), BF16),
        scratch_shapes=[pltpu.VMEM((s_len, MLA_Q_W), BF16),
                        pltpu.VMEM((s_len, (2 if has_ctx else 1) * MLA_O_W), BF16)],
        compiler_params=_cparams(("parallel", "arbitrary")),
        name="mla_attn_ctx" if has_ctx else "mla_attn",
    )(*args)


def _odd_post_kernel(l, n_x, n_out, n_cast, *refs):
    x_refs, refs = refs[:n_x], refs[n_x:]
    mod_ref, ap_ref, as_ref, wo_ref, gpm_ref, gpf_ref, gqf_ref, wg_ref, wu_ref, wd_ref = refs[:10]
    mixes = [_dot(a, wo_ref[...]) for a in _sub_rows(_read_stream((ap_ref, as_ref)))]
    x2 = _post_tail(l, _sub_rows(_read_stream(x_refs)), mod_ref[...], mixes, gpm_ref, gpf_ref, gqf_ref,
                    wg_ref, wu_ref, wd_ref)
    n_in = 10 + n_cast
    _write_stream(refs[n_in:n_in + n_out], x2)
    _cast_side_job(refs[10:n_in], refs[n_in + n_out:])


def _odd_post(l, xs, mod, at_p, at_s, w_out, ffn_args, split_out, cast=()):
    out_specs, out_shape = _out_stream(split_out)
    casts = [_cast_specs(w, lyr) for w, lyr in cast]
    return pl.pallas_call(
        functools.partial(_odd_post_kernel, l, len(xs), len(out_specs), len(casts)),
        grid=(N_TILES,),
        in_specs=_stream_specs(xs, D_MODEL) + [
            _mod_spec(l),
            _ctx_spec(MLA_O_W), _lat_spec(MLA_O_W),
            _layer_spec((MLA_O_W, D_MODEL), l // 2),
        ] + _ffn_specs() + [c[0] for c in casts],
        out_specs=out_specs + [c[1] for c in casts],
        out_shape=out_shape + [c[2] for c in casts],
        compiler_params=_cparams(("arbitrary",)),
        name="odd_post_ffn",
    )(*xs, mod, at_p, at_s, w_out, *ffn_args, *[w for w, _ in cast])


def _rope_angles(rot_dim):
    rows = DEC_SEQ // GRID_W
    row = np.repeat(np.arange(rows, dtype=np.float32), GRID_W)
    col = np.tile(np.arange(GRID_W, dtype=np.float32), rows)
    n_freq = rot_dim // 4
    inv = (np.float32(ROPE_BASE) ** (-np.arange(n_freq, dtype=np.float32) / np.float32(n_freq))).astype(np.float32)
    ang = np.concatenate([row[:, None] * inv, col[:, None] * inv], axis=-1).astype(np.float32)
    return np.cos(ang), np.sin(ang)


def _with_identity(cos, *sins):
    one = np.ones((TM, cos.shape[1]), np.float32)
    zero = np.zeros((TM, cos.shape[1]), np.float32)
    return (np.concatenate([one, cos]),) + tuple(np.concatenate([zero, s]) for s in sins)


def _diff_rope_tables():
    c, s = _rope_angles(DIFF_HEAD_DIM)
    z = np.zeros_like(s)
    reps = DIFF_WIDTH // DIFF_HEAD_DIM
    cos = np.tile(np.concatenate([c, c], axis=1), (1, reps))
    sa = np.tile(np.concatenate([z, s], axis=1), (1, reps))
    sb = np.tile(np.concatenate([-s, z], axis=1), (1, reps))
    return _with_identity(cos, sa, sb)


def _mla_rope_tables():
    c, s = _rope_angles(MLA_ROPE)
    n = c.shape[0]
    ones, zeros = (lambda w: np.ones((n, w), np.float32)), (lambda w: np.zeros((n, w), np.float32))
    cos = np.concatenate([ones(MLA_NOPE), c, c, ones(MLA_PAD)], axis=1)
    sin = np.concatenate([zeros(MLA_NOPE), -s, s, zeros(MLA_PAD)], axis=1)
    return _with_identity(cos, sin)


def _swap_rope_halves(w):
    half = MLA_ROPE // 2
    return jnp.concatenate([w[..., half:], w[..., :half]], axis=-1)


def _mla_weights(w_in_odd, w_q_up, w_kv_up):
    w_in_odd, w_q_up, w_kv_up = (w.astype(BF16) for w in (w_in_odd, w_q_up, w_kv_up))
    zeros = lambda *s: jnp.zeros(s, BF16)
    kr0 = MLA_Q_RANK + MLA_KV_RANK
    w_kr = w_in_odd[:, :, kr0:]
    lead = (N_ODD, D_MODEL)
    w_in = jnp.concatenate(
        [w_in_odd[:, :, :kr0],
         zeros(*lead, MLA_NOPE), w_kr, zeros(*lead, MLA_PAD),
         zeros(*lead, MLA_NOPE), _swap_rope_halves(w_kr), zeros(*lead, MLA_PAD)], axis=-1)
    wq4 = w_q_up.reshape(N_ODD, MLA_Q_RANK, MLA_HEADS, MLA_NOPE + MLA_ROPE)
    lead = (N_ODD, MLA_Q_RANK, MLA_HEADS)
    wq_main = jnp.concatenate([wq4, zeros(*lead, MLA_PAD)], axis=-1)
    wq_swap = jnp.concatenate([zeros(*lead, MLA_NOPE), _swap_rope_halves(wq4[..., MLA_NOPE:]),
                               zeros(*lead, MLA_PAD)], axis=-1)
    wq = jnp.concatenate([wq_main.reshape(N_ODD, MLA_Q_RANK, MLA_Q_W),
                          wq_swap.reshape(N_ODD, MLA_Q_RANK, MLA_Q_W)], axis=-1)
    wkv = w_kv_up.reshape(N_ODD, MLA_KV_RANK, MLA_HEADS, MLA_NOPE + MLA_V)
    wk_top = jnp.pad(wkv[..., :MLA_NOPE], ((0, 0), (0, 0), (0, 0), (0, MLA_HEAD_PAD - MLA_NOPE)))
    sel = jnp.pad(jnp.eye(MLA_ROPE, dtype=BF16), ((MLA_NOPE, MLA_PAD), (MLA_NOPE, MLA_PAD)))
    wk_bot = jnp.broadcast_to(jnp.tile(sel, (1, MLA_HEADS)), (N_ODD, MLA_HEAD_PAD, MLA_Q_W))
    wk = jnp.concatenate([wk_top.reshape(N_ODD, MLA_KV_RANK, MLA_Q_W), wk_bot], axis=1)
    even = (jnp.arange(MLA_HEADS) % 2 == 0)[:, None]
    wv4 = wkv[..., MLA_NOPE:]
    zero_v = jnp.zeros_like(wv4)
    wv = jnp.concatenate([jnp.where(even, wv4, zero_v).reshape(N_ODD, MLA_KV_RANK, MLA_O_W),
                          jnp.where(even, zero_v, wv4).reshape(N_ODD, MLA_KV_RANK, MLA_O_W)], axis=-1)
    return w_in, wq, wk, wv


def kernel(x_prompt, x_sample, state_s5_re, state_s5_im, cache_diff_k, cache_diff_v, cache_mla_ckv, cache_mla_krope, c, c_ctx, w_mod, b_mod, g_pre_mix, g_post_mix, g_pre_ffn, g_post_ffn, w_ffn_gate, w_ffn_up, w_ffn_down, w_in_even, w_out_even, s5_lam_re, s5_lam_im, s5_log_dt, s5_b_re, s5_b_im, s5_c_re, s5_c_im, s5_d, s5_glu_w, s5_glu_b, diff_lam_q1, diff_lam_k1, diff_lam_q2, diff_lam_k2, diff_subln_g, w_in_odd, mla_q_norm_g, mla_w_q_up, mla_kv_norm_g, mla_w_kv_up, w_out_odd):
    xs = (x_prompt.reshape(N_P, D_MODEL), x_sample.reshape(N_S, D_MODEL))
    conds = jnp.concatenate([c_ctx[None, :], c, jnp.zeros((N_COND - 1 - DEC_BATCH, D_MODEL), F32)], axis=0)
    mod = _modulation(conds, w_mod, b_mod).reshape(DEPTH, N_COND, N_MOD, D_MODEL)

    gains = (g_post_mix, g_pre_ffn, g_post_ffn)
    ffn_f32 = (w_ffn_gate, w_ffn_up, w_ffn_down)
    w_in_e = w_in_even.astype(BF16)
    w_out_e = w_out_even.astype(BF16)
    glu_w = s5_glu_w.astype(BF16)
    a_re, a_im, b_in, c_out = _s5_params(s5_lam_re, s5_lam_im, s5_log_dt, s5_b_re, s5_b_im, s5_c_re, s5_c_im)
    a = jnp.stack([a_re[:, 0], a_im[:, 0], a_re[:, 1], a_im[:, 1]], axis=1).reshape(N_EVEN, 4, S5_STATE_W)
    st = lambda s: jnp.moveaxis(s, 0, 2).reshape(N_EVEN, 2, 1, DEC_BATCH, S5_STATE_W)
    h0 = jnp.concatenate([jnp.zeros((N_EVEN, 2, 2, SUBLANES, 2 * S5_STATE_W), F32),
                          jnp.concatenate([st(state_s5_re), st(state_s5_im)], axis=-1)], axis=2)
    diff_ctx = (cache_diff_k.astype(BF16).reshape(DEC_BATCH, N_EVEN, PAST_LEN, DIFF_WIDTH),
                cache_diff_v.astype(BF16).reshape(DEC_BATCH, N_EVEN, PAST_LEN, DIFF_WIDTH))
    lam_vecs = (diff_lam_q1, diff_lam_k1, diff_lam_q2, diff_lam_k2)
    w_in_o, wq, wk, wv = _mla_weights(w_in_odd, mla_w_q_up, mla_w_kv_up)
    w_out_o = w_out_odd.astype(BF16)
    ckr_ctx = jnp.concatenate(
        [cache_mla_ckv.astype(BF16),
         jnp.pad(cache_mla_krope.astype(BF16), ((0, 0), (0, 0), (0, 0), (MLA_NOPE, MLA_PAD)))], axis=-1)
    diff_tabs = _diff_rope_tables()
    mla_tabs = _mla_rope_tables()

    s5_re_list, s5_im_list, dk_list, dv_list, ckv_list, kr_list = [], [], [], [], [], []
    for l in range(DEPTH):
        last = l == DEPTH - 1
        n_stream = 2 if last else 1
        next_cast = () if last else tuple((w, l + 1) for w in ffn_f32)
        if l % 2 == 0:
            e = l // 2
            lam_init = 0.8 - 0.6 * math.exp(-0.3 * l)
            pre = _even_pre(l, xs, mod, g_pre_mix, w_in_e, diff_tabs,
                            cast=tuple((w, 0) for w in ffn_f32) if l == 0 else ())
            u_ctx, u_lat, q, k, v, kf, vf = pre[:7]
            if l == 0:
                ffn_w = tuple(pre[7:])
            y_f, y_b, fin_f, fin_b = _s5_scan(e, u_ctx, u_lat, a, b_in, c_out, h0)
            fin = jnp.stack([fin_f[:2].reshape(BATCH, 2, S5_GROUPS, S5_STATE),
                             fin_b[:2].reshape(BATCH, 2, S5_GROUPS, S5_STATE)], axis=1)
            s5_re_list.append(fin[:, :, 0])
            s5_im_list.append(fin[:, :, 1])
            da_p = _diff_attention(e, q, k, v, None, lam_vecs, diff_subln_g, lam_init,
                                   batch=BATCH, n=SEQ, row0=0, tq=SEQ)
            da_s = _diff_attention(e, q, k, v, diff_ctx, lam_vecs, diff_subln_g, lam_init,
                                   batch=DEC_BATCH, n=DEC_SEQ, row0=N_P, tq=DIFF_TQ)
            post = _even_post(l, xs, mod, y_f, y_b, u_ctx, u_lat, s5_d, glu_w, s5_glu_b, da_p, da_s, w_out_e,
                              gains + ffn_w, last, cast=next_cast)
            xs, ffn_w = post[:n_stream], tuple(post[n_stream:])
            dk_list.append(kf.reshape(BATCH, SEQ, DIFF_HEADS, 2, DIFF_HEAD_DIM))
            dv_list.append(vf.reshape(BATCH, SEQ, DIFF_HEADS, 2 * DIFF_HEAD_DIM))
        else:
            o = l // 2
            q, ckr, ckv, krp = _odd_pre(l, xs, mod, g_pre_mix, w_in_o, mla_q_norm_g, mla_kv_norm_g, wq, mla_tabs)
            at_p = _mla_attention(o, q, ckr, None, wk, wv, batch=BATCH, n=SEQ, row0=0, tq=SEQ)
            at_s = _mla_attention(o, q, ckr, ckr_ctx, wk, wv, batch=DEC_BATCH, n=DEC_SEQ, row0=N_P, tq=MLA_TQ)
            post = _odd_post(l, xs, mod, at_p, at_s, w_out_o, gains + ffn_w, last, cast=next_cast)
            xs, ffn_w = post[:n_stream], tuple(post[n_stream:])
            ckv_list.append(ckv.reshape(BATCH, SEQ, MLA_KV_RANK))
            kr_list.append(krp[:, MLA_NOPE:MLA_NOPE + MLA_ROPE].reshape(BATCH, SEQ, MLA_ROPE))

    return (xs[0].reshape(BATCH, SEQ, D_MODEL), xs[1].reshape(DEC_BATCH, DEC_SEQ, D_MODEL),
            jnp.stack(s5_re_list, axis=1), jnp.stack(s5_im_list, axis=1),
            jnp.stack(dk_list, axis=1), jnp.stack(dv_list, axis=1),
            jnp.stack(ckv_list, axis=1), jnp.stack(kr_list, axis=1))
```
